```python
import jax, jax.numpy as jnp
from jax import lax
import numpy as np

D_MODEL = 2048
BATCH = 8
SEQ = 4096
DEPTH = 4

N_META = 16
BLOCK = 128
WINDOW = 128
HEAD_DIM = 64
SWA_HEADS = 16
SWA_KV_HEADS = 4
SWA_GROUP = SWA_HEADS // SWA_KV_HEADS
SB_HEADS = 16
SWA_WIDTH = SWA_HEADS * HEAD_DIM
SWA_KV_WIDTH = SWA_KV_HEADS * HEAD_DIM
SB_WIDTH = SB_HEADS * HEAD_DIM
MIX_WIDTH = SWA_WIDTH + SB_WIDTH
PROJ_WIDTH = SWA_WIDTH + 2 * SWA_KV_WIDTH + 3 * SB_WIDTH
D_FF = ((8 * D_MODEL + 3 * 256 - 1) // (3 * 256)) * 256
EPS = 1e-6
NEG = -1e30

kernel_name = "hybrid_swa_sink_stickbreaking_swiglu"


def rmsnorm(x, g):
    xf = x.astype(jnp.float32)
    y = xf * lax.rsqrt(jnp.mean(xf * xf, axis=-1, keepdims=True) + EPS)
    return (y * g.astype(jnp.float32)).astype(x.dtype)


def alibi_slopes(n_heads):
    h = jnp.arange(1, n_heads + 1, dtype=jnp.float32)
    return jnp.exp2(-8.0 * h / n_heads)


def swa_attention(q, k, v, sinks):
    b, L, _, d = q.shape
    nb = (L - N_META) // BLOCK
    f32 = jnp.float32
    scale = d ** -0.5
    slopes = alibi_slopes(SWA_HEADS).reshape(SWA_KV_HEADS, SWA_GROUP)[:, :, None, None]
    sink = sinks.astype(f32).reshape(SWA_KV_HEADS, SWA_GROUP)

    qg = q.reshape(b, L, SWA_KV_HEADS, SWA_GROUP, d)
    qm, km, vm = qg[:, :N_META], k[:, :N_META], v[:, :N_META]
    qr = qg[:, N_META:].reshape(b, nb, BLOCK, SWA_KV_HEADS, SWA_GROUP, d)
    kr = k[:, N_META:].reshape(b, nb, BLOCK, SWA_KV_HEADS, d)
    vr = v[:, N_META:].reshape(b, nb, BLOCK, SWA_KV_HEADS, d)
    pad = jnp.zeros_like(kr[:, :1])
    kb = jnp.concatenate([jnp.concatenate([pad, kr[:, :-1]], axis=1), kr], axis=2)
    vb = jnp.concatenate([jnp.concatenate([pad, vr[:, :-1]], axis=1), vr], axis=2)

    blk = jnp.arange(nb)[:, None, None]
    qi = jnp.arange(BLOCK)[None, :, None]
    kj = jnp.arange(2 * BLOCK)[None, None, :]
    delta = qi - kj + BLOCK
    key_real = blk * BLOCK + kj - BLOCK
    band_ok = (delta >= 0) & (delta < WINDOW) & (key_real >= 0)

    s_band = jnp.einsum('bnqhgd,bnkhd->bnhgqk', qr, kb).astype(f32) * scale
    s_band = s_band - slopes * delta.astype(f32)
    s_band = jnp.where(band_ok[None, :, None, None], s_band, NEG)

    t_pos = N_META + blk * BLOCK + qi
    delta_m = (t_pos - jnp.arange(N_META)[None, None, :]).astype(f32)
    s_meta = jnp.einsum('bnqhgd,bmhd->bnhgqm', qr, km).astype(f32) * scale
    s_meta = s_meta - slopes * delta_m[:, None, None]

    sink_col = jnp.broadcast_to(sink[None, None, :, :, None, None], s_band.shape[:-1] + (1,))
    p = jax.nn.softmax(jnp.concatenate([s_band, s_meta, sink_col], axis=-1), axis=-1)
    p_band = p[..., :2 * BLOCK].astype(v.dtype)
    p_meta = p[..., 2 * BLOCK:2 * BLOCK + N_META].astype(v.dtype)
    out_r = (jnp.einsum('bnhgqk,bnkhd->bnqhgd', p_band, vb)
             + jnp.einsum('bnhgqm,bmhd->bnqhgd', p_meta, vm))
    out_r = out_r.reshape(b, nb * BLOCK, SWA_WIDTH)

    mpos = jnp.arange(N_META)
    delta_mm = (mpos[:, None] - mpos[None, :])
    s_mm = jnp.einsum('bqhgd,bkhd->bhgqk', qm, km).astype(f32) * scale
    s_mm = s_mm - slopes * delta_mm.astype(f32)
    s_mm = jnp.where((delta_mm >= 0) & (delta_mm < WINDOW), s_mm, NEG)
    sink_mm = jnp.broadcast_to(sink[None, :, :, None, None], s_mm.shape[:-1] + (1,))
    p_mm = jax.nn.softmax(jnp.concatenate([s_mm, sink_mm], axis=-1), axis=-1)[..., :N_META]
    out_m = jnp.einsum('bhgqk,bkhd->bqhgd', p_mm.astype(v.dtype), vm).reshape(b, N_META, SWA_WIDTH)
    return jnp.concatenate([out_m, out_r], axis=1)


def sb_block(qb, qpos, k, v, kpos):
    d = qb.shape[-1]
    z = jnp.einsum('bqhd,bkhd->bhqk', qb, k).astype(jnp.float32) * (d ** -0.5)
    causal = kpos[None, :] < qpos[:, None]
    log1m = jnp.where(causal, -jax.nn.softplus(z), 0.0)
    after = lax.cumsum(log1m, axis=3, reverse=True) - log1m
    a = jnp.where(causal, jnp.exp(jax.nn.log_sigmoid(z) + after), 0.0)
    return jnp.einsum('bhqk,bkhd->bqhd', a.astype(v.dtype), v)


def stick_breaking_attention(q, k, v):
    b, L, h, d = q.shape
    nb = (L - N_META) // BLOCK
    kpos = jnp.arange(L)
    qr = q[:, N_META:].reshape(b, nb, BLOCK, h, d).swapaxes(0, 1)
    qpos = N_META + jnp.arange(nb * BLOCK).reshape(nb, BLOCK)
    out_r = lax.map(lambda xs: sb_block(xs[0], xs[1], k, v, kpos), (qr, qpos))
    out_r = out_r.swapaxes(0, 1).reshape(b, nb * BLOCK, h * d)
    mpos = jnp.arange(N_META)
    out_m = sb_block(q[:, :N_META], mpos, k[:, :N_META], v[:, :N_META], mpos).reshape(b, N_META, h * d)
    return jnp.concatenate([out_m, out_r], axis=1)


def _fwd_setup_inputs(seed: int = 0) -> dict:
    key = jax.random.key(seed)
    ks = jax.random.split(key, 16)
    f32 = jnp.float32
    nrm = lambda k, shape, scale: jax.random.normal(k, shape, f32) * scale
    gain = lambda k, shape: 1.0 + 0.05 * jax.random.normal(k, shape, f32)
    return {
        "x": nrm(ks[0], (BATCH, SEQ, D_MODEL), 1.0),
        "meta_tokens": nrm(ks[1], (N_META, D_MODEL), 1.0),
        "attn_norm_g": gain(ks[2], (DEPTH, D_MODEL)),
        "w_in": nrm(ks[3], (DEPTH, D_MODEL, PROJ_WIDTH), D_MODEL ** -0.5),
        "q_norm_g": gain(ks[4], (DEPTH, HEAD_DIM)),
        "k_norm_g": gain(ks[5], (DEPTH, HEAD_DIM)),
        "attn_sinks": nrm(ks[6], (DEPTH, SWA_HEADS), 0.5),
        "swa_out_g": gain(ks[7], (DEPTH, SWA_WIDTH)),
        "sb_out_g": gain(ks[8], (DEPTH, SB_WIDTH)),
        "w_o": nrm(ks[9], (DEPTH, MIX_WIDTH, D_MODEL), MIX_WIDTH ** -0.5),
        "ffn_norm_g": gain(ks[10], (DEPTH, D_MODEL)),
        "w_gate": nrm(ks[11], (DEPTH, D_MODEL, D_FF), D_MODEL ** -0.5),
        "w_up": nrm(ks[12], (DEPTH, D_MODEL, D_FF), D_MODEL ** -0.5),
        "w_down": nrm(ks[13], (DEPTH, D_FF, D_MODEL), D_FF ** -0.5),
    }


def _fwd_reference(x, meta_tokens, attn_norm_g, w_in, q_norm_g, k_norm_g, attn_sinks,
              swa_out_g, sb_out_g, w_o, ffn_norm_g, w_gate, w_up, w_down):
    b = x.shape[0]
    meta = jnp.broadcast_to(meta_tokens[None].astype(x.dtype), (b, N_META, D_MODEL))
    h = jnp.concatenate([meta, x], axis=1)
    L = h.shape[1]
    split_at = np.cumsum([SWA_WIDTH, SWA_KV_WIDTH, SWA_KV_WIDTH, SB_WIDTH, SB_WIDTH]).tolist()
    for l in range(DEPTH):
        hn = rmsnorm(h, attn_norm_g[l])
        proj = jnp.einsum('bld,dp->blp', hn, w_in[l])
        qa, ka, va, qb, kb, vb = jnp.split(proj, split_at, axis=-1)
        qa = rmsnorm(qa.reshape(b, L, SWA_HEADS, HEAD_DIM), q_norm_g[l])
        ka = rmsnorm(ka.reshape(b, L, SWA_KV_HEADS, HEAD_DIM), k_norm_g[l])
        va = va.reshape(b, L, SWA_KV_HEADS, HEAD_DIM)
        out_a = swa_attention(qa, ka, va, attn_sinks[l])
        out_b = stick_breaking_attention(qb.reshape(b, L, SB_HEADS, HEAD_DIM),
                                         kb.reshape(b, L, SB_HEADS, HEAD_DIM),
                                         vb.reshape(b, L, SB_HEADS, HEAD_DIM))
        mixed = jnp.concatenate([rmsnorm(out_a, swa_out_g[l]), rmsnorm(out_b, sb_out_g[l])], axis=-1)
        h = h + jnp.einsum('blm,md->bld', mixed, w_o[l])
        hn = rmsnorm(h, ffn_norm_g[l])
        g = jnp.einsum('bld,df->blf', hn, w_gate[l])
        u = jnp.einsum('bld,df->blf', hn, w_up[l])
        h = h + jnp.einsum('blf,fd->bld', jax.nn.silu(g) * u, w_down[l])
    return h[:, N_META:]


import jax as _jax
import jax.numpy as _jnp

TWIN_FORMAT = 'train_step'
FWD_PARAMS = ['x', 'meta_tokens', 'attn_norm_g', 'w_in', 'q_norm_g', 'k_norm_g', 'attn_sinks', 'swa_out_g', 'sb_out_g', 'w_o', 'ffn_norm_g', 'w_gate', 'w_up', 'w_down']
TWIN_WEIGHTS = ['meta_tokens', 'attn_norm_g', 'w_in', 'q_norm_g', 'k_norm_g', 'attn_sinks', 'swa_out_g', 'sb_out_g', 'w_o', 'ffn_norm_g', 'w_gate', 'w_up', 'w_down']
TWIN_DIFF_INPUT = 'x'
TWIN_INPUTS = ['x', 'meta_tokens', 'attn_norm_g', 'w_in', 'q_norm_g', 'k_norm_g', 'attn_sinks', 'swa_out_g', 'sb_out_g', 'w_o', 'ffn_norm_g', 'w_gate', 'w_up', 'w_down', 'loss_target', 'm_meta_tokens', 'm_attn_norm_g', 'm_w_in', 'm_q_norm_g', 'm_k_norm_g', 'm_attn_sinks', 'm_swa_out_g', 'm_sb_out_g', 'm_w_o', 'm_ffn_norm_g', 'm_w_gate', 'm_w_up', 'm_w_down', 'v_meta_tokens', 'v_attn_norm_g', 'v_w_in', 'v_q_norm_g', 'v_k_norm_g', 'v_attn_sinks', 'v_swa_out_g', 'v_sb_out_g', 'v_w_o', 'v_ffn_norm_g', 'v_w_gate', 'v_w_up', 'v_w_down']
TWIN_OUTPUTS = ['loss', 'grad_x', 'grad_meta_tokens', 'grad_attn_norm_g', 'grad_w_in', 'grad_q_norm_g', 'grad_k_norm_g', 'grad_attn_sinks', 'grad_swa_out_g', 'grad_sb_out_g', 'grad_w_o', 'grad_ffn_norm_g', 'grad_w_gate', 'grad_w_up', 'grad_w_down', 'delta_meta_tokens', 'delta_attn_norm_g', 'delta_w_in', 'delta_q_norm_g', 'delta_k_norm_g', 'delta_attn_sinks', 'delta_swa_out_g', 'delta_sb_out_g', 'delta_w_o', 'delta_ffn_norm_g', 'delta_w_gate', 'delta_w_up', 'delta_w_down', 'new_m_meta_tokens', 'new_m_attn_norm_g', 'new_m_w_in', 'new_m_q_norm_g', 'new_m_k_norm_g', 'new_m_attn_sinks', 'new_m_swa_out_g', 'new_m_sb_out_g', 'new_m_w_o', 'new_m_ffn_norm_g', 'new_m_w_gate', 'new_m_w_up', 'new_m_w_down', 'new_v_meta_tokens', 'new_v_attn_norm_g', 'new_v_w_in', 'new_v_q_norm_g', 'new_v_k_norm_g', 'new_v_attn_sinks', 'new_v_swa_out_g', 'new_v_sb_out_g', 'new_v_w_o', 'new_v_ffn_norm_g', 'new_v_w_gate', 'new_v_w_up', 'new_v_w_down']
TWIN_LEAF_KINDS = {'loss': 'loss', 'grad_x': 'grad_x', 'grad_meta_tokens': 'grad_w', 'grad_attn_norm_g': 'grad_w', 'grad_w_in': 'grad_w', 'grad_q_norm_g': 'grad_w', 'grad_k_norm_g': 'grad_w', 'grad_attn_sinks': 'grad_w', 'grad_swa_out_g': 'grad_w', 'grad_sb_out_g': 'grad_w', 'grad_w_o': 'grad_w', 'grad_ffn_norm_g': 'grad_w', 'grad_w_gate': 'grad_w', 'grad_w_up': 'grad_w', 'grad_w_down': 'grad_w', 'delta_meta_tokens': 'delta_w', 'delta_attn_norm_g': 'delta_w', 'delta_w_in': 'delta_w', 'delta_q_norm_g': 'delta_w', 'delta_k_norm_g': 'delta_w', 'delta_attn_sinks': 'delta_w', 'delta_swa_out_g': 'delta_w', 'delta_sb_out_g': 'delta_w', 'delta_w_o': 'delta_w', 'delta_ffn_norm_g': 'delta_w', 'delta_w_gate': 'delta_w', 'delta_w_up': 'delta_w', 'delta_w_down': 'delta_w', 'new_m_meta_tokens': 'new_m', 'new_m_attn_norm_g': 'new_m', 'new_m_w_in': 'new_m', 'new_m_q_norm_g': 'new_m', 'new_m_k_norm_g': 'new_m', 'new_m_attn_sinks': 'new_m', 'new_m_swa_out_g': 'new_m', 'new_m_sb_out_g': 'new_m', 'new_m_w_o': 'new_m', 'new_m_ffn_norm_g': 'new_m', 'new_m_w_gate': 'new_m', 'new_m_w_up': 'new_m', 'new_m_w_down': 'new_m', 'new_v_meta_tokens': 'new_v', 'new_v_attn_norm_g': 'new_v', 'new_v_w_in': 'new_v', 'new_v_q_norm_g': 'new_v', 'new_v_k_norm_g': 'new_v', 'new_v_attn_sinks': 'new_v', 'new_v_swa_out_g': 'new_v', 'new_v_sb_out_g': 'new_v', 'new_v_w_o': 'new_v', 'new_v_ffn_norm_g': 'new_v', 'new_v_w_gate': 'new_v', 'new_v_w_up': 'new_v', 'new_v_w_down': 'new_v'}


def _forward(args):
    return _fwd_reference(*[args[k] for k in FWD_PARAMS])


def _output_shape():
    def fwd():
        inp = _fwd_setup_inputs(0)
        return _fwd_reference(*[inp[k] for k in FWD_PARAMS])
    out = _jax.eval_shape(fwd)
    return out.shape, out.dtype

N_MICROBATCH = 1
ADAM_LR = 0.001
ADAM_B1 = 0.9
ADAM_B2 = 0.999
ADAM_EPS = 1e-08
ADAM_WD = 0.01
ADAM_STEP = 10
PER_EXAMPLE_BATCH_AXIS = {'x': 0, 'loss_target': 0}
SHARED_INPUTS = []
_WEIGHT_DTYPES = {'meta_tokens': _jnp.float32, 'attn_norm_g': _jnp.float32, 'w_in': _jnp.float32, 'q_norm_g': _jnp.float32, 'k_norm_g': _jnp.float32, 'attn_sinks': _jnp.float32, 'swa_out_g': _jnp.float32, 'sb_out_g': _jnp.float32, 'w_o': _jnp.float32, 'ffn_norm_g': _jnp.float32, 'w_gate': _jnp.float32, 'w_up': _jnp.float32, 'w_down': _jnp.float32}
MOMENT_SCALE = {'meta_tokens': 1.873542e-01, 'attn_norm_g': 9.757872e-01, 'w_in': 6.580692e-01, 'q_norm_g': 1.422722e+00, 'k_norm_g': 1.398261e+00, 'attn_sinks': 1.488547e+00, 'swa_out_g': 1.931240e+01, 'sb_out_g': 1.599292e+01, 'w_o': 1.002800e+00, 'ffn_norm_g': 1.239197e+01, 'w_gate': 1.748039e-01, 'w_up': 1.892862e-01, 'w_down': 3.055942e-01}


def _to_microbatches(a, axis):
    t = _jnp.moveaxis(a, axis, 0)
    t = t.reshape((N_MICROBATCH, t.shape[0] // N_MICROBATCH) + t.shape[1:])
    return _jnp.moveaxis(t, 1, axis + 1)


def setup_inputs(seed: int = 0) -> dict:
    inp = _fwd_setup_inputs(seed)
    key = _jax.random.fold_in(_jax.random.key(seed), 7919)
    shape, _ = _output_shape()
    out = dict(inp)
    out["loss_target"] = _jax.random.normal(_jax.random.fold_in(key, 0), shape, _jnp.float32)
    for i, name in enumerate(TWIN_WEIGHTS):
        w = inp[name].astype(_jnp.float32)
        if MOMENT_SCALE is None:
            s = _jnp.sqrt(_jnp.mean(_jnp.square(w)) + 1e-30)
        else:
            s = MOMENT_SCALE[name]
        km, kv = _jax.random.split(_jax.random.fold_in(key, i + 1))
        out[name] = w
        out["m_" + name] = s * _jax.random.normal(km, w.shape, _jnp.float32)
        out["v_" + name] = (s * s) * _jax.random.uniform(kv, w.shape, _jnp.float32, 0.5, 1.5)
    if N_MICROBATCH > 1:
        for name, axis in PER_EXAMPLE_BATCH_AXIS.items():
            out[name] = _to_microbatches(out[name], axis)
    return {'x': out['x'], 'meta_tokens': out['meta_tokens'], 'attn_norm_g': out['attn_norm_g'], 'w_in': out['w_in'], 'q_norm_g': out['q_norm_g'], 'k_norm_g': out['k_norm_g'], 'attn_sinks': out['attn_sinks'], 'swa_out_g': out['swa_out_g'], 'sb_out_g': out['sb_out_g'], 'w_o': out['w_o'], 'ffn_norm_g': out['ffn_norm_g'], 'w_gate': out['w_gate'], 'w_up': out['w_up'], 'w_down': out['w_down'], 'loss_target': out['loss_target'], 'm_meta_tokens': out['m_meta_tokens'], 'm_attn_norm_g': out['m_attn_norm_g'], 'm_w_in': out['m_w_in'], 'm_q_norm_g': out['m_q_norm_g'], 'm_k_norm_g': out['m_k_norm_g'], 'm_attn_sinks': out['m_attn_sinks'], 'm_swa_out_g': out['m_swa_out_g'], 'm_sb_out_g': out['m_sb_out_g'], 'm_w_o': out['m_w_o'], 'm_ffn_norm_g': out['m_ffn_norm_g'], 'm_w_gate': out['m_w_gate'], 'm_w_up': out['m_w_up'], 'm_w_down': out['m_w_down'], 'v_meta_tokens': out['v_meta_tokens'], 'v_attn_norm_g': out['v_attn_norm_g'], 'v_w_in': out['v_w_in'], 'v_q_norm_g': out['v_q_norm_g'], 'v_k_norm_g': out['v_k_norm_g'], 'v_attn_sinks': out['v_attn_sinks'], 'v_swa_out_g': out['v_swa_out_g'], 'v_sb_out_g': out['v_sb_out_g'], 'v_w_o': out['v_w_o'], 'v_ffn_norm_g': out['v_ffn_norm_g'], 'v_w_gate': out['v_w_gate'], 'v_w_up': out['v_w_up'], 'v_w_down': out['v_w_down']}


def _loss(weights, diff, rest, loss_target):
    with _jax.named_scope("forward"):
        args = {**rest, TWIN_DIFF_INPUT: diff, **{k: w.astype(_WEIGHT_DTYPES[k]) for k, w in weights.items()}}
        y = _forward(args)
    with _jax.named_scope("loss_head"):
        err = _jnp.square(y.astype(_jnp.float32) - loss_target)
        return 0.5 * _jnp.sum(_jnp.mean(err, axis=-1)) if err.ndim else 0.5 * err


def _adamw(w, g, m, v):
    m = ADAM_B1 * m + (1.0 - ADAM_B1) * g
    v = ADAM_B2 * v + (1.0 - ADAM_B2) * _jnp.square(g)
    m_hat = m / (1.0 - ADAM_B1 ** ADAM_STEP)
    v_hat = v / (1.0 - ADAM_B2 ** ADAM_STEP)
    delta = -ADAM_LR * (m_hat / (_jnp.sqrt(v_hat) + ADAM_EPS) + ADAM_WD * w)
    return delta, m, v


def reference(x, meta_tokens, attn_norm_g, w_in, q_norm_g, k_norm_g, attn_sinks, swa_out_g, sb_out_g, w_o, ffn_norm_g, w_gate, w_up, w_down, loss_target, m_meta_tokens, m_attn_norm_g, m_w_in, m_q_norm_g, m_k_norm_g, m_attn_sinks, m_swa_out_g, m_sb_out_g, m_w_o, m_ffn_norm_g, m_w_gate, m_w_up, m_w_down, v_meta_tokens, v_attn_norm_g, v_w_in, v_q_norm_g, v_k_norm_g, v_attn_sinks, v_swa_out_g, v_sb_out_g, v_w_o, v_ffn_norm_g, v_w_gate, v_w_up, v_w_down):
    given = dict(x=x, meta_tokens=meta_tokens, attn_norm_g=attn_norm_g, w_in=w_in, q_norm_g=q_norm_g, k_norm_g=k_norm_g, attn_sinks=attn_sinks, swa_out_g=swa_out_g, sb_out_g=sb_out_g, w_o=w_o, ffn_norm_g=ffn_norm_g, w_gate=w_gate, w_up=w_up, w_down=w_down, loss_target=loss_target, m_meta_tokens=m_meta_tokens, m_attn_norm_g=m_attn_norm_g, m_w_in=m_w_in, m_q_norm_g=m_q_norm_g, m_k_norm_g=m_k_norm_g, m_attn_sinks=m_attn_sinks, m_swa_out_g=m_swa_out_g, m_sb_out_g=m_sb_out_g, m_w_o=m_w_o, m_ffn_norm_g=m_ffn_norm_g, m_w_gate=m_w_gate, m_w_up=m_w_up, m_w_down=m_w_down, v_meta_tokens=v_meta_tokens, v_attn_norm_g=v_attn_norm_g, v_w_in=v_w_in, v_q_norm_g=v_q_norm_g, v_k_norm_g=v_k_norm_g, v_attn_sinks=v_attn_sinks, v_swa_out_g=v_swa_out_g, v_sb_out_g=v_sb_out_g, v_w_o=v_w_o, v_ffn_norm_g=v_ffn_norm_g, v_w_gate=v_w_gate, v_w_up=v_w_up, v_w_down=v_w_down)
    weights = {n: given[n] for n in TWIN_WEIGHTS}
    shared = {n: given[n] for n in SHARED_INPUTS}
    per_example = {n: given[n] for n in ['x']}
    grad_fn = _jax.value_and_grad(_loss, argnums=(0, 1))

    def one_microbatch(ex, loss_target):
        ex = dict(ex)
        diff = ex.pop(TWIN_DIFF_INPUT)
        return grad_fn(weights, diff, {**shared, **ex}, loss_target)

    if N_MICROBATCH == 1:
        loss, (grad_w, grad_x) = one_microbatch(per_example, given["loss_target"])
    else:
        def body(carry, xs):
            loss_sum, grad_sum = carry
            l_k, (gw_k, gx_k) = one_microbatch(xs[0], xs[1])
            with _jax.named_scope("update"):
                return (loss_sum + l_k, _jax.tree.map(_jnp.add, grad_sum, gw_k)), gx_k

        init = (_jnp.zeros((), _jnp.float32), _jax.tree.map(_jnp.zeros_like, weights))
        (loss, grad_w), grad_x = _jax.lax.scan(body, init, (per_example, given["loss_target"]))
    with _jax.named_scope("update"):
        delta_w, new_m, new_v = {}, {}, {}
        for n in TWIN_WEIGHTS:
            delta_w[n], new_m[n], new_v[n] = _adamw(weights[n], grad_w[n], given["m_" + n], given["v_" + n])
    return (loss, grad_x, *[grad_w[n] for n in TWIN_WEIGHTS], *[delta_w[n] for n in TWIN_WEIGHTS],
            *[new_m[n] for n in TWIN_WEIGHTS], *[new_v[n] for n in TWIN_WEIGHTS])
```

```python
import jax
import jax.numpy as jnp
from jax import lax
from jax.experimental import pallas as pl
from jax.experimental.pallas import tpu as pltpu

f32, bf16 = jnp.float32, jnp.bfloat16
S = jax.ShapeDtypeStruct
BS = pl.BlockSpec

N_META = 16
TILE = 128
PAD = TILE - N_META
HEAD_DIM = 64
SWA_HEADS = 16
SWA_KV_HEADS = 4
SB_HEADS = 16
EPS = 1e-6
NEG = -1e30
ADAM_LR, ADAM_B1, ADAM_B2, ADAM_EPS, ADAM_WD, ADAM_STEP = 0.001, 0.9, 0.999, 1e-08, 0.01, 10
AXES = ("x", "y", "c")
NDEV = 8
VMEM_LIMIT_V7X = 56 * 1024 * 1024
MESH = pl.DeviceIdType.MESH
ANY = pl.BlockSpec(memory_space=pl.ANY)

NN = ((1,), (0,))
NT = ((1,), (1,))
TN = ((0,), (0,))


def _cp(*sem):
    return pltpu.CompilerParams(dimension_semantics=sem if sem else None, vmem_limit_bytes=VMEM_LIMIT_V7X)


def _pick(n, cands):
    for c in cands:
        if n % c == 0:
            return c
    raise ValueError(f"no tile for {n} in {cands}")


def _dot(a, b, dims):
    return lax.dot_general(a, b, (dims, ((), ())), preferred_element_type=f32)


def _embed(x, meta_full):
    seq, d = x.shape
    meta_pad = jnp.pad(meta_full, ((PAD, 0), (0, 0)))

    def body(x_ref, m_ref, o_ref):
        i = pl.program_id(0)

        @pl.when(i == 0)
        def _():
            o_ref[...] = m_ref[...]

        @pl.when(i > 0)
        def _():
            o_ref[...] = x_ref[...]

    return pl.pallas_call(
        body, out_shape=S((TILE + seq, d), f32), grid=(1 + seq // TILE,),
        in_specs=[BS((TILE, d), lambda i: (jnp.maximum(i - 1, 0), 0)), BS((TILE, d), lambda i: (0, 0))],
        out_specs=BS((TILE, d), lambda i: (i, 0)), name="embed", compiler_params=_cp("arbitrary"))(x, meta_pad)


def _rmsnorm_fwd(xs, gain, name):
    lp = xs[0].shape[0]
    widths = [x.shape[1] for x in xs]
    w = sum(widths)
    tr = _pick(lp, (384, 256, 128))
    n = len(xs)

    def body(*refs):
        g_ref, o_ref = refs[n], refs[n + 1]
        off = 0
        for k in range(n):
            sl = slice(off, off + widths[k])
            off += widths[k]
            xv = refs[k][...]
            r = lax.rsqrt(jnp.mean(xv * xv, axis=-1, keepdims=True) + EPS)
            o_ref[:, sl] = ((xv * r) * g_ref[:, sl]).astype(bf16)

    return pl.pallas_call(
        body, out_shape=S((lp, w), bf16), grid=(lp // tr,),
        in_specs=[BS((tr, wk), lambda i: (i, 0)) for wk in widths] + [BS((1, w), lambda i: (0, 0))],
        out_specs=BS((tr, w), lambda i: (i, 0)), name=name, compiler_params=_cp("parallel"))(*xs, gain)


def _rmsnorm_bwd(dy, xs, gain, res, name):
    lp = xs[0].shape[0]
    widths = [x.shape[1] for x in xs]
    w = sum(widths)
    tr = _pick(lp, (384, 256, 128))
    n = len(xs)
    has_res = res is not None
    assert not has_res or n == 1

    def body(*refs):
        dy_ref, g_ref = refs[0], refs[1 + n]
        res_ref = refs[2 + n] if has_res else None
        dx_refs = refs[2 + n + has_res:2 + 2 * n + has_res]
        dg_ref = refs[-1]

        @pl.when(pl.program_id(0) == 0)
        def _():
            dg_ref[...] = jnp.zeros_like(dg_ref)

        off = 0
        for k in range(n):
            sl = slice(off, off + widths[k])
            off += widths[k]
            xv = refs[1 + k][...]
            dyv = dy_ref[:, sl]
            r = lax.rsqrt(jnp.mean(xv * xv, axis=-1, keepdims=True) + EPS)
            xh = xv * r
            dg_ref[:, sl] += jnp.sum(dyv * xh, axis=0, keepdims=True)
            dxh = dyv * g_ref[:, sl]
            dx = r * (dxh - xh * jnp.mean(dxh * xh, axis=-1, keepdims=True))
            if has_res:
                dx = dx + res_ref[...]
            dx_refs[k][...] = dx

    rows = [BS((tr, wk), lambda i: (i, 0)) for wk in widths]
    full = BS((tr, w), lambda i: (i, 0))
    vec = BS((1, w), lambda i: (0, 0))
    args = (dy, *xs, gain) + ((res,) if has_res else ())
    out = pl.pallas_call(
        body, out_shape=tuple(S((lp, wk), f32) for wk in widths) + (S((1, w), f32),), grid=(lp // tr,),
        in_specs=[full] + rows + [vec] + ([full] if has_res else []), out_specs=tuple(rows) + (vec,),
        name=name, compiler_params=_cp("arbitrary"))(*args)
    return list(out[:n]), out[n]


def _loss_head(h, target):
    lp, d = h.shape

    def body(h_ref, t_ref, loss_ref, dh_ref):
        i = pl.program_id(0)

        @pl.when(i == 0)
        def _():
            loss_ref[...] = jnp.zeros_like(loss_ref)
            dh_ref[...] = jnp.zeros_like(dh_ref)

        @pl.when(i > 0)
        def _():
            e = h_ref[...] - t_ref[...]
            dh_ref[...] = e * (1.0 / d)
            loss_ref[...] += 0.5 * jnp.sum(jnp.sum(e * e, axis=-1, keepdims=True) * (1.0 / d))

    return pl.pallas_call(
        body, out_shape=(S((8, 128), f32), S((lp, d), f32)), grid=(lp // TILE,),
        in_specs=[BS((TILE, d), lambda i: (i, 0)), BS((TILE, d), lambda i: (jnp.maximum(i - 1, 0), 0))],
        out_specs=(BS((8, 128), lambda i: (0, 0)), BS((TILE, d), lambda i: (i, 0))),
        name="loss_head", compiler_params=_cp("arbitrary"))(h, target)


def _mm(a, b, layer, mode, name, out_dtype=f32, res=None):
    m, k = a.shape
    n = b.shape[2] if mode == "nn" else b.shape[1]
    tm = _pick(m, (384, 256, 128))
    tn = _pick(n, (512, 384, 256, 128))
    dims = NN if mode == "nn" else NT
    has_res = res is not None

    def body(*refs):
        if has_res:
            a_ref, b_ref, r_ref, o_ref = refs
        else:
            a_ref, b_ref, o_ref = refs
        acc = _dot(a_ref[...].astype(bf16), b_ref[...], dims)
        if has_res:
            acc = acc + r_ref[...]
        o_ref[...] = acc.astype(out_dtype)

    if mode == "nn":
        b_spec = BS((None, k, tn), lambda j, i: (layer, 0, j))
    else:
        b_spec = BS((None, tn, k), lambda j, i: (layer, j, 0))
    tile = BS((tm, tn), lambda j, i: (i, j))
    args = (a, b) + ((res,) if has_res else ())
    return pl.pallas_call(
        body, out_shape=S((m, n), out_dtype), grid=(n // tn, m // tm),
        in_specs=[BS((tm, k), lambda j, i: (i, 0)), b_spec] + ([tile] if has_res else []), out_specs=tile,
        name=name, compiler_params=_cp("parallel", "parallel"))(*args)


def _mm_tn(a, b, gbuf, slot, nslots, name):
    t, m = a.shape
    n = b.shape[1]
    tm = _pick(m, (512, 384, 256, 128))
    tk = _pick(t, (384, 256, 128))
    nk = t // tk

    def body(*refs):
        a_ref, b_ref, o_ref = refs[0], refs[1], refs[-1]
        kk = pl.program_id(1)

        @pl.when(kk == 0)
        def _():
            o_ref[...] = jnp.zeros_like(o_ref)

        o_ref[...] += _dot(a_ref[...].astype(bf16), b_ref[...].astype(bf16), TN)

    in_specs = [BS((tk, tm), lambda i, kk: (kk, i)), BS((tk, n), lambda i, kk: (kk, 0))]
    args = (a, b)
    aliases = {}
    if gbuf is not None:
        in_specs.append(ANY)
        args = args + (gbuf,)
        aliases = {2: 0}
    return pl.pallas_call(
        body, out_shape=S((nslots, m, n), f32), grid=(m // tm, nk), in_specs=in_specs,
        out_specs=BS((None, tm, n), lambda i, kk: (slot, i, 0)), input_output_aliases=aliases,
        name=name, compiler_params=_cp("parallel", "arbitrary"))(*args)


def _ffn_up(hn, wff, layer, nl):
    m, k = hn.shape
    n = wff.shape[1]
    tm = _pick(m, (384, 256, 128))
    tn = _pick(n, (512, 384, 256, 128))

    def body(a_ref, wg_ref, wu_ref, g_ref, u_ref, act_ref):
        a = a_ref[...]
        g = _dot(a, wg_ref[...], NT)
        u = _dot(a, wu_ref[...], NT)
        g_ref[...] = g
        u_ref[...] = u
        act_ref[...] = (g * jax.nn.sigmoid(g) * u).astype(bf16)

    tile = BS((tm, tn), lambda j, i: (i, j))
    return pl.pallas_call(
        body, out_shape=(S((m, n), f32), S((m, n), f32), S((m, n), bf16)), grid=(n // tn, m // tm),
        in_specs=[BS((tm, k), lambda j, i: (i, 0)), BS((None, tn, k), lambda j, i: (layer, j, 0)),
                  BS((None, tn, k), lambda j, i: (nl + layer, j, 0))],
        out_specs=(tile, tile, tile), name="ffn_up", compiler_params=_cp("parallel", "parallel"))(hn, wff, wff)


def _ffn_down_bwd(dh, wff, layer, nl, g, u):
    m, k = dh.shape
    n = wff.shape[1]
    tm = _pick(m, (384, 256, 128))
    tn = _pick(n, (512, 384, 256, 128))

    def body(a_ref, wd_ref, g_ref, u_ref, dg_ref, du_ref):
        dact = _dot(a_ref[...].astype(bf16), wd_ref[...], NT)
        gv = g_ref[...]
        sg = jax.nn.sigmoid(gv)
        dg_ref[...] = (dact * u_ref[...] * (sg * (1.0 + gv * (1.0 - sg)))).astype(bf16)
        du_ref[...] = (dact * (gv * sg)).astype(bf16)

    tile = BS((tm, tn), lambda j, i: (i, j))
    return pl.pallas_call(
        body, out_shape=(S((m, n), bf16), S((m, n), bf16)), grid=(n // tn, m // tm),
        in_specs=[BS((tm, k), lambda j, i: (i, 0)), BS((None, tn, k), lambda j, i: (2 * nl + layer, j, 0)), tile, tile],
        out_specs=(tile, tile), name="ffn_down_bwd", compiler_params=_cp("parallel", "parallel"))(dh, wff, g, u)


def _lane_lo():
    return lax.broadcasted_iota(jnp.int32, (1, TILE), 1) < HEAD_DIM


def _head_norm(x, g2):
    lo = _lane_lo()
    x2 = x * x
    s_lo = jnp.sum(jnp.where(lo, x2, 0.0), axis=-1, keepdims=True)
    s_hi = jnp.sum(jnp.where(lo, 0.0, x2), axis=-1, keepdims=True)
    r = jnp.where(lo, lax.rsqrt(s_lo * (1.0 / HEAD_DIM) + EPS), lax.rsqrt(s_hi * (1.0 / HEAD_DIM) + EPS))
    xh = x * r
    return xh * g2, xh, r


def _head_norm_bwd(dy, xh, r, g2):
    lo = _lane_lo()
    dgain = jnp.sum(dy * xh, axis=0, keepdims=True)
    dxh = dy * g2
    t = dxh * xh
    t_lo = jnp.sum(jnp.where(lo, t, 0.0), axis=-1, keepdims=True)
    t_hi = jnp.sum(jnp.where(lo, 0.0, t), axis=-1, keepdims=True)
    mt = jnp.where(lo, t_lo, t_hi) * (1.0 / HEAD_DIM)
    return r * (dxh - xh * mt), dgain


def _swa_slopes():
    return [2.0 ** (-8.0 * (h + 1) / SWA_HEADS) for h in range(SWA_HEADS)]


def _swa_masks(i):
    rows = lax.broadcasted_iota(jnp.int32, (TILE, TILE), 0)
    cols = lax.broadcasted_iota(jnp.int32, (TILE, TILE), 1)
    r_pos = i * TILE + rows
    prev = jnp.maximum(i - 1, 0)
    out = []
    for c, base in enumerate((0, prev * TILE, i * TILE)):
        s_pos = base + cols
        dist = r_pos - s_pos
        if c == 0:
            ok = (i >= 1) & (s_pos >= PAD)
        elif c == 1:
            ok = (i >= 2) & (dist < TILE)
        else:
            ok = (dist >= 0) & (s_pos >= PAD)
        out.append((ok, dist.astype(f32)))
    return out, prev


def _swa_logits(qm, kn, masks, slope, sink):
    scale = HEAD_DIM ** -0.5
    logits = []
    mx = jnp.full((TILE, 1), sink, f32)
    for c in range(3):
        ok, dist = masks[c]
        lg = jnp.where(ok, _dot(qm, kn[c], NT) * scale - slope * dist, NEG)
        logits.append(lg)
        mx = jnp.maximum(mx, jnp.max(lg, axis=-1, keepdims=True))
    return logits, mx


def _swa_specs(lp, pw):
    qw, kvw = SWA_HEADS * HEAD_DIM, SWA_KV_HEADS * HEAD_DIM
    kidx = qw // kvw
    return (BS((TILE, qw), lambda i: (i, 0)), BS((lp, kvw), lambda i: (0, kidx)), BS((lp, kvw), lambda i: (0, kidx + 1)),
            BS((1, TILE), lambda i: (0, 0)))


def _swa_fwd(proj, qg2, kg2, sinks):
    lp, pw = proj.shape
    qw = SWA_HEADS * HEAD_DIM
    group = SWA_HEADS // SWA_KV_HEADS
    slopes = _swa_slopes()

    def body(sink_ref, q_ref, k_ref, v_ref, qg_ref, kg_ref, o_ref):
        i = pl.program_id(0)
        lo = _lane_lo()
        masks, prev = _swa_masks(i)
        bases = (0, pl.multiple_of(prev * TILE, TILE), pl.multiple_of(i * TILE, TILE))
        qn = [_head_norm(q_ref[:, b * TILE:(b + 1) * TILE], qg_ref[...])[0] for b in range(qw // TILE)]
        acc = [jnp.zeros((TILE, TILE), f32) for _ in range(qw // TILE)]
        for kvh in range(SWA_KV_HEADS):
            kb, khalf = kvh // 2, kvh % 2
            kcols = slice(kb * TILE, (kb + 1) * TILE)
            ksel = lo if khalf == 0 else jnp.logical_not(lo)
            kn, vm = [], []
            for base in bases:
                kn.append(_head_norm(k_ref[pl.ds(base, TILE), kcols], kg_ref[...])[0].astype(bf16))
                vm.append(jnp.where(ksel, v_ref[pl.ds(base, TILE), kcols], 0.0).astype(bf16))
            for gi in range(group):
                h = kvh * group + gi
                qb, same = h // 2, (h % 2) == khalf
                qa = qn[qb] if same else pltpu.roll(qn[qb], HEAD_DIM, 1)
                qm = jnp.where(ksel, qa, 0.0).astype(bf16)
                sink = sink_ref[h]
                logits, mx = _swa_logits(qm, kn, masks, slopes[h], sink)
                den = jnp.exp(sink - mx)
                o = jnp.zeros((TILE, TILE), f32)
                for c in range(3):
                    p = jnp.exp(logits[c] - mx)
                    den = den + jnp.sum(p, axis=-1, keepdims=True)
                    o = o + _dot(p.astype(bf16), vm[c], NN)
                o = o / den
                acc[qb] = acc[qb] + (o if same else pltpu.roll(o, HEAD_DIM, 1))
        valid = (i * TILE + lax.broadcasted_iota(jnp.int32, (TILE, 1), 0)) >= PAD
        for b in range(qw // TILE):
            o_ref[:, b * TILE:(b + 1) * TILE] = jnp.where(valid, acc[b], 0.0)

    q_spec, k_spec, v_spec, vec = _swa_specs(lp, pw)
    return pl.pallas_call(
        body, out_shape=S((lp, qw), f32), grid=(lp // TILE,),
        in_specs=[BS(memory_space=pltpu.SMEM), q_spec, k_spec, v_spec, vec, vec],
        out_specs=BS((TILE, qw), lambda i: (i, 0)), name="swa_fwd", compiler_params=_cp("arbitrary"))(
            sinks, proj, proj, proj, qg2, kg2)


def _swa_bwd(proj, dout, qg2, kg2, sinks):
    lp, pw = proj.shape
    qw, kvw = SWA_HEADS * HEAD_DIM, SWA_KV_HEADS * HEAD_DIM
    group = SWA_HEADS // SWA_KV_HEADS
    scale = HEAD_DIM ** -0.5
    slopes = _swa_slopes()
    nt = lp // TILE

    def body(sink_ref, q_ref, k_ref, v_ref, do_ref, qg_ref, kg_ref, dq_ref, dk_ref, dv_ref, dqg_ref, dkg_ref, ds_ref,
             dkn_acc, dv_acc):
        i = pl.program_id(0)
        lo = _lane_lo()
        lane = lax.broadcasted_iota(jnp.int32, (1, TILE), 1)

        @pl.when(i == 0)
        def _():
            dkn_acc[...] = jnp.zeros_like(dkn_acc)
            dv_acc[...] = jnp.zeros_like(dv_acc)
            dqg_ref[...] = jnp.zeros_like(dqg_ref)
            dkg_ref[...] = jnp.zeros_like(dkg_ref)
            ds_ref[...] = jnp.zeros_like(ds_ref)

        masks, prev = _swa_masks(i)
        bases = (0, pl.multiple_of(prev * TILE, TILE), pl.multiple_of(i * TILE, TILE))
        qnorm = [_head_norm(q_ref[:, b * TILE:(b + 1) * TILE], qg_ref[...]) for b in range(qw // TILE)]
        dqn = [jnp.zeros((TILE, TILE), f32) for _ in range(qw // TILE)]
        for kvh in range(SWA_KV_HEADS):
            kb, khalf = kvh // 2, kvh % 2
            kcols = slice(kb * TILE, (kb + 1) * TILE)
            ksel = lo if khalf == 0 else jnp.logical_not(lo)
            kn, vm = [], []
            for base in bases:
                kn.append(_head_norm(k_ref[pl.ds(base, TILE), kcols], kg_ref[...])[0].astype(bf16))
                vm.append(jnp.where(ksel, v_ref[pl.ds(base, TILE), kcols], 0.0).astype(bf16))
            for gi in range(group):
                h = kvh * group + gi
                qb, same = h // 2, (h % 2) == khalf
                qa = qnorm[qb][0] if same else pltpu.roll(qnorm[qb][0], HEAD_DIM, 1)
                qm = jnp.where(ksel, qa, 0.0).astype(bf16)
                doa = do_ref[:, qb * TILE:(qb + 1) * TILE]
                doa = doa if same else pltpu.roll(doa, HEAD_DIM, 1)
                dom = jnp.where(ksel, doa, 0.0).astype(bf16)
                sink = sink_ref[h]
                logits, mx = _swa_logits(qm, kn, masks, slopes[h], sink)
                ps = [jnp.exp(lg - mx) for lg in logits]
                e_sink = jnp.exp(sink - mx)
                den = e_sink
                for c in range(3):
                    den = den + jnp.sum(ps[c], axis=-1, keepdims=True)
                inv = 1.0 / den
                ps = [p * inv for p in ps]
                dps = [_dot(dom, vm[c], NT) for c in range(3)]
                dsum = jnp.zeros((TILE, 1), f32)
                for c in range(3):
                    dsum = dsum + jnp.sum(ps[c] * dps[c], axis=-1, keepdims=True)
                ds_ref[...] += jnp.where(lane == h, -jnp.sum(e_sink * inv * dsum), 0.0)
                dq_h = jnp.zeros((TILE, TILE), f32)
                for c in range(3):
                    dsc = (ps[c] * (dps[c] - dsum) * scale).astype(bf16)
                    dq_h = dq_h + _dot(dsc, kn[c], NN)
                    dkn_acc[pl.ds(bases[c], TILE), kcols] += _dot(dsc, qm, TN)
                    dv_acc[pl.ds(bases[c], TILE), kcols] += _dot(ps[c].astype(bf16), dom, TN)
                dq_h = jnp.where(ksel, dq_h, 0.0)
                dqn[qb] = dqn[qb] + (dq_h if same else pltpu.roll(dq_h, HEAD_DIM, 1))
        for b in range(qw // TILE):
            _, xh, r = qnorm[b]
            dq, dgain = _head_norm_bwd(dqn[b], xh, r, qg_ref[...])
            dq_ref[:, b * TILE:(b + 1) * TILE] = dq.astype(bf16)
            dqg_ref[...] += dgain

        @pl.when(i == nt - 1)
        def _():
            dv_ref[...] = dv_acc[...].astype(bf16)

            def tile_step(t, carry):
                base = pl.multiple_of(t * TILE, TILE)
                for kb in range(kvw // TILE):
                    kcols = slice(kb * TILE, (kb + 1) * TILE)
                    _, xh, r = _head_norm(k_ref[pl.ds(base, TILE), kcols], kg_ref[...])
                    dk, dgain = _head_norm_bwd(dkn_acc[pl.ds(base, TILE), kcols], xh, r, kg_ref[...])
                    dk_ref[pl.ds(base, TILE), kcols] = dk.astype(bf16)
                    dkg_ref[...] += dgain
                return carry

            lax.fori_loop(0, nt, tile_step, 0)

    q_spec, k_spec, v_spec, vec = _swa_specs(lp, pw)
    whole = BS((lp, kvw), lambda i: (0, 0))
    return pl.pallas_call(
        body, out_shape=(S((lp, qw), bf16), S((lp, kvw), bf16), S((lp, kvw), bf16), S((1, TILE), f32), S((1, TILE), f32), S((1, TILE), f32)),
        grid=(nt,),
        in_specs=[BS(memory_space=pltpu.SMEM), q_spec, k_spec, v_spec, BS((TILE, qw), lambda i: (i, 0)), vec, vec],
        out_specs=(BS((TILE, qw), lambda i: (i, 0)), whole, whole, vec, vec, vec),
        scratch_shapes=[pltpu.VMEM((lp, kvw), f32), pltpu.VMEM((lp, kvw), f32)],
        name="swa_bwd", compiler_params=_cp("arbitrary"))(sinks, proj, proj, proj, dout, qg2, kg2)


def _split_dot(x, u):
    hi = x.astype(bf16)
    lo = (x - hi.astype(f32)).astype(bf16)
    return _dot(hi, u, NN) + _dot(lo, u, NN)


def _sb_mask(qi, kj):
    rows = lax.broadcasted_iota(jnp.int32, (TILE, TILE), 0)
    cols = lax.broadcasted_iota(jnp.int32, (TILE, TILE), 1)
    s_pos = kj * TILE + cols
    return (s_pos < qi * TILE + rows) & (s_pos >= PAD)


def _sb_tile(qm, km, qi, kj, c_run):
    rows = lax.broadcasted_iota(jnp.int32, (TILE, TILE), 0)
    cols = lax.broadcasted_iota(jnp.int32, (TILE, TILE), 1)
    mask = _sb_mask(qi, kj)
    z = _dot(qm, km, NT) * (HEAD_DIM ** -0.5)
    sp = jnp.maximum(z, 0.0) + jnp.log(1.0 + jnp.exp(-jnp.abs(z)))
    m = jnp.where(mask, -sp, 0.0)
    ls = z - sp
    after = (rows > cols).astype(bf16)
    a = jnp.where(mask, jnp.exp(ls + _split_dot(m, after) + c_run), 0.0)
    return mask, m, ls, a


def _sb_cols(lp, off):
    return BS((lp, TILE), lambda p: (0, off + p))


def _sb_fwd(proj):
    lp, pw = proj.shape
    sbw = SB_HEADS * HEAD_DIM
    npair = sbw // TILE
    nt = lp // TILE
    q0 = (pw - 3 * sbw) // TILE

    def body(q_ref, k_ref, v_ref, o_ref):
        lo = _lane_lo()
        sels = (lo, jnp.logical_not(lo))

        def q_step(qi, carry):
            qbase = pl.multiple_of(qi * TILE, TILE)
            qt = q_ref[pl.ds(qbase, TILE), :]
            out = jnp.zeros((TILE, TILE), f32)
            for hh in range(2):
                sel = sels[hh]
                qm = jnp.where(sel, qt, 0.0).astype(bf16)

                def k_step(t, st, sel=sel, qm=qm):
                    c_run, acc = st
                    kj = qi - t
                    kbase = pl.multiple_of(kj * TILE, TILE)
                    km = k_ref[pl.ds(kbase, TILE), :].astype(bf16)
                    vm = jnp.where(sel, v_ref[pl.ds(kbase, TILE), :], 0.0).astype(bf16)
                    _, m, _, a = _sb_tile(qm, km, qi, kj, c_run)
                    return c_run + jnp.sum(m, axis=-1, keepdims=True), acc + _dot(a.astype(bf16), vm, NN)

                _, acc = lax.fori_loop(0, qi + 1, k_step, (jnp.zeros((TILE, 1), f32), jnp.zeros((TILE, TILE), f32)))
                out = out + acc
            o_ref[pl.ds(qbase, TILE), :] = out
            return carry

        lax.fori_loop(0, nt, q_step, 0)

    return pl.pallas_call(
        body, out_shape=S((lp, sbw), f32), grid=(npair,),
        in_specs=[_sb_cols(lp, q0), _sb_cols(lp, q0 + npair), _sb_cols(lp, q0 + 2 * npair)],
        out_specs=_sb_cols(lp, 0), name="sb_fwd", compiler_params=_cp("parallel"))(proj, proj, proj)


def _sb_bwd(proj, dout):
    lp, pw = proj.shape
    sbw = SB_HEADS * HEAD_DIM
    npair = sbw // TILE
    nt = lp // TILE
    q0 = (pw - 3 * sbw) // TILE
    scale = HEAD_DIM ** -0.5

    def body(q_ref, k_ref, v_ref, do_ref, dq_ref, dk_ref, dv_ref, dk_acc, dv_acc, e_buf, b_buf):
        lo = _lane_lo()
        sels = (lo, jnp.logical_not(lo))
        dk_acc[...] = jnp.zeros_like(dk_acc)
        dv_acc[...] = jnp.zeros_like(dv_acc)
        rows = lax.broadcasted_iota(jnp.int32, (TILE, TILE), 0)
        cols = lax.broadcasted_iota(jnp.int32, (TILE, TILE), 1)
        before = (rows < cols).astype(bf16)

        def q_step(qi, carry):
            qbase = pl.multiple_of(qi * TILE, TILE)
            qt = q_ref[pl.ds(qbase, TILE), :]
            dot_ = do_ref[pl.ds(qbase, TILE), :]
            dq = jnp.zeros((TILE, TILE), f32)
            for hh in range(2):
                sel = sels[hh]
                qm = jnp.where(sel, qt, 0.0).astype(bf16)
                dom = jnp.where(sel, dot_, 0.0).astype(bf16)

                def right_to_left(t, c_run, sel=sel, qm=qm, dom=dom):
                    kj = qi - t
                    kbase = pl.multiple_of(kj * TILE, TILE)
                    km = k_ref[pl.ds(kbase, TILE), :].astype(bf16)
                    vm = jnp.where(sel, v_ref[pl.ds(kbase, TILE), :], 0.0).astype(bf16)
                    _, m, ls, a = _sb_tile(qm, km, qi, kj, c_run)
                    e_buf[kj] = _dot(dom, vm, NT) * a
                    b_buf[kj] = jnp.exp(ls)
                    dv_acc[pl.ds(kbase, TILE), :] += _dot(a.astype(bf16), dom, TN)
                    return c_run + jnp.sum(m, axis=-1, keepdims=True)

                lax.fori_loop(0, qi + 1, right_to_left, jnp.zeros((TILE, 1), f32))

                def left_to_right(kj, st, sel=sel, qm=qm):
                    e_run, dq_acc = st
                    kbase = pl.multiple_of(kj * TILE, TILE)
                    km = jnp.where(sel, k_ref[pl.ds(kbase, TILE), :], 0.0).astype(bf16)
                    e = e_buf[kj]
                    beta = b_buf[kj]
                    e_left = e_run + _split_dot(e, before)
                    dz = jnp.where(_sb_mask(qi, kj), e * (1.0 - beta) - beta * e_left, 0.0) * scale
                    dzb = dz.astype(bf16)
                    dk_acc[pl.ds(kbase, TILE), :] += _dot(dzb, qm, TN)
                    return e_run + jnp.sum(e, axis=-1, keepdims=True), dq_acc + _dot(dzb, km, NN)

                _, dq_h = lax.fori_loop(0, qi + 1, left_to_right, (jnp.zeros((TILE, 1), f32), jnp.zeros((TILE, TILE), f32)))
                dq = dq + dq_h
            dq_ref[pl.ds(qbase, TILE), :] = dq.astype(bf16)
            return carry

        lax.fori_loop(0, nt, q_step, 0)
        dk_ref[...] = dk_acc[...].astype(bf16)
        dv_ref[...] = dv_acc[...].astype(bf16)

    return pl.pallas_call(
        body, out_shape=(S((lp, sbw), bf16),) * 3, grid=(npair,),
        in_specs=[_sb_cols(lp, q0), _sb_cols(lp, q0 + npair), _sb_cols(lp, q0 + 2 * npair), _sb_cols(lp, 0)],
        out_specs=(_sb_cols(lp, 0),) * 3,
        scratch_shapes=[pltpu.VMEM((lp, TILE), f32), pltpu.VMEM((lp, TILE), f32),
                        pltpu.VMEM((nt, TILE, TILE), f32), pltpu.VMEM((nt, TILE, TILE), f32)],
        name="sb_bwd", compiler_params=_cp("parallel"))(proj, proj, proj, dout)


def _place():
    x, y, c = lax.axis_index("x"), lax.axis_index("y"), lax.axis_index("c")
    chips = [(1 - x, y), (x, 1 - y), (1 - x, 1 - y)]
    return x, y, c, chips


def _two_level_gather_body(rows, x_ref, send_sems, recv_sems, local_sem):
    x, y, c, chips = _place()
    me, sibling = (x, y, c), (x, y, 1 - c)

    def copy(k, block, to, src=None):
        return pltpu.make_async_remote_copy(
            src_ref=rows(*block) if src is None else src, dst_ref=rows(*block),
            send_sem=send_sems.at[k], recv_sem=recv_sems.at[k], device_id=to, device_id_type=MESH)

    mine = pltpu.make_async_copy(x_ref, rows(*me), local_sem)
    mine.start()
    first = [copy(0, me, sibling, src=x_ref)]
    first += [copy(1 + j, me, (*chip, c), src=x_ref) for j, chip in enumerate(chips)]
    for cp in first:
        cp.start()
    passed = [copy(4 + j, (*chip, c), sibling) for j, chip in enumerate(chips)]
    for j, chip in enumerate(chips):
        copy(1 + j, (*chip, c), me).wait_recv()
        passed[j].start()
    copy(0, sibling, me).wait_recv()
    for j, chip in enumerate(chips):
        copy(4 + j, (*chip, 1 - c), me).wait_recv()
    for cp in first + passed:
        cp.wait_send()
    mine.wait()


_GATHER_SEMS = [pltpu.SemaphoreType.DMA((7,)), pltpu.SemaphoreType.DMA((7,)), pltpu.SemaphoreType.DMA]


def _ag_rows(shard, name):
    ns, r, cdim = shard.shape

    def body(x_ref, o_ref, send_sems, recv_sems, local_sem):
        def rows(px, py, pc):
            return o_ref.at[:, pl.ds(pl.multiple_of((4 * px + 2 * py + pc) * r, 16), r), :]
        _two_level_gather_body(rows, x_ref, send_sems, recv_sems, local_sem)

    return pl.pallas_call(
        body, out_shape=S((ns, NDEV * r, cdim), shard.dtype), in_specs=[ANY], out_specs=ANY,
        scratch_shapes=_GATHER_SEMS, name=name)(shard)


def _ag_small(shard, name):
    r, w = shard.shape

    def body(x_ref, o_ref, send_sems, recv_sems, local_sem):
        def rows(px, py, pc):
            return o_ref.at[pl.ds(pl.multiple_of((4 * px + 2 * py + pc) * r, 8), r), :]
        _two_level_gather_body(rows, x_ref, send_sems, recv_sems, local_sem)

    vmem = BS(memory_space=pltpu.VMEM)
    return pl.pallas_call(
        body, out_shape=S((NDEV * r, w), shard.dtype), in_specs=[vmem], out_specs=vmem,
        scratch_shapes=_GATHER_SEMS, name=name)(shard)


def _rs_swap(g, name):
    ns, rows8, cdim = g.shape
    r = rows8 // NDEV

    def body(g_ref, o_ref, send_sems, recv_sems):
        x, y, c, chips = _place()
        rel = [(x, y)] + chips
        copies = []
        for j, (px, py) in enumerate(rel):
            d = 4 * px + 2 * py + (1 - c)
            copies.append(pltpu.make_async_remote_copy(
                src_ref=g_ref.at[:, pl.ds(pl.multiple_of(d * r, 8), r), :], dst_ref=o_ref.at[j],
                send_sem=send_sems.at[j], recv_sem=recv_sems.at[j], device_id=(x, y, 1 - c), device_id_type=MESH))
        for cp in copies:
            cp.start()
        for cp in copies:
            cp.wait_recv()
        for cp in copies:
            cp.wait_send()

    return pl.pallas_call(
        body, out_shape=S((4, ns, r, cdim), f32), in_specs=[ANY], out_specs=ANY,
        scratch_shapes=[pltpu.SemaphoreType.DMA((4,)), pltpu.SemaphoreType.DMA((4,))], name=name)(g)


def _rel_ids():
    x, y, c = lax.axis_index("x"), lax.axis_index("y"), lax.axis_index("c")
    rel = [(x, y), (1 - x, y), (x, 1 - y), (1 - x, 1 - y)]
    return jnp.stack([4 * px + 2 * py + c for px, py in rel]).astype(jnp.int32)


def _rs_partial(g, sib, ids, name):
    ns, rows8, cdim = g.shape
    r = rows8 // NDEV
    tr = r // 2
    nb = r // tr

    def own_body(ids_ref, g_ref, s_ref, o_ref):
        o_ref[...] = g_ref[...] + s_ref[...]

    own = pl.pallas_call(
        own_body, out_shape=S((ns, r, cdim), f32),
        grid_spec=pltpu.PrefetchScalarGridSpec(
            num_scalar_prefetch=1, grid=(ns, nb),
            in_specs=[BS((None, tr, cdim), lambda s, t, ids_ref: (s, ids_ref[0] * nb + t, 0)),
                      BS((None, None, tr, cdim), lambda s, t, ids_ref: (0, s, t, 0))],
            out_specs=BS((None, tr, cdim), lambda s, t, ids_ref: (s, t, 0))),
        name=name + "_own", compiler_params=_cp("parallel", "parallel"))(ids, g, sib)

    def send_body(ids_ref, g_ref, s_ref, o_ref):
        o_ref[...] = (g_ref[...] + s_ref[...]).astype(bf16)

    send = pl.pallas_call(
        send_body, out_shape=S((3, ns, r, cdim), bf16),
        grid_spec=pltpu.PrefetchScalarGridSpec(
            num_scalar_prefetch=1, grid=(3, ns, nb),
            in_specs=[BS((None, tr, cdim), lambda j, s, t, ids_ref: (s, ids_ref[j + 1] * nb + t, 0)),
                      BS((None, None, tr, cdim), lambda j, s, t, ids_ref: (j + 1, s, t, 0))],
            out_specs=BS((None, None, tr, cdim), lambda j, s, t, ids_ref: (j, s, t, 0))),
        name=name + "_send", compiler_params=_cp("parallel", "parallel", "parallel"))(ids, g, sib)
    return own, send


def _rs_ici(send, name):
    def body(s_ref, o_ref, send_sems, recv_sems):
        x, y, c, chips = _place()
        copies = [pltpu.make_async_remote_copy(
            src_ref=s_ref.at[j], dst_ref=o_ref.at[j], send_sem=send_sems.at[j], recv_sem=recv_sems.at[j],
            device_id=(*chip, c), device_id_type=MESH) for j, chip in enumerate(chips)]
        for cp in copies:
            cp.start()
        for cp in copies:
            cp.wait_recv()
        for cp in copies:
            cp.wait_send()

    return pl.pallas_call(
        body, out_shape=S(send.shape, send.dtype), in_specs=[ANY], out_specs=ANY,
        scratch_shapes=[pltpu.SemaphoreType.DMA((3,)), pltpu.SemaphoreType.DMA((3,))], name=name)(send)


def _reduce_scatter(g, ids, name):
    sib = _rs_swap(g, name + "_swap")
    own, send = _rs_partial(g, sib, ids, name + "_part")
    return own, _rs_ici(send, name + "_ici")


def _prep(w, name):
    nl, r, cdim = w.shape
    blk = BS((None, r, cdim), lambda l: (l, 0, 0))

    def body(w_ref, o_ref):
        o_ref[...] = w_ref[...].astype(bf16)

    return pl.pallas_call(body, out_shape=S(w.shape, bf16), grid=(nl,), in_specs=[blk], out_specs=blk,
                          name=name, compiler_params=_cp("parallel"))(w)


def _prep_t(w, name):
    nl, d, r = w.shape
    tc = _pick(d, (256, 128))

    def body(w_ref, o_ref):
        o_ref[...] = w_ref[...].T.astype(bf16)

    return pl.pallas_call(
        body, out_shape=S((nl, r, d), bf16), grid=(nl, d // tc),
        in_specs=[BS((None, tc, r), lambda l, t: (l, t, 0))], out_specs=BS((None, r, tc), lambda l, t: (l, 0, t)),
        name=name, compiler_params=_cp("parallel", "parallel"))(w)


def _adam_math(w, g, m, v):
    m2 = ADAM_B1 * m + (1.0 - ADAM_B1) * g
    v2 = ADAM_B2 * v + (1.0 - ADAM_B2) * (g * g)
    m_hat = m2 / (1.0 - ADAM_B1 ** ADAM_STEP)
    v_hat = v2 / (1.0 - ADAM_B2 ** ADAM_STEP)
    delta = -ADAM_LR * (m_hat / (jnp.sqrt(v_hat) + ADAM_EPS) + ADAM_WD * w)
    return delta, m2, v2


def _adam_big(w, m, v, own, arrived, slot0, transposed, name):
    nl, a, b = w.shape
    if transposed:
        ta = _pick(a, (256, 128))
        w_blk = BS((None, ta, b), lambda l, t: (l, t, 0))
        own_blk = BS((None, b, ta), lambda l, t: (slot0 + l, 0, t))
        arr_blk = [BS((None, None, b, ta), lambda l, t, j=j: (j, slot0 + l, 0, t)) for j in range(3)]
    else:
        ta = _pick(a, (a // 4, a // 2, a)) if (a // 4) % 16 == 0 else a
        w_blk = BS((None, ta, b), lambda l, t: (l, t, 0))
        own_blk = BS((None, ta, b), lambda l, t: (slot0 + l, t, 0))
        arr_blk = [BS((None, None, ta, b), lambda l, t, j=j: (j, slot0 + l, t, 0)) for j in range(3)]

    def body(w_ref, m_ref, v_ref, own_ref, a0_ref, a1_ref, a2_ref, g_ref, d_ref, m2_ref, v2_ref):
        g = ((own_ref[...] + a0_ref[...].astype(f32)) + a1_ref[...].astype(f32)) + a2_ref[...].astype(f32)
        if transposed:
            g = g.T
        delta, m2, v2 = _adam_math(w_ref[...], g, m_ref[...], v_ref[...])
        g_ref[...] = g
        d_ref[...] = delta
        m2_ref[...] = m2
        v2_ref[...] = v2

    return pl.pallas_call(
        body, out_shape=(S(w.shape, f32),) * 4, grid=(nl, a // ta),
        in_specs=[w_blk, w_blk, w_blk, own_blk] + arr_blk, out_specs=(w_blk,) * 4,
        name=name, compiler_params=_cp("parallel", "parallel"))(w, m, v, own, arrived, arrived, arrived)


def _adam_small(gathered, w, m, v):
    r, wd = w.shape

    def body(g_ref, w_ref, m_ref, v_ref, gs_ref, d_ref, m2_ref, v2_ref):
        g = g_ref[0:r, :]
        for dev in range(1, NDEV):
            g = g + g_ref[dev * r:(dev + 1) * r, :]
        delta, m2, v2 = _adam_math(w_ref[...], g, m_ref[...], v_ref[...])
        gs_ref[...] = g
        d_ref[...] = delta
        m2_ref[...] = m2
        v2_ref[...] = v2

    return pl.pallas_call(body, out_shape=(S((r, wd), f32),) * 4, name="adam_small", compiler_params=_cp())(gathered, w, m, v)


def _pack(arrays, width):
    flat = jnp.concatenate([a.reshape(-1) for a in arrays])
    rows = -(-flat.shape[0] // (8 * width)) * 8
    return jnp.pad(flat, (0, rows * width - flat.shape[0])).reshape(rows, width)


def _unpack(packed, shapes):
    flat = packed.reshape(-1)
    out, off = [], 0
    for shp in shapes:
        n = 1
        for s in shp:
            n *= s
        out.append(flat[off:off + n].reshape(shp))
        off += n
    return out


def kernel(x, meta_tokens, attn_norm_g, w_in, q_norm_g, k_norm_g, attn_sinks, swa_out_g, sb_out_g, w_o, ffn_norm_g, w_gate, w_up, w_down, loss_target, m_meta_tokens, m_attn_norm_g, m_w_in, m_q_norm_g, m_k_norm_g, m_attn_sinks, m_swa_out_g, m_sb_out_g, m_w_o, m_ffn_norm_g, m_w_gate, m_w_up, m_w_down, v_meta_tokens, v_attn_norm_g, v_w_in, v_q_norm_g, v_k_norm_g, v_attn_sinks, v_swa_out_g, v_sb_out_g, v_w_o, v_ffn_norm_g, v_w_gate, v_w_up, v_w_down):
    nl, d = attn_norm_g.shape
    x2, target = x[0], loss_target[0]
    me = 4 * lax.axis_index("x") + 2 * lax.axis_index("y") + lax.axis_index("c")
    ids = _rel_ids()

    meta_all = _ag_small(meta_tokens, "ag_meta")
    meta_full = meta_all.reshape(NDEV, N_META, -1).transpose(1, 0, 2).reshape(N_META, d)
    w_in_t = _ag_rows(_prep_t(w_in, "prep_w_in"), "ag_w_in")
    w_o_f = _ag_rows(_prep(w_o, "prep_w_o"), "ag_w_o")
    ff_shard = jnp.concatenate([_prep_t(w_gate, "prep_w_gate"), _prep_t(w_up, "prep_w_up"), _prep(w_down, "prep_w_down")], axis=0)
    w_ff = _ag_rows(ff_shard, "ag_w_ff")

    qg2 = jnp.tile(q_norm_g, (1, TILE // HEAD_DIM))
    kg2 = jnp.tile(k_norm_g, (1, TILE // HEAD_DIM))
    out_g = jnp.concatenate([swa_out_g, sb_out_g], axis=1)

    h = _embed(x2, meta_full)
    saved = []
    for l in range(nl):
        hn = _rmsnorm_fwd([h], attn_norm_g[l:l + 1], "attn_norm")
        proj = _mm(hn, w_in_t, l, "nt", "proj")
        out_a = _swa_fwd(proj, qg2[l:l + 1], kg2[l:l + 1], attn_sinks[l])
        out_b = _sb_fwd(proj)
        mixed = _rmsnorm_fwd([out_a, out_b], out_g[l:l + 1], "out_norm")
        h_mid = _mm(mixed, w_o_f, l, "nn", "attn_out", res=h)
        hn2 = _rmsnorm_fwd([h_mid], ffn_norm_g[l:l + 1], "ffn_norm")
        g, u, act = _ffn_up(hn2, w_ff, l, nl)
        h_out = _mm(act, w_ff, 2 * nl + l, "nn", "ffn_down", res=h_mid)
        saved.append((h, hn, proj, out_a, out_b, mixed, h_mid, hn2, g, u, act))
        h = h_out

    loss_tile, dh = _loss_head(h, target)
    loss = lax.psum(loss_tile[0, 0], AXES)

    g_in = g_o = g_ff = None
    small = {k: [None] * nl for k in ("attn", "ffn", "out", "q", "k", "sink")}
    for l in reversed(range(nl)):
        h_in, hn, proj, out_a, out_b, mixed, h_mid, hn2, g, u, act = saved[l]
        dg, du = _ffn_down_bwd(dh, w_ff, l, nl, g, u)
        g_ff = _mm_tn(act, dh, g_ff, 2 * nl + l, 3 * nl, "grad_w_down")
        g_ff = _mm_tn(dg, hn2, g_ff, l, 3 * nl, "grad_w_gate")
        g_ff = _mm_tn(du, hn2, g_ff, nl + l, 3 * nl, "grad_w_up")
        dhn2 = _mm(du, w_ff, nl + l, "nn", "d_ffn_in_up", res=_mm(dg, w_ff, l, "nn", "d_ffn_in_gate"))
        (dh_mid,), small["ffn"][l] = _rmsnorm_bwd(dhn2, [h_mid], ffn_norm_g[l:l + 1], dh, "ffn_norm_bwd")
        dmixed = _mm(dh_mid, w_o_f, l, "nt", "d_mixed")
        g_o = _mm_tn(mixed, dh_mid, g_o, l, nl, "grad_w_o")
        (dout_a, dout_b), small["out"][l] = _rmsnorm_bwd(dmixed, [out_a, out_b], out_g[l:l + 1], None, "out_norm_bwd")
        dq_b, dk_b, dv_b = _sb_bwd(proj, dout_b)
        dq_a, dk_a, dv_a, dqg, dkg, dsk = _swa_bwd(proj, dout_a, qg2[l:l + 1], kg2[l:l + 1], attn_sinks[l])
        small["q"][l] = dqg[0, :HEAD_DIM] + dqg[0, HEAD_DIM:]
        small["k"][l] = dkg[0, :HEAD_DIM] + dkg[0, HEAD_DIM:]
        small["sink"][l] = dsk[0, :SWA_HEADS]
        dproj = jnp.concatenate([dq_a, dk_a, dv_a, dq_b, dk_b, dv_b], axis=1)
        dhn = _mm(dproj, w_in_t, l, "nn", "d_attn_in")
        g_in = _mm_tn(dproj, hn, g_in, l, nl, "grad_w_in")
        (dh,), small["attn"][l] = _rmsnorm_bwd(dhn, [h_in], attn_norm_g[l:l + 1], dh_mid, "attn_norm_bwd")

    grad_x = dh[TILE:][None]
    d_meta = dh[PAD:TILE]

    qw = SWA_HEADS * HEAD_DIM
    d_out = jnp.concatenate(small["out"], axis=0)
    small_grads = [jnp.concatenate(small["attn"], axis=0), jnp.stack(small["q"]), jnp.stack(small["k"]), jnp.stack(small["sink"]),
                   d_out[:, :qw], d_out[:, qw:], jnp.concatenate(small["ffn"], axis=0), d_meta]
    col0 = me * meta_tokens.shape[1]

    def widen(a):
        return lax.dynamic_update_slice(jnp.zeros((N_META, d), f32), a, (0, col0))

    small_w = [attn_norm_g, q_norm_g, k_norm_g, attn_sinks, swa_out_g, sb_out_g, ffn_norm_g]
    small_m = [m_attn_norm_g, m_q_norm_g, m_k_norm_g, m_attn_sinks, m_swa_out_g, m_sb_out_g, m_ffn_norm_g]
    small_v = [v_attn_norm_g, v_q_norm_g, v_k_norm_g, v_attn_sinks, v_swa_out_g, v_sb_out_g, v_ffn_norm_g]
    shapes = [a.shape for a in small_w] + [(N_META, d)]
    packed = _adam_small(_ag_small(_pack(small_grads, d), "ag_small_grads"),
                         _pack(small_w + [widen(meta_tokens)], d), _pack(small_m + [widen(m_meta_tokens)], d),
                         _pack(small_v + [widen(v_meta_tokens)], d))
    small_out = []
    for p in packed:
        parts = _unpack(p, shapes)
        parts[-1] = lax.dynamic_slice(parts[-1], (0, col0), meta_tokens.shape)
        small_out.append(parts)

    own_in, arr_in = _reduce_scatter(g_in, ids, "rs_w_in")
    own_o, arr_o = _reduce_scatter(g_o, ids, "rs_w_o")
    own_ff, arr_ff = _reduce_scatter(g_ff, ids, "rs_w_ff")
    big = {
        "w_in": _adam_big(w_in, m_w_in, v_w_in, own_in, arr_in, 0, True, "adam_w_in"),
        "w_o": _adam_big(w_o, m_w_o, v_w_o, own_o, arr_o, 0, False, "adam_w_o"),
        "w_gate": _adam_big(w_gate, m_w_gate, v_w_gate, own_ff, arr_ff, 0, True, "adam_w_gate"),
        "w_up": _adam_big(w_up, m_w_up, v_w_up, own_ff, arr_ff, nl, True, "adam_w_up"),
        "w_down": _adam_big(w_down, m_w_down, v_w_down, own_ff, arr_ff, 2 * nl, False, "adam_w_down"),
    }

    def group(k):
        sm = small_out[k]
        return [sm[7], sm[0], big["w_in"][k], sm[1], sm[2], sm[3], sm[4], sm[5], big["w_o"][k], sm[6],
                big["w_gate"][k], big["w_up"][k], big["w_down"][k]]

    return (loss, grad_x, *group(0), *group(1), *group(2), *group(3))
```

```python
import jax
import jax.numpy as jnp
from jax import lax
from jax.experimental import pallas as pl
from jax.experimental.pallas import tpu as pltpu

f32, bf16 = jnp.float32, jnp.bfloat16
S = jax.ShapeDtypeStruct
BS = pl.BlockSpec

N_META = 16
TILE = 128
PAD = TILE - N_META
HEAD_DIM = 64
SWA_HEADS = 16
SWA_KV_HEADS = 4
SB_HEADS = 16
EPS = 1e-6
NEG = -1e30
ADAM_LR, ADAM_B1, ADAM_B2, ADAM_EPS, ADAM_WD, ADAM_STEP = 0.001, 0.9, 0.999, 1e-08, 0.01, 10
AXES = ("x", "y", "c")
NDEV = 8
VMEM_LIMIT_V7X = 56 * 1024 * 1024
MESH = pl.DeviceIdType.MESH
ANY = pl.BlockSpec(memory_space=pl.ANY)

NN = ((1,), (0,))
NT = ((1,), (1,))
TN = ((0,), (0,))


def _cp(*sem):
    return pltpu.CompilerParams(dimension_semantics=sem if sem else None, vmem_limit_bytes=VMEM_LIMIT_V7X)


def _pick(n, cands):
    for c in cands:
        if n % c == 0:
            return c
    raise ValueError(f"no tile for {n} in {cands}")


def _dot(a, b, dims):
    return lax.dot_general(a, b, (dims, ((), ())), preferred_element_type=f32)


def _embed(x, meta_full):
    seq, d = x.shape
    meta_pad = jnp.pad(meta_full, ((PAD, 0), (0, 0)))

    def body(x_ref, m_ref, o_ref):
        i = pl.program_id(0)

        @pl.when(i == 0)
        def _():
            o_ref[...] = m_ref[...]

        @pl.when(i > 0)
        def _():
            o_ref[...] = x_ref[...]

    return pl.pallas_call(
        body, out_shape=S((TILE + seq, d), f32), grid=(1 + seq // TILE,),
        in_specs=[BS((TILE, d), lambda i: (jnp.maximum(i - 1, 0), 0)), BS((TILE, d), lambda i: (0, 0))],
        out_specs=BS((TILE, d), lambda i: (i, 0)), name="embed", compiler_params=_cp("arbitrary"))(x, meta_pad)


def _rmsnorm_fwd(xs, gain, name):
    lp = xs[0].shape[0]
    widths = [x.shape[1] for x in xs]
    w = sum(widths)
    tr = _pick(lp, (384, 256, 128))
    n = len(xs)

    def body(*refs):
        g_ref, o_ref = refs[n], refs[n + 1]
        off = 0
        for k in range(n):
            sl = slice(off, off + widths[k])
            off += widths[k]
            xv = refs[k][...]
            r = lax.rsqrt(jnp.mean(xv * xv, axis=-1, keepdims=True) + EPS)
            o_ref[:, sl] = ((xv * r) * g_ref[:, sl]).astype(bf16)

    return pl.pallas_call(
        body, out_shape=S((lp, w), bf16), grid=(lp // tr,),
        in_specs=[BS((tr, wk), lambda i: (i, 0)) for wk in widths] + [BS((1, w), lambda i: (0, 0))],
        out_specs=BS((tr, w), lambda i: (i, 0)), name=name, compiler_params=_cp("parallel"))(*xs, gain)


def _rmsnorm_bwd(dy, xs, gain, res, name):
    lp = xs[0].shape[0]
    widths = [x.shape[1] for x in xs]
    w = sum(widths)
    tr = _pick(lp, (384, 256, 128))
    n = len(xs)
    has_res = res is not None
    assert not has_res or n == 1

    def body(*refs):
        dy_ref, g_ref = refs[0], refs[1 + n]
        res_ref = refs[2 + n] if has_res else None
        dx_refs = refs[2 + n + has_res:2 + 2 * n + has_res]
        dg_ref = refs[-1]

        @pl.when(pl.program_id(0) == 0)
        def _():
            dg_ref[...] = jnp.zeros_like(dg_ref)

        off = 0
        for k in range(n):
            sl = slice(off, off + widths[k])
            off += widths[k]
            xv = refs[1 + k][...]
            dyv = dy_ref[:, sl]
            r = lax.rsqrt(jnp.mean(xv * xv, axis=-1, keepdims=True) + EPS)
            xh = xv * r
            dg_ref[:, sl] += jnp.sum(dyv * xh, axis=0, keepdims=True)
            dxh = dyv * g_ref[:, sl]
            dx = r * (dxh - xh * jnp.mean(dxh * xh, axis=-1, keepdims=True))
            if has_res:
                dx = dx + res_ref[...]
            dx_refs[k][...] = dx

    rows = [BS((tr, wk), lambda i: (i, 0)) for wk in widths]
    full = BS((tr, w), lambda i: (i, 0))
    vec = BS((1, w), lambda i: (0, 0))
    args = (dy, *xs, gain) + ((res,) if has_res else ())
    out = pl.pallas_call(
        body, out_shape=tuple(S((lp, wk), f32) for wk in widths) + (S((1, w), f32),), grid=(lp // tr,),
        in_specs=[full] + rows + [vec] + ([full] if has_res else []), out_specs=tuple(rows) + (vec,),
        name=name, compiler_params=_cp("arbitrary"))(*args)
    return list(out[:n]), out[n]


def _loss_head(h, target):
    lp, d = h.shape

    def body(h_ref, t_ref, loss_ref, dh_ref):
        i = pl.program_id(0)

        @pl.when(i == 0)
        def _():
            loss_ref[...] = jnp.zeros_like(loss_ref)
            dh_ref[...] = jnp.zeros_like(dh_ref)

        @pl.when(i > 0)
        def _():
            e = h_ref[...] - t_ref[...]
            dh_ref[...] = e * (1.0 / d)
            loss_ref[...] += 0.5 * jnp.sum(jnp.sum(e * e, axis=-1, keepdims=True) * (1.0 / d))

    return pl.pallas_call(
        body, out_shape=(S((8, 128), f32), S((lp, d), f32)), grid=(lp // TILE,),
        in_specs=[BS((TILE, d), lambda i: (i, 0)), BS((TILE, d), lambda i: (jnp.maximum(i - 1, 0), 0))],
        out_specs=(BS((8, 128), lambda i: (0, 0)), BS((TILE, d), lambda i: (i, 0))),
        name="loss_head", compiler_params=_cp("arbitrary"))(h, target)


def _mm(a, b, layer, mode, name, out_dtype=f32, res=None):
    m, k = a.shape
    n = b.shape[2] if mode == "nn" else b.shape[1]
    tm = _pick(m, (384, 256, 128))
    tn = _pick(n, (512, 384, 256, 128))
    dims = NN if mode == "nn" else NT
    has_res = res is not None

    def body(*refs):
        if has_res:
            a_ref, b_ref, r_ref, o_ref = refs
        else:
            a_ref, b_ref, o_ref = refs
        acc = _dot(a_ref[...].astype(bf16), b_ref[...], dims)
        if has_res:
            acc = acc + r_ref[...]
        o_ref[...] = acc.astype(out_dtype)

    if mode == "nn":
        b_spec = BS((None, k, tn), lambda j, i: (layer, 0, j))
    else:
        b_spec = BS((None, tn, k), lambda j, i: (layer, j, 0))
    tile = BS((tm, tn), lambda j, i: (i, j))
    args = (a, b) + ((res,) if has_res else ())
    return pl.pallas_call(
        body, out_shape=S((m, n), out_dtype), grid=(n // tn, m // tm),
        in_specs=[BS((tm, k), lambda j, i: (i, 0)), b_spec] + ([tile] if has_res else []), out_specs=tile,
        name=name, compiler_params=_cp("parallel", "parallel"))(*args)


def _mm_tn(a, b, gbuf, slot, nslots, name):
    t, m = a.shape
    n = b.shape[1]
    tm = _pick(m, (512, 384, 256, 128))
    tk = _pick(t, (384, 256, 128))
    nk = t // tk

    def body(*refs):
        a_ref, b_ref, o_ref = refs[0], refs[1], refs[-1]
        kk = pl.program_id(1)

        @pl.when(kk == 0)
        def _():
            o_ref[...] = jnp.zeros_like(o_ref)

        o_ref[...] += _dot(a_ref[...].astype(bf16), b_ref[...].astype(bf16), TN)

    in_specs = [BS((tk, tm), lambda i, kk: (kk, i)), BS((tk, n), lambda i, kk: (kk, 0))]
    args = (a, b)
    aliases = {}
    if gbuf is not None:
        in_specs.append(ANY)
        args = args + (gbuf,)
        aliases = {2: 0}
    return pl.pallas_call(
        body, out_shape=S((nslots, m, n), f32), grid=(m // tm, nk), in_specs=in_specs,
        out_specs=BS((None, tm, n), lambda i, kk: (slot, i, 0)), input_output_aliases=aliases,
        name=name, compiler_params=_cp("parallel", "arbitrary"))(*args)


def _ffn_up(hn, wff, layer, nl):
    m, k = hn.shape
    n = wff.shape[1]
    tm = _pick(m, (384, 256, 128))
    tn = _pick(n, (512, 384, 256, 128))

    def body(a_ref, wg_ref, wu_ref, g_ref, u_ref, act_ref):
        a = a_ref[...]
        g = _dot(a, wg_ref[...], NT)
        u = _dot(a, wu_ref[...], NT)
        g_ref[...] = g
        u_ref[...] = u
        act_ref[...] = (g * jax.nn.sigmoid(g) * u).astype(bf16)

    tile = BS((tm, tn), lambda j, i: (i, j))
    return pl.pallas_call(
        body, out_shape=(S((m, n), f32), S((m, n), f32), S((m, n), bf16)), grid=(n // tn, m // tm),
        in_specs=[BS((tm, k), lambda j, i: (i, 0)), BS((None, tn, k), lambda j, i: (layer, j, 0)),
                  BS((None, tn, k), lambda j, i: (nl + layer, j, 0))],
        out_specs=(tile, tile, tile), name="ffn_up", compiler_params=_cp("parallel", "parallel"))(hn, wff, wff)


def _ffn_down_bwd(dh, wff, layer, nl, g, u):
    m, k = dh.shape
    n = wff.shape[1]
    tm = _pick(m, (384, 256, 128))
    tn = _pick(n, (512, 384, 256, 128))

    def body(a_ref, wd_ref, g_ref, u_ref, dg_ref, du_ref):
        dact = _dot(a_ref[...].astype(bf16), wd_ref[...], NT)
        gv = g_ref[...]
        sg = jax.nn.sigmoid(gv)
        dg_ref[...] = (dact * u_ref[...] * (sg * (1.0 + gv * (1.0 - sg)))).astype(bf16)
        du_ref[...] = (dact * (gv * sg)).astype(bf16)

    tile = BS((tm, tn), lambda j, i: (i, j))
    return pl.pallas_call(
        body, out_shape=(S((m, n), bf16), S((m, n), bf16)), grid=(n // tn, m // tm),
        in_specs=[BS((tm, k), lambda j, i: (i, 0)), BS((None, tn, k), lambda j, i: (2 * nl + layer, j, 0)), tile, tile],
        out_specs=(tile, tile), name="ffn_down_bwd", compiler_params=_cp("parallel", "parallel"))(dh, wff, g, u)


def _lane_lo():
    return lax.broadcasted_iota(jnp.int32, (1, TILE), 1) < HEAD_DIM


def _head_norm(x, g2):
    lo = _lane_lo()
    x2 = x * x
    s_lo = jnp.sum(jnp.where(lo, x2, 0.0), axis=-1, keepdims=True)
    s_hi = jnp.sum(jnp.where(lo, 0.0, x2), axis=-1, keepdims=True)
    r = jnp.where(lo, lax.rsqrt(s_lo * (1.0 / HEAD_DIM) + EPS), lax.rsqrt(s_hi * (1.0 / HEAD_DIM) + EPS))
    xh = x * r
    return xh * g2, xh, r


def _head_norm_bwd(dy, xh, r, g2):
    lo = _lane_lo()
    dgain = jnp.sum(dy * xh, axis=0, keepdims=True)
    dxh = dy * g2
    t = dxh * xh
    t_lo = jnp.sum(jnp.where(lo, t, 0.0), axis=-1, keepdims=True)
    t_hi = jnp.sum(jnp.where(lo, 0.0, t), axis=-1, keepdims=True)
    mt = jnp.where(lo, t_lo, t_hi) * (1.0 / HEAD_DIM)
    return r * (dxh - xh * mt), dgain


def _swa_slopes():
    return [2.0 ** (-8.0 * (h + 1) / SWA_HEADS) for h in range(SWA_HEADS)]


def _swa_masks(i):
    rows = lax.broadcasted_iota(jnp.int32, (TILE, TILE), 0)
    cols = lax.broadcasted_iota(jnp.int32, (TILE, TILE), 1)
    r_pos = i * TILE + rows
    prev = jnp.maximum(i - 1, 0)
    out = []
    for c, base in enumerate((0, prev * TILE, i * TILE)):
        s_pos = base + cols
        dist = r_pos - s_pos
        if c == 0:
            ok = (i >= 1) & (s_pos >= PAD)
        elif c == 1:
            ok = (i >= 2) & (dist < TILE)
        else:
            ok = (dist >= 0) & (s_pos >= PAD)
        out.append((ok, dist.astype(f32)))
    return out, prev


def _swa_logits(qm, kn, masks, slope, sink):
    scale = HEAD_DIM ** -0.5
    logits = []
    mx = jnp.full((TILE, 1), sink, f32)
    for c in range(3):
        ok, dist = masks[c]
        lg = jnp.where(ok, _dot(qm, kn[c], NT) * scale - slope * dist, NEG)
        logits.append(lg)
        mx = jnp.maximum(mx, jnp.max(lg, axis=-1, keepdims=True))
    return logits, mx


def _swa_specs(lp, pw):
    qw, kvw = SWA_HEADS * HEAD_DIM, SWA_KV_HEADS * HEAD_DIM
    kidx = qw // kvw
    return (BS((TILE, qw), lambda i: (i, 0)), BS((lp, kvw), lambda i: (0, kidx)), BS((lp, kvw), lambda i: (0, kidx + 1)),
            BS((1, TILE), lambda i: (0, 0)))


def _swa_fwd(proj, qg2, kg2, sinks):
    lp, pw = proj.shape
    qw = SWA_HEADS * HEAD_DIM
    group = SWA_HEADS // SWA_KV_HEADS
    slopes = _swa_slopes()

    def body(sink_ref, q_ref, k_ref, v_ref, qg_ref, kg_ref, o_ref):
        i = pl.program_id(0)
        lo = _lane_lo()
        masks, prev = _swa_masks(i)
        bases = (0, pl.multiple_of(prev * TILE, TILE), pl.multiple_of(i * TILE, TILE))
        qn = [_head_norm(q_ref[:, b * TILE:(b + 1) * TILE], qg_ref[...])[0] for b in range(qw // TILE)]
        acc = [jnp.zeros((TILE, TILE), f32) for _ in range(qw // TILE)]
        for kvh in range(SWA_KV_HEADS):
            kb, khalf = kvh // 2, kvh % 2
            kcols = slice(kb * TILE, (kb + 1) * TILE)
            ksel = lo if khalf == 0 else jnp.logical_not(lo)
            kn, vm = [], []
            for base in bases:
                kn.append(_head_norm(k_ref[pl.ds(base, TILE), kcols], kg_ref[...])[0].astype(bf16))
                vm.append(jnp.where(ksel, v_ref[pl.ds(base, TILE), kcols], 0.0).astype(bf16))
            for gi in range(group):
                h = kvh * group + gi
                qb, same = h // 2, (h % 2) == khalf
                qa = qn[qb] if same else pltpu.roll(qn[qb], HEAD_DIM, 1)
                qm = jnp.where(ksel, qa, 0.0).astype(bf16)
                sink = sink_ref[h]
                logits, mx = _swa_logits(qm, kn, masks, slopes[h], sink)
                den = jnp.exp(sink - mx)
                o = jnp.zeros((TILE, TILE), f32)
                for c in range(3):
                    p = jnp.exp(logits[c] - mx)
                    den = den + jnp.sum(p, axis=-1, keepdims=True)
                    o = o + _dot(p.astype(bf16), vm[c], NN)
                o = o / den
                acc[qb] = acc[qb] + (o if same else pltpu.roll(o, HEAD_DIM, 1))
        valid = (i * TILE + lax.broadcasted_iota(jnp.int32, (TILE, 1), 0)) >= PAD
        for b in range(qw // TILE):
            o_ref[:, b * TILE:(b + 1) * TILE] = jnp.where(valid, acc[b], 0.0)

    q_spec, k_spec, v_spec, vec = _swa_specs(lp, pw)
    return pl.pallas_call(
        body, out_shape=S((lp, qw), f32), grid=(lp // TILE,),
        in_specs=[BS(memory_space=pltpu.SMEM), q_spec, k_spec, v_spec, vec, vec],
        out_specs=BS((TILE, qw), lambda i: (i, 0)), name="swa_fwd", compiler_params=_cp("arbitrary"))(
            sinks, proj, proj, proj, qg2, kg2)


def _swa_bwd(proj, dout, qg2, kg2, sinks):
    lp, pw = proj.shape
    qw, kvw = SWA_HEADS * HEAD_DIM, SWA_KV_HEADS * HEAD_DIM
    group = SWA_HEADS // SWA_KV_HEADS
    scale = HEAD_DIM ** -0.5
    slopes = _swa_slopes()
    nt = lp // TILE

    def body(sink_ref, q_ref, k_ref, v_ref, do_ref, qg_ref, kg_ref, dq_ref, dk_ref, dv_ref, dqg_ref, dkg_ref, ds_ref,
             dkn_acc, dv_acc):
        i = pl.program_id(0)
        lo = _lane_lo()
        lane = lax.broadcasted_iota(jnp.int32, (1, TILE), 1)

        @pl.when(i == 0)
        def _():
            dkn_acc[...] = jnp.zeros_like(dkn_acc)
            dv_acc[...] = jnp.zeros_like(dv_acc)
            dqg_ref[...] = jnp.zeros_like(dqg_ref)
            dkg_ref[...] = jnp.zeros_like(dkg_ref)
            ds_ref[...] = jnp.zeros_like(ds_ref)

        masks, prev = _swa_masks(i)
        bases = (0, pl.multiple_of(prev * TILE, TILE), pl.multiple_of(i * TILE, TILE))
        qnorm = [_head_norm(q_ref[:, b * TILE:(b + 1) * TILE], qg_ref[...]) for b in range(qw // TILE)]
        dqn = [jnp.zeros((TILE, TILE), f32) for _ in range(qw // TILE)]
        for kvh in range(SWA_KV_HEADS):
            kb, khalf = kvh // 2, kvh % 2
            kcols = slice(kb * TILE, (kb + 1) * TILE)
            ksel = lo if khalf == 0 else jnp.logical_not(lo)
            kn, vm = [], []
            for base in bases:
                kn.append(_head_norm(k_ref[pl.ds(base, TILE), kcols], kg_ref[...])[0].astype(bf16))
                vm.append(jnp.where(ksel, v_ref[pl.ds(base, TILE), kcols], 0.0).astype(bf16))
            for gi in range(group):
                h = kvh * group + gi
                qb, same = h // 2, (h % 2) == khalf
                qa = qnorm[qb][0] if same else pltpu.roll(qnorm[qb][0], HEAD_DIM, 1)
                qm = jnp.where(ksel, qa, 0.0).astype(bf16)
                doa = do_ref[:, qb * TILE:(qb + 1) * TILE]
                doa = doa if same else pltpu.roll(doa, HEAD_DIM, 1)
                dom = jnp.where(ksel, doa, 0.0).astype(bf16)
                sink = sink_ref[h]
                logits, mx = _swa_logits(qm, kn, masks, slopes[h], sink)
                ps = [jnp.exp(lg - mx) for lg in logits]
                e_sink = jnp.exp(sink - mx)
                den = e_sink
                for c in range(3):
                    den = den + jnp.sum(ps[c], axis=-1, keepdims=True)
                inv = 1.0 / den
                ps = [p * inv for p in ps]
                dps = [_dot(dom, vm[c], NT) for c in range(3)]
                dsum = jnp.zeros((TILE, 1), f32)
                for c in range(3):
                    dsum = dsum + jnp.sum(ps[c] * dps[c], axis=-1, keepdims=True)
                ds_ref[...] += jnp.where(lane == h, -jnp.sum(e_sink * inv * dsum), 0.0)
                dq_h = jnp.zeros((TILE, TILE), f32)
                for c in range(3):
                    dsc = (ps[c] * (dps[c] - dsum) * scale).astype(bf16)
                    dq_h = dq_h + _dot(dsc, kn[c], NN)
                    dkn_acc[pl.ds(bases[c], TILE), kcols] += _dot(dsc, qm, TN)
                    dv_acc[pl.ds(bases[c], TILE), kcols] += _dot(ps[c].astype(bf16), dom, TN)
                dq_h = jnp.where(ksel, dq_h, 0.0)
                dqn[qb] = dqn[qb] + (dq_h if same else pltpu.roll(dq_h, HEAD_DIM, 1))
        for b in range(qw // TILE):
            _, xh, r = qnorm[b]
            dq, dgain = _head_norm_bwd(dqn[b], xh, r, qg_ref[...])
            dq_ref[:, b * TILE:(b + 1) * TILE] = dq.astype(bf16)
            dqg_ref[...] += dgain

        @pl.when(i == nt - 1)
        def _():
            dv_ref[...] = dv_acc[...].astype(bf16)

            def tile_step(t, carry):
                base = pl.multiple_of(t * TILE, TILE)
                for kb in range(kvw // TILE):
                    kcols = slice(kb * TILE, (kb + 1) * TILE)
                    _, xh, r = _head_norm(k_ref[pl.ds(base, TILE), kcols], kg_ref[...])
                    dk, dgain = _head_norm_bwd(dkn_acc[pl.ds(base, TILE), kcols], xh, r, kg_ref[...])
                    dk_ref[pl.ds(base, TILE), kcols] = dk.astype(bf16)
                    dkg_ref[...] += dgain
                return carry

            lax.fori_loop(0, nt, tile_step, 0)

    q_spec, k_spec, v_spec, vec = _swa_specs(lp, pw)
    whole = BS((lp, kvw), lambda i: (0, 0))
    return pl.pallas_call(
        body, out_shape=(S((lp, qw), bf16), S((lp, kvw), bf16), S((lp, kvw), bf16), S((1, TILE), f32), S((1, TILE), f32), S((1, TILE), f32)),
        grid=(nt,),
        in_specs=[BS(memory_space=pltpu.SMEM), q_spec, k_spec, v_spec, BS((TILE, qw), lambda i: (i, 0)), vec, vec],
        out_specs=(BS((TILE, qw), lambda i: (i, 0)), whole, whole, vec, vec, vec),
        scratch_shapes=[pltpu.VMEM((lp, kvw), f32), pltpu.VMEM((lp, kvw), f32)],
        name="swa_bwd", compiler_params=_cp("arbitrary"))(sinks, proj, proj, proj, dout, qg2, kg2)


LOG2E = 1.4426950408889634
ROWS2 = 2 * TILE


def _sb_stack(x):
    lo = _lane_lo()
    return jnp.concatenate([jnp.where(lo, x, 0.0), jnp.where(lo, 0.0, x)], axis=0).astype(bf16)


def _sb_unstack(x):
    return jnp.where(_lane_lo(), x[:TILE], x[TILE:])


def _sb_mask(qi, kj):
    rows = lax.broadcasted_iota(jnp.int32, (ROWS2, TILE), 0) & (TILE - 1)
    cols = lax.broadcasted_iota(jnp.int32, (ROWS2, TILE), 1)
    s_pos = kj * TILE + cols
    return (s_pos < qi * TILE + rows) & (s_pos >= PAD)


def _sb_sum_matrix(after):
    rows = lax.broadcasted_iota(jnp.int32, (ROWS2, ROWS2), 0) & (TILE - 1)
    cols = lax.broadcasted_iota(jnp.int32, (ROWS2, ROWS2), 1)
    tri = (rows > cols) if after else (rows < cols)
    return (tri | (cols >= TILE)).astype(bf16)


def _sb_sums(x, w, sum_mat):
    hi = x.astype(bf16)
    lo = (x - hi.astype(f32)).astype(bf16)
    return [_dot(jnp.concatenate([hi[:, j * TILE:(j + 1) * TILE], lo[:, j * TILE:(j + 1) * TILE]], axis=1), sum_mat, NN)
            for j in range(w)]


def _sb_block(q2, k_ref, first, w, valid, c_rep, after_ones):
    kwin = k_ref[pl.ds(pl.multiple_of(first * TILE, TILE), w * TILE), :].astype(bf16)
    z2 = _dot(q2, kwin, NT) * (HEAD_DIM ** -0.5 * LOG2E)
    ls2 = jnp.minimum(z2, 0.0) - jnp.log2(1.0 + jnp.exp2(jnp.minimum(z2, -z2)))
    m2 = ls2 - z2
    if valid is not None:
        m2 = jnp.where(valid, m2, 0.0)
    sums = _sb_sums(m2, w, after_ones)
    parts = [None] * w
    for j in reversed(range(w)):
        a = jnp.exp2(ls2[:, j * TILE:(j + 1) * TILE] + sums[j][:, :TILE] + c_rep)
        parts[j] = a if valid is None else jnp.where(valid, a, 0.0)
        c_rep = c_rep + sums[j][:, TILE:]
    return ls2, (parts[0] if w == 1 else jnp.concatenate(parts, axis=1)), c_rep


def _sb_walk(qi, block, state, right_to_left):
    mid = jnp.maximum(qi - 1, 0)
    n4, n2, n1 = mid // 4, (mid % 4) // 2, mid % 2
    loop = lax.fori_loop
    if right_to_left:
        state = block(qi, 1, True, state)
        state = loop(0, n4, lambda t, st: block(qi - 4 * (t + 1), 4, False, st), state)
        state = loop(0, n2, lambda t, st: block(1 + n1, 2, False, st), state)
        state = loop(0, n1, lambda t, st: block(1, 1, False, st), state)
        return loop(0, jnp.minimum(qi, 1), lambda t, st: block(0, 1, True, st), state)
    state = loop(0, jnp.minimum(qi, 1), lambda t, st: block(0, 1, True, st), state)
    state = loop(0, n1, lambda t, st: block(1, 1, False, st), state)
    state = loop(0, n2, lambda t, st: block(1 + n1, 2, False, st), state)
    state = loop(0, n4, lambda t, st: block(qi - 4 * (n4 - t), 4, False, st), state)
    return block(qi, 1, True, state)


def _sb_cols(lp, off):
    return BS((lp, TILE), lambda p: (0, off + p))


def _sb_fwd(proj):
    lp, pw = proj.shape
    sbw = SB_HEADS * HEAD_DIM
    npair = sbw // TILE
    nt = lp // TILE
    q0 = (pw - 3 * sbw) // TILE

    def body(q_ref, k_ref, v_ref, o_ref):
        after_ones = _sb_sum_matrix(True)

        def q_step(qi, carry):
            qbase = pl.multiple_of(qi * TILE, TILE)
            q2 = _sb_stack(q_ref[pl.ds(qbase, TILE), :])

            def block(first, w, masked, st):
                c_rep, acc = st
                _, a, c_rep = _sb_block(q2, k_ref, first, w, _sb_mask(qi, first) if masked else None, c_rep, after_ones)
                vwin = v_ref[pl.ds(pl.multiple_of(first * TILE, TILE), w * TILE), :].astype(bf16)
                return c_rep, acc + _dot(a.astype(bf16), vwin, NN)

            zero = jnp.zeros((ROWS2, TILE), f32)
            _, acc = _sb_walk(qi, block, (zero, zero), True)
            o_ref[pl.ds(qbase, TILE), :] = _sb_unstack(acc)
            return carry

        lax.fori_loop(0, nt, q_step, 0)

    return pl.pallas_call(
        body, out_shape=S((lp, sbw), f32), grid=(npair,),
        in_specs=[_sb_cols(lp, q0), _sb_cols(lp, q0 + npair), _sb_cols(lp, q0 + 2 * npair)],
        out_specs=_sb_cols(lp, 0), name="sb_fwd", compiler_params=_cp("parallel"))(proj, proj, proj)


def _sb_bwd(proj, dout):
    lp, pw = proj.shape
    sbw = SB_HEADS * HEAD_DIM
    npair = sbw // TILE
    nt = lp // TILE
    q0 = (pw - 3 * sbw) // TILE
    scale = HEAD_DIM ** -0.5

    def body(q_ref, k_ref, v_ref, do_ref, dq_ref, dk_ref, dv_ref, dk_acc, dv_acc, e_buf, b_buf):
        dk_acc[...] = jnp.zeros_like(dk_acc)
        dv_acc[...] = jnp.zeros_like(dv_acc)
        after_ones = _sb_sum_matrix(True)
        before_ones = _sb_sum_matrix(False)

        def q_step(qi, carry):
            qbase = pl.multiple_of(qi * TILE, TILE)
            q2 = _sb_stack(q_ref[pl.ds(qbase, TILE), :])
            do2 = _sb_stack(do_ref[pl.ds(qbase, TILE), :])

            def block_rl(first, w, masked, c_rep):
                ls2, a, c_rep = _sb_block(q2, k_ref, first, w, _sb_mask(qi, first) if masked else None, c_rep, after_ones)
                rows = pl.ds(pl.multiple_of(first * TILE, TILE), w * TILE)
                e = _dot(do2, v_ref[rows, :].astype(bf16), NT) * a
                beta = jnp.exp2(ls2)
                for j in range(w):
                    e_buf[first + j] = e[:, j * TILE:(j + 1) * TILE]
                    b_buf[first + j] = beta[:, j * TILE:(j + 1) * TILE]
                dv_acc[rows, :] += _dot(a.astype(bf16), do2, TN)
                return c_rep

            zero = jnp.zeros((ROWS2, TILE), f32)
            _sb_walk(qi, block_rl, zero, True)

            def block_lr(first, w, masked, st):
                e_rep, dq_acc = st
                rows = pl.ds(pl.multiple_of(first * TILE, TILE), w * TILE)
                es = [e_buf[first + j] for j in range(w)]
                sums = _sb_sums(es[0] if w == 1 else jnp.concatenate(es, axis=1), w, before_ones)
                parts = []
                for j in range(w):
                    beta = b_buf[first + j]
                    dz = (es[j] - beta * (es[j] + e_rep + sums[j][:, :TILE])) * scale
                    parts.append(jnp.where(_sb_mask(qi, first), dz, 0.0) if masked else dz)
                    e_rep = e_rep + sums[j][:, TILE:]
                dz = (parts[0] if w == 1 else jnp.concatenate(parts, axis=1)).astype(bf16)
                dk_acc[rows, :] += _dot(dz, q2, TN)
                return e_rep, dq_acc + _dot(dz, k_ref[rows, :].astype(bf16), NN)

            _, dq = _sb_walk(qi, block_lr, (zero, zero), False)
            dq_ref[pl.ds(qbase, TILE), :] = _sb_unstack(dq).astype(bf16)
            return carry

        lax.fori_loop(0, nt, q_step, 0)
        dk_ref[...] = dk_acc[...].astype(bf16)
        dv_ref[...] = dv_acc[...].astype(bf16)

    return pl.pallas_call(
        body, out_shape=(S((lp, sbw), bf16),) * 3, grid=(npair,),
        in_specs=[_sb_cols(lp, q0), _sb_cols(lp, q0 + npair), _sb_cols(lp, q0 + 2 * npair), _sb_cols(lp, 0)],
        out_specs=(_sb_cols(lp, 0),) * 3,
        scratch_shapes=[pltpu.VMEM((lp, TILE), f32), pltpu.VMEM((lp, TILE), f32),
                        pltpu.VMEM((nt, ROWS2, TILE), f32), pltpu.VMEM((nt, ROWS2, TILE), f32)],
        name="sb_bwd", compiler_params=_cp("parallel"))(proj, proj, proj, dout)


def _place():
    x, y, c = lax.axis_index("x"), lax.axis_index("y"), lax.axis_index("c")
    chips = [(1 - x, y), (x, 1 - y), (1 - x, 1 - y)]
    return x, y, c, chips


def _two_level_gather_body(rows, x_ref, send_sems, recv_sems, local_sem):
    x, y, c, chips = _place()
    me, sibling = (x, y, c), (x, y, 1 - c)

    def copy(k, block, to, src=None):
        return pltpu.make_async_remote_copy(
            src_ref=rows(*block) if src is None else src, dst_ref=rows(*block),
            send_sem=send_sems.at[k], recv_sem=recv_sems.at[k], device_id=to, device_id_type=MESH)

    mine = pltpu.make_async_copy(x_ref, rows(*me), local_sem)
    mine.start()
    first = [copy(0, me, sibling, src=x_ref)]
    first += [copy(1 + j, me, (*chip, c), src=x_ref) for j, chip in enumerate(chips)]
    for cp in first:
        cp.start()
    passed = [copy(4 + j, (*chip, c), sibling) for j, chip in enumerate(chips)]
    for j, chip in enumerate(chips):
        copy(1 + j, (*chip, c), me).wait_recv()
        passed[j].start()
    copy(0, sibling, me).wait_recv()
    for j, chip in enumerate(chips):
        copy(4 + j, (*chip, 1 - c), me).wait_recv()
    for cp in first + passed:
        cp.wait_send()
    mine.wait()


_GATHER_SEMS = [pltpu.SemaphoreType.DMA((7,)), pltpu.SemaphoreType.DMA((7,)), pltpu.SemaphoreType.DMA]


def _ag_rows(shard, name):
    ns, r, cdim = shard.shape

    def body(x_ref, o_ref, send_sems, recv_sems, local_sem):
        def rows(px, py, pc):
            return o_ref.at[:, pl.ds(pl.multiple_of((4 * px + 2 * py + pc) * r, 16), r), :]
        _two_level_gather_body(rows, x_ref, send_sems, recv_sems, local_sem)

    return pl.pallas_call(
        body, out_shape=S((ns, NDEV * r, cdim), shard.dtype), in_specs=[ANY], out_specs=ANY,
        scratch_shapes=_GATHER_SEMS, name=name)(shard)


def _ag_small(shard, name):
    r, w = shard.shape

    def body(x_ref, o_ref, send_sems, recv_sems, local_sem):
        def rows(px, py, pc):
            return o_ref.at[pl.ds(pl.multiple_of((4 * px + 2 * py + pc) * r, 8), r), :]
        _two_level_gather_body(rows, x_ref, send_sems, recv_sems, local_sem)

    vmem = BS(memory_space=pltpu.VMEM)
    return pl.pallas_call(
        body, out_shape=S((NDEV * r, w), shard.dtype), in_specs=[vmem], out_specs=vmem,
        scratch_shapes=_GATHER_SEMS, name=name)(shard)


def _rs_swap(g, name):
    ns, rows8, cdim = g.shape
    r = rows8 // NDEV

    def body(g_ref, o_ref, send_sems, recv_sems):
        x, y, c, chips = _place()
        rel = [(x, y)] + chips
        copies = []
        for j, (px, py) in enumerate(rel):
            d = 4 * px + 2 * py + (1 - c)
            copies.append(pltpu.make_async_remote_copy(
                src_ref=g_ref.at[:, pl.ds(pl.multiple_of(d * r, 8), r), :], dst_ref=o_ref.at[j],
                send_sem=send_sems.at[j], recv_sem=recv_sems.at[j], device_id=(x, y, 1 - c), device_id_type=MESH))
        for cp in copies:
            cp.start()
        for cp in copies:
            cp.wait_recv()
        for cp in copies:
            cp.wait_send()

    return pl.pallas_call(
        body, out_shape=S((4, ns, r, cdim), f32), in_specs=[ANY], out_specs=ANY,
        scratch_shapes=[pltpu.SemaphoreType.DMA((4,)), pltpu.SemaphoreType.DMA((4,))], name=name)(g)


def _rel_ids():
    x, y, c = lax.axis_index("x"), lax.axis_index("y"), lax.axis_index("c")
    rel = [(x, y), (1 - x, y), (x, 1 - y), (1 - x, 1 - y)]
    return jnp.stack([4 * px + 2 * py + c for px, py in rel]).astype(jnp.int32)


def _rs_partial(g, sib, ids, name):
    ns, rows8, cdim = g.shape
    r = rows8 // NDEV
    tr = r // 2
    nb = r // tr

    def own_body(ids_ref, g_ref, s_ref, o_ref):
        o_ref[...] = g_ref[...] + s_ref[...]

    own = pl.pallas_call(
        own_body, out_shape=S((ns, r, cdim), f32),
        grid_spec=pltpu.PrefetchScalarGridSpec(
            num_scalar_prefetch=1, grid=(ns, nb),
            in_specs=[BS((None, tr, cdim), lambda s, t, ids_ref: (s, ids_ref[0] * nb + t, 0)),
                      BS((None, None, tr, cdim), lambda s, t, ids_ref: (0, s, t, 0))],
            out_specs=BS((None, tr, cdim), lambda s, t, ids_ref: (s, t, 0))),
        name=name + "_own", compiler_params=_cp("parallel", "parallel"))(ids, g, sib)

    def send_body(ids_ref, g_ref, s_ref, o_ref):
        o_ref[...] = (g_ref[...] + s_ref[...]).astype(bf16)

    send = pl.pallas_call(
        send_body, out_shape=S((3, ns, r, cdim), bf16),
        grid_spec=pltpu.PrefetchScalarGridSpec(
            num_scalar_prefetch=1, grid=(3, ns, nb),
            in_specs=[BS((None, tr, cdim), lambda j, s, t, ids_ref: (s, ids_ref[j + 1] * nb + t, 0)),
                      BS((None, None, tr, cdim), lambda j, s, t, ids_ref: (j + 1, s, t, 0))],
            out_specs=BS((None, None, tr, cdim), lambda j, s, t, ids_ref: (j, s, t, 0))),
        name=name + "_send", compiler_params=_cp("parallel", "parallel", "parallel"))(ids, g, sib)
    return own, send


def _rs_ici(send, name):
    def body(s_ref, o_ref, send_sems, recv_sems):
        x, y, c, chips = _place()
        copies = [pltpu.make_async_remote_copy(
            src_ref=s_ref.at[j], dst_ref=o_ref.at[j], send_sem=send_sems.at[j], recv_sem=recv_sems.at[j],
            device_id=(*chip, c), device_id_type=MESH) for j, chip in enumerate(chips)]
        for cp in copies:
            cp.start()
        for cp in copies:
            cp.wait_recv()
        for cp in copies:
            cp.wait_send()

    return pl.pallas_call(
        body, out_shape=S(send.shape, send.dtype), in_specs=[ANY], out_specs=ANY,
        scratch_shapes=[pltpu.SemaphoreType.DMA((3,)), pltpu.SemaphoreType.DMA((3,))], name=name)(send)


def _reduce_scatter(g, ids, name):
    sib = _rs_swap(g, name + "_swap")
    own, send = _rs_partial(g, sib, ids, name + "_part")
    return own, _rs_ici(send, name + "_ici")


def _prep(w, name):
    nl, r, cdim = w.shape
    blk = BS((None, r, cdim), lambda l: (l, 0, 0))

    def body(w_ref, o_ref):
        o_ref[...] = w_ref[...].astype(bf16)

    return pl.pallas_call(body, out_shape=S(w.shape, bf16), grid=(nl,), in_specs=[blk], out_specs=blk,
                          name=name, compiler_params=_cp("parallel"))(w)


def _prep_t(w, name):
    nl, d, r = w.shape
    tc = _pick(d, (256, 128))

    def body(w_ref, o_ref):
        o_ref[...] = w_ref[...].T.astype(bf16)

    return pl.pallas_call(
        body, out_shape=S((nl, r, d), bf16), grid=(nl, d // tc),
        in_specs=[BS((None, tc, r), lambda l, t: (l, t, 0))], out_specs=BS((None, r, tc), lambda l, t: (l, 0, t)),
        name=name, compiler_params=_cp("parallel", "parallel"))(w)


def _adam_math(w, g, m, v):
    m2 = ADAM_B1 * m + (1.0 - ADAM_B1) * g
    v2 = ADAM_B2 * v + (1.0 - ADAM_B2) * (g * g)
    m_hat = m2 / (1.0 - ADAM_B1 ** ADAM_STEP)
    v_hat = v2 / (1.0 - ADAM_B2 ** ADAM_STEP)
    delta = -ADAM_LR * (m_hat / (jnp.sqrt(v_hat) + ADAM_EPS) + ADAM_WD * w)
    return delta, m2, v2


def _adam_big(w, m, v, own, arrived, slot0, transposed, name):
    nl, a, b = w.shape
    if transposed:
        ta = _pick(a, (256, 128))
        w_blk = BS((None, ta, b), lambda l, t: (l, t, 0))
        own_blk = BS((None, b, ta), lambda l, t: (slot0 + l, 0, t))
        arr_blk = [BS((None, None, b, ta), lambda l, t, j=j: (j, slot0 + l, 0, t)) for j in range(3)]
    else:
        ta = _pick(a, (a // 4, a // 2, a)) if (a // 4) % 16 == 0 else a
        w_blk = BS((None, ta, b), lambda l, t: (l, t, 0))
        own_blk = BS((None, ta, b), lambda l, t: (slot0 + l, t, 0))
        arr_blk = [BS((None, None, ta, b), lambda l, t, j=j: (j, slot0 + l, t, 0)) for j in range(3)]

    def body(w_ref, m_ref, v_ref, own_ref, a0_ref, a1_ref, a2_ref, g_ref, d_ref, m2_ref, v2_ref):
        g = ((own_ref[...] + a0_ref[...].astype(f32)) + a1_ref[...].astype(f32)) + a2_ref[...].astype(f32)
        if transposed:
            g = g.T
        delta, m2, v2 = _adam_math(w_ref[...], g, m_ref[...], v_ref[...])
        g_ref[...] = g
        d_ref[...] = delta
        m2_ref[...] = m2
        v2_ref[...] = v2

    return pl.pallas_call(
        body, out_shape=(S(w.shape, f32),) * 4, grid=(nl, a // ta),
        in_specs=[w_blk, w_blk, w_blk, own_blk] + arr_blk, out_specs=(w_blk,) * 4,
        name=name, compiler_params=_cp("parallel", "parallel"))(w, m, v, own, arrived, arrived, arrived)


def _adam_small(gathered, w, m, v):
    r, wd = w.shape

    def body(g_ref, w_ref, m_ref, v_ref, gs_ref, d_ref, m2_ref, v2_ref):
        g = g_ref[0:r, :]
        for dev in range(1, NDEV):
            g = g + g_ref[dev * r:(dev + 1) * r, :]
        delta, m2, v2 = _adam_math(w_ref[...], g, m_ref[...], v_ref[...])
        gs_ref[...] = g
        d_ref[...] = delta
        m2_ref[...] = m2
        v2_ref[...] = v2

    return pl.pallas_call(body, out_shape=(S((r, wd), f32),) * 4, name="adam_small", compiler_params=_cp())(gathered, w, m, v)


def _pack(arrays, width):
    flat = jnp.concatenate([a.reshape(-1) for a in arrays])
    rows = -(-flat.shape[0] // (8 * width)) * 8
    return jnp.pad(flat, (0, rows * width - flat.shape[0])).reshape(rows, width)


def _unpack(packed, shapes):
    flat = packed.reshape(-1)
    out, off = [], 0
    for shp in shapes:
        n = 1
        for s in shp:
            n *= s
        out.append(flat[off:off + n].reshape(shp))
        off += n
    return out


def kernel(x, meta_tokens, attn_norm_g, w_in, q_norm_g, k_norm_g, attn_sinks, swa_out_g, sb_out_g, w_o, ffn_norm_g, w_gate, w_up, w_down, loss_target, m_meta_tokens, m_attn_norm_g, m_w_in, m_q_norm_g, m_k_norm_g, m_attn_sinks, m_swa_out_g, m_sb_out_g, m_w_o, m_ffn_norm_g, m_w_gate, m_w_up, m_w_down, v_meta_tokens, v_attn_norm_g, v_w_in, v_q_norm_g, v_k_norm_g, v_attn_sinks, v_swa_out_g, v_sb_out_g, v_w_o, v_ffn_norm_g, v_w_gate, v_w_up, v_w_down):
    nl, d = attn_norm_g.shape
    x2, target = x[0], loss_target[0]
    me = 4 * lax.axis_index("x") + 2 * lax.axis_index("y") + lax.axis_index("c")
    ids = _rel_ids()

    meta_all = _ag_small(meta_tokens, "ag_meta")
    meta_full = meta_all.reshape(NDEV, N_META, -1).transpose(1, 0, 2).reshape(N_META, d)
    w_in_t = _ag_rows(_prep_t(w_in, "prep_w_in"), "ag_w_in")
    w_o_f = _ag_rows(_prep(w_o, "prep_w_o"), "ag_w_o")
    ff_shard = jnp.concatenate([_prep_t(w_gate, "prep_w_gate"), _prep_t(w_up, "prep_w_up"), _prep(w_down, "prep_w_down")], axis=0)
    w_ff = _ag_rows(ff_shard, "ag_w_ff")

    qg2 = jnp.tile(q_norm_g, (1, TILE // HEAD_DIM))
    kg2 = jnp.tile(k_norm_g, (1, TILE // HEAD_DIM))
    out_g = jnp.concatenate([swa_out_g, sb_out_g], axis=1)

    h = _embed(x2, meta_full)
    saved = []
    for l in range(nl):
        hn = _rmsnorm_fwd([h], attn_norm_g[l:l + 1], "attn_norm")
        proj = _mm(hn, w_in_t, l, "nt", "proj")
        out_a = _swa_fwd(proj, qg2[l:l + 1], kg2[l:l + 1], attn_sinks[l])
        out_b = _sb_fwd(proj)
        mixed = _rmsnorm_fwd([out_a, out_b], out_g[l:l + 1], "out_norm")
        h_mid = _mm(mixed, w_o_f, l, "nn", "attn_out", res=h)
        hn2 = _rmsnorm_fwd([h_mid], ffn_norm_g[l:l + 1], "ffn_norm")
        g, u, act = _ffn_up(hn2, w_ff, l, nl)
        h_out = _mm(act, w_ff, 2 * nl + l, "nn", "ffn_down", res=h_mid)
        saved.append((h, hn, proj, out_a, out_b, mixed, h_mid, hn2, g, u, act))
        h = h_out

    loss_tile, dh = _loss_head(h, target)
    loss = lax.psum(loss_tile[0, 0], AXES)

    g_in = g_o = g_ff = None
    small = {k: [None] * nl for k in ("attn", "ffn", "out", "q", "k", "sink")}
    for l in reversed(range(nl)):
        h_in, hn, proj, out_a, out_b, mixed, h_mid, hn2, g, u, act = saved[l]
        dg, du = _ffn_down_bwd(dh, w_ff, l, nl, g, u)
        g_ff = _mm_tn(act, dh, g_ff, 2 * nl + l, 3 * nl, "grad_w_down")
        g_ff = _mm_tn(dg, hn2, g_ff, l, 3 * nl, "grad_w_gate")
        g_ff = _mm_tn(du, hn2, g_ff, nl + l, 3 * nl, "grad_w_up")
        dhn2 = _mm(du, w_ff, nl + l, "nn", "d_ffn_in_up", res=_mm(dg, w_ff, l, "nn", "d_ffn_in_gate"))
        (dh_mid,), small["ffn"][l] = _rmsnorm_bwd(dhn2, [h_mid], ffn_norm_g[l:l + 1], dh, "ffn_norm_bwd")
        dmixed = _mm(dh_mid, w_o_f, l, "nt", "d_mixed")
        g_o = _mm_tn(mixed, dh_mid, g_o, l, nl, "grad_w_o")
        (dout_a, dout_b), small["out"][l] = _rmsnorm_bwd(dmixed, [out_a, out_b], out_g[l:l + 1], None, "out_norm_bwd")
        dq_b, dk_b, dv_b = _sb_bwd(proj, dout_b)
        dq_a, dk_a, dv_a, dqg, dkg, dsk = _swa_bwd(proj, dout_a, qg2[l:l + 1], kg2[l:l + 1], attn_sinks[l])
        small["q"][l] = dqg[0, :HEAD_DIM] + dqg[0, HEAD_DIM:]
        small["k"][l] = dkg[0, :HEAD_DIM] + dkg[0, HEAD_DIM:]
        small["sink"][l] = dsk[0, :SWA_HEADS]
        dproj = jnp.concatenate([dq_a, dk_a, dv_a, dq_b, dk_b, dv_b], axis=1)
        dhn = _mm(dproj, w_in_t, l, "nn", "d_attn_in")
        g_in = _mm_tn(dproj, hn, g_in, l, nl, "grad_w_in")
        (dh,), small["attn"][l] = _rmsnorm_bwd(dhn, [h_in], attn_norm_g[l:l + 1], dh_mid, "attn_norm_bwd")

    grad_x = dh[TILE:][None]
    d_meta = dh[PAD:TILE]

    qw = SWA_HEADS * HEAD_DIM
    d_out = jnp.concatenate(small["out"], axis=0)
    small_grads = [jnp.concatenate(small["attn"], axis=0), jnp.stack(small["q"]), jnp.stack(small["k"]), jnp.stack(small["sink"]),
                   d_out[:, :qw], d_out[:, qw:], jnp.concatenate(small["ffn"], axis=0), d_meta]
    col0 = me * meta_tokens.shape[1]

    def widen(a):
        return lax.dynamic_update_slice(jnp.zeros((N_META, d), f32), a, (0, col0))

    small_w = [attn_norm_g, q_norm_g, k_norm_g, attn_sinks, swa_out_g, sb_out_g, ffn_norm_g]
    small_m = [m_attn_norm_g, m_q_norm_g, m_k_norm_g, m_attn_sinks, m_swa_out_g, m_sb_out_g, m_ffn_norm_g]
    small_v = [v_attn_norm_g, v_q_norm_g, v_k_norm_g, v_attn_sinks, v_swa_out_g, v_sb_out_g, v_ffn_norm_g]
    shapes = [a.shape for a in small_w] + [(N_META, d)]
    packed = _adam_small(_ag_small(_pack(small_grads, d), "ag_small_grads"),
                         _pack(small_w + [widen(meta_tokens)], d), _pack(small_m + [widen(m_meta_tokens)], d),
                         _pack(small_v + [widen(v_meta_tokens)], d))
    small_out = []
    for p in packed:
        parts = _unpack(p, shapes)
        parts[-1] = lax.dynamic_slice(parts[-1], (0, col0), meta_tokens.shape)
        small_out.append(parts)

    own_in, arr_in = _reduce_scatter(g_in, ids, "rs_w_in")
    own_o, arr_o = _reduce_scatter(g_o, ids, "rs_w_o")
    own_ff, arr_ff = _reduce_scatter(g_ff, ids, "rs_w_ff")
    big = {
        "w_in": _adam_big(w_in, m_w_in, v_w_in, own_in, arr_in, 0, True, "adam_w_in"),
        "w_o": _adam_big(w_o, m_w_o, v_w_o, own_o, arr_o, 0, False, "adam_w_o"),
        "w_gate": _adam_big(w_gate, m_w_gate, v_w_gate, own_ff, arr_ff, 0, True, "adam_w_gate"),
        "w_up": _adam_big(w_up, m_w_up, v_w_up, own_ff, arr_ff, nl, True, "adam_w_up"),
        "w_down": _adam_big(w_down, m_w_down, v_w_down, own_ff, arr_ff, 2 * nl, False, "adam_w_down"),
    }

    def group(k):
        sm = small_out[k]
        return [sm[7], sm[0], big["w_in"][k], sm[1], sm[2], sm[3], sm[4], sm[5], big["w_o"][k], sm[6],
                big["w_gate"][k], big["w_up"][k], big["w_down"][k]]

    return (loss, grad_x, *group(0), *group(1), *group(2), *group(3))
```

```python
import jax
import jax.numpy as jnp
from jax import lax
from jax.experimental import pallas as pl
from jax.experimental.pallas import tpu as pltpu

f32, bf16 = jnp.float32, jnp.bfloat16
S = jax.ShapeDtypeStruct
BS = pl.BlockSpec

N_META = 16
TILE = 128
PAD = TILE - N_META
HEAD_DIM = 64
SWA_HEADS = 16
SWA_KV_HEADS = 4
SB_HEADS = 16
EPS = 1e-6
NEG = -1e30
ADAM_LR, ADAM_B1, ADAM_B2, ADAM_EPS, ADAM_WD, ADAM_STEP = 0.001, 0.9, 0.999, 1e-08, 0.01, 10
AXES = ("x", "y", "c")
NDEV = 8
VMEM_LIMIT_V7X = 56 * 1024 * 1024
MESH = pl.DeviceIdType.MESH
ANY = pl.BlockSpec(memory_space=pl.ANY)

NN = ((1,), (0,))
NT = ((1,), (1,))
TN = ((0,), (0,))


def _cp(*sem):
    return pltpu.CompilerParams(dimension_semantics=sem if sem else None, vmem_limit_bytes=VMEM_LIMIT_V7X)


def _pick(n, cands):
    for c in cands:
        if n % c == 0:
            return c
    raise ValueError(f"no tile for {n} in {cands}")


def _dot(a, b, dims):
    return lax.dot_general(a, b, (dims, ((), ())), preferred_element_type=f32)


def _embed(x, meta_full):
    seq, d = x.shape
    meta_pad = jnp.pad(meta_full, ((PAD, 0), (0, 0)))

    def body(x_ref, m_ref, o_ref):
        i = pl.program_id(0)

        @pl.when(i == 0)
        def _():
            o_ref[...] = m_ref[...]

        @pl.when(i > 0)
        def _():
            o_ref[...] = x_ref[...]

    return pl.pallas_call(
        body, out_shape=S((TILE + seq, d), f32), grid=(1 + seq // TILE,),
        in_specs=[BS((TILE, d), lambda i: (jnp.maximum(i - 1, 0), 0)), BS((TILE, d), lambda i: (0, 0))],
        out_specs=BS((TILE, d), lambda i: (i, 0)), name="embed", compiler_params=_cp("arbitrary"))(x, meta_pad)


def _rmsnorm_fwd(xs, gain, name):
    lp = xs[0].shape[0]
    widths = [x.shape[1] for x in xs]
    w = sum(widths)
    tr = _pick(lp, (384, 256, 128))
    n = len(xs)

    def body(*refs):
        g_ref, o_ref = refs[n], refs[n + 1]
        off = 0
        for k in range(n):
            sl = slice(off, off + widths[k])
            off += widths[k]
            xv = refs[k][...]
            r = lax.rsqrt(jnp.mean(xv * xv, axis=-1, keepdims=True) + EPS)
            o_ref[:, sl] = ((xv * r) * g_ref[:, sl]).astype(bf16)

    return pl.pallas_call(
        body, out_shape=S((lp, w), bf16), grid=(lp // tr,),
        in_specs=[BS((tr, wk), lambda i: (i, 0)) for wk in widths] + [BS((1, w), lambda i: (0, 0))],
        out_specs=BS((tr, w), lambda i: (i, 0)), name=name, compiler_params=_cp("parallel"))(*xs, gain)


def _rmsnorm_bwd(dy, xs, gain, res, name, out_dtype=f32):
    lp = xs[0].shape[0]
    widths = [x.shape[1] for x in xs]
    w = sum(widths)
    tr = _pick(lp, (384, 256, 128))
    n = len(xs)
    has_res = res is not None
    assert not has_res or n == 1

    def body(*refs):
        dy_ref, g_ref = refs[0], refs[1 + n]
        res_ref = refs[2 + n] if has_res else None
        dx_refs = refs[2 + n + has_res:2 + 2 * n + has_res]
        dg_ref = refs[-1]

        @pl.when(pl.program_id(0) == 0)
        def _():
            dg_ref[...] = jnp.zeros_like(dg_ref)

        off = 0
        for k in range(n):
            sl = slice(off, off + widths[k])
            off += widths[k]
            xv = refs[1 + k][...]
            dyv = dy_ref[:, sl]
            r = lax.rsqrt(jnp.mean(xv * xv, axis=-1, keepdims=True) + EPS)
            xh = xv * r
            dg_ref[:, sl] += jnp.sum(dyv * xh, axis=0, keepdims=True)
            dxh = dyv * g_ref[:, sl]
            dx = r * (dxh - xh * jnp.mean(dxh * xh, axis=-1, keepdims=True))
            if has_res:
                dx = dx + res_ref[...]
            dx_refs[k][...] = dx.astype(out_dtype)

    rows = [BS((tr, wk), lambda i: (i, 0)) for wk in widths]
    full = BS((tr, w), lambda i: (i, 0))
    vec = BS((1, w), lambda i: (0, 0))
    args = (dy, *xs, gain) + ((res,) if has_res else ())
    out = pl.pallas_call(
        body, out_shape=tuple(S((lp, wk), out_dtype) for wk in widths) + (S((1, w), f32),), grid=(lp // tr,),
        in_specs=[full] + rows + [vec] + ([full] if has_res else []), out_specs=tuple(rows) + (vec,),
        name=name, compiler_params=_cp("arbitrary"))(*args)
    return list(out[:n]), out[n]


def _loss_head(h, target):
    lp, d = h.shape

    def body(h_ref, t_ref, loss_ref, dh_ref):
        i = pl.program_id(0)

        @pl.when(i == 0)
        def _():
            loss_ref[...] = jnp.zeros_like(loss_ref)
            dh_ref[...] = jnp.zeros_like(dh_ref)

        @pl.when(i > 0)
        def _():
            e = h_ref[...] - t_ref[...]
            dh_ref[...] = e * (1.0 / d)
            loss_ref[...] += 0.5 * jnp.sum(jnp.sum(e * e, axis=-1, keepdims=True) * (1.0 / d))

    return pl.pallas_call(
        body, out_shape=(S((8, 128), f32), S((lp, d), f32)), grid=(lp // TILE,),
        in_specs=[BS((TILE, d), lambda i: (i, 0)), BS((TILE, d), lambda i: (jnp.maximum(i - 1, 0), 0))],
        out_specs=(BS((8, 128), lambda i: (0, 0)), BS((TILE, d), lambda i: (i, 0))),
        name="loss_head", compiler_params=_cp("arbitrary"))(h, target)


def _mm(a, b, layer, mode, name, out_dtype=f32, res=None, cols=None):
    m, k = a.shape
    n = b.shape[2] if mode == "nn" else b.shape[1]
    c0 = 0
    if cols is not None:
        assert mode == "nt"
        c0, n = cols
    tn = _pick(n, (512, 384, 256, 128))
    while c0 % tn:
        tn -= TILE
    assert n % tn == 0
    joff = c0 // tn
    tm = _pick(m, (384, 256, 128))
    dims = NN if mode == "nn" else NT
    has_res = res is not None

    def body(*refs):
        if has_res:
            a_ref, b_ref, r_ref, o_ref = refs
        else:
            a_ref, b_ref, o_ref = refs
        acc = _dot(a_ref[...].astype(bf16), b_ref[...], dims)
        if has_res:
            acc = acc + r_ref[...]
        o_ref[...] = acc.astype(out_dtype)

    if mode == "nn":
        b_spec = BS((None, k, tn), lambda j, i: (layer, 0, j))
    else:
        b_spec = BS((None, tn, k), lambda j, i: (layer, joff + j, 0))
    tile = BS((tm, tn), lambda j, i: (i, j))
    args = (a, b) + ((res,) if has_res else ())
    return pl.pallas_call(
        body, out_shape=S((m, n), out_dtype), grid=(n // tn, m // tm),
        in_specs=[BS((tm, k), lambda j, i: (i, 0)), b_spec] + ([tile] if has_res else []), out_specs=tile,
        name=name, compiler_params=_cp("parallel", "parallel"))(*args)


def _mm_tn(a, b, gbuf, slot, nslots, name):
    t, m = a.shape
    n = b.shape[1]
    tm = _pick(m, (512, 384, 256, 128))
    tk = _pick(t, (1408, 1024, 512, 384, 256, 128))
    nk = t // tk

    def body(*refs):
        a_ref, b_ref, o_ref = refs[0], refs[1], refs[-1]
        kk = pl.program_id(1)

        @pl.when(kk == 0)
        def _():
            o_ref[...] = jnp.zeros_like(o_ref)

        o_ref[...] += _dot(a_ref[...].astype(bf16), b_ref[...].astype(bf16), TN)

    in_specs = [BS((tk, tm), lambda i, kk: (kk, i)), BS((tk, n), lambda i, kk: (kk, 0))]
    args = (a, b)
    aliases = {}
    if gbuf is not None:
        in_specs.append(ANY)
        args = args + (gbuf,)
        aliases = {2: 0}
    return pl.pallas_call(
        body, out_shape=S((nslots, m, n), f32), grid=(m // tm, nk), in_specs=in_specs,
        out_specs=BS((None, tm, n), lambda i, kk: (slot, i, 0)), input_output_aliases=aliases,
        name=name, compiler_params=_cp("parallel", "arbitrary"))(*args)


def _ffn_up(hn, wff, layer, nl):
    m, k = hn.shape
    n = wff.shape[1]
    tm = _pick(m, (384, 256, 128))
    tn = _pick(n, (512, 384, 256, 128))

    def body(a_ref, wg_ref, wu_ref, g_ref, u_ref, act_ref):
        a = a_ref[...]
        g = _dot(a, wg_ref[...], NT)
        u = _dot(a, wu_ref[...], NT)
        g_ref[...] = g
        u_ref[...] = u
        act_ref[...] = (g * jax.nn.sigmoid(g) * u).astype(bf16)

    tile = BS((tm, tn), lambda j, i: (i, j))
    return pl.pallas_call(
        body, out_shape=(S((m, n), f32), S((m, n), f32), S((m, n), bf16)), grid=(n // tn, m // tm),
        in_specs=[BS((tm, k), lambda j, i: (i, 0)), BS((None, tn, k), lambda j, i: (layer, j, 0)),
                  BS((None, tn, k), lambda j, i: (nl + layer, j, 0))],
        out_specs=(tile, tile, tile), name="ffn_up", compiler_params=_cp("parallel", "parallel"))(hn, wff, wff)


def _ffn_down_bwd(dh, wff, layer, nl, g, u):
    m, k = dh.shape
    n = wff.shape[1]
    tm = _pick(m, (384, 256, 128))
    tn = _pick(n, (512, 384, 256, 128))

    def body(a_ref, wd_ref, g_ref, u_ref, dg_ref, du_ref):
        dact = _dot(a_ref[...].astype(bf16), wd_ref[...], NT)
        gv = g_ref[...]
        sg = jax.nn.sigmoid(gv)
        dg_ref[...] = (dact * u_ref[...] * (sg * (1.0 + gv * (1.0 - sg)))).astype(bf16)
        du_ref[...] = (dact * (gv * sg)).astype(bf16)

    tile = BS((tm, tn), lambda j, i: (i, j))
    return pl.pallas_call(
        body, out_shape=(S((m, n), bf16), S((m, n), bf16)), grid=(n // tn, m // tm),
        in_specs=[BS((tm, k), lambda j, i: (i, 0)), BS((None, tn, k), lambda j, i: (2 * nl + layer, j, 0)), tile, tile],
        out_specs=(tile, tile), name="ffn_down_bwd", compiler_params=_cp("parallel", "parallel"))(dh, wff, g, u)


def _lane_lo():
    return lax.broadcasted_iota(jnp.int32, (1, TILE), 1) < HEAD_DIM


def _head_norm(x, g2):
    lo = _lane_lo()
    x2 = x * x
    s_lo = jnp.sum(jnp.where(lo, x2, 0.0), axis=-1, keepdims=True)
    s_hi = jnp.sum(jnp.where(lo, 0.0, x2), axis=-1, keepdims=True)
    r = jnp.where(lo, lax.rsqrt(s_lo * (1.0 / HEAD_DIM) + EPS), lax.rsqrt(s_hi * (1.0 / HEAD_DIM) + EPS))
    xh = x * r
    return xh * g2, xh, r


def _head_norm_bwd(dy, xh, r, g2):
    lo = _lane_lo()
    dgain = jnp.sum(dy * xh, axis=0, keepdims=True)
    dxh = dy * g2
    t = dxh * xh
    t_lo = jnp.sum(jnp.where(lo, t, 0.0), axis=-1, keepdims=True)
    t_hi = jnp.sum(jnp.where(lo, 0.0, t), axis=-1, keepdims=True)
    mt = jnp.where(lo, t_lo, t_hi) * (1.0 / HEAD_DIM)
    return r * (dxh - xh * mt), dgain


def _swa_slopes():
    return [2.0 ** (-8.0 * (h + 1) / SWA_HEADS) for h in range(SWA_HEADS)]


def _swa_masks(i):
    rows = lax.broadcasted_iota(jnp.int32, (TILE, TILE), 0)
    cols = lax.broadcasted_iota(jnp.int32, (TILE, TILE), 1)
    r_pos = i * TILE + rows
    prev = jnp.maximum(i - 1, 0)
    out = []
    for c, base in enumerate((0, prev * TILE, i * TILE)):
        s_pos = base + cols
        dist = r_pos - s_pos
        if c == 0:
            ok = (i >= 1) & (s_pos >= PAD)
        elif c == 1:
            ok = (i >= 2) & (dist < TILE)
        else:
            ok = (dist >= 0) & (s_pos >= PAD)
        out.append((ok, dist.astype(f32)))
    return out, prev


def _swa_logits(qm, kn, masks, slope, sink):
    scale = HEAD_DIM ** -0.5
    logits = []
    mx = jnp.full((TILE, 1), sink, f32)
    for c in range(3):
        ok, dist = masks[c]
        lg = jnp.where(ok, _dot(qm, kn[c], NT) * scale - slope * dist, NEG)
        logits.append(lg)
        mx = jnp.maximum(mx, jnp.max(lg, axis=-1, keepdims=True))
    return logits, mx


def _swa_specs(lp, pw):
    qw, kvw = SWA_HEADS * HEAD_DIM, SWA_KV_HEADS * HEAD_DIM
    kidx = qw // kvw
    return (BS((TILE, qw), lambda i: (i, 0)), BS((lp, kvw), lambda i: (0, kidx)), BS((lp, kvw), lambda i: (0, kidx + 1)),
            BS((1, TILE), lambda i: (0, 0)))


def _swa_fwd(proj, qg2, kg2, sinks):
    lp, pw = proj.shape
    qw = SWA_HEADS * HEAD_DIM
    group = SWA_HEADS // SWA_KV_HEADS
    slopes = _swa_slopes()

    def body(sink_ref, q_ref, k_ref, v_ref, qg_ref, kg_ref, o_ref):
        i = pl.program_id(0)
        lo = _lane_lo()
        masks, prev = _swa_masks(i)
        bases = (0, pl.multiple_of(prev * TILE, TILE), pl.multiple_of(i * TILE, TILE))
        qn = [_head_norm(q_ref[:, b * TILE:(b + 1) * TILE], qg_ref[...])[0] for b in range(qw // TILE)]
        acc = [jnp.zeros((TILE, TILE), f32) for _ in range(qw // TILE)]
        for kvh in range(SWA_KV_HEADS):
            kb, khalf = kvh // 2, kvh % 2
            kcols = slice(kb * TILE, (kb + 1) * TILE)
            ksel = lo if khalf == 0 else jnp.logical_not(lo)
            kn, vm = [], []
            for base in bases:
                kn.append(_head_norm(k_ref[pl.ds(base, TILE), kcols], kg_ref[...])[0].astype(bf16))
                vm.append(jnp.where(ksel, v_ref[pl.ds(base, TILE), kcols], 0.0).astype(bf16))
            for gi in range(group):
                h = kvh * group + gi
                qb, same = h // 2, (h % 2) == khalf
                qa = qn[qb] if same else pltpu.roll(qn[qb], HEAD_DIM, 1)
                qm = jnp.where(ksel, qa, 0.0).astype(bf16)
                sink = sink_ref[h]
                logits, mx = _swa_logits(qm, kn, masks, slopes[h], sink)
                den = jnp.exp(sink - mx)
                o = jnp.zeros((TILE, TILE), f32)
                for c in range(3):
                    p = jnp.exp(logits[c] - mx)
                    den = den + jnp.sum(p, axis=-1, keepdims=True)
                    o = o + _dot(p.astype(bf16), vm[c], NN)
                o = o / den
                acc[qb] = acc[qb] + (o if same else pltpu.roll(o, HEAD_DIM, 1))
        valid = (i * TILE + lax.broadcasted_iota(jnp.int32, (TILE, 1), 0)) >= PAD
        for b in range(qw // TILE):
            o_ref[:, b * TILE:(b + 1) * TILE] = jnp.where(valid, acc[b], 0.0)

    q_spec, k_spec, v_spec, vec = _swa_specs(lp, pw)
    return pl.pallas_call(
        body, out_shape=S((lp, qw), f32), grid=(lp // TILE,),
        in_specs=[BS(memory_space=pltpu.SMEM), q_spec, k_spec, v_spec, vec, vec],
        out_specs=BS((TILE, qw), lambda i: (i, 0)), name="swa_fwd", compiler_params=_cp("arbitrary"))(
            sinks, proj, proj, proj, qg2, kg2)


def _swa_bwd(proj, dout, qg2, kg2, sinks):
    lp, pw = proj.shape
    qw, kvw = SWA_HEADS * HEAD_DIM, SWA_KV_HEADS * HEAD_DIM
    group = SWA_HEADS // SWA_KV_HEADS
    scale = HEAD_DIM ** -0.5
    slopes = _swa_slopes()
    nt = lp // TILE

    def body(sink_ref, q_ref, k_ref, v_ref, do_ref, qg_ref, kg_ref, dq_ref, dk_ref, dv_ref, dqg_ref, dkg_ref, ds_ref,
             dkn_acc, dv_acc):
        i = pl.program_id(0)
        lo = _lane_lo()
        lane = lax.broadcasted_iota(jnp.int32, (1, TILE), 1)

        @pl.when(i == 0)
        def _():
            dkn_acc[...] = jnp.zeros_like(dkn_acc)
            dv_acc[...] = jnp.zeros_like(dv_acc)
            dqg_ref[...] = jnp.zeros_like(dqg_ref)
            dkg_ref[...] = jnp.zeros_like(dkg_ref)
            ds_ref[...] = jnp.zeros_like(ds_ref)

        masks, prev = _swa_masks(i)
        bases = (0, pl.multiple_of(prev * TILE, TILE), pl.multiple_of(i * TILE, TILE))
        qnorm = [_head_norm(q_ref[:, b * TILE:(b + 1) * TILE], qg_ref[...]) for b in range(qw // TILE)]
        dqn = [jnp.zeros((TILE, TILE), f32) for _ in range(qw // TILE)]
        for kvh in range(SWA_KV_HEADS):
            kb, khalf = kvh // 2, kvh % 2
            kcols = slice(kb * TILE, (kb + 1) * TILE)
            ksel = lo if khalf == 0 else jnp.logical_not(lo)
            kn, vm = [], []
            for base in bases:
                kn.append(_head_norm(k_ref[pl.ds(base, TILE), kcols], kg_ref[...])[0].astype(bf16))
                vm.append(jnp.where(ksel, v_ref[pl.ds(base, TILE), kcols], 0.0).astype(bf16))
            for gi in range(group):
                h = kvh * group + gi
                qb, same = h // 2, (h % 2) == khalf
                qa = qnorm[qb][0] if same else pltpu.roll(qnorm[qb][0], HEAD_DIM, 1)
                qm = jnp.where(ksel, qa, 0.0).astype(bf16)
                doa = do_ref[:, qb * TILE:(qb + 1) * TILE].astype(f32)
                doa = doa if same else pltpu.roll(doa, HEAD_DIM, 1)
                dom = jnp.where(ksel, doa, 0.0).astype(bf16)
                sink = sink_ref[h]
                logits, mx = _swa_logits(qm, kn, masks, slopes[h], sink)
                ps = [jnp.exp(lg - mx) for lg in logits]
                e_sink = jnp.exp(sink - mx)
                den = e_sink
                for c in range(3):
                    den = den + jnp.sum(ps[c], axis=-1, keepdims=True)
                inv = 1.0 / den
                ps = [p * inv for p in ps]
                dps = [_dot(dom, vm[c], NT) for c in range(3)]
                dsum = jnp.zeros((TILE, 1), f32)
                for c in range(3):
                    dsum = dsum + jnp.sum(ps[c] * dps[c], axis=-1, keepdims=True)
                ds_ref[...] += jnp.where(lane == h, -jnp.sum(e_sink * inv * dsum), 0.0)
                dq_h = jnp.zeros((TILE, TILE), f32)
                for c in range(3):
                    dsc = (ps[c] * (dps[c] - dsum) * scale).astype(bf16)
                    dq_h = dq_h + _dot(dsc, kn[c], NN)
                    dkn_acc[pl.ds(bases[c], TILE), kcols] += _dot(dsc, qm, TN)
                    dv_acc[pl.ds(bases[c], TILE), kcols] += _dot(ps[c].astype(bf16), dom, TN)
                dq_h = jnp.where(ksel, dq_h, 0.0)
                dqn[qb] = dqn[qb] + (dq_h if same else pltpu.roll(dq_h, HEAD_DIM, 1))
        for b in range(qw // TILE):
            _, xh, r = qnorm[b]
            dq, dgain = _head_norm_bwd(dqn[b], xh, r, qg_ref[...])
            dq_ref[:, b * TILE:(b + 1) * TILE] = dq.astype(bf16)
            dqg_ref[...] += dgain

        @pl.when(i == nt - 1)
        def _():
            dv_ref[...] = dv_acc[...].astype(bf16)

            def tile_step(t, carry):
                base = pl.multiple_of(t * TILE, TILE)
                for kb in range(kvw // TILE):
                    kcols = slice(kb * TILE, (kb + 1) * TILE)
                    _, xh, r = _head_norm(k_ref[pl.ds(base, TILE), kcols], kg_ref[...])
                    dk, dgain = _head_norm_bwd(dkn_acc[pl.ds(base, TILE), kcols], xh, r, kg_ref[...])
                    dk_ref[pl.ds(base, TILE), kcols] = dk.astype(bf16)
                    dkg_ref[...] += dgain
                return carry

            lax.fori_loop(0, nt, tile_step, 0)

    q_spec, k_spec, v_spec, vec = _swa_specs(lp, pw)
    whole = BS((lp, kvw), lambda i: (0, 0))
    return pl.pallas_call(
        body, out_shape=(S((lp, qw), bf16), S((lp, kvw), bf16), S((lp, kvw), bf16), S((1, TILE), f32), S((1, TILE), f32), S((1, TILE), f32)),
        grid=(nt,),
        in_specs=[BS(memory_space=pltpu.SMEM), q_spec, k_spec, v_spec, BS((TILE, qw), lambda i: (i, 0)), vec, vec],
        out_specs=(BS((TILE, qw), lambda i: (i, 0)), whole, whole, vec, vec, vec),
        scratch_shapes=[pltpu.VMEM((lp, kvw), f32), pltpu.VMEM((lp, kvw), f32)],
        name="swa_bwd", compiler_params=_cp("arbitrary"))(sinks, proj, proj, proj, dout, qg2, kg2)


LOG2E = 1.4426950408889634
SB_PAIRS = 2
ROWS2 = 2 * TILE
SB_ROWS = SB_PAIRS * ROWS2


def _sb_stack(x):
    lo = _lane_lo()
    zero = jnp.zeros((TILE, TILE), x.dtype)
    parts = []
    for p in range(SB_PAIRS):
        xp = x[:, p * TILE:(p + 1) * TILE]
        parts += [jnp.where(lo, xp, zero), jnp.where(lo, zero, xp)]
    return jnp.concatenate(parts, axis=0)


def _sb_unstack(x):
    lo = _lane_lo()
    parts = [jnp.where(lo, x[p * ROWS2:p * ROWS2 + TILE], x[p * ROWS2 + TILE:(p + 1) * ROWS2]) for p in range(SB_PAIRS)]
    return parts[0] if SB_PAIRS == 1 else jnp.concatenate(parts, axis=1)


def _sb_pair_dot(a, b, dims, b_lanes):
    parts = []
    for p in range(SB_PAIRS):
        bp = b[:, p * TILE:(p + 1) * TILE] if b_lanes else b[p * ROWS2:(p + 1) * ROWS2]
        parts.append(_dot(a[p * ROWS2:(p + 1) * ROWS2], bp, dims))
    return parts


def _sb_mask(qi, kj):
    rows = lax.broadcasted_iota(jnp.int32, (SB_ROWS, TILE), 0) & (TILE - 1)
    cols = lax.broadcasted_iota(jnp.int32, (SB_ROWS, TILE), 1)
    s_pos = kj * TILE + cols
    return (s_pos < qi * TILE + rows) & (s_pos >= PAD)


def _sb_sum_matrix(after):
    rows = lax.broadcasted_iota(jnp.int32, (ROWS2, ROWS2), 0) & (TILE - 1)
    cols = lax.broadcasted_iota(jnp.int32, (ROWS2, ROWS2), 1)
    tri = (rows > cols) if after else (rows < cols)
    return (tri | (cols >= TILE)).astype(bf16)


def _sb_sums(x, w, sum_mat):
    hi = x.astype(bf16)
    lo = (x - hi.astype(f32)).astype(bf16)
    return [_dot(jnp.concatenate([hi[:, j * TILE:(j + 1) * TILE], lo[:, j * TILE:(j + 1) * TILE]], axis=1), sum_mat, NN)
            for j in range(w)]


def _sb_block(q2, k_ref, first, w, valid, c_rep, after_ones):
    kwin = k_ref[pl.ds(pl.multiple_of(first * TILE, TILE), w * TILE), :]
    z2 = jnp.concatenate(_sb_pair_dot(q2, kwin, NT, True), axis=0) * (HEAD_DIM ** -0.5 * LOG2E)
    ls2 = jnp.minimum(z2, 0.0) - jnp.log2(1.0 + jnp.exp2(jnp.minimum(z2, -z2)))
    m2 = ls2 - z2
    if valid is not None:
        m2 = jnp.where(valid, m2, 0.0)
    sums = _sb_sums(m2, w, after_ones)
    parts = [None] * w
    for j in reversed(range(w)):
        a = jnp.exp2(ls2[:, j * TILE:(j + 1) * TILE] + sums[j][:, :TILE] + c_rep)
        parts[j] = a if valid is None else jnp.where(valid, a, 0.0)
        c_rep = c_rep + sums[j][:, TILE:]
    return ls2, (parts[0] if w == 1 else jnp.concatenate(parts, axis=1)), c_rep


def _sb_walk(qi, block, state, right_to_left):
    mid = jnp.maximum(qi - 1, 0)
    n4, n2, n1 = mid // 4, (mid % 4) // 2, mid % 2
    loop = lax.fori_loop
    if right_to_left:
        state = block(qi, 1, True, state)
        state = loop(0, n4, lambda t, st: block(qi - 4 * (t + 1), 4, False, st), state)
        state = loop(0, n2, lambda t, st: block(1 + n1, 2, False, st), state)
        state = loop(0, n1, lambda t, st: block(1, 1, False, st), state)
        return loop(0, jnp.minimum(qi, 1), lambda t, st: block(0, 1, True, st), state)
    state = loop(0, jnp.minimum(qi, 1), lambda t, st: block(0, 1, True, st), state)
    state = loop(0, n1, lambda t, st: block(1, 1, False, st), state)
    state = loop(0, n2, lambda t, st: block(1 + n1, 2, False, st), state)
    state = loop(0, n4, lambda t, st: block(qi - 4 * (n4 - t), 4, False, st), state)
    return block(qi, 1, True, state)


def _sb_cols(lp, off, single=False):
    mode = dict(pipeline_mode=pl.Buffered(1)) if single else {}
    return BS((lp, SB_PAIRS * TILE), lambda g: (0, off + g), **mode)


def _sb_fwd(proj):
    lp = proj.shape[0]
    sbw = SB_HEADS * HEAD_DIM
    ngrp = sbw // (SB_PAIRS * TILE)
    nt = lp // TILE

    def body(q_ref, k_ref, v_ref, o_ref):
        after_ones = _sb_sum_matrix(True)

        def q_step(qi, carry):
            qbase = pl.multiple_of(qi * TILE, TILE)
            q2 = _sb_stack(q_ref[pl.ds(qbase, TILE), :])

            def block(first, w, masked, st):
                c_rep, acc = st
                _, a, c_rep = _sb_block(q2, k_ref, first, w, _sb_mask(qi, first) if masked else None, c_rep, after_ones)
                vwin = v_ref[pl.ds(pl.multiple_of(first * TILE, TILE), w * TILE), :]
                return c_rep, acc + jnp.concatenate(_sb_pair_dot(a.astype(bf16), vwin, NN, True), axis=0)

            zero = jnp.zeros((SB_ROWS, TILE), f32)
            _, acc = _sb_walk(qi, block, (zero, zero), True)
            o_ref[pl.ds(qbase, TILE), :] = _sb_unstack(acc)
            return carry

        lax.fori_loop(0, nt, q_step, 0)

    return pl.pallas_call(
        body, out_shape=S((lp, sbw), f32), grid=(ngrp,),
        in_specs=[_sb_cols(lp, 0), _sb_cols(lp, ngrp), _sb_cols(lp, 2 * ngrp)],
        out_specs=_sb_cols(lp, 0), name="sb_fwd", compiler_params=_cp("parallel"))(proj, proj, proj)


def _sb_bwd(proj, dout):
    lp = proj.shape[0]
    sbw = SB_HEADS * HEAD_DIM
    ngrp = sbw // (SB_PAIRS * TILE)
    nt = lp // TILE
    scale = HEAD_DIM ** -0.5

    def body(q_ref, k_ref, v_ref, do_ref, dq_ref, dk_ref, dv_ref, dk_acc, dv_acc, e_buf, b_buf):
        dk_acc[...] = jnp.zeros_like(dk_acc)
        dv_acc[...] = jnp.zeros_like(dv_acc)
        after_ones = _sb_sum_matrix(True)
        before_ones = _sb_sum_matrix(False)

        def add_cols(acc_ref, rows, parts):
            for p in range(SB_PAIRS):
                acc_ref[rows, p * TILE:(p + 1) * TILE] += parts[p]

        def q_step(qi, carry):
            qbase = pl.multiple_of(qi * TILE, TILE)
            q2 = _sb_stack(q_ref[pl.ds(qbase, TILE), :])
            do2 = _sb_stack(do_ref[pl.ds(qbase, TILE), :])

            def block_rl(first, w, masked, c_rep):
                ls2, a, c_rep = _sb_block(q2, k_ref, first, w, _sb_mask(qi, first) if masked else None, c_rep, after_ones)
                rows = pl.ds(pl.multiple_of(first * TILE, TILE), w * TILE)
                e = jnp.concatenate(_sb_pair_dot(do2, v_ref[rows, :], NT, True), axis=0) * a
                beta = jnp.exp2(ls2)
                for j in range(w):
                    e_buf[first + j] = e[:, j * TILE:(j + 1) * TILE]
                    b_buf[first + j] = beta[:, j * TILE:(j + 1) * TILE]
                add_cols(dv_acc, rows, _sb_pair_dot(a.astype(bf16), do2, TN, False))
                return c_rep

            zero = jnp.zeros((SB_ROWS, TILE), f32)
            _sb_walk(qi, block_rl, zero, True)

            def block_lr(first, w, masked, st):
                e_rep, dq_acc = st
                rows = pl.ds(pl.multiple_of(first * TILE, TILE), w * TILE)
                es = [e_buf[first + j] for j in range(w)]
                sums = _sb_sums(es[0] if w == 1 else jnp.concatenate(es, axis=1), w, before_ones)
                parts = []
                for j in range(w):
                    beta = b_buf[first + j]
                    dz = (es[j] - beta * (es[j] + e_rep + sums[j][:, :TILE])) * scale
                    parts.append(jnp.where(_sb_mask(qi, first), dz, 0.0) if masked else dz)
                    e_rep = e_rep + sums[j][:, TILE:]
                dz = (parts[0] if w == 1 else jnp.concatenate(parts, axis=1)).astype(bf16)
                add_cols(dk_acc, rows, _sb_pair_dot(dz, q2, TN, False))
                return e_rep, dq_acc + jnp.concatenate(_sb_pair_dot(dz, k_ref[rows, :], NN, True), axis=0)

            _, dq = _sb_walk(qi, block_lr, (zero, zero), False)
            dq_ref[pl.ds(qbase, TILE), :] = _sb_unstack(dq).astype(bf16)
            return carry

        lax.fori_loop(0, nt, q_step, 0)
        dk_ref[...] = dk_acc[...].astype(bf16)
        dv_ref[...] = dv_acc[...].astype(bf16)

    wide = SB_PAIRS * TILE
    return pl.pallas_call(
        body, out_shape=(S((lp, sbw), bf16),) * 3, grid=(ngrp,),
        in_specs=[_sb_cols(lp, 0, True), _sb_cols(lp, ngrp, True), _sb_cols(lp, 2 * ngrp, True), _sb_cols(lp, 0, True)],
        out_specs=(_sb_cols(lp, 0, True),) * 3,
        scratch_shapes=[pltpu.VMEM((lp, wide), f32), pltpu.VMEM((lp, wide), f32),
                        pltpu.VMEM((nt, SB_ROWS, TILE), f32), pltpu.VMEM((nt, SB_ROWS, TILE), f32)],
        name="sb_bwd", compiler_params=_cp("parallel"))(proj, proj, proj, dout)


def _place():
    x, y, c = lax.axis_index("x"), lax.axis_index("y"), lax.axis_index("c")
    chips = [(1 - x, y), (x, 1 - y), (1 - x, 1 - y)]
    return x, y, c, chips


def _two_level_gather_body(rows, x_ref, send_sems, recv_sems, local_sem):
    x, y, c, chips = _place()
    me, sibling = (x, y, c), (x, y, 1 - c)

    def copy(k, block, to, src=None):
        return pltpu.make_async_remote_copy(
            src_ref=rows(*block) if src is None else src, dst_ref=rows(*block),
            send_sem=send_sems.at[k], recv_sem=recv_sems.at[k], device_id=to, device_id_type=MESH)

    mine = pltpu.make_async_copy(x_ref, rows(*me), local_sem)
    mine.start()
    first = [copy(0, me, sibling, src=x_ref)]
    first += [copy(1 + j, me, (*chip, c), src=x_ref) for j, chip in enumerate(chips)]
    for cp in first:
        cp.start()
    passed = [copy(4 + j, (*chip, c), sibling) for j, chip in enumerate(chips)]
    for j, chip in enumerate(chips):
        copy(1 + j, (*chip, c), me).wait_recv()
        passed[j].start()
    copy(0, sibling, me).wait_recv()
    for j, chip in enumerate(chips):
        copy(4 + j, (*chip, 1 - c), me).wait_recv()
    for cp in first + passed:
        cp.wait_send()
    mine.wait()


_GATHER_SEMS = [pltpu.SemaphoreType.DMA((7,)), pltpu.SemaphoreType.DMA((7,)), pltpu.SemaphoreType.DMA]


def _ag_rows(shard, name):
    ns, r, cdim = shard.shape

    def body(x_ref, o_ref, send_sems, recv_sems, local_sem):
        def rows(px, py, pc):
            return o_ref.at[:, pl.ds(pl.multiple_of((4 * px + 2 * py + pc) * r, 16), r), :]
        _two_level_gather_body(rows, x_ref, send_sems, recv_sems, local_sem)

    return pl.pallas_call(
        body, out_shape=S((ns, NDEV * r, cdim), shard.dtype), in_specs=[ANY], out_specs=ANY,
        scratch_shapes=_GATHER_SEMS, name=name)(shard)


def _ag_small(shard, name):
    r, w = shard.shape

    def body(x_ref, o_ref, send_sems, recv_sems, local_sem):
        def rows(px, py, pc):
            return o_ref.at[pl.ds(pl.multiple_of((4 * px + 2 * py + pc) * r, 8), r), :]
        _two_level_gather_body(rows, x_ref, send_sems, recv_sems, local_sem)

    vmem = BS(memory_space=pltpu.VMEM)
    return pl.pallas_call(
        body, out_shape=S((NDEV * r, w), shard.dtype), in_specs=[vmem], out_specs=vmem,
        scratch_shapes=_GATHER_SEMS, name=name)(shard)


def _rs_swap(g, name):
    ns, rows8, cdim = g.shape
    r = rows8 // NDEV

    def body(g_ref, o_ref, send_sems, recv_sems):
        x, y, c, chips = _place()
        rel = [(x, y)] + chips
        copies = []
        for j, (px, py) in enumerate(rel):
            d = 4 * px + 2 * py + (1 - c)
            copies.append(pltpu.make_async_remote_copy(
                src_ref=g_ref.at[:, pl.ds(pl.multiple_of(d * r, 8), r), :], dst_ref=o_ref.at[j],
                send_sem=send_sems.at[j], recv_sem=recv_sems.at[j], device_id=(x, y, 1 - c), device_id_type=MESH))
        for cp in copies:
            cp.start()
        for cp in copies:
            cp.wait_recv()
        for cp in copies:
            cp.wait_send()

    return pl.pallas_call(
        body, out_shape=S((4, ns, r, cdim), f32), in_specs=[ANY], out_specs=ANY,
        scratch_shapes=[pltpu.SemaphoreType.DMA((4,)), pltpu.SemaphoreType.DMA((4,))], name=name)(g)


def _rel_ids():
    x, y, c = lax.axis_index("x"), lax.axis_index("y"), lax.axis_index("c")
    rel = [(x, y), (1 - x, y), (x, 1 - y), (1 - x, 1 - y)]
    return jnp.stack([4 * px + 2 * py + c for px, py in rel]).astype(jnp.int32)


def _rs_partial(g, sib, ids, name):
    ns, rows8, cdim = g.shape
    r = rows8 // NDEV
    tr = r // 2
    nb = r // tr

    def own_body(ids_ref, g_ref, s_ref, o_ref):
        o_ref[...] = g_ref[...] + s_ref[...]

    own = pl.pallas_call(
        own_body, out_shape=S((ns, r, cdim), f32),
        grid_spec=pltpu.PrefetchScalarGridSpec(
            num_scalar_prefetch=1, grid=(ns, nb),
            in_specs=[BS((None, tr, cdim), lambda s, t, ids_ref: (s, ids_ref[0] * nb + t, 0)),
                      BS((None, None, tr, cdim), lambda s, t, ids_ref: (0, s, t, 0))],
            out_specs=BS((None, tr, cdim), lambda s, t, ids_ref: (s, t, 0))),
        name=name + "_own", compiler_params=_cp("parallel", "parallel"))(ids, g, sib)

    def send_body(ids_ref, g_ref, s_ref, o_ref):
        o_ref[...] = (g_ref[...] + s_ref[...]).astype(bf16)

    send = pl.pallas_call(
        send_body, out_shape=S((3, ns, r, cdim), bf16),
        grid_spec=pltpu.PrefetchScalarGridSpec(
            num_scalar_prefetch=1, grid=(3, ns, nb),
            in_specs=[BS((None, tr, cdim), lambda j, s, t, ids_ref: (s, ids_ref[j + 1] * nb + t, 0)),
                      BS((None, None, tr, cdim), lambda j, s, t, ids_ref: (j + 1, s, t, 0))],
            out_specs=BS((None, None, tr, cdim), lambda j, s, t, ids_ref: (j, s, t, 0))),
        name=name + "_send", compiler_params=_cp("parallel", "parallel", "parallel"))(ids, g, sib)
    return own, send


def _rs_ici(send, name):
    def body(s_ref, o_ref, send_sems, recv_sems):
        x, y, c, chips = _place()
        copies = [pltpu.make_async_remote_copy(
            src_ref=s_ref.at[j], dst_ref=o_ref.at[j], send_sem=send_sems.at[j], recv_sem=recv_sems.at[j],
            device_id=(*chip, c), device_id_type=MESH) for j, chip in enumerate(chips)]
        for cp in copies:
            cp.start()
        for cp in copies:
            cp.wait_recv()
        for cp in copies:
            cp.wait_send()

    return pl.pallas_call(
        body, out_shape=S(send.shape, send.dtype), in_specs=[ANY], out_specs=ANY,
        scratch_shapes=[pltpu.SemaphoreType.DMA((3,)), pltpu.SemaphoreType.DMA((3,))], name=name)(send)


def _reduce_scatter(g, ids, name):
    sib = _rs_swap(g, name + "_swap")
    own, send = _rs_partial(g, sib, ids, name + "_part")
    return own, _rs_ici(send, name + "_ici")


def _prep(w, name):
    nl, r, cdim = w.shape
    blk = BS((None, r, cdim), lambda l: (l, 0, 0))

    def body(w_ref, o_ref):
        o_ref[...] = w_ref[...].astype(bf16)

    return pl.pallas_call(body, out_shape=S(w.shape, bf16), grid=(nl,), in_specs=[blk], out_specs=blk,
                          name=name, compiler_params=_cp("parallel"))(w)


def _prep_t(w, name):
    nl, d, r = w.shape
    tc = _pick(d, (256, 128))

    def body(w_ref, o_ref):
        o_ref[...] = w_ref[...].T.astype(bf16)

    return pl.pallas_call(
        body, out_shape=S((nl, r, d), bf16), grid=(nl, d // tc),
        in_specs=[BS((None, tc, r), lambda l, t: (l, t, 0))], out_specs=BS((None, r, tc), lambda l, t: (l, 0, t)),
        name=name, compiler_params=_cp("parallel", "parallel"))(w)


def _adam_math(w, g, m, v):
    m2 = ADAM_B1 * m + (1.0 - ADAM_B1) * g
    v2 = ADAM_B2 * v + (1.0 - ADAM_B2) * (g * g)
    m_hat = m2 / (1.0 - ADAM_B1 ** ADAM_STEP)
    v_hat = v2 / (1.0 - ADAM_B2 ** ADAM_STEP)
    delta = -ADAM_LR * (m_hat / (jnp.sqrt(v_hat) + ADAM_EPS) + ADAM_WD * w)
    return delta, m2, v2


def _adam_big(w, m, v, own, arrived, slot0, transposed, name):
    nl, a, b = w.shape
    if transposed:
        ta = _pick(a, (256, 128))
        w_blk = BS((None, ta, b), lambda l, t: (l, t, 0))
        own_blk = BS((None, b, ta), lambda l, t: (slot0 + l, 0, t))
        arr_blk = [BS((None, None, b, ta), lambda l, t, j=j: (j, slot0 + l, 0, t)) for j in range(3)]
    else:
        ta = _pick(a, (a // 4, a // 2, a)) if (a // 4) % 16 == 0 else a
        w_blk = BS((None, ta, b), lambda l, t: (l, t, 0))
        own_blk = BS((None, ta, b), lambda l, t: (slot0 + l, t, 0))
        arr_blk = [BS((None, None, ta, b), lambda l, t, j=j: (j, slot0 + l, t, 0)) for j in range(3)]

    def body(w_ref, m_ref, v_ref, own_ref, a0_ref, a1_ref, a2_ref, g_ref, d_ref, m2_ref, v2_ref):
        g = ((own_ref[...] + a0_ref[...].astype(f32)) + a1_ref[...].astype(f32)) + a2_ref[...].astype(f32)
        if transposed:
            g = g.T
        delta, m2, v2 = _adam_math(w_ref[...], g, m_ref[...], v_ref[...])
        g_ref[...] = g
        d_ref[...] = delta
        m2_ref[...] = m2
        v2_ref[...] = v2

    return pl.pallas_call(
        body, out_shape=(S(w.shape, f32),) * 4, grid=(nl, a // ta),
        in_specs=[w_blk, w_blk, w_blk, own_blk] + arr_blk, out_specs=(w_blk,) * 4,
        name=name, compiler_params=_cp("parallel", "parallel"))(w, m, v, own, arrived, arrived, arrived)


def _adam_small(gathered, w, m, v):
    r, wd = w.shape

    def body(g_ref, w_ref, m_ref, v_ref, gs_ref, d_ref, m2_ref, v2_ref):
        g = g_ref[0:r, :]
        for dev in range(1, NDEV):
            g = g + g_ref[dev * r:(dev + 1) * r, :]
        delta, m2, v2 = _adam_math(w_ref[...], g, m_ref[...], v_ref[...])
        gs_ref[...] = g
        d_ref[...] = delta
        m2_ref[...] = m2
        v2_ref[...] = v2

    return pl.pallas_call(body, out_shape=(S((r, wd), f32),) * 4, name="adam_small", compiler_params=_cp())(gathered, w, m, v)


def _pack(arrays, width):
    flat = jnp.concatenate([a.reshape(-1) for a in arrays])
    rows = -(-flat.shape[0] // (8 * width)) * 8
    return jnp.pad(flat, (0, rows * width - flat.shape[0])).reshape(rows, width)


def _unpack(packed, shapes):
    flat = packed.reshape(-1)
    out, off = [], 0
    for shp in shapes:
        n = 1
        for s in shp:
            n *= s
        out.append(flat[off:off + n].reshape(shp))
        off += n
    return out


def kernel(x, meta_tokens, attn_norm_g, w_in, q_norm_g, k_norm_g, attn_sinks, swa_out_g, sb_out_g, w_o, ffn_norm_g, w_gate, w_up, w_down, loss_target, m_meta_tokens, m_attn_norm_g, m_w_in, m_q_norm_g, m_k_norm_g, m_attn_sinks, m_swa_out_g, m_sb_out_g, m_w_o, m_ffn_norm_g, m_w_gate, m_w_up, m_w_down, v_meta_tokens, v_attn_norm_g, v_w_in, v_q_norm_g, v_k_norm_g, v_attn_sinks, v_swa_out_g, v_sb_out_g, v_w_o, v_ffn_norm_g, v_w_gate, v_w_up, v_w_down):
    nl, d = attn_norm_g.shape
    x2, target = x[0], loss_target[0]
    me = 4 * lax.axis_index("x") + 2 * lax.axis_index("y") + lax.axis_index("c")
    ids = _rel_ids()

    meta_all = _ag_small(meta_tokens, "ag_meta")
    meta_full = meta_all.reshape(NDEV, N_META, -1).transpose(1, 0, 2).reshape(N_META, d)
    w_in_t = _ag_rows(_prep_t(w_in, "prep_w_in"), "ag_w_in")
    w_o_f = _ag_rows(_prep(w_o, "prep_w_o"), "ag_w_o")
    ff_shard = jnp.concatenate([_prep_t(w_gate, "prep_w_gate"), _prep_t(w_up, "prep_w_up"), _prep(w_down, "prep_w_down")], axis=0)
    w_ff = _ag_rows(ff_shard, "ag_w_ff")

    qg2 = jnp.tile(q_norm_g, (1, TILE // HEAD_DIM))
    kg2 = jnp.tile(k_norm_g, (1, TILE // HEAD_DIM))
    out_g = jnp.concatenate([swa_out_g, sb_out_g], axis=1)
    swa_cols = (SWA_HEADS + 2 * SWA_KV_HEADS) * HEAD_DIM

    h = _embed(x2, meta_full)
    saved = []
    for l in range(nl):
        hn = _rmsnorm_fwd([h], attn_norm_g[l:l + 1], "attn_norm")
        proj = _mm(hn, w_in_t, l, "nt", "proj_swa", cols=(0, swa_cols))
        proj_sb = _mm(hn, w_in_t, l, "nt", "proj_sb", out_dtype=bf16, cols=(swa_cols, 3 * SB_HEADS * HEAD_DIM))
        out_a = _swa_fwd(proj, qg2[l:l + 1], kg2[l:l + 1], attn_sinks[l])
        out_b = _sb_fwd(proj_sb)
        mixed = _rmsnorm_fwd([out_a, out_b], out_g[l:l + 1], "out_norm")
        h_mid = _mm(mixed, w_o_f, l, "nn", "attn_out", res=h)
        hn2 = _rmsnorm_fwd([h_mid], ffn_norm_g[l:l + 1], "ffn_norm")
        g, u, act = _ffn_up(hn2, w_ff, l, nl)
        h_out = _mm(act, w_ff, 2 * nl + l, "nn", "ffn_down", res=h_mid)
        saved.append((h, hn, proj, proj_sb, out_a, out_b, mixed, h_mid, hn2, g, u, act))
        h = h_out

    loss_tile, dh = _loss_head(h, target)
    loss = lax.psum(loss_tile[0, 0], AXES)

    g_in = g_o = g_ff = None
    small = {k: [None] * nl for k in ("attn", "ffn", "out", "q", "k", "sink")}
    for l in reversed(range(nl)):
        h_in, hn, proj, proj_sb, out_a, out_b, mixed, h_mid, hn2, g, u, act = saved[l]
        dg, du = _ffn_down_bwd(dh, w_ff, l, nl, g, u)
        g_ff = _mm_tn(act, dh, g_ff, 2 * nl + l, 3 * nl, "grad_w_down")
        g_ff = _mm_tn(dg, hn2, g_ff, l, 3 * nl, "grad_w_gate")
        g_ff = _mm_tn(du, hn2, g_ff, nl + l, 3 * nl, "grad_w_up")
        dhn2 = _mm(du, w_ff, nl + l, "nn", "d_ffn_in_up", res=_mm(dg, w_ff, l, "nn", "d_ffn_in_gate"))
        (dh_mid,), small["ffn"][l] = _rmsnorm_bwd(dhn2, [h_mid], ffn_norm_g[l:l + 1], dh, "ffn_norm_bwd")
        dmixed = _mm(dh_mid, w_o_f, l, "nt", "d_mixed")
        g_o = _mm_tn(mixed, dh_mid, g_o, l, nl, "grad_w_o")
        (dout_a, dout_b), small["out"][l] = _rmsnorm_bwd(dmixed, [out_a, out_b], out_g[l:l + 1], None, "out_norm_bwd", bf16)
        dq_b, dk_b, dv_b = _sb_bwd(proj_sb, dout_b)
        dq_a, dk_a, dv_a, dqg, dkg, dsk = _swa_bwd(proj, dout_a, qg2[l:l + 1], kg2[l:l + 1], attn_sinks[l])
        small["q"][l] = dqg[0, :HEAD_DIM] + dqg[0, HEAD_DIM:]
        small["k"][l] = dkg[0, :HEAD_DIM] + dkg[0, HEAD_DIM:]
        small["sink"][l] = dsk[0, :SWA_HEADS]
        dproj = jnp.concatenate([dq_a, dk_a, dv_a, dq_b, dk_b, dv_b], axis=1)
        dhn = _mm(dproj, w_in_t, l, "nn", "d_attn_in")
        g_in = _mm_tn(dproj, hn, g_in, l, nl, "grad_w_in")
        (dh,), small["attn"][l] = _rmsnorm_bwd(dhn, [h_in], attn_norm_g[l:l + 1], dh_mid, "attn_norm_bwd")

    grad_x = dh[TILE:][None]
    d_meta = dh[PAD:TILE]

    qw = SWA_HEADS * HEAD_DIM
    d_out = jnp.concatenate(small["out"], axis=0)
    small_grads = [jnp.concatenate(small["attn"], axis=0), jnp.stack(small["q"]), jnp.stack(small["k"]), jnp.stack(small["sink"]),
                   d_out[:, :qw], d_out[:, qw:], jnp.concatenate(small["ffn"], axis=0), d_meta]
    col0 = me * meta_tokens.shape[1]

    def widen(a):
        return lax.dynamic_update_slice(jnp.zeros((N_META, d), f32), a, (0, col0))

    small_w = [attn_norm_g, q_norm_g, k_norm_g, attn_sinks, swa_out_g, sb_out_g, ffn_norm_g]
    small_m = [m_attn_norm_g, m_q_norm_g, m_k_norm_g, m_attn_sinks, m_swa_out_g, m_sb_out_g, m_ffn_norm_g]
    small_v = [v_attn_norm_g, v_q_norm_g, v_k_norm_g, v_attn_sinks, v_swa_out_g, v_sb_out_g, v_ffn_norm_g]
    shapes = [a.shape for a in small_w] + [(N_META, d)]
    packed = _adam_small(_ag_small(_pack(small_grads, d), "ag_small_grads"),
                         _pack(small_w + [widen(meta_tokens)], d), _pack(small_m + [widen(m_meta_tokens)], d),
                         _pack(small_v + [widen(v_meta_tokens)], d))
    small_out = []
    for p in packed:
        parts = _unpack(p, shapes)
        parts[-1] = lax.dynamic_slice(parts[-1], (0, col0), meta_tokens.shape)
        small_out.append(parts)

    own_in, arr_in = _reduce_scatter(g_in, ids, "rs_w_in")
    own_o, arr_o = _reduce_scatter(g_o, ids, "rs_w_o")
    own_ff, arr_ff = _reduce_scatter(g_ff, ids, "rs_w_ff")
    big = {
        "w_in": _adam_big(w_in, m_w_in, v_w_in, own_in, arr_in, 0, True, "adam_w_in"),
        "w_o": _adam_big(w_o, m_w_o, v_w_o, own_o, arr_o, 0, False, "adam_w_o"),
        "w_gate": _adam_big(w_gate, m_w_gate, v_w_gate, own_ff, arr_ff, 0, True, "adam_w_gate"),
        "w_up": _adam_big(w_up, m_w_up, v_w_up, own_ff, arr_ff, nl, True, "adam_w_up"),
        "w_down": _adam_big(w_down, m_w_down, v_w_down, own_ff, arr_ff, 2 * nl, False, "adam_w_down"),
    }

    def group(k):
        sm = small_out[k]
        return [sm[7], sm[0], big["w_in"][k], sm[1], sm[2], sm[3], sm[4], sm[5], big["w_o"][k], sm[6],
                big["w_gate"][k], big["w_up"][k], big["w_down"][k]]

    return (loss, grad_x, *group(0), *group(1), *group(2), *group(3))
```

```python
import jax
import jax.numpy as jnp
from jax import lax
from jax.experimental import pallas as pl
from jax.experimental.pallas import tpu as pltpu

f32, bf16 = jnp.float32, jnp.bfloat16
S = jax.ShapeDtypeStruct
BS = pl.BlockSpec

N_META = 16
TILE = 128
PAD = TILE - N_META
HEAD_DIM = 64
SWA_HEADS = 16
SWA_KV_HEADS = 4
SB_HEADS = 16
EPS = 1e-6
NEG = -1e30
ADAM_LR, ADAM_B1, ADAM_B2, ADAM_EPS, ADAM_WD, ADAM_STEP = 0.001, 0.9, 0.999, 1e-08, 0.01, 10
AXES = ("x", "y", "c")
NDEV = 8
VMEM_LIMIT_V7X = 56 * 1024 * 1024
MESH = pl.DeviceIdType.MESH
ANY = pl.BlockSpec(memory_space=pl.ANY)

NN = ((1,), (0,))
NT = ((1,), (1,))
TN = ((0,), (0,))


def _cp(*sem):
    return pltpu.CompilerParams(dimension_semantics=sem if sem else None, vmem_limit_bytes=VMEM_LIMIT_V7X)


def _pick(n, cands):
    for c in cands:
        if n % c == 0:
            return c
    raise ValueError(f"no tile for {n} in {cands}")


def _dot(a, b, dims):
    return lax.dot_general(a, b, (dims, ((), ())), preferred_element_type=f32)


def _embed(x, meta_full):
    seq, d = x.shape
    meta_pad = jnp.pad(meta_full, ((PAD, 0), (0, 0)))

    def body(x_ref, m_ref, o_ref):
        i = pl.program_id(0)

        @pl.when(i == 0)
        def _():
            o_ref[...] = m_ref[...]

        @pl.when(i > 0)
        def _():
            o_ref[...] = x_ref[...]

    return pl.pallas_call(
        body, out_shape=S((TILE + seq, d), f32), grid=(1 + seq // TILE,),
        in_specs=[BS((TILE, d), lambda i: (jnp.maximum(i - 1, 0), 0)), BS((TILE, d), lambda i: (0, 0))],
        out_specs=BS((TILE, d), lambda i: (i, 0)), name="embed", compiler_params=_cp("arbitrary"))(x, meta_pad)


def _rmsnorm_fwd(xs, gain, name):
    lp = xs[0].shape[0]
    widths = [x.shape[1] for x in xs]
    w = sum(widths)
    tr = _pick(lp, (384, 256, 128))
    n = len(xs)

    def body(*refs):
        g_ref, o_ref = refs[n], refs[n + 1]
        off = 0
        for k in range(n):
            sl = slice(off, off + widths[k])
            off += widths[k]
            xv = refs[k][...]
            r = lax.rsqrt(jnp.mean(xv * xv, axis=-1, keepdims=True) + EPS)
            o_ref[:, sl] = ((xv * r) * g_ref[:, sl]).astype(bf16)

    return pl.pallas_call(
        body, out_shape=S((lp, w), bf16), grid=(lp // tr,),
        in_specs=[BS((tr, wk), lambda i: (i, 0)) for wk in widths] + [BS((1, w), lambda i: (0, 0))],
        out_specs=BS((tr, w), lambda i: (i, 0)), name=name, compiler_params=_cp("parallel"))(*xs, gain)


def _rmsnorm_bwd(dy, xs, gain, res, name, out_dtype=f32):
    lp = xs[0].shape[0]
    widths = [x.shape[1] for x in xs]
    w = sum(widths)
    tr = _pick(lp, (384, 256, 128))
    n = len(xs)
    has_res = res is not None
    assert not has_res or n == 1

    def body(*refs):
        dy_ref, g_ref = refs[0], refs[1 + n]
        res_ref = refs[2 + n] if has_res else None
        dx_refs = refs[2 + n + has_res:2 + 2 * n + has_res]
        dg_ref = refs[-1]

        @pl.when(pl.program_id(0) == 0)
        def _():
            dg_ref[...] = jnp.zeros_like(dg_ref)

        off = 0
        for k in range(n):
            sl = slice(off, off + widths[k])
            off += widths[k]
            xv = refs[1 + k][...]
            dyv = dy_ref[:, sl]
            r = lax.rsqrt(jnp.mean(xv * xv, axis=-1, keepdims=True) + EPS)
            xh = xv * r
            dg_ref[:, sl] += jnp.sum(dyv * xh, axis=0, keepdims=True)
            dxh = dyv * g_ref[:, sl]
            dx = r * (dxh - xh * jnp.mean(dxh * xh, axis=-1, keepdims=True))
            if has_res:
                dx = dx + res_ref[...]
            dx_refs[k][...] = dx.astype(out_dtype)

    rows = [BS((tr, wk), lambda i: (i, 0)) for wk in widths]
    full = BS((tr, w), lambda i: (i, 0))
    vec = BS((1, w), lambda i: (0, 0))
    args = (dy, *xs, gain) + ((res,) if has_res else ())
    out = pl.pallas_call(
        body, out_shape=tuple(S((lp, wk), out_dtype) for wk in widths) + (S((1, w), f32),), grid=(lp // tr,),
        in_specs=[full] + rows + [vec] + ([full] if has_res else []), out_specs=tuple(rows) + (vec,),
        name=name, compiler_params=_cp("arbitrary"))(*args)
    return list(out[:n]), out[n]


def _loss_head(h, target):
    lp, d = h.shape

    def body(h_ref, t_ref, loss_ref, dh_ref):
        i = pl.program_id(0)

        @pl.when(i == 0)
        def _():
            loss_ref[...] = jnp.zeros_like(loss_ref)
            dh_ref[...] = jnp.zeros_like(dh_ref)

        @pl.when(i > 0)
        def _():
            e = h_ref[...] - t_ref[...]
            dh_ref[...] = e * (1.0 / d)
            loss_ref[...] += 0.5 * jnp.sum(jnp.sum(e * e, axis=-1, keepdims=True) * (1.0 / d))

    return pl.pallas_call(
        body, out_shape=(S((8, 128), f32), S((lp, d), f32)), grid=(lp // TILE,),
        in_specs=[BS((TILE, d), lambda i: (i, 0)), BS((TILE, d), lambda i: (jnp.maximum(i - 1, 0), 0))],
        out_specs=(BS((8, 128), lambda i: (0, 0)), BS((TILE, d), lambda i: (i, 0))),
        name="loss_head", compiler_params=_cp("arbitrary"))(h, target)


def _mm(a, b, layer, mode, name, out_dtype=f32, res=None, cols=None):
    m, k = a.shape
    n = b.shape[2] if mode == "nn" else b.shape[1]
    c0 = 0
    if cols is not None:
        assert mode == "nt"
        c0, n = cols
    tn = _pick(n, (512, 384, 256, 128))
    while c0 % tn:
        tn -= TILE
    assert n % tn == 0
    joff = c0 // tn
    tm = _pick(m, (384, 256, 128))
    dims = NN if mode == "nn" else NT
    has_res = res is not None

    def body(*refs):
        if has_res:
            a_ref, b_ref, r_ref, o_ref = refs
        else:
            a_ref, b_ref, o_ref = refs
        acc = _dot(a_ref[...].astype(bf16), b_ref[...], dims)
        if has_res:
            acc = acc + r_ref[...]
        o_ref[...] = acc.astype(out_dtype)

    if mode == "nn":
        b_spec = BS((None, k, tn), lambda j, i: (layer, 0, j))
    else:
        b_spec = BS((None, tn, k), lambda j, i: (layer, joff + j, 0))
    tile = BS((tm, tn), lambda j, i: (i, j))
    args = (a, b) + ((res,) if has_res else ())
    return pl.pallas_call(
        body, out_shape=S((m, n), out_dtype), grid=(n // tn, m // tm),
        in_specs=[BS((tm, k), lambda j, i: (i, 0)), b_spec] + ([tile] if has_res else []), out_specs=tile,
        name=name, compiler_params=_cp("parallel", "parallel"))(*args)


def _mm_tn(a, b, gbuf, slot, nslots, name):
    t, m = a.shape
    n = b.shape[1]
    tm = _pick(m, (512, 384, 256, 128))
    tk = _pick(t, (1408, 1024, 512, 384, 256, 128))
    nk = t // tk

    def body(*refs):
        a_ref, b_ref, o_ref = refs[0], refs[1], refs[-1]
        kk = pl.program_id(1)

        @pl.when(kk == 0)
        def _():
            o_ref[...] = jnp.zeros_like(o_ref)

        o_ref[...] += _dot(a_ref[...].astype(bf16), b_ref[...].astype(bf16), TN)

    in_specs = [BS((tk, tm), lambda i, kk: (kk, i)), BS((tk, n), lambda i, kk: (kk, 0))]
    args = (a, b)
    aliases = {}
    if gbuf is not None:
        in_specs.append(ANY)
        args = args + (gbuf,)
        aliases = {2: 0}
    return pl.pallas_call(
        body, out_shape=S((nslots, m, n), f32), grid=(m // tm, nk), in_specs=in_specs,
        out_specs=BS((None, tm, n), lambda i, kk: (slot, i, 0)), input_output_aliases=aliases,
        name=name, compiler_params=_cp("parallel", "arbitrary"))(*args)


def _ffn_up(hn, wg, wu):
    m, k = hn.shape
    n = wg.shape[1]
    tm = _pick(m, (384, 256, 128))
    tn = _pick(n, (512, 384, 256, 128))

    def body(a_ref, wg_ref, wu_ref, g_ref, u_ref, act_ref):
        a = a_ref[...]
        g = _dot(a, wg_ref[...], NT)
        u = _dot(a, wu_ref[...], NT)
        g_ref[...] = g
        u_ref[...] = u
        act_ref[...] = (g * jax.nn.sigmoid(g) * u).astype(bf16)

    tile = BS((tm, tn), lambda j, i: (i, j))
    return pl.pallas_call(
        body, out_shape=(S((m, n), f32), S((m, n), f32), S((m, n), bf16)), grid=(n // tn, m // tm),
        in_specs=[BS((tm, k), lambda j, i: (i, 0)), BS((None, tn, k), lambda j, i: (0, j, 0)),
                  BS((None, tn, k), lambda j, i: (0, j, 0))],
        out_specs=(tile, tile, tile), name="ffn_up", compiler_params=_cp("parallel", "parallel"))(hn, wg, wu)


def _ffn_down_bwd(dh, wd, g, u, side=None):
    m, k = dh.shape
    n = wd.shape[1]
    tm = _pick(m, (384, 256, 128))
    tn = _pick(n, (512, 384, 256, 128))
    grid = (n // tn, m // tm)

    def body(*refs):
        (a_ref, wd_ref, g_ref, u_ref, dg_ref, du_ref), side_refs = _side_split(side, refs, 4, 2, 0)
        _side_start(side, side_refs, grid)
        dact = _dot(a_ref[...].astype(bf16), wd_ref[...], NT)
        gv = g_ref[...]
        sg = jax.nn.sigmoid(gv)
        dg_ref[...] = (dact * u_ref[...] * (sg * (1.0 + gv * (1.0 - sg)))).astype(bf16)
        du_ref[...] = (dact * (gv * sg)).astype(bf16)
        _side_finish(side, side_refs, grid)

    tile = BS((tm, tn), lambda j, i: (i, j))
    s_in, s_in_specs, s_out, s_out_specs, s_scratch = _side_io(side)
    sem = ("parallel", "parallel") if side is None else ("arbitrary", "arbitrary")
    out = pl.pallas_call(
        body, out_shape=(S((m, n), bf16), S((m, n), bf16), *s_out), grid=grid,
        in_specs=[BS((tm, k), lambda j, i: (i, 0)), BS((None, tn, k), lambda j, i: (0, j, 0)), tile, tile, *s_in_specs],
        out_specs=(tile, tile, *s_out_specs), scratch_shapes=s_scratch,
        name="ffn_down_bwd" if side is None else "ffn_down_bwd_comm", compiler_params=_cp(*sem))(dh, wd, g, u, *s_in)
    return out[0], out[1], list(out[2:])


def _lane_lo():
    return lax.broadcasted_iota(jnp.int32, (1, TILE), 1) < HEAD_DIM


def _head_norm(x, g2):
    lo = _lane_lo()
    x2 = x * x
    s_lo = jnp.sum(jnp.where(lo, x2, 0.0), axis=-1, keepdims=True)
    s_hi = jnp.sum(jnp.where(lo, 0.0, x2), axis=-1, keepdims=True)
    r = jnp.where(lo, lax.rsqrt(s_lo * (1.0 / HEAD_DIM) + EPS), lax.rsqrt(s_hi * (1.0 / HEAD_DIM) + EPS))
    xh = x * r
    return xh * g2, xh, r


def _head_norm_bwd(dy, xh, r, g2):
    lo = _lane_lo()
    dgain = jnp.sum(dy * xh, axis=0, keepdims=True)
    dxh = dy * g2
    t = dxh * xh
    t_lo = jnp.sum(jnp.where(lo, t, 0.0), axis=-1, keepdims=True)
    t_hi = jnp.sum(jnp.where(lo, 0.0, t), axis=-1, keepdims=True)
    mt = jnp.where(lo, t_lo, t_hi) * (1.0 / HEAD_DIM)
    return r * (dxh - xh * mt), dgain


def _swa_slopes():
    return [2.0 ** (-8.0 * (h + 1) / SWA_HEADS) for h in range(SWA_HEADS)]


def _swa_masks(i):
    rows = lax.broadcasted_iota(jnp.int32, (TILE, TILE), 0)
    cols = lax.broadcasted_iota(jnp.int32, (TILE, TILE), 1)
    r_pos = i * TILE + rows
    prev = jnp.maximum(i - 1, 0)
    out = []
    for c, base in enumerate((0, prev * TILE, i * TILE)):
        s_pos = base + cols
        dist = r_pos - s_pos
        if c == 0:
            ok = (i >= 1) & (s_pos >= PAD)
        elif c == 1:
            ok = (i >= 2) & (dist < TILE)
        else:
            ok = (dist >= 0) & (s_pos >= PAD)
        out.append((ok, dist.astype(f32)))
    return out, prev


def _swa_logits(qm, kn, masks, slope, sink):
    scale = HEAD_DIM ** -0.5
    logits = []
    mx = jnp.full((TILE, 1), sink, f32)
    for c in range(3):
        ok, dist = masks[c]
        lg = jnp.where(ok, _dot(qm, kn[c], NT) * scale - slope * dist, NEG)
        logits.append(lg)
        mx = jnp.maximum(mx, jnp.max(lg, axis=-1, keepdims=True))
    return logits, mx


def _swa_specs(lp, pw):
    qw, kvw = SWA_HEADS * HEAD_DIM, SWA_KV_HEADS * HEAD_DIM
    kidx = qw // kvw
    return (BS((TILE, qw), lambda i: (i, 0)), BS((lp, kvw), lambda i: (0, kidx)), BS((lp, kvw), lambda i: (0, kidx + 1)),
            BS((1, TILE), lambda i: (0, 0)))


def _swa_fwd(proj, qg2, kg2, sinks):
    lp, pw = proj.shape
    qw = SWA_HEADS * HEAD_DIM
    group = SWA_HEADS // SWA_KV_HEADS
    slopes = _swa_slopes()

    def body(sink_ref, q_ref, k_ref, v_ref, qg_ref, kg_ref, o_ref):
        i = pl.program_id(0)
        lo = _lane_lo()
        masks, prev = _swa_masks(i)
        bases = (0, pl.multiple_of(prev * TILE, TILE), pl.multiple_of(i * TILE, TILE))
        qn = [_head_norm(q_ref[:, b * TILE:(b + 1) * TILE], qg_ref[...])[0] for b in range(qw // TILE)]
        acc = [jnp.zeros((TILE, TILE), f32) for _ in range(qw // TILE)]
        for kvh in range(SWA_KV_HEADS):
            kb, khalf = kvh // 2, kvh % 2
            kcols = slice(kb * TILE, (kb + 1) * TILE)
            ksel = lo if khalf == 0 else jnp.logical_not(lo)
            kn, vm = [], []
            for base in bases:
                kn.append(_head_norm(k_ref[pl.ds(base, TILE), kcols], kg_ref[...])[0].astype(bf16))
                vm.append(jnp.where(ksel, v_ref[pl.ds(base, TILE), kcols], 0.0).astype(bf16))
            for gi in range(group):
                h = kvh * group + gi
                qb, same = h // 2, (h % 2) == khalf
                qa = qn[qb] if same else pltpu.roll(qn[qb], HEAD_DIM, 1)
                qm = jnp.where(ksel, qa, 0.0).astype(bf16)
                sink = sink_ref[h]
                logits, mx = _swa_logits(qm, kn, masks, slopes[h], sink)
                den = jnp.exp(sink - mx)
                o = jnp.zeros((TILE, TILE), f32)
                for c in range(3):
                    p = jnp.exp(logits[c] - mx)
                    den = den + jnp.sum(p, axis=-1, keepdims=True)
                    o = o + _dot(p.astype(bf16), vm[c], NN)
                o = o / den
                acc[qb] = acc[qb] + (o if same else pltpu.roll(o, HEAD_DIM, 1))
        valid = (i * TILE + lax.broadcasted_iota(jnp.int32, (TILE, 1), 0)) >= PAD
        for b in range(qw // TILE):
            o_ref[:, b * TILE:(b + 1) * TILE] = jnp.where(valid, acc[b], 0.0)

    q_spec, k_spec, v_spec, vec = _swa_specs(lp, pw)
    return pl.pallas_call(
        body, out_shape=S((lp, qw), f32), grid=(lp // TILE,),
        in_specs=[BS(memory_space=pltpu.SMEM), q_spec, k_spec, v_spec, vec, vec],
        out_specs=BS((TILE, qw), lambda i: (i, 0)), name="swa_fwd", compiler_params=_cp("arbitrary"))(
            sinks, proj, proj, proj, qg2, kg2)


def _swa_bwd(proj, dout, qg2, kg2, sinks):
    lp, pw = proj.shape
    qw, kvw = SWA_HEADS * HEAD_DIM, SWA_KV_HEADS * HEAD_DIM
    group = SWA_HEADS // SWA_KV_HEADS
    scale = HEAD_DIM ** -0.5
    slopes = _swa_slopes()
    nt = lp // TILE

    def body(sink_ref, q_ref, k_ref, v_ref, do_ref, qg_ref, kg_ref, dq_ref, dk_ref, dv_ref, dqg_ref, dkg_ref, ds_ref,
             dkn_acc, dv_acc):
        i = pl.program_id(0)
        lo = _lane_lo()
        lane = lax.broadcasted_iota(jnp.int32, (1, TILE), 1)

        @pl.when(i == 0)
        def _():
            dkn_acc[...] = jnp.zeros_like(dkn_acc)
            dv_acc[...] = jnp.zeros_like(dv_acc)
            dqg_ref[...] = jnp.zeros_like(dqg_ref)
            dkg_ref[...] = jnp.zeros_like(dkg_ref)
            ds_ref[...] = jnp.zeros_like(ds_ref)

        masks, prev = _swa_masks(i)
        bases = (0, pl.multiple_of(prev * TILE, TILE), pl.multiple_of(i * TILE, TILE))
        qnorm = [_head_norm(q_ref[:, b * TILE:(b + 1) * TILE], qg_ref[...]) for b in range(qw // TILE)]
        dqn = [jnp.zeros((TILE, TILE), f32) for _ in range(qw // TILE)]
        for kvh in range(SWA_KV_HEADS):
            kb, khalf = kvh // 2, kvh % 2
            kcols = slice(kb * TILE, (kb + 1) * TILE)
            ksel = lo if khalf == 0 else jnp.logical_not(lo)
            kn, vm = [], []
            for base in bases:
                kn.append(_head_norm(k_ref[pl.ds(base, TILE), kcols], kg_ref[...])[0].astype(bf16))
                vm.append(jnp.where(ksel, v_ref[pl.ds(base, TILE), kcols], 0.0).astype(bf16))
            for gi in range(group):
                h = kvh * group + gi
                qb, same = h // 2, (h % 2) == khalf
                qa = qnorm[qb][0] if same else pltpu.roll(qnorm[qb][0], HEAD_DIM, 1)
                qm = jnp.where(ksel, qa, 0.0).astype(bf16)
                doa = do_ref[:, qb * TILE:(qb + 1) * TILE].astype(f32)
                doa = doa if same else pltpu.roll(doa, HEAD_DIM, 1)
                dom = jnp.where(ksel, doa, 0.0).astype(bf16)
                sink = sink_ref[h]
                logits, mx = _swa_logits(qm, kn, masks, slopes[h], sink)
                ps = [jnp.exp(lg - mx) for lg in logits]
                e_sink = jnp.exp(sink - mx)
                den = e_sink
                for c in range(3):
                    den = den + jnp.sum(ps[c], axis=-1, keepdims=True)
                inv = 1.0 / den
                ps = [p * inv for p in ps]
                dps = [_dot(dom, vm[c], NT) for c in range(3)]
                dsum = jnp.zeros((TILE, 1), f32)
                for c in range(3):
                    dsum = dsum + jnp.sum(ps[c] * dps[c], axis=-1, keepdims=True)
                ds_ref[...] += jnp.where(lane == h, -jnp.sum(e_sink * inv * dsum), 0.0)
                dq_h = jnp.zeros((TILE, TILE), f32)
                for c in range(3):
                    dsc = (ps[c] * (dps[c] - dsum) * scale).astype(bf16)
                    dq_h = dq_h + _dot(dsc, kn[c], NN)
                    dkn_acc[pl.ds(bases[c], TILE), kcols] += _dot(dsc, qm, TN)
                    dv_acc[pl.ds(bases[c], TILE), kcols] += _dot(ps[c].astype(bf16), dom, TN)
                dq_h = jnp.where(ksel, dq_h, 0.0)
                dqn[qb] = dqn[qb] + (dq_h if same else pltpu.roll(dq_h, HEAD_DIM, 1))
        for b in range(qw // TILE):
            _, xh, r = qnorm[b]
            dq, dgain = _head_norm_bwd(dqn[b], xh, r, qg_ref[...])
            dq_ref[:, b * TILE:(b + 1) * TILE] = dq.astype(bf16)
            dqg_ref[...] += dgain

        @pl.when(i == nt - 1)
        def _():
            dv_ref[...] = dv_acc[...].astype(bf16)

            def tile_step(t, carry):
                base = pl.multiple_of(t * TILE, TILE)
                for kb in range(kvw // TILE):
                    kcols = slice(kb * TILE, (kb + 1) * TILE)
                    _, xh, r = _head_norm(k_ref[pl.ds(base, TILE), kcols], kg_ref[...])
                    dk, dgain = _head_norm_bwd(dkn_acc[pl.ds(base, TILE), kcols], xh, r, kg_ref[...])
                    dk_ref[pl.ds(base, TILE), kcols] = dk.astype(bf16)
                    dkg_ref[...] += dgain
                return carry

            lax.fori_loop(0, nt, tile_step, 0)

    q_spec, k_spec, v_spec, vec = _swa_specs(lp, pw)
    whole = BS((lp, kvw), lambda i: (0, 0))
    return pl.pallas_call(
        body, out_shape=(S((lp, qw), bf16), S((lp, kvw), bf16), S((lp, kvw), bf16), S((1, TILE), f32), S((1, TILE), f32), S((1, TILE), f32)),
        grid=(nt,),
        in_specs=[BS(memory_space=pltpu.SMEM), q_spec, k_spec, v_spec, BS((TILE, qw), lambda i: (i, 0)), vec, vec],
        out_specs=(BS((TILE, qw), lambda i: (i, 0)), whole, whole, vec, vec, vec),
        scratch_shapes=[pltpu.VMEM((lp, kvw), f32), pltpu.VMEM((lp, kvw), f32)],
        name="swa_bwd", compiler_params=_cp("arbitrary"))(sinks, proj, proj, proj, dout, qg2, kg2)


LOG2E = 1.4426950408889634
SB_PAIRS = 2
ROWS2 = 2 * TILE
SB_ROWS = SB_PAIRS * ROWS2


def _sb_stack(x):
    lo = _lane_lo()
    zero = jnp.zeros((TILE, TILE), x.dtype)
    parts = []
    for p in range(SB_PAIRS):
        xp = x[:, p * TILE:(p + 1) * TILE]
        parts += [jnp.where(lo, xp, zero), jnp.where(lo, zero, xp)]
    return jnp.concatenate(parts, axis=0)


def _sb_unstack(x):
    lo = _lane_lo()
    parts = [jnp.where(lo, x[p * ROWS2:p * ROWS2 + TILE], x[p * ROWS2 + TILE:(p + 1) * ROWS2]) for p in range(SB_PAIRS)]
    return parts[0] if SB_PAIRS == 1 else jnp.concatenate(parts, axis=1)


def _sb_pair_dot(a, b, dims, b_lanes):
    parts = []
    for p in range(SB_PAIRS):
        bp = b[:, p * TILE:(p + 1) * TILE] if b_lanes else b[p * ROWS2:(p + 1) * ROWS2]
        parts.append(_dot(a[p * ROWS2:(p + 1) * ROWS2], bp, dims))
    return parts


def _sb_mask(qi, kj):
    rows = lax.broadcasted_iota(jnp.int32, (SB_ROWS, TILE), 0) & (TILE - 1)
    cols = lax.broadcasted_iota(jnp.int32, (SB_ROWS, TILE), 1)
    s_pos = kj * TILE + cols
    return (s_pos < qi * TILE + rows) & (s_pos >= PAD)


def _sb_sum_matrix(after):
    rows = lax.broadcasted_iota(jnp.int32, (ROWS2, ROWS2), 0) & (TILE - 1)
    cols = lax.broadcasted_iota(jnp.int32, (ROWS2, ROWS2), 1)
    tri = (rows > cols) if after else (rows < cols)
    return (tri | (cols >= TILE)).astype(bf16)


def _sb_sums(x, w, sum_mat):
    hi = x.astype(bf16)
    lo = (x - hi.astype(f32)).astype(bf16)
    return [_dot(jnp.concatenate([hi[:, j * TILE:(j + 1) * TILE], lo[:, j * TILE:(j + 1) * TILE]], axis=1), sum_mat, NN)
            for j in range(w)]


def _sb_block(q2, k_ref, first, w, valid, c_rep, after_ones):
    kwin = k_ref[pl.ds(pl.multiple_of(first * TILE, TILE), w * TILE), :]
    z2 = jnp.concatenate(_sb_pair_dot(q2, kwin, NT, True), axis=0) * (HEAD_DIM ** -0.5 * LOG2E)
    ls2 = jnp.minimum(z2, 0.0) - jnp.log2(1.0 + jnp.exp2(jnp.minimum(z2, -z2)))
    m2 = ls2 - z2
    if valid is not None:
        m2 = jnp.where(valid, m2, 0.0)
    sums = _sb_sums(m2, w, after_ones)
    parts = [None] * w
    for j in reversed(range(w)):
        a = jnp.exp2(ls2[:, j * TILE:(j + 1) * TILE] + sums[j][:, :TILE] + c_rep)
        parts[j] = a if valid is None else jnp.where(valid, a, 0.0)
        c_rep = c_rep + sums[j][:, TILE:]
    return ls2, (parts[0] if w == 1 else jnp.concatenate(parts, axis=1)), c_rep


def _sb_walk(qi, block, state, right_to_left):
    mid = jnp.maximum(qi - 1, 0)
    n4, n2, n1 = mid // 4, (mid % 4) // 2, mid % 2
    loop = lax.fori_loop
    if right_to_left:
        state = block(qi, 1, True, state)
        state = loop(0, n4, lambda t, st: block(qi - 4 * (t + 1), 4, False, st), state)
        state = loop(0, n2, lambda t, st: block(1 + n1, 2, False, st), state)
        state = loop(0, n1, lambda t, st: block(1, 1, False, st), state)
        return loop(0, jnp.minimum(qi, 1), lambda t, st: block(0, 1, True, st), state)
    state = loop(0, jnp.minimum(qi, 1), lambda t, st: block(0, 1, True, st), state)
    state = loop(0, n1, lambda t, st: block(1, 1, False, st), state)
    state = loop(0, n2, lambda t, st: block(1 + n1, 2, False, st), state)
    state = loop(0, n4, lambda t, st: block(qi - 4 * (n4 - t), 4, False, st), state)
    return block(qi, 1, True, state)


def _sb_cols(lp, off, single=False):
    mode = dict(pipeline_mode=pl.Buffered(1)) if single else {}
    return BS((lp, SB_PAIRS * TILE), lambda g: (0, off + g), **mode)


def _sb_fwd(proj, side=None):
    lp = proj.shape[0]
    sbw = SB_HEADS * HEAD_DIM
    ngrp = sbw // (SB_PAIRS * TILE)
    nt = lp // TILE

    def body(*refs):
        (q_ref, k_ref, v_ref, o_ref), side_refs = _side_split(side, refs, 3, 1, 0)
        _side_start(side, side_refs, (ngrp,))
        after_ones = _sb_sum_matrix(True)

        def q_step(qi, carry):
            qbase = pl.multiple_of(qi * TILE, TILE)
            q2 = _sb_stack(q_ref[pl.ds(qbase, TILE), :])

            def block(first, w, masked, st):
                c_rep, acc = st
                _, a, c_rep = _sb_block(q2, k_ref, first, w, _sb_mask(qi, first) if masked else None, c_rep, after_ones)
                vwin = v_ref[pl.ds(pl.multiple_of(first * TILE, TILE), w * TILE), :]
                return c_rep, acc + jnp.concatenate(_sb_pair_dot(a.astype(bf16), vwin, NN, True), axis=0)

            zero = jnp.zeros((SB_ROWS, TILE), f32)
            _, acc = _sb_walk(qi, block, (zero, zero), True)
            o_ref[pl.ds(qbase, TILE), :] = _sb_unstack(acc)
            return carry

        lax.fori_loop(0, nt, q_step, 0)
        _side_finish(side, side_refs, (ngrp,))

    s_in, s_in_specs, s_out, s_out_specs, s_scratch = _side_io(side)
    out = pl.pallas_call(
        body, out_shape=(S((lp, sbw), f32), *s_out), grid=(ngrp,),
        in_specs=[_sb_cols(lp, 0), _sb_cols(lp, ngrp), _sb_cols(lp, 2 * ngrp), *s_in_specs],
        out_specs=(_sb_cols(lp, 0), *s_out_specs), scratch_shapes=s_scratch,
        name="sb_fwd" if side is None else "sb_fwd_comm",
        compiler_params=_cp("parallel" if side is None else "arbitrary"))(proj, proj, proj, *s_in)
    return out[0], list(out[1:])


def _sb_bwd(proj, dout, side=None):
    lp = proj.shape[0]
    sbw = SB_HEADS * HEAD_DIM
    ngrp = sbw // (SB_PAIRS * TILE)
    nt = lp // TILE
    scale = HEAD_DIM ** -0.5

    def body(*refs):
        (q_ref, k_ref, v_ref, do_ref, dq_ref, dk_ref, dv_ref, dk_acc, dv_acc, e_buf, b_buf), side_refs = _side_split(side, refs, 4, 3, 4)
        _side_start(side, side_refs, (ngrp,))
        dk_acc[...] = jnp.zeros_like(dk_acc)
        dv_acc[...] = jnp.zeros_like(dv_acc)
        after_ones = _sb_sum_matrix(True)
        before_ones = _sb_sum_matrix(False)

        def add_cols(acc_ref, rows, parts):
            for p in range(SB_PAIRS):
                acc_ref[rows, p * TILE:(p + 1) * TILE] += parts[p]

        def q_step(qi, carry):
            qbase = pl.multiple_of(qi * TILE, TILE)
            q2 = _sb_stack(q_ref[pl.ds(qbase, TILE), :])
            do2 = _sb_stack(do_ref[pl.ds(qbase, TILE), :])

            def block_rl(first, w, masked, c_rep):
                ls2, a, c_rep = _sb_block(q2, k_ref, first, w, _sb_mask(qi, first) if masked else None, c_rep, after_ones)
                rows = pl.ds(pl.multiple_of(first * TILE, TILE), w * TILE)
                e = jnp.concatenate(_sb_pair_dot(do2, v_ref[rows, :], NT, True), axis=0) * a
                beta = jnp.exp2(ls2)
                for j in range(w):
                    e_buf[first + j] = e[:, j * TILE:(j + 1) * TILE]
                    b_buf[first + j] = beta[:, j * TILE:(j + 1) * TILE]
                add_cols(dv_acc, rows, _sb_pair_dot(a.astype(bf16), do2, TN, False))
                return c_rep

            zero = jnp.zeros((SB_ROWS, TILE), f32)
            _sb_walk(qi, block_rl, zero, True)

            def block_lr(first, w, masked, st):
                e_rep, dq_acc = st
                rows = pl.ds(pl.multiple_of(first * TILE, TILE), w * TILE)
                es = [e_buf[first + j] for j in range(w)]
                sums = _sb_sums(es[0] if w == 1 else jnp.concatenate(es, axis=1), w, before_ones)
                parts = []
                for j in range(w):
                    beta = b_buf[first + j]
                    dz = (es[j] - beta * (es[j] + e_rep + sums[j][:, :TILE])) * scale
                    parts.append(jnp.where(_sb_mask(qi, first), dz, 0.0) if masked else dz)
                    e_rep = e_rep + sums[j][:, TILE:]
                dz = (parts[0] if w == 1 else jnp.concatenate(parts, axis=1)).astype(bf16)
                add_cols(dk_acc, rows, _sb_pair_dot(dz, q2, TN, False))
                return e_rep, dq_acc + jnp.concatenate(_sb_pair_dot(dz, k_ref[rows, :], NN, True), axis=0)

            _, dq = _sb_walk(qi, block_lr, (zero, zero), False)
            dq_ref[pl.ds(qbase, TILE), :] = _sb_unstack(dq).astype(bf16)
            return carry

        lax.fori_loop(0, nt, q_step, 0)
        dk_ref[...] = dk_acc[...].astype(bf16)
        dv_ref[...] = dv_acc[...].astype(bf16)
        _side_finish(side, side_refs, (ngrp,))

    wide = SB_PAIRS * TILE
    s_in, s_in_specs, s_out, s_out_specs, s_scratch = _side_io(side)
    out = pl.pallas_call(
        body, out_shape=(S((lp, sbw), bf16),) * 3 + tuple(s_out), grid=(ngrp,),
        in_specs=[_sb_cols(lp, 0, True), _sb_cols(lp, ngrp, True), _sb_cols(lp, 2 * ngrp, True), _sb_cols(lp, 0, True), *s_in_specs],
        out_specs=(_sb_cols(lp, 0, True),) * 3 + tuple(s_out_specs),
        scratch_shapes=[pltpu.VMEM((lp, wide), f32), pltpu.VMEM((lp, wide), f32),
                        pltpu.VMEM((nt, SB_ROWS, TILE), f32), pltpu.VMEM((nt, SB_ROWS, TILE), f32), *s_scratch],
        name="sb_bwd" if side is None else "sb_bwd_comm",
        compiler_params=_cp("parallel" if side is None else "arbitrary"))(proj, proj, proj, dout, *s_in)
    return out[0], out[1], out[2], list(out[3:])


def _place():
    x, y, c = lax.axis_index("x"), lax.axis_index("y"), lax.axis_index("c")
    chips = [(1 - x, y), (x, 1 - y), (1 - x, 1 - y)]
    return x, y, c, chips


def _two_level_gather_body(rows, x_ref, send_sems, recv_sems, local_sem):
    x, y, c, chips = _place()
    me, sibling = (x, y, c), (x, y, 1 - c)

    def copy(k, block, to, src=None):
        return pltpu.make_async_remote_copy(
            src_ref=rows(*block) if src is None else src, dst_ref=rows(*block),
            send_sem=send_sems.at[k], recv_sem=recv_sems.at[k], device_id=to, device_id_type=MESH)

    mine = pltpu.make_async_copy(x_ref, rows(*me), local_sem)
    mine.start()
    first = [copy(0, me, sibling, src=x_ref)]
    first += [copy(1 + j, me, (*chip, c), src=x_ref) for j, chip in enumerate(chips)]
    for cp in first:
        cp.start()
    passed = [copy(4 + j, (*chip, c), sibling) for j, chip in enumerate(chips)]
    for j, chip in enumerate(chips):
        copy(1 + j, (*chip, c), me).wait_recv()
        passed[j].start()
    copy(0, sibling, me).wait_recv()
    for j, chip in enumerate(chips):
        copy(4 + j, (*chip, 1 - c), me).wait_recv()
    for cp in first + passed:
        cp.wait_send()
    mine.wait()


_GATHER_SEMS = [pltpu.SemaphoreType.DMA((7,)), pltpu.SemaphoreType.DMA((7,)), pltpu.SemaphoreType.DMA]


class _Side:
    def __init__(self, inputs, out_shapes, nsem, start, finish):
        self.inputs, self.out_shapes, self.start, self.finish = list(inputs), list(out_shapes), start, finish
        self.scratch = [pltpu.SemaphoreType.DMA((nsem,)), pltpu.SemaphoreType.DMA((nsem,)), pltpu.SemaphoreType.DMA((len(inputs),))]


def _side_io(side):
    if side is None:
        return [], [], [], [], []
    return side.inputs, [ANY] * len(side.inputs), side.out_shapes, [ANY] * len(side.out_shapes), side.scratch


def _side_split(side, refs, n_in, n_out, n_scratch):
    if side is None:
        return refs, None
    si, so = len(side.inputs), len(side.out_shapes)
    a, b, c, d = n_in, n_in + si, n_in + si + n_out, n_in + si + n_out + so
    return refs[:a] + refs[b:c] + refs[d:d + n_scratch], (refs[a:b], refs[c:d], refs[d + n_scratch:])


def _side_first_last(grid):
    first = last = None
    for k, n in enumerate(grid):
        i = pl.program_id(k)
        first = (i == 0) if first is None else first & (i == 0)
        last = (i == n - 1) if last is None else last & (i == n - 1)
    return first, last


def _side_start(side, side_refs, grid):
    if side is not None:
        pl.when(_side_first_last(grid)[0])(lambda: side.start(*side_refs))


def _side_finish(side, side_refs, grid):
    if side is not None:
        pl.when(_side_first_last(grid)[1])(lambda: side.finish(*side_refs))


def _run_side(side, name):
    def body(*refs):
        _, side_refs = _side_split(side, refs, 0, 0, 0)
        side.start(*side_refs)
        side.finish(*side_refs)

    inputs, in_specs, out_shapes, out_specs, scratch = _side_io(side)
    return pl.pallas_call(body, out_shape=tuple(out_shapes), in_specs=in_specs, out_specs=tuple(out_specs),
                          scratch_shapes=scratch, name=name)(*inputs)


def _gather_side(shards, layer):
    def plan(in_refs, out_refs, sems, starting):
        send_sems, recv_sems, local_sems = sems
        x, y, c, chips = _place()
        me, sibling = (x, y, c), (x, y, 1 - c)
        jobs = []
        for a, (x_ref, o_ref) in enumerate(zip(in_refs, out_refs)):
            r = x_ref.shape[1]
            src = x_ref.at[layer]

            def rows(px, py, pc, o_ref=o_ref, r=r):
                return o_ref.at[0, pl.ds(pl.multiple_of((4 * px + 2 * py + pc) * r, 16), r), :]

            def copy(k, block, to, from_shard=False, a=a, rows=rows, src=src):
                return pltpu.make_async_remote_copy(
                    src_ref=src if from_shard else rows(*block), dst_ref=rows(*block),
                    send_sem=send_sems.at[7 * a + k], recv_sem=recv_sems.at[7 * a + k], device_id=to, device_id_type=MESH)

            job = dict(
                mine=pltpu.make_async_copy(src, rows(*me), local_sems.at[a]),
                first=[copy(0, me, sibling, True)] + [copy(1 + j, me, (*chip, c), True) for j, chip in enumerate(chips)])
            if not starting:
                job.update(
                    passed=[copy(4 + j, (*chip, c), sibling) for j, chip in enumerate(chips)],
                    from_chips=[copy(1 + j, (*chip, c), me) for j, chip in enumerate(chips)],
                    from_sibling=[copy(0, sibling, me)] + [copy(4 + j, (*chip, 1 - c), me) for j, chip in enumerate(chips)])
            jobs.append(job)
        return jobs

    def start(*refs):
        for job in plan(*refs, starting=True):
            job["mine"].start()
            for cp in job["first"]:
                cp.start()

    def finish(*refs):
        jobs = plan(*refs, starting=False)
        for job in jobs:
            for arrived, onward in zip(job["from_chips"], job["passed"]):
                arrived.wait_recv()
                onward.start()
        for job in jobs:
            for cp in job["from_sibling"]:
                cp.wait_recv()
            for cp in job["first"] + job["passed"]:
                cp.wait_send()
            job["mine"].wait()

    shapes = [S((1, NDEV * s.shape[1], s.shape[2]), s.dtype) for s in shards]
    return _Side(shards, shapes, 7 * len(shards), start, finish)


def _swap_side(gs):
    def plan(in_refs, out_refs, sems):
        send_sems, recv_sems, _ = sems
        x, y, c, chips = _place()
        copies = []
        for a, (g_ref, o_ref) in enumerate(zip(in_refs, out_refs)):
            r = g_ref.shape[1] // NDEV
            for j, (px, py) in enumerate([(x, y)] + chips):
                d = 4 * px + 2 * py + (1 - c)
                copies.append(pltpu.make_async_remote_copy(
                    src_ref=g_ref.at[:, pl.ds(pl.multiple_of(d * r, 8), r), :], dst_ref=o_ref.at[j],
                    send_sem=send_sems.at[4 * a + j], recv_sem=recv_sems.at[4 * a + j], device_id=(x, y, 1 - c), device_id_type=MESH))
        return copies

    return _exchange_side(gs, [S((4, g.shape[0], g.shape[1] // NDEV, g.shape[2]), f32) for g in gs], 4 * len(gs), plan)


def _ici_side(sends):
    def plan(in_refs, out_refs, sems):
        send_sems, recv_sems, _ = sems
        x, y, c, chips = _place()
        return [pltpu.make_async_remote_copy(
            src_ref=s_ref.at[j], dst_ref=o_ref.at[j], send_sem=send_sems.at[3 * a + j], recv_sem=recv_sems.at[3 * a + j],
            device_id=(*chip, c), device_id_type=MESH)
            for a, (s_ref, o_ref) in enumerate(zip(in_refs, out_refs)) for j, chip in enumerate(chips)]

    return _exchange_side(sends, [S(s.shape, s.dtype) for s in sends], 3 * len(sends), plan)


def _exchange_side(inputs, shapes, nsem, plan):
    def start(*refs):
        for cp in plan(*refs):
            cp.start()

    def finish(*refs):
        copies = plan(*refs)
        for cp in copies:
            cp.wait_recv()
        for cp in copies:
            cp.wait_send()

    return _Side(inputs, shapes, nsem, start, finish)


def _ag_small(shard, name):
    r, w = shard.shape

    def body(x_ref, o_ref, send_sems, recv_sems, local_sem):
        def rows(px, py, pc):
            return o_ref.at[pl.ds(pl.multiple_of((4 * px + 2 * py + pc) * r, 8), r), :]
        _two_level_gather_body(rows, x_ref, send_sems, recv_sems, local_sem)

    vmem = BS(memory_space=pltpu.VMEM)
    return pl.pallas_call(
        body, out_shape=S((NDEV * r, w), shard.dtype), in_specs=[vmem], out_specs=vmem,
        scratch_shapes=_GATHER_SEMS, name=name)(shard)


def _rel_ids():
    x, y, c = lax.axis_index("x"), lax.axis_index("y"), lax.axis_index("c")
    rel = [(x, y), (1 - x, y), (x, 1 - y), (1 - x, 1 - y)]
    return jnp.stack([4 * px + 2 * py + c for px, py in rel]).astype(jnp.int32)


def _rs_partial(g, sib, ids, name):
    ns, rows8, cdim = g.shape
    r = rows8 // NDEV
    tr = r // 2
    nb = r // tr

    def own_body(ids_ref, g_ref, s_ref, o_ref):
        o_ref[...] = g_ref[...] + s_ref[...]

    own = pl.pallas_call(
        own_body, out_shape=S((ns, r, cdim), f32),
        grid_spec=pltpu.PrefetchScalarGridSpec(
            num_scalar_prefetch=1, grid=(ns, nb),
            in_specs=[BS((None, tr, cdim), lambda s, t, ids_ref: (s, ids_ref[0] * nb + t, 0)),
                      BS((None, None, tr, cdim), lambda s, t, ids_ref: (0, s, t, 0))],
            out_specs=BS((None, tr, cdim), lambda s, t, ids_ref: (s, t, 0))),
        name=name + "_own", compiler_params=_cp("parallel", "parallel"))(ids, g, sib)

    def send_body(ids_ref, g_ref, s_ref, o_ref):
        o_ref[...] = (g_ref[...] + s_ref[...]).astype(bf16)

    send = pl.pallas_call(
        send_body, out_shape=S((3, ns, r, cdim), bf16),
        grid_spec=pltpu.PrefetchScalarGridSpec(
            num_scalar_prefetch=1, grid=(3, ns, nb),
            in_specs=[BS((None, tr, cdim), lambda j, s, t, ids_ref: (s, ids_ref[j + 1] * nb + t, 0)),
                      BS((None, None, tr, cdim), lambda j, s, t, ids_ref: (j + 1, s, t, 0))],
            out_specs=BS((None, None, tr, cdim), lambda j, s, t, ids_ref: (j, s, t, 0))),
        name=name + "_send", compiler_params=_cp("parallel", "parallel", "parallel"))(ids, g, sib)
    return own, send


def _prep(w, name):
    nl, r, cdim = w.shape
    blk = BS((None, r, cdim), lambda l: (l, 0, 0))

    def body(w_ref, o_ref):
        o_ref[...] = w_ref[...].astype(bf16)

    return pl.pallas_call(body, out_shape=S(w.shape, bf16), grid=(nl,), in_specs=[blk], out_specs=blk,
                          name=name, compiler_params=_cp("parallel"))(w)


def _prep_t(w, name):
    nl, d, r = w.shape
    tc = _pick(d, (256, 128))

    def body(w_ref, o_ref):
        o_ref[...] = w_ref[...].T.astype(bf16)

    return pl.pallas_call(
        body, out_shape=S((nl, r, d), bf16), grid=(nl, d // tc),
        in_specs=[BS((None, tc, r), lambda l, t: (l, t, 0))], out_specs=BS((None, r, tc), lambda l, t: (l, 0, t)),
        name=name, compiler_params=_cp("parallel", "parallel"))(w)


def _adam_math(w, g, m, v):
    m2 = ADAM_B1 * m + (1.0 - ADAM_B1) * g
    v2 = ADAM_B2 * v + (1.0 - ADAM_B2) * (g * g)
    m_hat = m2 / (1.0 - ADAM_B1 ** ADAM_STEP)
    v_hat = v2 / (1.0 - ADAM_B2 ** ADAM_STEP)
    delta = -ADAM_LR * (m_hat / (jnp.sqrt(v_hat) + ADAM_EPS) + ADAM_WD * w)
    return delta, m2, v2


def _adam_big(w, m, v, own, arrived, layer, slot, transposed, name, prev=None):
    nl, a, b = w.shape
    if transposed:
        ta = _pick(a, (256, 128))
        w_blk = BS((None, ta, b), lambda t: (layer, t, 0))
        own_blk = BS((None, b, ta), lambda t: (slot, 0, t))
        arr_blk = [BS((None, None, b, ta), lambda t, j=j: (j, slot, 0, t)) for j in range(3)]
    else:
        ta = _pick(a, (a // 4, a // 2, a)) if (a // 4) % 16 == 0 else a
        w_blk = BS((None, ta, b), lambda t: (layer, t, 0))
        own_blk = BS((None, ta, b), lambda t: (slot, t, 0))
        arr_blk = [BS((None, None, ta, b), lambda t, j=j: (j, slot, t, 0)) for j in range(3)]

    def body(*refs):
        w_ref, m_ref, v_ref, own_ref, a0_ref, a1_ref, a2_ref = refs[:7]
        g_ref, d_ref, m2_ref, v2_ref = refs[-4:]
        g = ((own_ref[...] + a0_ref[...].astype(f32)) + a1_ref[...].astype(f32)) + a2_ref[...].astype(f32)
        if transposed:
            g = g.T
        delta, m2, v2 = _adam_math(w_ref[...], g, m_ref[...], v_ref[...])
        g_ref[...] = g
        d_ref[...] = delta
        m2_ref[...] = m2
        v2_ref[...] = v2

    threaded = () if prev is None else tuple(prev)
    return pl.pallas_call(
        body, out_shape=(S(w.shape, f32),) * 4, grid=(a // ta,),
        in_specs=[w_blk, w_blk, w_blk, own_blk] + arr_blk + [ANY] * len(threaded), out_specs=(w_blk,) * 4,
        input_output_aliases={7 + k: k for k in range(len(threaded))},
        name=name, compiler_params=_cp("parallel"))(w, m, v, own, arrived, arrived, arrived, *threaded)


def _adam_small(gathered, w, m, v):
    r, wd = w.shape

    def body(g_ref, w_ref, m_ref, v_ref, gs_ref, d_ref, m2_ref, v2_ref):
        g = g_ref[0:r, :]
        for dev in range(1, NDEV):
            g = g + g_ref[dev * r:(dev + 1) * r, :]
        delta, m2, v2 = _adam_math(w_ref[...], g, m_ref[...], v_ref[...])
        gs_ref[...] = g
        d_ref[...] = delta
        m2_ref[...] = m2
        v2_ref[...] = v2

    return pl.pallas_call(body, out_shape=(S((r, wd), f32),) * 4, name="adam_small", compiler_params=_cp())(gathered, w, m, v)


def _pack(arrays, width):
    flat = jnp.concatenate([a.reshape(-1) for a in arrays])
    rows = -(-flat.shape[0] // (8 * width)) * 8
    return jnp.pad(flat, (0, rows * width - flat.shape[0])).reshape(rows, width)


def _unpack(packed, shapes):
    flat = packed.reshape(-1)
    out, off = [], 0
    for shp in shapes:
        n = 1
        for s in shp:
            n *= s
        out.append(flat[off:off + n].reshape(shp))
        off += n
    return out


def kernel(x, meta_tokens, attn_norm_g, w_in, q_norm_g, k_norm_g, attn_sinks, swa_out_g, sb_out_g, w_o, ffn_norm_g, w_gate, w_up, w_down, loss_target, m_meta_tokens, m_attn_norm_g, m_w_in, m_q_norm_g, m_k_norm_g, m_attn_sinks, m_swa_out_g, m_sb_out_g, m_w_o, m_ffn_norm_g, m_w_gate, m_w_up, m_w_down, v_meta_tokens, v_attn_norm_g, v_w_in, v_q_norm_g, v_k_norm_g, v_attn_sinks, v_swa_out_g, v_sb_out_g, v_w_o, v_ffn_norm_g, v_w_gate, v_w_up, v_w_down):
    nl, d = attn_norm_g.shape
    x2, target = x[0], loss_target[0]
    me = 4 * lax.axis_index("x") + 2 * lax.axis_index("y") + lax.axis_index("c")
    ids = _rel_ids()

    meta_all = _ag_small(meta_tokens, "ag_meta")
    meta_full = meta_all.reshape(NDEV, N_META, -1).transpose(1, 0, 2).reshape(N_META, d)
    shards = [_prep_t(w_in, "prep_w_in"), _prep(w_o, "prep_w_o"), _prep_t(w_gate, "prep_w_gate"), _prep_t(w_up, "prep_w_up"),
              _prep(w_down, "prep_w_down")]
    weights = [_run_side(_gather_side(shards, 0), "ag_layer0")] + [None] * (nl - 1)

    qg2 = jnp.tile(q_norm_g, (1, TILE // HEAD_DIM))
    kg2 = jnp.tile(k_norm_g, (1, TILE // HEAD_DIM))
    out_g = jnp.concatenate([swa_out_g, sb_out_g], axis=1)
    swa_cols = (SWA_HEADS + 2 * SWA_KV_HEADS) * HEAD_DIM

    h = _embed(x2, meta_full)
    saved = []
    for l in range(nl):
        w_in_t, w_o_f, w_gate_t, w_up_t, w_down_f = weights[l]
        hn = _rmsnorm_fwd([h], attn_norm_g[l:l + 1], "attn_norm")
        proj = _mm(hn, w_in_t, 0, "nt", "proj_swa", cols=(0, swa_cols))
        proj_sb = _mm(hn, w_in_t, 0, "nt", "proj_sb", out_dtype=bf16, cols=(swa_cols, 3 * SB_HEADS * HEAD_DIM))
        out_a = _swa_fwd(proj, qg2[l:l + 1], kg2[l:l + 1], attn_sinks[l])
        if l + 1 < nl:
            out_b, weights[l + 1] = _sb_fwd(proj_sb, _gather_side(shards, l + 1))
        else:
            out_b, _ = _sb_fwd(proj_sb)
        mixed = _rmsnorm_fwd([out_a, out_b], out_g[l:l + 1], "out_norm")
        h_mid = _mm(mixed, w_o_f, 0, "nn", "attn_out", res=h)
        hn2 = _rmsnorm_fwd([h_mid], ffn_norm_g[l:l + 1], "ffn_norm")
        g, u, act = _ffn_up(hn2, w_gate_t, w_up_t)
        h_out = _mm(act, w_down_f, 0, "nn", "ffn_down", res=h_mid)
        saved.append((h, hn, proj, proj_sb, out_a, out_b, mixed, h_mid, hn2, g, u, act))
        h = h_out

    loss_tile, dh = _loss_head(h, target)
    loss = lax.psum(loss_tile[0, 0], AXES)

    small = {k: [None] * nl for k in ("attn", "ffn", "out", "q", "k", "sink")}
    pending = None
    reduced = [None] * nl

    def partial_sums(slabs, sibs):
        return [_rs_partial(gg, sib, ids, name) for gg, sib, name in zip(slabs, sibs, ("rs_part_w_in", "rs_part_w_o", "rs_part_w_ff"))]

    for l in reversed(range(nl)):
        w_in_t, w_o_f, w_gate_t, w_up_t, w_down_f = weights[l]
        h_in, hn, proj, proj_sb, out_a, out_b, mixed, h_mid, hn2, g, u, act = saved[l]
        dg, du, sibs = _ffn_down_bwd(dh, w_down_f, g, u, None if pending is None else _swap_side(pending))
        parts = None if pending is None else partial_sums(pending, sibs)
        g_ff = _mm_tn(act, dh, None, 2, 3, "grad_w_down")
        g_ff = _mm_tn(dg, hn2, g_ff, 0, 3, "grad_w_gate")
        g_ff = _mm_tn(du, hn2, g_ff, 1, 3, "grad_w_up")
        dhn2 = _mm(du, w_up_t, 0, "nn", "d_ffn_in_up", res=_mm(dg, w_gate_t, 0, "nn", "d_ffn_in_gate"))
        (dh_mid,), small["ffn"][l] = _rmsnorm_bwd(dhn2, [h_mid], ffn_norm_g[l:l + 1], dh, "ffn_norm_bwd")
        dmixed = _mm(dh_mid, w_o_f, 0, "nt", "d_mixed")
        g_o = _mm_tn(mixed, dh_mid, None, 0, 1, "grad_w_o")
        (dout_a, dout_b), small["out"][l] = _rmsnorm_bwd(dmixed, [out_a, out_b], out_g[l:l + 1], None, "out_norm_bwd", bf16)
        dq_b, dk_b, dv_b, arrived = _sb_bwd(proj_sb, dout_b, None if parts is None else _ici_side([p[1] for p in parts]))
        if parts is not None:
            reduced[l + 1] = ([p[0] for p in parts], arrived)
        dq_a, dk_a, dv_a, dqg, dkg, dsk = _swa_bwd(proj, dout_a, qg2[l:l + 1], kg2[l:l + 1], attn_sinks[l])
        small["q"][l] = dqg[0, :HEAD_DIM] + dqg[0, HEAD_DIM:]
        small["k"][l] = dkg[0, :HEAD_DIM] + dkg[0, HEAD_DIM:]
        small["sink"][l] = dsk[0, :SWA_HEADS]
        dproj = jnp.concatenate([dq_a, dk_a, dv_a, dq_b, dk_b, dv_b], axis=1)
        dhn = _mm(dproj, w_in_t, 0, "nn", "d_attn_in")
        g_in = _mm_tn(dproj, hn, None, 0, 1, "grad_w_in")
        (dh,), small["attn"][l] = _rmsnorm_bwd(dhn, [h_in], attn_norm_g[l:l + 1], dh_mid, "attn_norm_bwd")
        pending = [g_in, g_o, g_ff]

    parts = partial_sums(pending, _run_side(_swap_side(pending), "rs_swap_layer0"))
    reduced[0] = ([p[0] for p in parts], list(_run_side(_ici_side([p[1] for p in parts]), "rs_ici_layer0")))

    grad_x = dh[TILE:][None]
    d_meta = dh[PAD:TILE]

    qw = SWA_HEADS * HEAD_DIM
    d_out = jnp.concatenate(small["out"], axis=0)
    small_grads = [jnp.concatenate(small["attn"], axis=0), jnp.stack(small["q"]), jnp.stack(small["k"]), jnp.stack(small["sink"]),
                   d_out[:, :qw], d_out[:, qw:], jnp.concatenate(small["ffn"], axis=0), d_meta]
    col0 = me * meta_tokens.shape[1]

    def widen(a):
        return lax.dynamic_update_slice(jnp.zeros((N_META, d), f32), a, (0, col0))

    small_w = [attn_norm_g, q_norm_g, k_norm_g, attn_sinks, swa_out_g, sb_out_g, ffn_norm_g]
    small_m = [m_attn_norm_g, m_q_norm_g, m_k_norm_g, m_attn_sinks, m_swa_out_g, m_sb_out_g, m_ffn_norm_g]
    small_v = [v_attn_norm_g, v_q_norm_g, v_k_norm_g, v_attn_sinks, v_swa_out_g, v_sb_out_g, v_ffn_norm_g]
    shapes = [a.shape for a in small_w] + [(N_META, d)]
    packed = _adam_small(_ag_small(_pack(small_grads, d), "ag_small_grads"),
                         _pack(small_w + [widen(meta_tokens)], d), _pack(small_m + [widen(m_meta_tokens)], d),
                         _pack(small_v + [widen(v_meta_tokens)], d))
    small_out = []
    for p in packed:
        parts = _unpack(p, shapes)
        parts[-1] = lax.dynamic_slice(parts[-1], (0, col0), meta_tokens.shape)
        small_out.append(parts)

    big = dict.fromkeys(("w_in", "w_o", "w_gate", "w_up", "w_down"))
    for l in range(nl):
        (own_in, own_o, own_ff), (arr_in, arr_o, arr_ff) = reduced[l]
        big["w_in"] = _adam_big(w_in, m_w_in, v_w_in, own_in, arr_in, l, 0, True, "adam_w_in", big["w_in"])
        big["w_o"] = _adam_big(w_o, m_w_o, v_w_o, own_o, arr_o, l, 0, False, "adam_w_o", big["w_o"])
        big["w_gate"] = _adam_big(w_gate, m_w_gate, v_w_gate, own_ff, arr_ff, l, 0, True, "adam_w_gate", big["w_gate"])
        big["w_up"] = _adam_big(w_up, m_w_up, v_w_up, own_ff, arr_ff, l, 1, True, "adam_w_up", big["w_up"])
        big["w_down"] = _adam_big(w_down, m_w_down, v_w_down, own_ff, arr_ff, l, 2, False, "adam_w_down", big["w_down"])

    def group(k):
        sm = small_out[k]
        return [sm[7], sm[0], big["w_in"][k], sm[1], sm[2], sm[3], sm[4], sm[5], big["w_o"][k], sm[6],
                big["w_gate"][k], big["w_up"][k], big["w_down"][k]]

    return (loss, grad_x, *group(0), *group(1), *group(2), *group(3))
```

```python
import jax
import jax.numpy as jnp
from jax import lax
from jax.experimental import pallas as pl
from jax.experimental.pallas import tpu as pltpu

f32, bf16 = jnp.float32, jnp.bfloat16
S = jax.ShapeDtypeStruct
BS = pl.BlockSpec

N_META = 16
TILE = 128
PAD = TILE - N_META
HEAD_DIM = 64
SWA_HEADS = 16
SWA_KV_HEADS = 4
SB_HEADS = 16
EPS = 1e-6
NEG = -1e30
ADAM_LR, ADAM_B1, ADAM_B2, ADAM_EPS, ADAM_WD, ADAM_STEP = 0.001, 0.9, 0.999, 1e-08, 0.01, 10
AXES = ("x", "y", "c")
NDEV = 8
VMEM_LIMIT_V7X = 56 * 1024 * 1024
MM_RESIDENT_K = 2048
MESH = pl.DeviceIdType.MESH
ANY = pl.BlockSpec(memory_space=pl.ANY)

NN = ((1,), (0,))
NT = ((1,), (1,))
TN = ((0,), (0,))


def _cp(*sem):
    return pltpu.CompilerParams(dimension_semantics=sem if sem else None, vmem_limit_bytes=VMEM_LIMIT_V7X)


def _pick(n, cands):
    for c in cands:
        if n % c == 0:
            return c
    raise ValueError(f"no tile for {n} in {cands}")


def _dot(a, b, dims):
    return lax.dot_general(a, b, (dims, ((), ())), preferred_element_type=f32)


def _embed(x, meta_full):
    seq, d = x.shape
    meta_pad = jnp.pad(meta_full, ((PAD, 0), (0, 0)))

    def body(x_ref, m_ref, o_ref):
        i = pl.program_id(0)

        @pl.when(i == 0)
        def _():
            o_ref[...] = m_ref[...]

        @pl.when(i > 0)
        def _():
            o_ref[...] = x_ref[...]

    return pl.pallas_call(
        body, out_shape=S((TILE + seq, d), f32), grid=(1 + seq // TILE,),
        in_specs=[BS((TILE, d), lambda i: (jnp.maximum(i - 1, 0), 0)), BS((TILE, d), lambda i: (0, 0))],
        out_specs=BS((TILE, d), lambda i: (i, 0)), name="embed", compiler_params=_cp("arbitrary"))(x, meta_pad)


def _rmsnorm_fwd(xs, gain, name):
    lp = xs[0].shape[0]
    widths = [x.shape[1] for x in xs]
    w = sum(widths)
    tr = _pick(lp, (384, 256, 128))
    n = len(xs)

    def body(*refs):
        g_ref, o_ref = refs[n], refs[n + 1]
        off = 0
        for k in range(n):
            sl = slice(off, off + widths[k])
            off += widths[k]
            xv = refs[k][...]
            r = lax.rsqrt(jnp.mean(xv * xv, axis=-1, keepdims=True) + EPS)
            o_ref[:, sl] = ((xv * r) * g_ref[:, sl]).astype(bf16)

    return pl.pallas_call(
        body, out_shape=S((lp, w), bf16), grid=(lp // tr,),
        in_specs=[BS((tr, wk), lambda i: (i, 0)) for wk in widths] + [BS((1, w), lambda i: (0, 0))],
        out_specs=BS((tr, w), lambda i: (i, 0)), name=name, compiler_params=_cp("parallel"))(*xs, gain)


def _rmsnorm_bwd(dy, xs, gain, res, name, out_dtype=f32):
    lp = xs[0].shape[0]
    widths = [x.shape[1] for x in xs]
    w = sum(widths)
    tr = _pick(lp, (384, 256, 128))
    n = len(xs)
    has_res = res is not None
    assert not has_res or n == 1

    def body(*refs):
        dy_ref, g_ref = refs[0], refs[1 + n]
        res_ref = refs[2 + n] if has_res else None
        dx_refs = refs[2 + n + has_res:2 + 2 * n + has_res]
        dg_ref = refs[-1]

        @pl.when(pl.program_id(0) == 0)
        def _():
            dg_ref[...] = jnp.zeros_like(dg_ref)

        off = 0
        for k in range(n):
            sl = slice(off, off + widths[k])
            off += widths[k]
            xv = refs[1 + k][...]
            dyv = dy_ref[:, sl]
            r = lax.rsqrt(jnp.mean(xv * xv, axis=-1, keepdims=True) + EPS)
            xh = xv * r
            dg_ref[:, sl] += jnp.sum(dyv * xh, axis=0, keepdims=True)
            dxh = dyv * g_ref[:, sl]
            dx = r * (dxh - xh * jnp.mean(dxh * xh, axis=-1, keepdims=True))
            if has_res:
                dx = dx + res_ref[...]
            dx_refs[k][...] = dx.astype(out_dtype)

    rows = [BS((tr, wk), lambda i: (i, 0)) for wk in widths]
    full = BS((tr, w), lambda i: (i, 0))
    vec = BS((1, w), lambda i: (0, 0))
    args = (dy, *xs, gain) + ((res,) if has_res else ())
    out = pl.pallas_call(
        body, out_shape=tuple(S((lp, wk), out_dtype) for wk in widths) + (S((1, w), f32),), grid=(lp // tr,),
        in_specs=[full] + rows + [vec] + ([full] if has_res else []), out_specs=tuple(rows) + (vec,),
        name=name, compiler_params=_cp("arbitrary"))(*args)
    return list(out[:n]), out[n]


def _loss_head(h, target):
    lp, d = h.shape

    def body(h_ref, t_ref, loss_ref, dh_ref):
        i = pl.program_id(0)

        @pl.when(i == 0)
        def _():
            loss_ref[...] = jnp.zeros_like(loss_ref)
            dh_ref[...] = jnp.zeros_like(dh_ref)

        @pl.when(i > 0)
        def _():
            e = h_ref[...] - t_ref[...]
            dh_ref[...] = e * (1.0 / d)
            loss_ref[...] += 0.5 * jnp.sum(jnp.sum(e * e, axis=-1, keepdims=True) * (1.0 / d))

    return pl.pallas_call(
        body, out_shape=(S((8, 128), f32), S((lp, d), f32)), grid=(lp // TILE,),
        in_specs=[BS((TILE, d), lambda i: (i, 0)), BS((TILE, d), lambda i: (jnp.maximum(i - 1, 0), 0))],
        out_specs=(BS((8, 128), lambda i: (0, 0)), BS((TILE, d), lambda i: (i, 0))),
        name="loss_head", compiler_params=_cp("arbitrary"))(h, target)


def _mm(a, b, layer, mode, name, out_dtype=f32, res=None, cols=None):
    m, k = a.shape
    n = b.shape[2] if mode == "nn" else b.shape[1]
    c0 = 0
    if cols is not None:
        assert mode == "nt"
        c0, n = cols
    tn = _pick(n, (512, 384, 256, 128))
    while c0 % tn:
        tn -= TILE
    assert n % tn == 0
    joff = c0 // tn
    rows_outer = k <= MM_RESIDENT_K
    tm = _pick(m, (1408, 384, 256, 128) if rows_outer else (384, 256, 128))
    dims = NN if mode == "nn" else NT
    has_res = res is not None

    def body(*refs):
        if has_res:
            a_ref, b_ref, r_ref, o_ref = refs
        else:
            a_ref, b_ref, o_ref = refs
        acc = _dot(a_ref[...].astype(bf16), b_ref[...], dims)
        if has_res:
            acc = acc + r_ref[...]
        o_ref[...] = acc.astype(out_dtype)

    def ij(f):
        return (lambda i, j: f(i, j)) if rows_outer else (lambda j, i: f(i, j))

    if mode == "nn":
        b_spec = BS((None, k, tn), ij(lambda i, j: (layer, 0, j)))
    else:
        b_spec = BS((None, tn, k), ij(lambda i, j: (layer, joff + j, 0)))
    tile = BS((tm, tn), ij(lambda i, j: (i, j)))
    args = (a, b) + ((res,) if has_res else ())
    return pl.pallas_call(
        body, out_shape=S((m, n), out_dtype), grid=(m // tm, n // tn) if rows_outer else (n // tn, m // tm),
        in_specs=[BS((tm, k), ij(lambda i, j: (i, 0))), b_spec] + ([tile] if has_res else []), out_specs=tile,
        name=name, compiler_params=_cp("parallel", "parallel"))(*args)


def _mm_tn(a, b, gbuf, slot, nslots, name):
    t, m = a.shape
    n = b.shape[1]
    tm = _pick(m, (512, 384, 256, 128))
    tk = _pick(t, (1408, 1024, 512, 384, 256, 128))
    nk = t // tk

    def body(*refs):
        a_ref, b_ref, o_ref = refs[0], refs[1], refs[-1]
        kk = pl.program_id(1)

        @pl.when(kk == 0)
        def _():
            o_ref[...] = jnp.zeros_like(o_ref)

        o_ref[...] += _dot(a_ref[...].astype(bf16), b_ref[...].astype(bf16), TN)

    in_specs = [BS((tk, tm), lambda i, kk: (kk, i)), BS((tk, n), lambda i, kk: (kk, 0))]
    args = (a, b)
    aliases = {}
    if gbuf is not None:
        in_specs.append(ANY)
        args = args + (gbuf,)
        aliases = {2: 0}
    return pl.pallas_call(
        body, out_shape=S((nslots, m, n), f32), grid=(m // tm, nk), in_specs=in_specs,
        out_specs=BS((None, tm, n), lambda i, kk: (slot, i, 0)), input_output_aliases=aliases,
        name=name, compiler_params=_cp("parallel", "arbitrary"))(*args)


def _ffn_up(hn, wg, wu):
    m, k = hn.shape
    n = wg.shape[1]
    tm = _pick(m, (1408, 384, 256, 128))
    tn = _pick(n, (512, 384, 256, 128))

    def body(a_ref, wg_ref, wu_ref, g_ref, u_ref, act_ref):
        a = a_ref[...]
        g = _dot(a, wg_ref[...], NT)
        u = _dot(a, wu_ref[...], NT)
        g_ref[...] = g
        u_ref[...] = u
        act_ref[...] = (g * jax.nn.sigmoid(g) * u).astype(bf16)

    tile = BS((tm, tn), lambda i, j: (i, j))
    return pl.pallas_call(
        body, out_shape=(S((m, n), f32), S((m, n), f32), S((m, n), bf16)), grid=(m // tm, n // tn),
        in_specs=[BS((tm, k), lambda i, j: (i, 0)), BS((None, tn, k), lambda i, j: (0, j, 0)),
                  BS((None, tn, k), lambda i, j: (0, j, 0))],
        out_specs=(tile, tile, tile), name="ffn_up", compiler_params=_cp("parallel", "parallel"))(hn, wg, wu)


def _ffn_down_bwd(dh, wd, g, u, side=None):
    m, k = dh.shape
    n = wd.shape[1]
    tm = _pick(m, (1408, 384, 256, 128))
    tn = _pick(n, (512, 384, 256, 128))
    grid = (m // tm, n // tn)

    def body(*refs):
        (a_ref, wd_ref, g_ref, u_ref, dg_ref, du_ref), side_refs = _side_split(side, refs, 4, 2, 0)
        _side_start(side, side_refs, grid)
        dact = _dot(a_ref[...].astype(bf16), wd_ref[...], NT)
        gv = g_ref[...]
        sg = jax.nn.sigmoid(gv)
        dg_ref[...] = (dact * u_ref[...] * (sg * (1.0 + gv * (1.0 - sg)))).astype(bf16)
        du_ref[...] = (dact * (gv * sg)).astype(bf16)
        _side_finish(side, side_refs, grid)

    tile = BS((tm, tn), lambda i, j: (i, j))
    s_in, s_in_specs, s_out, s_out_specs, s_scratch = _side_io(side)
    sem = ("parallel", "parallel") if side is None else ("arbitrary", "arbitrary")
    out = pl.pallas_call(
        body, out_shape=(S((m, n), bf16), S((m, n), bf16), *s_out), grid=grid,
        in_specs=[BS((tm, k), lambda i, j: (i, 0)), BS((None, tn, k), lambda i, j: (0, j, 0)), tile, tile, *s_in_specs],
        out_specs=(tile, tile, *s_out_specs), scratch_shapes=s_scratch,
        name="ffn_down_bwd" if side is None else "ffn_down_bwd_comm", compiler_params=_cp(*sem))(dh, wd, g, u, *s_in)
    return out[0], out[1], list(out[2:])


def _lane_lo():
    return lax.broadcasted_iota(jnp.int32, (1, TILE), 1) < HEAD_DIM


def _head_norm(x, g2):
    lo = _lane_lo()
    x2 = x * x
    s_lo = jnp.sum(jnp.where(lo, x2, 0.0), axis=-1, keepdims=True)
    s_hi = jnp.sum(jnp.where(lo, 0.0, x2), axis=-1, keepdims=True)
    r = jnp.where(lo, lax.rsqrt(s_lo * (1.0 / HEAD_DIM) + EPS), lax.rsqrt(s_hi * (1.0 / HEAD_DIM) + EPS))
    xh = x * r
    return xh * g2, xh, r


def _head_norm_bwd(dy, xh, r, g2):
    lo = _lane_lo()
    dgain = jnp.sum(dy * xh, axis=0, keepdims=True)
    dxh = dy * g2
    t = dxh * xh
    t_lo = jnp.sum(jnp.where(lo, t, 0.0), axis=-1, keepdims=True)
    t_hi = jnp.sum(jnp.where(lo, 0.0, t), axis=-1, keepdims=True)
    mt = jnp.where(lo, t_lo, t_hi) * (1.0 / HEAD_DIM)
    return r * (dxh - xh * mt), dgain


def _swa_slopes():
    return [2.0 ** (-8.0 * (h + 1) / SWA_HEADS) for h in range(SWA_HEADS)]


def _swa_masks(i):
    rows = lax.broadcasted_iota(jnp.int32, (TILE, TILE), 0)
    cols = lax.broadcasted_iota(jnp.int32, (TILE, TILE), 1)
    r_pos = i * TILE + rows
    prev = jnp.maximum(i - 1, 0)
    out = []
    for c, base in enumerate((0, prev * TILE, i * TILE)):
        s_pos = base + cols
        dist = r_pos - s_pos
        if c == 0:
            ok = (i >= 1) & (s_pos >= PAD)
        elif c == 1:
            ok = (i >= 2) & (dist < TILE)
        else:
            ok = (dist >= 0) & (s_pos >= PAD)
        out.append((ok, dist.astype(f32)))
    return out, prev


def _swa_geometry(i, group):
    masks, prev = _swa_masks(i)
    bases = (0, pl.multiple_of(prev * TILE, TILE), pl.multiple_of(i * TILE, TILE))
    ok = jnp.concatenate([m[0] for m in masks], axis=1)
    dist = jnp.concatenate([m[1] for m in masks], axis=1)
    return bases, jnp.concatenate([ok] * group, axis=0), jnp.concatenate([dist] * group, axis=0)


def _swa_col(values):
    return jnp.concatenate([jnp.full((TILE, 1), v, f32) for v in values], axis=0)


def _swa_keys(k_ref, v_ref, kg_ref, bases, kvh):
    kcols = slice((kvh // 2) * TILE, (kvh // 2 + 1) * TILE)
    ksel = _lane_lo() if kvh % 2 == 0 else jnp.logical_not(_lane_lo())
    kcat = jnp.concatenate([_head_norm(k_ref[pl.ds(b, TILE), kcols], kg_ref[...])[0] for b in bases], axis=0).astype(bf16)
    vcat = jnp.concatenate([jnp.where(ksel, v_ref[pl.ds(b, TILE), kcols], 0.0) for b in bases], axis=0).astype(bf16)
    return kcat, vcat, ksel


def _swa_stack(blocks, kvh, group, ksel):
    parts = []
    for gi in range(group):
        same = ((kvh * group + gi) % 2) == (kvh % 2)
        parts.append(jnp.where(ksel, blocks[gi] if same else pltpu.roll(blocks[gi], HEAD_DIM, 1), 0.0))
    return jnp.concatenate(parts, axis=0)


def _swa_unstack_add(acc, stacked, kvh, group):
    for gi in range(group):
        h = kvh * group + gi
        part = stacked[gi * TILE:(gi + 1) * TILE]
        acc[h // 2] = acc[h // 2] + (part if (h % 2) == (kvh % 2) else pltpu.roll(part, HEAD_DIM, 1))


def _swa_probs(q4, kcat, ok_g, dist_g, slope_col, sink_col):
    logits = _dot(q4, kcat, NT) * (HEAD_DIM ** -0.5) + jnp.where(ok_g, -slope_col * dist_g, NEG)
    mx = jnp.maximum(jnp.max(logits, axis=-1, keepdims=True), sink_col)
    return jnp.exp(logits - mx), jnp.exp(sink_col - mx)


def _swa_specs(lp, pw):
    qw, kvw = SWA_HEADS * HEAD_DIM, SWA_KV_HEADS * HEAD_DIM
    kidx = qw // kvw
    return (BS((TILE, qw), lambda i: (i, 0)), BS((lp, kvw), lambda i: (0, kidx)), BS((lp, kvw), lambda i: (0, kidx + 1)),
            BS((1, TILE), lambda i: (0, 0)))


def _swa_fwd(proj, qg2, kg2, sinks):
    lp, pw = proj.shape
    qw = SWA_HEADS * HEAD_DIM
    group = SWA_HEADS // SWA_KV_HEADS
    slopes = _swa_slopes()

    def body(sink_ref, q_ref, k_ref, v_ref, qg_ref, kg_ref, o_ref):
        i = pl.program_id(0)
        bases, ok_g, dist_g = _swa_geometry(i, group)
        qn = [_head_norm(q_ref[:, b * TILE:(b + 1) * TILE], qg_ref[...])[0] for b in range(qw // TILE)]
        acc = [jnp.zeros((TILE, TILE), f32) for _ in range(qw // TILE)]
        for kvh in range(SWA_KV_HEADS):
            kcat, vcat, ksel = _swa_keys(k_ref, v_ref, kg_ref, bases, kvh)
            q4 = _swa_stack([qn[(kvh * group + gi) // 2] for gi in range(group)], kvh, group, ksel).astype(bf16)
            sink_col = _swa_col([sink_ref[kvh * group + gi] for gi in range(group)])
            p, e_sink = _swa_probs(q4, kcat, ok_g, dist_g, _swa_col(slopes[kvh * group:(kvh + 1) * group]), sink_col)
            den = e_sink + jnp.sum(p, axis=-1, keepdims=True)
            o = _dot(p.astype(bf16), vcat, NN) / den
            _swa_unstack_add(acc, o, kvh, group)
        valid = (i * TILE + lax.broadcasted_iota(jnp.int32, (TILE, 1), 0)) >= PAD
        for b in range(qw // TILE):
            o_ref[:, b * TILE:(b + 1) * TILE] = jnp.where(valid, acc[b], 0.0)

    q_spec, k_spec, v_spec, vec = _swa_specs(lp, pw)
    return pl.pallas_call(
        body, out_shape=S((lp, qw), f32), grid=(lp // TILE,),
        in_specs=[BS(memory_space=pltpu.SMEM), q_spec, k_spec, v_spec, vec, vec],
        out_specs=BS((TILE, qw), lambda i: (i, 0)), name="swa_fwd", compiler_params=_cp("arbitrary"))(
            sinks, proj, proj, proj, qg2, kg2)


def _swa_bwd(proj, dout, qg2, kg2, sinks):
    lp, pw = proj.shape
    qw, kvw = SWA_HEADS * HEAD_DIM, SWA_KV_HEADS * HEAD_DIM
    group = SWA_HEADS // SWA_KV_HEADS
    scale = HEAD_DIM ** -0.5
    slopes = _swa_slopes()
    nt = lp // TILE

    def body(sink_ref, q_ref, k_ref, v_ref, do_ref, qg_ref, kg_ref, dq_ref, dk_ref, dv_ref, dqg_ref, dkg_ref, ds_ref,
             dkn_acc, dv_acc):
        i = pl.program_id(0)
        lo = _lane_lo()
        lane = lax.broadcasted_iota(jnp.int32, (1, TILE), 1)

        @pl.when(i == 0)
        def _():
            dkn_acc[...] = jnp.zeros_like(dkn_acc)
            dv_acc[...] = jnp.zeros_like(dv_acc)
            dqg_ref[...] = jnp.zeros_like(dqg_ref)
            dkg_ref[...] = jnp.zeros_like(dkg_ref)
            ds_ref[...] = jnp.zeros_like(ds_ref)

        bases, ok_g, dist_g = _swa_geometry(i, group)
        qnorm = [_head_norm(q_ref[:, b * TILE:(b + 1) * TILE], qg_ref[...]) for b in range(qw // TILE)]
        dqn = [jnp.zeros((TILE, TILE), f32) for _ in range(qw // TILE)]
        for kvh in range(SWA_KV_HEADS):
            kcols = slice((kvh // 2) * TILE, (kvh // 2 + 1) * TILE)
            heads = [kvh * group + gi for gi in range(group)]
            kcat, vcat, ksel = _swa_keys(k_ref, v_ref, kg_ref, bases, kvh)
            q4 = _swa_stack([qnorm[h // 2][0] for h in heads], kvh, group, ksel).astype(bf16)
            do4 = _swa_stack([do_ref[:, (h // 2) * TILE:(h // 2 + 1) * TILE].astype(f32) for h in heads], kvh, group, ksel).astype(bf16)
            sink_col = _swa_col([sink_ref[h] for h in heads])
            p, e_sink = _swa_probs(q4, kcat, ok_g, dist_g, _swa_col([slopes[h] for h in heads]), sink_col)
            inv = 1.0 / (e_sink + jnp.sum(p, axis=-1, keepdims=True))
            p = p * inv
            dp = _dot(do4, vcat, NT)
            dsum = jnp.sum(p * dp, axis=-1, keepdims=True)
            d_sink = e_sink * inv * dsum
            for gi, h in enumerate(heads):
                ds_ref[...] += jnp.where(lane == h, -jnp.sum(d_sink[gi * TILE:(gi + 1) * TILE]), 0.0)
            dsc = (p * (dp - dsum) * scale).astype(bf16)
            _swa_unstack_add(dqn, jnp.where(ksel, _dot(dsc, kcat, NN), 0.0), kvh, group)
            dkn = _dot(dsc, q4, TN)
            dvv = _dot(p.astype(bf16), do4, TN)
            for c in range(3):
                dkn_acc[pl.ds(bases[c], TILE), kcols] += dkn[c * TILE:(c + 1) * TILE]
                dv_acc[pl.ds(bases[c], TILE), kcols] += dvv[c * TILE:(c + 1) * TILE]
        for b in range(qw // TILE):
            _, xh, r = qnorm[b]
            dq, dgain = _head_norm_bwd(dqn[b], xh, r, qg_ref[...])
            dq_ref[:, b * TILE:(b + 1) * TILE] = dq.astype(bf16)
            dqg_ref[...] += dgain

        @pl.when(i == nt - 1)
        def _():
            dv_ref[...] = dv_acc[...].astype(bf16)

            def tile_step(t, carry):
                base = pl.multiple_of(t * TILE, TILE)
                for kb in range(kvw // TILE):
                    kcols = slice(kb * TILE, (kb + 1) * TILE)
                    _, xh, r = _head_norm(k_ref[pl.ds(base, TILE), kcols], kg_ref[...])
                    dk, dgain = _head_norm_bwd(dkn_acc[pl.ds(base, TILE), kcols], xh, r, kg_ref[...])
                    dk_ref[pl.ds(base, TILE), kcols] = dk.astype(bf16)
                    dkg_ref[...] += dgain
                return carry

            lax.fori_loop(0, nt, tile_step, 0)

    q_spec, k_spec, v_spec, vec = _swa_specs(lp, pw)
    whole = BS((lp, kvw), lambda i: (0, 0))
    return pl.pallas_call(
        body, out_shape=(S((lp, qw), bf16), S((lp, kvw), bf16), S((lp, kvw), bf16), S((1, TILE), f32), S((1, TILE), f32), S((1, TILE), f32)),
        grid=(nt,),
        in_specs=[BS(memory_space=pltpu.SMEM), q_spec, k_spec, v_spec, BS((TILE, qw), lambda i: (i, 0)), vec, vec],
        out_specs=(BS((TILE, qw), lambda i: (i, 0)), whole, whole, vec, vec, vec),
        scratch_shapes=[pltpu.VMEM((lp, kvw), f32), pltpu.VMEM((lp, kvw), f32)],
        name="swa_bwd", compiler_params=_cp("arbitrary"))(sinks, proj, proj, proj, dout, qg2, kg2)


LOG2E = 1.4426950408889634
SB_PAIRS = 2
ROWS2 = 2 * TILE
SB_ROWS = SB_PAIRS * ROWS2


def _sb_stack(x):
    lo = _lane_lo()
    zero = jnp.zeros((TILE, TILE), x.dtype)
    parts = []
    for p in range(SB_PAIRS):
        xp = x[:, p * TILE:(p + 1) * TILE]
        parts += [jnp.where(lo, xp, zero), jnp.where(lo, zero, xp)]
    return jnp.concatenate(parts, axis=0)


def _sb_unstack(x):
    lo = _lane_lo()
    parts = [jnp.where(lo, x[p * ROWS2:p * ROWS2 + TILE], x[p * ROWS2 + TILE:(p + 1) * ROWS2]) for p in range(SB_PAIRS)]
    return parts[0] if SB_PAIRS == 1 else jnp.concatenate(parts, axis=1)


def _sb_pair_dot(a, b, dims, b_lanes):
    parts = []
    for p in range(SB_PAIRS):
        bp = b[:, p * TILE:(p + 1) * TILE] if b_lanes else b[p * ROWS2:(p + 1) * ROWS2]
        parts.append(_dot(a[p * ROWS2:(p + 1) * ROWS2], bp, dims))
    return parts


def _sb_sum_matrix(after):
    rows = lax.broadcasted_iota(jnp.int32, (ROWS2, ROWS2), 0) & (TILE - 1)
    cols = lax.broadcasted_iota(jnp.int32, (ROWS2, ROWS2), 1)
    tri = (rows > cols) if after else (rows < cols)
    return (tri | (cols >= TILE)).astype(bf16)


def _sb_sums(x, w, sum_mat):
    hi = x.astype(bf16)
    lo = (x - hi.astype(f32)).astype(bf16)
    return [_dot(jnp.concatenate([hi[:, j * TILE:(j + 1) * TILE], lo[:, j * TILE:(j + 1) * TILE]], axis=1), sum_mat, NN)
            for j in range(w)]


def _sb_edge_masks(qi, first, w):
    rows = lax.broadcasted_iota(jnp.int32, (SB_ROWS, TILE), 0) & (TILE - 1)
    cols = lax.broadcasted_iota(jnp.int32, (SB_ROWS, TILE), 1)
    not_pad = first * TILE + cols >= PAD
    before_query = (first + w - 1 - qi) * TILE + cols < rows
    if w == 1:
        return [not_pad & before_query]
    return [not_pad] + [None] * (w - 2) + [before_query]


def _sb_apply(valid, x):
    return x if valid is None else jnp.where(valid, x, 0.0)


def _sb_block(q2, k_ref, first, w, valids, c_rep, after_ones):
    kwin = k_ref[pl.ds(pl.multiple_of(first * TILE, TILE), w * TILE), :]
    z2 = jnp.concatenate(_sb_pair_dot(q2, kwin, NT, True), axis=0) * (HEAD_DIM ** -0.5 * LOG2E)
    ls2 = jnp.minimum(z2, 0.0) - jnp.log2(1.0 + jnp.exp2(jnp.minimum(z2, -z2)))
    m2 = ls2 - z2
    if w == 1:
        m2 = _sb_apply(valids[0], m2)
    else:
        m2 = jnp.concatenate([_sb_apply(valids[j], m2[:, j * TILE:(j + 1) * TILE]) for j in range(w)], axis=1)
    sums = _sb_sums(m2, w, after_ones)
    parts = [None] * w
    for j in reversed(range(w)):
        parts[j] = _sb_apply(valids[j], jnp.exp2(ls2[:, j * TILE:(j + 1) * TILE] + sums[j][:, :TILE] + c_rep))
        c_rep = c_rep + sums[j][:, TILE:]
    return ls2, (parts[0] if w == 1 else jnp.concatenate(parts, axis=1)), c_rep


SB_BLOCKS = (8, 4, 2, 1)


def _sb_walk(qi, block, state, right_to_left):
    n = qi + 1
    big = SB_BLOCKS[0]
    steps = [(n // big, lambda t, big=big: n - big * (t + 1), big)]
    steps += [((n % (2 * s)) // s, lambda t, s=s: n % s, s) for s in SB_BLOCKS[1:]]
    if not right_to_left:
        steps = [(cnt, (lambda t, cnt=cnt, pos=pos: pos(cnt - 1 - t)), s) for cnt, pos, s in reversed(steps)]
    for cnt, pos, s in steps:
        state = lax.fori_loop(0, cnt, lambda t, st, pos=pos, s=s: block(pos(t), s, st), state)
    return state


def _sb_cols(lp, off, single=False):
    mode = dict(pipeline_mode=pl.Buffered(1)) if single else {}
    return BS((lp, SB_PAIRS * TILE), lambda g: (0, off + g), **mode)


def _sb_fwd(proj, side=None):
    lp = proj.shape[0]
    sbw = SB_HEADS * HEAD_DIM
    ngrp = sbw // (SB_PAIRS * TILE)
    nt = lp // TILE

    def body(*refs):
        (q_ref, k_ref, v_ref, o_ref), side_refs = _side_split(side, refs, 3, 1, 0)
        _side_start(side, side_refs, (ngrp,))
        after_ones = _sb_sum_matrix(True)

        def q_step(qi, carry):
            qbase = pl.multiple_of(qi * TILE, TILE)
            q2 = _sb_stack(q_ref[pl.ds(qbase, TILE), :])

            def block(first, w, st):
                c_rep, acc = st
                _, a, c_rep = _sb_block(q2, k_ref, first, w, _sb_edge_masks(qi, first, w), c_rep, after_ones)
                vwin = v_ref[pl.ds(pl.multiple_of(first * TILE, TILE), w * TILE), :]
                return c_rep, acc + jnp.concatenate(_sb_pair_dot(a.astype(bf16), vwin, NN, True), axis=0)

            zero = jnp.zeros((SB_ROWS, TILE), f32)
            _, acc = _sb_walk(qi, block, (zero, zero), True)
            o_ref[pl.ds(qbase, TILE), :] = _sb_unstack(acc)
            return carry

        lax.fori_loop(0, nt, q_step, 0)
        _side_finish(side, side_refs, (ngrp,))

    s_in, s_in_specs, s_out, s_out_specs, s_scratch = _side_io(side)
    out = pl.pallas_call(
        body, out_shape=(S((lp, sbw), f32), *s_out), grid=(ngrp,),
        in_specs=[_sb_cols(lp, 0), _sb_cols(lp, ngrp), _sb_cols(lp, 2 * ngrp), *s_in_specs],
        out_specs=(_sb_cols(lp, 0), *s_out_specs), scratch_shapes=s_scratch,
        name="sb_fwd" if side is None else "sb_fwd_comm",
        compiler_params=_cp("parallel" if side is None else "arbitrary"))(proj, proj, proj, *s_in)
    return out[0], list(out[1:])


def _sb_bwd(proj, dout, side=None):
    lp = proj.shape[0]
    sbw = SB_HEADS * HEAD_DIM
    ngrp = sbw // (SB_PAIRS * TILE)
    nt = lp // TILE
    scale = HEAD_DIM ** -0.5

    def body(*refs):
        (q_ref, k_ref, v_ref, do_ref, dq_ref, dk_ref, dv_ref, dk_acc, dv_acc, e_buf, b_buf), side_refs = _side_split(side, refs, 4, 3, 4)
        _side_start(side, side_refs, (ngrp,))
        dk_acc[...] = jnp.zeros_like(dk_acc)
        dv_acc[...] = jnp.zeros_like(dv_acc)
        after_ones = _sb_sum_matrix(True)
        before_ones = _sb_sum_matrix(False)

        def add_cols(acc_ref, rows, parts):
            for p in range(SB_PAIRS):
                acc_ref[rows, p * TILE:(p + 1) * TILE] += parts[p]

        def q_step(qi, carry):
            qbase = pl.multiple_of(qi * TILE, TILE)
            q2 = _sb_stack(q_ref[pl.ds(qbase, TILE), :])
            do2 = _sb_stack(do_ref[pl.ds(qbase, TILE), :])

            def block_rl(first, w, c_rep):
                ls2, a, c_rep = _sb_block(q2, k_ref, first, w, _sb_edge_masks(qi, first, w), c_rep, after_ones)
                rows = pl.ds(pl.multiple_of(first * TILE, TILE), w * TILE)
                e = jnp.concatenate(_sb_pair_dot(do2, v_ref[rows, :], NT, True), axis=0) * a
                beta = jnp.exp2(ls2)
                for j in range(w):
                    e_buf[first + j] = e[:, j * TILE:(j + 1) * TILE]
                    b_buf[first + j] = beta[:, j * TILE:(j + 1) * TILE]
                add_cols(dv_acc, rows, _sb_pair_dot(a.astype(bf16), do2, TN, False))
                return c_rep

            zero = jnp.zeros((SB_ROWS, TILE), f32)
            _sb_walk(qi, block_rl, zero, True)

            def block_lr(first, w, st):
                e_rep, dq_acc = st
                rows = pl.ds(pl.multiple_of(first * TILE, TILE), w * TILE)
                valids = _sb_edge_masks(qi, first, w)
                es = [e_buf[first + j] for j in range(w)]
                sums = _sb_sums(es[0] if w == 1 else jnp.concatenate(es, axis=1), w, before_ones)
                parts = []
                for j in range(w):
                    beta = b_buf[first + j]
                    parts.append(_sb_apply(valids[j], (es[j] - beta * (es[j] + e_rep + sums[j][:, :TILE])) * scale))
                    e_rep = e_rep + sums[j][:, TILE:]
                dz = (parts[0] if w == 1 else jnp.concatenate(parts, axis=1)).astype(bf16)
                add_cols(dk_acc, rows, _sb_pair_dot(dz, q2, TN, False))
                return e_rep, dq_acc + jnp.concatenate(_sb_pair_dot(dz, k_ref[rows, :], NN, True), axis=0)

            _, dq = _sb_walk(qi, block_lr, (zero, zero), False)
            dq_ref[pl.ds(qbase, TILE), :] = _sb_unstack(dq).astype(bf16)
            return carry

        lax.fori_loop(0, nt, q_step, 0)
        dk_ref[...] = dk_acc[...].astype(bf16)
        dv_ref[...] = dv_acc[...].astype(bf16)
        _side_finish(side, side_refs, (ngrp,))

    wide = SB_PAIRS * TILE
    s_in, s_in_specs, s_out, s_out_specs, s_scratch = _side_io(side)
    out = pl.pallas_call(
        body, out_shape=(S((lp, sbw), bf16),) * 3 + tuple(s_out), grid=(ngrp,),
        in_specs=[_sb_cols(lp, 0, True), _sb_cols(lp, ngrp, True), _sb_cols(lp, 2 * ngrp, True), _sb_cols(lp, 0, True), *s_in_specs],
        out_specs=(_sb_cols(lp, 0, True),) * 3 + tuple(s_out_specs),
        scratch_shapes=[pltpu.VMEM((lp, wide), f32), pltpu.VMEM((lp, wide), f32),
                        pltpu.VMEM((nt, SB_ROWS, TILE), f32), pltpu.VMEM((nt, SB_ROWS, TILE), f32), *s_scratch],
        name="sb_bwd" if side is None else "sb_bwd_comm",
        compiler_params=_cp("parallel" if side is None else "arbitrary"))(proj, proj, proj, dout, *s_in)
    return out[0], out[1], out[2], list(out[3:])


def _place():
    x, y, c = lax.axis_index("x"), lax.axis_index("y"), lax.axis_index("c")
    chips = [(1 - x, y), (x, 1 - y), (1 - x, 1 - y)]
    return x, y, c, chips


def _two_level_gather_body(rows, x_ref, send_sems, recv_sems, local_sem):
    x, y, c, chips = _place()
    me, sibling = (x, y, c), (x, y, 1 - c)

    def copy(k, block, to, src=None):
        return pltpu.make_async_remote_copy(
            src_ref=rows(*block) if src is None else src, dst_ref=rows(*block),
            send_sem=send_sems.at[k], recv_sem=recv_sems.at[k], device_id=to, device_id_type=MESH)

    mine = pltpu.make_async_copy(x_ref, rows(*me), local_sem)
    mine.start()
    first = [copy(0, me, sibling, src=x_ref)]
    first += [copy(1 + j, me, (*chip, c), src=x_ref) for j, chip in enumerate(chips)]
    for cp in first:
        cp.start()
    passed = [copy(4 + j, (*chip, c), sibling) for j, chip in enumerate(chips)]
    for j, chip in enumerate(chips):
        copy(1 + j, (*chip, c), me).wait_recv()
        passed[j].start()
    copy(0, sibling, me).wait_recv()
    for j, chip in enumerate(chips):
        copy(4 + j, (*chip, 1 - c), me).wait_recv()
    for cp in first + passed:
        cp.wait_send()
    mine.wait()


_GATHER_SEMS = [pltpu.SemaphoreType.DMA((7,)), pltpu.SemaphoreType.DMA((7,)), pltpu.SemaphoreType.DMA]


class _Side:
    def __init__(self, inputs, out_shapes, nsem, start, finish):
        self.inputs, self.out_shapes, self.start, self.finish = list(inputs), list(out_shapes), start, finish
        self.scratch = [pltpu.SemaphoreType.DMA((nsem,)), pltpu.SemaphoreType.DMA((nsem,)), pltpu.SemaphoreType.DMA((len(inputs),))]


def _side_io(side):
    if side is None:
        return [], [], [], [], []
    return side.inputs, [ANY] * len(side.inputs), side.out_shapes, [ANY] * len(side.out_shapes), side.scratch


def _side_split(side, refs, n_in, n_out, n_scratch):
    if side is None:
        return refs, None
    si, so = len(side.inputs), len(side.out_shapes)
    a, b, c, d = n_in, n_in + si, n_in + si + n_out, n_in + si + n_out + so
    return refs[:a] + refs[b:c] + refs[d:d + n_scratch], (refs[a:b], refs[c:d], refs[d + n_scratch:])


def _side_first_last(grid):
    first = last = None
    for k, n in enumerate(grid):
        i = pl.program_id(k)
        first = (i == 0) if first is None else first & (i == 0)
        last = (i == n - 1) if last is None else last & (i == n - 1)
    return first, last


def _side_start(side, side_refs, grid):
    if side is not None:
        pl.when(_side_first_last(grid)[0])(lambda: side.start(*side_refs))


def _side_finish(side, side_refs, grid):
    if side is not None:
        pl.when(_side_first_last(grid)[1])(lambda: side.finish(*side_refs))


def _run_side(side, name):
    def body(*refs):
        _, side_refs = _side_split(side, refs, 0, 0, 0)
        side.start(*side_refs)
        side.finish(*side_refs)

    inputs, in_specs, out_shapes, out_specs, scratch = _side_io(side)
    return pl.pallas_call(body, out_shape=tuple(out_shapes), in_specs=in_specs, out_specs=tuple(out_specs),
                          scratch_shapes=scratch, name=name)(*inputs)


def _gather_side(shards, layer):
    def plan(in_refs, out_refs, sems, starting):
        send_sems, recv_sems, local_sems = sems
        x, y, c, chips = _place()
        me, sibling = (x, y, c), (x, y, 1 - c)
        jobs = []
        for a, (x_ref, o_ref) in enumerate(zip(in_refs, out_refs)):
            r = x_ref.shape[1]
            src = x_ref.at[layer]

            def rows(px, py, pc, o_ref=o_ref, r=r):
                return o_ref.at[0, pl.ds(pl.multiple_of((4 * px + 2 * py + pc) * r, 16), r), :]

            def copy(k, block, to, from_shard=False, a=a, rows=rows, src=src):
                return pltpu.make_async_remote_copy(
                    src_ref=src if from_shard else rows(*block), dst_ref=rows(*block),
                    send_sem=send_sems.at[7 * a + k], recv_sem=recv_sems.at[7 * a + k], device_id=to, device_id_type=MESH)

            job = dict(
                mine=pltpu.make_async_copy(src, rows(*me), local_sems.at[a]),
                first=[copy(0, me, sibling, True)] + [copy(1 + j, me, (*chip, c), True) for j, chip in enumerate(chips)])
            if not starting:
                job.update(
                    passed=[copy(4 + j, (*chip, c), sibling) for j, chip in enumerate(chips)],
                    from_chips=[copy(1 + j, (*chip, c), me) for j, chip in enumerate(chips)],
                    from_sibling=[copy(0, sibling, me)] + [copy(4 + j, (*chip, 1 - c), me) for j, chip in enumerate(chips)])
            jobs.append(job)
        return jobs

    def start(*refs):
        for job in plan(*refs, starting=True):
            job["mine"].start()
            for cp in job["first"]:
                cp.start()

    def finish(*refs):
        jobs = plan(*refs, starting=False)
        for job in jobs:
            for arrived, onward in zip(job["from_chips"], job["passed"]):
                arrived.wait_recv()
                onward.start()
        for job in jobs:
            for cp in job["from_sibling"]:
                cp.wait_recv()
            for cp in job["first"] + job["passed"]:
                cp.wait_send()
            job["mine"].wait()

    shapes = [S((1, NDEV * s.shape[1], s.shape[2]), s.dtype) for s in shards]
    return _Side(shards, shapes, 7 * len(shards), start, finish)


def _swap_side(gs):
    def plan(in_refs, out_refs, sems):
        send_sems, recv_sems, _ = sems
        x, y, c, chips = _place()
        copies = []
        for a, (g_ref, o_ref) in enumerate(zip(in_refs, out_refs)):
            r = g_ref.shape[1] // NDEV
            for j, (px, py) in enumerate([(x, y)] + chips):
                d = 4 * px + 2 * py + (1 - c)
                copies.append(pltpu.make_async_remote_copy(
                    src_ref=g_ref.at[:, pl.ds(pl.multiple_of(d * r, 8), r), :], dst_ref=o_ref.at[j],
                    send_sem=send_sems.at[4 * a + j], recv_sem=recv_sems.at[4 * a + j], device_id=(x, y, 1 - c), device_id_type=MESH))
        return copies

    return _exchange_side(gs, [S((4, g.shape[0], g.shape[1] // NDEV, g.shape[2]), f32) for g in gs], 4 * len(gs), plan)


def _ici_side(sends):
    def plan(in_refs, out_refs, sems):
        send_sems, recv_sems, _ = sems
        x, y, c, chips = _place()
        return [pltpu.make_async_remote_copy(
            src_ref=s_ref.at[j], dst_ref=o_ref.at[j], send_sem=send_sems.at[3 * a + j], recv_sem=recv_sems.at[3 * a + j],
            device_id=(*chip, c), device_id_type=MESH)
            for a, (s_ref, o_ref) in enumerate(zip(in_refs, out_refs)) for j, chip in enumerate(chips)]

    return _exchange_side(sends, [S(s.shape, s.dtype) for s in sends], 3 * len(sends), plan)


def _exchange_side(inputs, shapes, nsem, plan):
    def start(*refs):
        for cp in plan(*refs):
            cp.start()

    def finish(*refs):
        copies = plan(*refs)
        for cp in copies:
            cp.wait_recv()
        for cp in copies:
            cp.wait_send()

    return _Side(inputs, shapes, nsem, start, finish)


def _ag_small(shard, name):
    r, w = shard.shape

    def body(x_ref, o_ref, send_sems, recv_sems, local_sem):
        def rows(px, py, pc):
            return o_ref.at[pl.ds(pl.multiple_of((4 * px + 2 * py + pc) * r, 8), r), :]
        _two_level_gather_body(rows, x_ref, send_sems, recv_sems, local_sem)

    vmem = BS(memory_space=pltpu.VMEM)
    return pl.pallas_call(
        body, out_shape=S((NDEV * r, w), shard.dtype), in_specs=[vmem], out_specs=vmem,
        scratch_shapes=_GATHER_SEMS, name=name)(shard)


def _rel_ids():
    x, y, c = lax.axis_index("x"), lax.axis_index("y"), lax.axis_index("c")
    rel = [(x, y), (1 - x, y), (x, 1 - y), (1 - x, 1 - y)]
    return jnp.stack([4 * px + 2 * py + c for px, py in rel]).astype(jnp.int32)


def _rs_partial(g, sib, ids, name):
    ns, rows8, cdim = g.shape
    r = rows8 // NDEV
    tr = r // 2
    nb = r // tr

    def own_body(ids_ref, g_ref, s_ref, o_ref):
        o_ref[...] = g_ref[...] + s_ref[...]

    own = pl.pallas_call(
        own_body, out_shape=S((ns, r, cdim), f32),
        grid_spec=pltpu.PrefetchScalarGridSpec(
            num_scalar_prefetch=1, grid=(ns, nb),
            in_specs=[BS((None, tr, cdim), lambda s, t, ids_ref: (s, ids_ref[0] * nb + t, 0)),
                      BS((None, None, tr, cdim), lambda s, t, ids_ref: (0, s, t, 0))],
            out_specs=BS((None, tr, cdim), lambda s, t, ids_ref: (s, t, 0))),
        name=name + "_own", compiler_params=_cp("parallel", "parallel"))(ids, g, sib)

    def send_body(ids_ref, g_ref, s_ref, o_ref):
        o_ref[...] = (g_ref[...] + s_ref[...]).astype(bf16)

    send = pl.pallas_call(
        send_body, out_shape=S((3, ns, r, cdim), bf16),
        grid_spec=pltpu.PrefetchScalarGridSpec(
            num_scalar_prefetch=1, grid=(3, ns, nb),
            in_specs=[BS((None, tr, cdim), lambda j, s, t, ids_ref: (s, ids_ref[j + 1] * nb + t, 0)),
                      BS((None, None, tr, cdim), lambda j, s, t, ids_ref: (j + 1, s, t, 0))],
            out_specs=BS((None, None, tr, cdim), lambda j, s, t, ids_ref: (j, s, t, 0))),
        name=name + "_send", compiler_params=_cp("parallel", "parallel", "parallel"))(ids, g, sib)
    return own, send


def _prep(w, name):
    nl, r, cdim = w.shape
    blk = BS((None, r, cdim), lambda l: (l, 0, 0))

    def body(w_ref, o_ref):
        o_ref[...] = w_ref[...].astype(bf16)

    return pl.pallas_call(body, out_shape=S(w.shape, bf16), grid=(nl,), in_specs=[blk], out_specs=blk,
                          name=name, compiler_params=_cp("parallel"))(w)


def _prep_t(w, name):
    nl, d, r = w.shape
    tc = _pick(d, (256, 128))

    def body(w_ref, o_ref):
        o_ref[...] = w_ref[...].T.astype(bf16)

    return pl.pallas_call(
        body, out_shape=S((nl, r, d), bf16), grid=(nl, d // tc),
        in_specs=[BS((None, tc, r), lambda l, t: (l, t, 0))], out_specs=BS((None, r, tc), lambda l, t: (l, 0, t)),
        name=name, compiler_params=_cp("parallel", "parallel"))(w)


def _adam_math(w, g, m, v):
    m2 = ADAM_B1 * m + (1.0 - ADAM_B1) * g
    v2 = ADAM_B2 * v + (1.0 - ADAM_B2) * (g * g)
    m_hat = m2 / (1.0 - ADAM_B1 ** ADAM_STEP)
    v_hat = v2 / (1.0 - ADAM_B2 ** ADAM_STEP)
    delta = -ADAM_LR * (m_hat / (jnp.sqrt(v_hat) + ADAM_EPS) + ADAM_WD * w)
    return delta, m2, v2


def _adam_big(w, m, v, own, arrived, layer, slot, transposed, name, prev=None):
    nl, a, b = w.shape
    if transposed:
        ta = _pick(a, (256, 128))
        w_blk = BS((None, ta, b), lambda t: (layer, t, 0))
        own_blk = BS((None, b, ta), lambda t: (slot, 0, t))
        arr_blk = [BS((None, None, b, ta), lambda t, j=j: (j, slot, 0, t)) for j in range(3)]
    else:
        ta = _pick(a, (a // 4, a // 2, a)) if (a // 4) % 16 == 0 else a
        w_blk = BS((None, ta, b), lambda t: (layer, t, 0))
        own_blk = BS((None, ta, b), lambda t: (slot, t, 0))
        arr_blk = [BS((None, None, ta, b), lambda t, j=j: (j, slot, t, 0)) for j in range(3)]

    def body(*refs):
        w_ref, m_ref, v_ref, own_ref, a0_ref, a1_ref, a2_ref = refs[:7]
        g_ref, d_ref, m2_ref, v2_ref = refs[-4:]
        g = ((own_ref[...] + a0_ref[...].astype(f32)) + a1_ref[...].astype(f32)) + a2_ref[...].astype(f32)
        if transposed:
            g = g.T
        delta, m2, v2 = _adam_math(w_ref[...], g, m_ref[...], v_ref[...])
        g_ref[...] = g
        d_ref[...] = delta
        m2_ref[...] = m2
        v2_ref[...] = v2

    threaded = () if prev is None else tuple(prev)
    return pl.pallas_call(
        body, out_shape=(S(w.shape, f32),) * 4, grid=(a // ta,),
        in_specs=[w_blk, w_blk, w_blk, own_blk] + arr_blk + [ANY] * len(threaded), out_specs=(w_blk,) * 4,
        input_output_aliases={7 + k: k for k in range(len(threaded))},
        name=name, compiler_params=_cp("parallel"))(w, m, v, own, arrived, arrived, arrived, *threaded)


def _adam_small(gathered, w, m, v):
    r, wd = w.shape

    def body(g_ref, w_ref, m_ref, v_ref, gs_ref, d_ref, m2_ref, v2_ref):
        g = g_ref[0:r, :]
        for dev in range(1, NDEV):
            g = g + g_ref[dev * r:(dev + 1) * r, :]
        delta, m2, v2 = _adam_math(w_ref[...], g, m_ref[...], v_ref[...])
        gs_ref[...] = g
        d_ref[...] = delta
        m2_ref[...] = m2
        v2_ref[...] = v2

    return pl.pallas_call(body, out_shape=(S((r, wd), f32),) * 4, name="adam_small", compiler_params=_cp())(gathered, w, m, v)


def _pack(arrays, width):
    flat = jnp.concatenate([a.reshape(-1) for a in arrays])
    rows = -(-flat.shape[0] // (8 * width)) * 8
    return jnp.pad(flat, (0, rows * width - flat.shape[0])).reshape(rows, width)


def _unpack(packed, shapes):
    flat = packed.reshape(-1)
    out, off = [], 0
    for shp in shapes:
        n = 1
        for s in shp:
            n *= s
        out.append(flat[off:off + n].reshape(shp))
        off += n
    return out


def kernel(x, meta_tokens, attn_norm_g, w_in, q_norm_g, k_norm_g, attn_sinks, swa_out_g, sb_out_g, w_o, ffn_norm_g, w_gate, w_up, w_down, loss_target, m_meta_tokens, m_attn_norm_g, m_w_in, m_q_norm_g, m_k_norm_g, m_attn_sinks, m_swa_out_g, m_sb_out_g, m_w_o, m_ffn_norm_g, m_w_gate, m_w_up, m_w_down, v_meta_tokens, v_attn_norm_g, v_w_in, v_q_norm_g, v_k_norm_g, v_attn_sinks, v_swa_out_g, v_sb_out_g, v_w_o, v_ffn_norm_g, v_w_gate, v_w_up, v_w_down):
    nl, d = attn_norm_g.shape
    x2, target = x[0], loss_target[0]
    me = 4 * lax.axis_index("x") + 2 * lax.axis_index("y") + lax.axis_index("c")
    ids = _rel_ids()

    meta_all = _ag_small(meta_tokens, "ag_meta")
    meta_full = meta_all.reshape(NDEV, N_META, -1).transpose(1, 0, 2).reshape(N_META, d)
    shards = [_prep_t(w_in, "prep_w_in"), _prep(w_o, "prep_w_o"), _prep_t(w_gate, "prep_w_gate"), _prep_t(w_up, "prep_w_up"),
              _prep(w_down, "prep_w_down")]
    weights = [_run_side(_gather_side(shards, 0), "ag_layer0")] + [None] * (nl - 1)

    qg2 = jnp.tile(q_norm_g, (1, TILE // HEAD_DIM))
    kg2 = jnp.tile(k_norm_g, (1, TILE // HEAD_DIM))
    out_g = jnp.concatenate([swa_out_g, sb_out_g], axis=1)
    swa_cols = (SWA_HEADS + 2 * SWA_KV_HEADS) * HEAD_DIM

    h = _embed(x2, meta_full)
    saved = []
    for l in range(nl):
        w_in_t, w_o_f, w_gate_t, w_up_t, w_down_f = weights[l]
        hn = _rmsnorm_fwd([h], attn_norm_g[l:l + 1], "attn_norm")
        proj = _mm(hn, w_in_t, 0, "nt", "proj_swa", cols=(0, swa_cols))
        proj_sb = _mm(hn, w_in_t, 0, "nt", "proj_sb", out_dtype=bf16, cols=(swa_cols, 3 * SB_HEADS * HEAD_DIM))
        out_a = _swa_fwd(proj, qg2[l:l + 1], kg2[l:l + 1], attn_sinks[l])
        if l + 1 < nl:
            out_b, weights[l + 1] = _sb_fwd(proj_sb, _gather_side(shards, l + 1))
        else:
            out_b, _ = _sb_fwd(proj_sb)
        mixed = _rmsnorm_fwd([out_a, out_b], out_g[l:l + 1], "out_norm")
        h_mid = _mm(mixed, w_o_f, 0, "nn", "attn_out", res=h)
        hn2 = _rmsnorm_fwd([h_mid], ffn_norm_g[l:l + 1], "ffn_norm")
        g, u, act = _ffn_up(hn2, w_gate_t, w_up_t)
        h_out = _mm(act, w_down_f, 0, "nn", "ffn_down", res=h_mid)
        saved.append((h, hn, proj, proj_sb, out_a, out_b, mixed, h_mid, hn2, g, u, act))
        h = h_out

    loss_tile, dh = _loss_head(h, target)
    loss = lax.psum(loss_tile[0, 0], AXES)

    small = {k: [None] * nl for k in ("attn", "ffn", "out", "q", "k", "sink")}
    pending = None
    reduced = [None] * nl

    def partial_sums(slabs, sibs):
        return [_rs_partial(gg, sib, ids, name) for gg, sib, name in zip(slabs, sibs, ("rs_part_w_in", "rs_part_w_o", "rs_part_w_ff"))]

    for l in reversed(range(nl)):
        w_in_t, w_o_f, w_gate_t, w_up_t, w_down_f = weights[l]
        h_in, hn, proj, proj_sb, out_a, out_b, mixed, h_mid, hn2, g, u, act = saved[l]
        dg, du, sibs = _ffn_down_bwd(dh, w_down_f, g, u, None if pending is None else _swap_side(pending))
        parts = None if pending is None else partial_sums(pending, sibs)
        g_ff = _mm_tn(act, dh, None, 2, 3, "grad_w_down")
        g_ff = _mm_tn(dg, hn2, g_ff, 0, 3, "grad_w_gate")
        g_ff = _mm_tn(du, hn2, g_ff, 1, 3, "grad_w_up")
        dhn2 = _mm(du, w_up_t, 0, "nn", "d_ffn_in_up", res=_mm(dg, w_gate_t, 0, "nn", "d_ffn_in_gate"))
        (dh_mid,), small["ffn"][l] = _rmsnorm_bwd(dhn2, [h_mid], ffn_norm_g[l:l + 1], dh, "ffn_norm_bwd")
        dmixed = _mm(dh_mid, w_o_f, 0, "nt", "d_mixed")
        g_o = _mm_tn(mixed, dh_mid, None, 0, 1, "grad_w_o")
        (dout_a, dout_b), small["out"][l] = _rmsnorm_bwd(dmixed, [out_a, out_b], out_g[l:l + 1], None, "out_norm_bwd", bf16)
        dq_b, dk_b, dv_b, arrived = _sb_bwd(proj_sb, dout_b, None if parts is None else _ici_side([p[1] for p in parts]))
        if parts is not None:
            reduced[l + 1] = ([p[0] for p in parts], arrived)
        dq_a, dk_a, dv_a, dqg, dkg, dsk = _swa_bwd(proj, dout_a, qg2[l:l + 1], kg2[l:l + 1], attn_sinks[l])
        small["q"][l] = dqg[0, :HEAD_DIM] + dqg[0, HEAD_DIM:]
        small["k"][l] = dkg[0, :HEAD_DIM] + dkg[0, HEAD_DIM:]
        small["sink"][l] = dsk[0, :SWA_HEADS]
        dproj = jnp.concatenate([dq_a, dk_a, dv_a, dq_b, dk_b, dv_b], axis=1)
        dhn = _mm(dproj, w_in_t, 0, "nn", "d_attn_in")
        g_in = _mm_tn(dproj, hn, None, 0, 1, "grad_w_in")
        (dh,), small["attn"][l] = _rmsnorm_bwd(dhn, [h_in], attn_norm_g[l:l + 1], dh_mid, "attn_norm_bwd")
        pending = [g_in, g_o, g_ff]

    parts = partial_sums(pending, _run_side(_swap_side(pending), "rs_swap_layer0"))
    reduced[0] = ([p[0] for p in parts], list(_run_side(_ici_side([p[1] for p in parts]), "rs_ici_layer0")))

    grad_x = dh[TILE:][None]
    d_meta = dh[PAD:TILE]

    qw = SWA_HEADS * HEAD_DIM
    d_out = jnp.concatenate(small["out"], axis=0)
    small_grads = [jnp.concatenate(small["attn"], axis=0), jnp.stack(small["q"]), jnp.stack(small["k"]), jnp.stack(small["sink"]),
                   d_out[:, :qw], d_out[:, qw:], jnp.concatenate(small["ffn"], axis=0), d_meta]
    col0 = me * meta_tokens.shape[1]

    def widen(a):
        return lax.dynamic_update_slice(jnp.zeros((N_META, d), f32), a, (0, col0))

    small_w = [attn_norm_g, q_norm_g, k_norm_g, attn_sinks, swa_out_g, sb_out_g, ffn_norm_g]
    small_m = [m_attn_norm_g, m_q_norm_g, m_k_norm_g, m_attn_sinks, m_swa_out_g, m_sb_out_g, m_ffn_norm_g]
    small_v = [v_attn_norm_g, v_q_norm_g, v_k_norm_g, v_attn_sinks, v_swa_out_g, v_sb_out_g, v_ffn_norm_g]
    shapes = [a.shape for a in small_w] + [(N_META, d)]
    packed = _adam_small(_ag_small(_pack(small_grads, d), "ag_small_grads"),
                         _pack(small_w + [widen(meta_tokens)], d), _pack(small_m + [widen(m_meta_tokens)], d),
                         _pack(small_v + [widen(v_meta_tokens)], d))
    small_out = []
    for p in packed:
        parts = _unpack(p, shapes)
        parts[-1] = lax.dynamic_slice(parts[-1], (0, col0), meta_tokens.shape)
        small_out.append(parts)

    big = dict.fromkeys(("w_in", "w_o", "w_gate", "w_up", "w_down"))
    for l in range(nl):
        (own_in, own_o, own_ff), (arr_in, arr_o, arr_ff) = reduced[l]
        big["w_in"] = _adam_big(w_in, m_w_in, v_w_in, own_in, arr_in, l, 0, True, "adam_w_in", big["w_in"])
        big["w_o"] = _adam_big(w_o, m_w_o, v_w_o, own_o, arr_o, l, 0, False, "adam_w_o", big["w_o"])
        big["w_gate"] = _adam_big(w_gate, m_w_gate, v_w_gate, own_ff, arr_ff, l, 0, True, "adam_w_gate", big["w_gate"])
        big["w_up"] = _adam_big(w_up, m_w_up, v_w_up, own_ff, arr_ff, l, 1, True, "adam_w_up", big["w_up"])
        big["w_down"] = _adam_big(w_down, m_w_down, v_w_down, own_ff, arr_ff, l, 2, False, "adam_w_down", big["w_down"])

    def group(k):
        sm = small_out[k]
        return [sm[7], sm[0], big["w_in"][k], sm[1], sm[2], sm[3], sm[4], sm[5], big["w_o"][k], sm[6],
                big["w_gate"][k], big["w_up"][k], big["w_down"][k]]

    return (loss, grad_x, *group(0), *group(1), *group(2), *group(3))
```

```python
import jax
import jax.numpy as jnp
from jax import lax
from jax.experimental import pallas as pl
from jax.experimental.pallas import tpu as pltpu

f32, bf16 = jnp.float32, jnp.bfloat16
S = jax.ShapeDtypeStruct
BS = pl.BlockSpec

N_META = 16
TILE = 128
PAD = TILE - N_META
HEAD_DIM = 64
SWA_HEADS = 16
SWA_KV_HEADS = 4
SB_HEADS = 16
EPS = 1e-6
NEG = -1e30
ADAM_LR, ADAM_B1, ADAM_B2, ADAM_EPS, ADAM_WD, ADAM_STEP = 0.001, 0.9, 0.999, 1e-08, 0.01, 10
AXES = ("x", "y", "c")
NDEV = 8
VMEM_LIMIT_V7X = 56 * 1024 * 1024
MM_RESIDENT_K = 2048
MESH = pl.DeviceIdType.MESH
ANY = pl.BlockSpec(memory_space=pl.ANY)

NN = ((1,), (0,))
NT = ((1,), (1,))
TN = ((0,), (0,))


def _cp(*sem):
    return pltpu.CompilerParams(dimension_semantics=sem if sem else None, vmem_limit_bytes=VMEM_LIMIT_V7X)


def _pick(n, cands):
    for c in cands:
        if n % c == 0:
            return c
    raise ValueError(f"no tile for {n} in {cands}")


def _dot(a, b, dims):
    return lax.dot_general(a, b, (dims, ((), ())), preferred_element_type=f32)


def _embed(x, meta_full):
    seq, d = x.shape
    meta_pad = jnp.pad(meta_full, ((PAD, 0), (0, 0)))

    def body(x_ref, m_ref, o_ref):
        i = pl.program_id(0)

        @pl.when(i == 0)
        def _():
            o_ref[...] = m_ref[...]

        @pl.when(i > 0)
        def _():
            o_ref[...] = x_ref[...]

    return pl.pallas_call(
        body, out_shape=S((TILE + seq, d), f32), grid=(1 + seq // TILE,),
        in_specs=[BS((TILE, d), lambda i: (jnp.maximum(i - 1, 0), 0)), BS((TILE, d), lambda i: (0, 0))],
        out_specs=BS((TILE, d), lambda i: (i, 0)), name="embed", compiler_params=_cp("arbitrary"))(x, meta_pad)


def _rmsnorm_fwd(xs, gain, name):
    lp = xs[0].shape[0]
    widths = [x.shape[1] for x in xs]
    w = sum(widths)
    tr = _pick(lp, (384, 256, 128))
    n = len(xs)

    def body(*refs):
        g_ref, o_ref = refs[n], refs[n + 1]
        off = 0
        for k in range(n):
            sl = slice(off, off + widths[k])
            off += widths[k]
            xv = refs[k][...]
            r = lax.rsqrt(jnp.mean(xv * xv, axis=-1, keepdims=True) + EPS)
            o_ref[:, sl] = ((xv * r) * g_ref[:, sl]).astype(bf16)

    return pl.pallas_call(
        body, out_shape=S((lp, w), bf16), grid=(lp // tr,),
        in_specs=[BS((tr, wk), lambda i: (i, 0)) for wk in widths] + [BS((1, w), lambda i: (0, 0))],
        out_specs=BS((tr, w), lambda i: (i, 0)), name=name, compiler_params=_cp("parallel"))(*xs, gain)


def _rmsnorm_bwd(dy, xs, gain, res, name, out_dtype=f32):
    lp = xs[0].shape[0]
    widths = [x.shape[1] for x in xs]
    w = sum(widths)
    tr = _pick(lp, (384, 256, 128))
    n = len(xs)
    has_res = res is not None
    assert not has_res or n == 1

    def body(*refs):
        dy_ref, g_ref = refs[0], refs[1 + n]
        res_ref = refs[2 + n] if has_res else None
        dx_refs = refs[2 + n + has_res:2 + 2 * n + has_res]
        dg_ref = refs[-1]

        @pl.when(pl.program_id(0) == 0)
        def _():
            dg_ref[...] = jnp.zeros_like(dg_ref)

        off = 0
        for k in range(n):
            sl = slice(off, off + widths[k])
            off += widths[k]
            xv = refs[1 + k][...]
            dyv = dy_ref[:, sl]
            r = lax.rsqrt(jnp.mean(xv * xv, axis=-1, keepdims=True) + EPS)
            xh = xv * r
            dg_ref[:, sl] += jnp.sum(dyv * xh, axis=0, keepdims=True)
            dxh = dyv * g_ref[:, sl]
            dx = r * (dxh - xh * jnp.mean(dxh * xh, axis=-1, keepdims=True))
            if has_res:
                dx = dx + res_ref[...]
            dx_refs[k][...] = dx.astype(out_dtype)

    rows = [BS((tr, wk), lambda i: (i, 0)) for wk in widths]
    full = BS((tr, w), lambda i: (i, 0))
    vec = BS((1, w), lambda i: (0, 0))
    args = (dy, *xs, gain) + ((res,) if has_res else ())
    out = pl.pallas_call(
        body, out_shape=tuple(S((lp, wk), out_dtype) for wk in widths) + (S((1, w), f32),), grid=(lp // tr,),
        in_specs=[full] + rows + [vec] + ([full] if has_res else []), out_specs=tuple(rows) + (vec,),
        name=name, compiler_params=_cp("arbitrary"))(*args)
    return list(out[:n]), out[n]


def _loss_head(h, target):
    lp, d = h.shape

    def body(h_ref, t_ref, loss_ref, dh_ref):
        i = pl.program_id(0)

        @pl.when(i == 0)
        def _():
            loss_ref[...] = jnp.zeros_like(loss_ref)
            dh_ref[...] = jnp.zeros_like(dh_ref)

        @pl.when(i > 0)
        def _():
            e = h_ref[...] - t_ref[...]
            dh_ref[...] = e * (1.0 / d)
            loss_ref[...] += 0.5 * jnp.sum(jnp.sum(e * e, axis=-1, keepdims=True) * (1.0 / d))

    return pl.pallas_call(
        body, out_shape=(S((8, 128), f32), S((lp, d), f32)), grid=(lp // TILE,),
        in_specs=[BS((TILE, d), lambda i: (i, 0)), BS((TILE, d), lambda i: (jnp.maximum(i - 1, 0), 0))],
        out_specs=(BS((8, 128), lambda i: (0, 0)), BS((TILE, d), lambda i: (i, 0))),
        name="loss_head", compiler_params=_cp("arbitrary"))(h, target)


def _mm(a, b, layer, mode, name, out_dtype=f32, res=None, cols=None, side=None):
    m, k = a.shape
    n = b.shape[2] if mode == "nn" else b.shape[1]
    c0 = 0
    if cols is not None:
        assert mode == "nt"
        c0, n = cols
    tn = _pick(n, (512, 384, 256, 128))
    while c0 % tn:
        tn -= TILE
    assert n % tn == 0
    joff = c0 // tn
    rows_outer = k <= MM_RESIDENT_K
    tm = _pick(m, (1408, 384, 256, 128) if rows_outer else (384, 256, 128))
    dims = NN if mode == "nn" else NT
    has_res = res is not None

    grid = (m // tm, n // tn) if rows_outer else (n // tn, m // tm)

    def body(*refs):
        refs, side_refs = _side_split(side, refs, 2 + has_res, 1, 0)
        _side_start(side, side_refs, grid)
        if has_res:
            a_ref, b_ref, r_ref, o_ref = refs
        else:
            a_ref, b_ref, o_ref = refs
        acc = _dot(a_ref[...].astype(bf16), b_ref[...], dims)
        if has_res:
            acc = acc + r_ref[...]
        o_ref[...] = acc.astype(out_dtype)
        _side_finish(side, side_refs, grid)

    def ij(f):
        return (lambda i, j: f(i, j)) if rows_outer else (lambda j, i: f(i, j))

    if mode == "nn":
        b_spec = BS((None, k, tn), ij(lambda i, j: (layer, 0, j)))
    else:
        b_spec = BS((None, tn, k), ij(lambda i, j: (layer, joff + j, 0)))
    tile = BS((tm, tn), ij(lambda i, j: (i, j)))
    args = (a, b) + ((res,) if has_res else ())
    s_in, s_in_specs, s_out, s_out_specs, s_scratch = _side_io(side)
    sem = ("parallel", "parallel") if side is None else ("arbitrary", "arbitrary")
    out = pl.pallas_call(
        body, out_shape=(S((m, n), out_dtype), *s_out), grid=grid,
        in_specs=[BS((tm, k), ij(lambda i, j: (i, 0))), b_spec] + ([tile] if has_res else []) + s_in_specs,
        out_specs=(tile, *s_out_specs), scratch_shapes=s_scratch,
        name=name if side is None else name + "_comm", compiler_params=_cp(*sem))(*args, *s_in)
    return out[0] if side is None else (out[0], list(out[1:]))


def _mm_tn(a, b, gbuf, slot, nslots, name):
    t, m = a.shape
    n = b.shape[1]
    tm = _pick(m, (512, 384, 256, 128))
    tk = _pick(t, (1408, 1024, 512, 384, 256, 128))
    nk = t // tk

    def body(*refs):
        a_ref, b_ref, o_ref = refs[0], refs[1], refs[-1]
        kk = pl.program_id(1)

        @pl.when(kk == 0)
        def _():
            o_ref[...] = jnp.zeros_like(o_ref)

        o_ref[...] += _dot(a_ref[...].astype(bf16), b_ref[...].astype(bf16), TN)

    in_specs = [BS((tk, tm), lambda i, kk: (kk, i)), BS((tk, n), lambda i, kk: (kk, 0))]
    args = (a, b)
    aliases = {}
    if gbuf is not None:
        in_specs.append(ANY)
        args = args + (gbuf,)
        aliases = {2: 0}
    return pl.pallas_call(
        body, out_shape=S((nslots, m, n), f32), grid=(m // tm, nk), in_specs=in_specs,
        out_specs=BS((None, tm, n), lambda i, kk: (slot, i, 0)), input_output_aliases=aliases,
        name=name, compiler_params=_cp("parallel", "arbitrary"))(*args)


def _ffn_up(hn, wg, wu):
    m, k = hn.shape
    n = wg.shape[1]
    tm = _pick(m, (1408, 384, 256, 128))
    tn = _pick(n, (512, 384, 256, 128))

    def body(a_ref, wg_ref, wu_ref, g_ref, u_ref, act_ref):
        a = a_ref[...]
        g = _dot(a, wg_ref[...], NT)
        u = _dot(a, wu_ref[...], NT)
        g_ref[...] = g
        u_ref[...] = u
        act_ref[...] = (g * jax.nn.sigmoid(g) * u).astype(bf16)

    tile = BS((tm, tn), lambda i, j: (i, j))
    return pl.pallas_call(
        body, out_shape=(S((m, n), f32), S((m, n), f32), S((m, n), bf16)), grid=(m // tm, n // tn),
        in_specs=[BS((tm, k), lambda i, j: (i, 0)), BS((None, tn, k), lambda i, j: (0, j, 0)),
                  BS((None, tn, k), lambda i, j: (0, j, 0))],
        out_specs=(tile, tile, tile), name="ffn_up", compiler_params=_cp("parallel", "parallel"))(hn, wg, wu)


def _ffn_down_bwd(dh, wd, g, u, side=None):
    m, k = dh.shape
    n = wd.shape[1]
    tm = _pick(m, (1408, 384, 256, 128))
    tn = _pick(n, (512, 384, 256, 128))
    grid = (m // tm, n // tn)

    def body(*refs):
        (a_ref, wd_ref, g_ref, u_ref, dg_ref, du_ref), side_refs = _side_split(side, refs, 4, 2, 0)
        _side_start(side, side_refs, grid)
        dact = _dot(a_ref[...].astype(bf16), wd_ref[...], NT)
        gv = g_ref[...]
        sg = jax.nn.sigmoid(gv)
        dg_ref[...] = (dact * u_ref[...] * (sg * (1.0 + gv * (1.0 - sg)))).astype(bf16)
        du_ref[...] = (dact * (gv * sg)).astype(bf16)
        _side_finish(side, side_refs, grid)

    tile = BS((tm, tn), lambda i, j: (i, j))
    s_in, s_in_specs, s_out, s_out_specs, s_scratch = _side_io(side)
    sem = ("parallel", "parallel") if side is None else ("arbitrary", "arbitrary")
    out = pl.pallas_call(
        body, out_shape=(S((m, n), bf16), S((m, n), bf16), *s_out), grid=grid,
        in_specs=[BS((tm, k), lambda i, j: (i, 0)), BS((None, tn, k), lambda i, j: (0, j, 0)), tile, tile, *s_in_specs],
        out_specs=(tile, tile, *s_out_specs), scratch_shapes=s_scratch,
        name="ffn_down_bwd" if side is None else "ffn_down_bwd_comm", compiler_params=_cp(*sem))(dh, wd, g, u, *s_in)
    return out[0], out[1], list(out[2:])


def _lane_lo():
    return lax.broadcasted_iota(jnp.int32, (1, TILE), 1) < HEAD_DIM


def _head_norm(x, g2):
    lo = _lane_lo()
    x2 = x * x
    s_lo = jnp.sum(jnp.where(lo, x2, 0.0), axis=-1, keepdims=True)
    s_hi = jnp.sum(jnp.where(lo, 0.0, x2), axis=-1, keepdims=True)
    r = jnp.where(lo, lax.rsqrt(s_lo * (1.0 / HEAD_DIM) + EPS), lax.rsqrt(s_hi * (1.0 / HEAD_DIM) + EPS))
    xh = x * r
    return xh * g2, xh, r


def _head_norm_bwd(dy, xh, r, g2):
    lo = _lane_lo()
    dgain = jnp.sum(dy * xh, axis=0, keepdims=True)
    dxh = dy * g2
    t = dxh * xh
    t_lo = jnp.sum(jnp.where(lo, t, 0.0), axis=-1, keepdims=True)
    t_hi = jnp.sum(jnp.where(lo, 0.0, t), axis=-1, keepdims=True)
    mt = jnp.where(lo, t_lo, t_hi) * (1.0 / HEAD_DIM)
    return r * (dxh - xh * mt), dgain


def _swa_slopes():
    return [2.0 ** (-8.0 * (h + 1) / SWA_HEADS) for h in range(SWA_HEADS)]


def _swa_masks(i):
    rows = lax.broadcasted_iota(jnp.int32, (TILE, TILE), 0)
    cols = lax.broadcasted_iota(jnp.int32, (TILE, TILE), 1)
    r_pos = i * TILE + rows
    prev = jnp.maximum(i - 1, 0)
    out = []
    for c, base in enumerate((0, prev * TILE, i * TILE)):
        s_pos = base + cols
        dist = r_pos - s_pos
        if c == 0:
            ok = (i >= 1) & (s_pos >= PAD)
        elif c == 1:
            ok = (i >= 2) & (dist < TILE)
        else:
            ok = (dist >= 0) & (s_pos >= PAD)
        out.append((ok, dist.astype(f32)))
    return out, prev


def _swa_geometry(i, group):
    masks, prev = _swa_masks(i)
    bases = (0, pl.multiple_of(prev * TILE, TILE), pl.multiple_of(i * TILE, TILE))
    ok = jnp.concatenate([m[0] for m in masks], axis=1)
    dist = jnp.concatenate([m[1] for m in masks], axis=1)
    return bases, jnp.concatenate([ok] * group, axis=0), jnp.concatenate([dist] * group, axis=0)


def _swa_col(values):
    return jnp.concatenate([jnp.full((TILE, 1), v, f32) for v in values], axis=0)


def _swa_keys(k_ref, v_ref, kg_ref, bases, kvh):
    kcols = slice((kvh // 2) * TILE, (kvh // 2 + 1) * TILE)
    ksel = _lane_lo() if kvh % 2 == 0 else jnp.logical_not(_lane_lo())
    kcat = jnp.concatenate([_head_norm(k_ref[pl.ds(b, TILE), kcols], kg_ref[...])[0] for b in bases], axis=0).astype(bf16)
    vcat = jnp.concatenate([jnp.where(ksel, v_ref[pl.ds(b, TILE), kcols], 0.0) for b in bases], axis=0).astype(bf16)
    return kcat, vcat, ksel


def _swa_stack(blocks, kvh, group, ksel):
    parts = []
    for gi in range(group):
        same = ((kvh * group + gi) % 2) == (kvh % 2)
        parts.append(jnp.where(ksel, blocks[gi] if same else pltpu.roll(blocks[gi], HEAD_DIM, 1), 0.0))
    return jnp.concatenate(parts, axis=0)


def _swa_unstack_add(acc, stacked, kvh, group):
    for gi in range(group):
        h = kvh * group + gi
        part = stacked[gi * TILE:(gi + 1) * TILE]
        acc[h // 2] = acc[h // 2] + (part if (h % 2) == (kvh % 2) else pltpu.roll(part, HEAD_DIM, 1))


def _swa_probs(q4, kcat, ok_g, dist_g, slope_col, sink_col):
    logits = _dot(q4, kcat, NT) * (HEAD_DIM ** -0.5) + jnp.where(ok_g, -slope_col * dist_g, NEG)
    mx = jnp.maximum(jnp.max(logits, axis=-1, keepdims=True), sink_col)
    return jnp.exp(logits - mx), jnp.exp(sink_col - mx)


def _swa_specs(lp, pw):
    qw, kvw = SWA_HEADS * HEAD_DIM, SWA_KV_HEADS * HEAD_DIM
    kidx = qw // kvw
    return (BS((TILE, qw), lambda i: (i, 0)), BS((lp, kvw), lambda i: (0, kidx)), BS((lp, kvw), lambda i: (0, kidx + 1)),
            BS((1, TILE), lambda i: (0, 0)))


def _swa_fwd(proj, qg2, kg2, sinks):
    lp, pw = proj.shape
    qw = SWA_HEADS * HEAD_DIM
    group = SWA_HEADS // SWA_KV_HEADS
    slopes = _swa_slopes()

    def body(sink_ref, q_ref, k_ref, v_ref, qg_ref, kg_ref, o_ref):
        i = pl.program_id(0)
        bases, ok_g, dist_g = _swa_geometry(i, group)
        qn = [_head_norm(q_ref[:, b * TILE:(b + 1) * TILE], qg_ref[...])[0] for b in range(qw // TILE)]
        acc = [jnp.zeros((TILE, TILE), f32) for _ in range(qw // TILE)]
        for kvh in range(SWA_KV_HEADS):
            kcat, vcat, ksel = _swa_keys(k_ref, v_ref, kg_ref, bases, kvh)
            q4 = _swa_stack([qn[(kvh * group + gi) // 2] for gi in range(group)], kvh, group, ksel).astype(bf16)
            sink_col = _swa_col([sink_ref[kvh * group + gi] for gi in range(group)])
            p, e_sink = _swa_probs(q4, kcat, ok_g, dist_g, _swa_col(slopes[kvh * group:(kvh + 1) * group]), sink_col)
            den = e_sink + jnp.sum(p, axis=-1, keepdims=True)
            o = _dot(p.astype(bf16), vcat, NN) / den
            _swa_unstack_add(acc, o, kvh, group)
        valid = (i * TILE + lax.broadcasted_iota(jnp.int32, (TILE, 1), 0)) >= PAD
        for b in range(qw // TILE):
            o_ref[:, b * TILE:(b + 1) * TILE] = jnp.where(valid, acc[b], 0.0)

    q_spec, k_spec, v_spec, vec = _swa_specs(lp, pw)
    return pl.pallas_call(
        body, out_shape=S((lp, qw), f32), grid=(lp // TILE,),
        in_specs=[BS(memory_space=pltpu.SMEM), q_spec, k_spec, v_spec, vec, vec],
        out_specs=BS((TILE, qw), lambda i: (i, 0)), name="swa_fwd", compiler_params=_cp("arbitrary"))(
            sinks, proj, proj, proj, qg2, kg2)


def _swa_bwd(proj, dout, qg2, kg2, sinks):
    lp, pw = proj.shape
    qw, kvw = SWA_HEADS * HEAD_DIM, SWA_KV_HEADS * HEAD_DIM
    group = SWA_HEADS // SWA_KV_HEADS
    scale = HEAD_DIM ** -0.5
    slopes = _swa_slopes()
    nt = lp // TILE

    def body(sink_ref, q_ref, k_ref, v_ref, do_ref, qg_ref, kg_ref, dq_ref, dk_ref, dv_ref, dqg_ref, dkg_ref, ds_ref,
             dkn_acc, dv_acc):
        i = pl.program_id(0)
        lo = _lane_lo()
        lane = lax.broadcasted_iota(jnp.int32, (1, TILE), 1)

        @pl.when(i == 0)
        def _():
            dkn_acc[...] = jnp.zeros_like(dkn_acc)
            dv_acc[...] = jnp.zeros_like(dv_acc)
            dqg_ref[...] = jnp.zeros_like(dqg_ref)
            dkg_ref[...] = jnp.zeros_like(dkg_ref)
            ds_ref[...] = jnp.zeros_like(ds_ref)

        bases, ok_g, dist_g = _swa_geometry(i, group)
        qnorm = [_head_norm(q_ref[:, b * TILE:(b + 1) * TILE], qg_ref[...]) for b in range(qw // TILE)]
        dqn = [jnp.zeros((TILE, TILE), f32) for _ in range(qw // TILE)]
        for kvh in range(SWA_KV_HEADS):
            kcols = slice((kvh // 2) * TILE, (kvh // 2 + 1) * TILE)
            heads = [kvh * group + gi for gi in range(group)]
            kcat, vcat, ksel = _swa_keys(k_ref, v_ref, kg_ref, bases, kvh)
            q4 = _swa_stack([qnorm[h // 2][0] for h in heads], kvh, group, ksel).astype(bf16)
            do4 = _swa_stack([do_ref[:, (h // 2) * TILE:(h // 2 + 1) * TILE].astype(f32) for h in heads], kvh, group, ksel).astype(bf16)
            sink_col = _swa_col([sink_ref[h] for h in heads])
            p, e_sink = _swa_probs(q4, kcat, ok_g, dist_g, _swa_col([slopes[h] for h in heads]), sink_col)
            inv = 1.0 / (e_sink + jnp.sum(p, axis=-1, keepdims=True))
            p = p * inv
            dp = _dot(do4, vcat, NT)
            dsum = jnp.sum(p * dp, axis=-1, keepdims=True)
            d_sink = e_sink * inv * dsum
            for gi, h in enumerate(heads):
                ds_ref[...] += jnp.where(lane == h, -jnp.sum(d_sink[gi * TILE:(gi + 1) * TILE]), 0.0)
            dsc = (p * (dp - dsum) * scale).astype(bf16)
            _swa_unstack_add(dqn, jnp.where(ksel, _dot(dsc, kcat, NN), 0.0), kvh, group)
            dkn = _dot(dsc, q4, TN)
            dvv = _dot(p.astype(bf16), do4, TN)
            for c in range(3):
                dkn_acc[pl.ds(bases[c], TILE), kcols] += dkn[c * TILE:(c + 1) * TILE]
                dv_acc[pl.ds(bases[c], TILE), kcols] += dvv[c * TILE:(c + 1) * TILE]
        for b in range(qw // TILE):
            _, xh, r = qnorm[b]
            dq, dgain = _head_norm_bwd(dqn[b], xh, r, qg_ref[...])
            dq_ref[:, b * TILE:(b + 1) * TILE] = dq.astype(bf16)
            dqg_ref[...] += dgain

        @pl.when(i == nt - 1)
        def _():
            dv_ref[...] = dv_acc[...].astype(bf16)

            def tile_step(t, carry):
                base = pl.multiple_of(t * TILE, TILE)
                for kb in range(kvw // TILE):
                    kcols = slice(kb * TILE, (kb + 1) * TILE)
                    _, xh, r = _head_norm(k_ref[pl.ds(base, TILE), kcols], kg_ref[...])
                    dk, dgain = _head_norm_bwd(dkn_acc[pl.ds(base, TILE), kcols], xh, r, kg_ref[...])
                    dk_ref[pl.ds(base, TILE), kcols] = dk.astype(bf16)
                    dkg_ref[...] += dgain
                return carry

            lax.fori_loop(0, nt, tile_step, 0)

    q_spec, k_spec, v_spec, vec = _swa_specs(lp, pw)
    whole = BS((lp, kvw), lambda i: (0, 0))
    return pl.pallas_call(
        body, out_shape=(S((lp, qw), bf16), S((lp, kvw), bf16), S((lp, kvw), bf16), S((1, TILE), f32), S((1, TILE), f32), S((1, TILE), f32)),
        grid=(nt,),
        in_specs=[BS(memory_space=pltpu.SMEM), q_spec, k_spec, v_spec, BS((TILE, qw), lambda i: (i, 0)), vec, vec],
        out_specs=(BS((TILE, qw), lambda i: (i, 0)), whole, whole, vec, vec, vec),
        scratch_shapes=[pltpu.VMEM((lp, kvw), f32), pltpu.VMEM((lp, kvw), f32)],
        name="swa_bwd", compiler_params=_cp("arbitrary"))(sinks, proj, proj, proj, dout, qg2, kg2)


LOG2E = 1.4426950408889634
SB_PAIRS = 2
ROWS2 = 2 * TILE
SB_ROWS = SB_PAIRS * ROWS2


def _sb_stack(x):
    lo = _lane_lo()
    zero = jnp.zeros((TILE, TILE), x.dtype)
    parts = []
    for p in range(SB_PAIRS):
        xp = x[:, p * TILE:(p + 1) * TILE]
        parts += [jnp.where(lo, xp, zero), jnp.where(lo, zero, xp)]
    return jnp.concatenate(parts, axis=0)


def _sb_unstack(x):
    lo = _lane_lo()
    parts = [jnp.where(lo, x[p * ROWS2:p * ROWS2 + TILE], x[p * ROWS2 + TILE:(p + 1) * ROWS2]) for p in range(SB_PAIRS)]
    return parts[0] if SB_PAIRS == 1 else jnp.concatenate(parts, axis=1)


def _sb_pair_dot(a, b, dims, b_lanes):
    parts = []
    for p in range(SB_PAIRS):
        bp = b[:, p * TILE:(p + 1) * TILE] if b_lanes else b[p * ROWS2:(p + 1) * ROWS2]
        parts.append(_dot(a[p * ROWS2:(p + 1) * ROWS2], bp, dims))
    return parts


def _sb_sum_matrix(after):
    rows = lax.broadcasted_iota(jnp.int32, (ROWS2, ROWS2), 0) & (TILE - 1)
    cols = lax.broadcasted_iota(jnp.int32, (ROWS2, ROWS2), 1)
    tri = (rows > cols) if after else (rows < cols)
    return (tri | (cols >= TILE)).astype(bf16)


def _sb_sums(x, w, sum_mat):
    hi = x.astype(bf16)
    lo = (x - hi.astype(f32)).astype(bf16)
    return [_dot(jnp.concatenate([hi[:, j * TILE:(j + 1) * TILE], lo[:, j * TILE:(j + 1) * TILE]], axis=1), sum_mat, NN)
            for j in range(w)]


def _sb_edge_masks(qi, first, w):
    rows = lax.broadcasted_iota(jnp.int32, (SB_ROWS, TILE), 0) & (TILE - 1)
    cols = lax.broadcasted_iota(jnp.int32, (SB_ROWS, TILE), 1)
    not_pad = first * TILE + cols >= PAD
    before_query = (first + w - 1 - qi) * TILE + cols < rows
    if w == 1:
        return [not_pad & before_query]
    return [not_pad] + [None] * (w - 2) + [before_query]


def _sb_apply(valid, x):
    return x if valid is None else jnp.where(valid, x, 0.0)


def _sb_block(q2, k_ref, first, w, valids, c_rep, after_ones):
    kwin = k_ref[pl.ds(pl.multiple_of(first * TILE, TILE), w * TILE), :]
    z2 = jnp.concatenate(_sb_pair_dot(q2, kwin, NT, True), axis=0) * (HEAD_DIM ** -0.5 * LOG2E)
    ls2 = jnp.minimum(z2, 0.0) - jnp.log2(1.0 + jnp.exp2(jnp.minimum(z2, -z2)))
    m2 = ls2 - z2
    if w == 1:
        m2 = _sb_apply(valids[0], m2)
    else:
        m2 = jnp.concatenate([_sb_apply(valids[j], m2[:, j * TILE:(j + 1) * TILE]) for j in range(w)], axis=1)
    sums = _sb_sums(m2, w, after_ones)
    parts = [None] * w
    for j in reversed(range(w)):
        parts[j] = _sb_apply(valids[j], jnp.exp2(ls2[:, j * TILE:(j + 1) * TILE] + sums[j][:, :TILE] + c_rep))
        c_rep = c_rep + sums[j][:, TILE:]
    return ls2, (parts[0] if w == 1 else jnp.concatenate(parts, axis=1)), c_rep


SB_BLOCKS = (8, 4, 2, 1)


def _sb_walk(qi, block, state, right_to_left):
    n = qi + 1
    big = SB_BLOCKS[0]
    steps = [(n // big, lambda t, big=big: n - big * (t + 1), big)]
    steps += [((n % (2 * s)) // s, lambda t, s=s: n % s, s) for s in SB_BLOCKS[1:]]
    if not right_to_left:
        steps = [(cnt, (lambda t, cnt=cnt, pos=pos: pos(cnt - 1 - t)), s) for cnt, pos, s in reversed(steps)]
    for cnt, pos, s in steps:
        state = lax.fori_loop(0, cnt, lambda t, st, pos=pos, s=s: block(pos(t), s, st), state)
    return state


def _sb_cols(lp, off, single=False):
    mode = dict(pipeline_mode=pl.Buffered(1)) if single else {}
    return BS((lp, SB_PAIRS * TILE), lambda g: (0, off + g), **mode)


def _sb_fwd(proj, side=None):
    lp = proj.shape[0]
    sbw = SB_HEADS * HEAD_DIM
    ngrp = sbw // (SB_PAIRS * TILE)
    nt = lp // TILE

    def body(*refs):
        (q_ref, k_ref, v_ref, o_ref), side_refs = _side_split(side, refs, 3, 1, 0)
        _side_start(side, side_refs, (ngrp,))
        after_ones = _sb_sum_matrix(True)

        def q_step(qi, carry):
            qbase = pl.multiple_of(qi * TILE, TILE)
            q2 = _sb_stack(q_ref[pl.ds(qbase, TILE), :])

            def block(first, w, st):
                c_rep, acc = st
                _, a, c_rep = _sb_block(q2, k_ref, first, w, _sb_edge_masks(qi, first, w), c_rep, after_ones)
                vwin = v_ref[pl.ds(pl.multiple_of(first * TILE, TILE), w * TILE), :]
                return c_rep, acc + jnp.concatenate(_sb_pair_dot(a.astype(bf16), vwin, NN, True), axis=0)

            zero = jnp.zeros((SB_ROWS, TILE), f32)
            _, acc = _sb_walk(qi, block, (zero, zero), True)
            o_ref[pl.ds(qbase, TILE), :] = _sb_unstack(acc)
            return carry

        lax.fori_loop(0, nt, q_step, 0)
        _side_finish(side, side_refs, (ngrp,))

    s_in, s_in_specs, s_out, s_out_specs, s_scratch = _side_io(side)
    out = pl.pallas_call(
        body, out_shape=(S((lp, sbw), f32), *s_out), grid=(ngrp,),
        in_specs=[_sb_cols(lp, 0), _sb_cols(lp, ngrp), _sb_cols(lp, 2 * ngrp), *s_in_specs],
        out_specs=(_sb_cols(lp, 0), *s_out_specs), scratch_shapes=s_scratch,
        name="sb_fwd" if side is None else "sb_fwd_comm",
        compiler_params=_cp("parallel" if side is None else "arbitrary"))(proj, proj, proj, *s_in)
    return out[0], list(out[1:])


def _sb_bwd(proj, dout, side=None):
    lp = proj.shape[0]
    sbw = SB_HEADS * HEAD_DIM
    ngrp = sbw // (SB_PAIRS * TILE)
    nt = lp // TILE
    scale = HEAD_DIM ** -0.5

    def body(*refs):
        (q_ref, k_ref, v_ref, do_ref, dq_ref, dk_ref, dv_ref, dk_acc, dv_acc, e_buf, b_buf), side_refs = _side_split(side, refs, 4, 3, 4)
        _side_start(side, side_refs, (ngrp,))
        dk_acc[...] = jnp.zeros_like(dk_acc)
        dv_acc[...] = jnp.zeros_like(dv_acc)
        after_ones = _sb_sum_matrix(True)
        before_ones = _sb_sum_matrix(False)

        def add_cols(acc_ref, rows, parts):
            for p in range(SB_PAIRS):
                acc_ref[rows, p * TILE:(p + 1) * TILE] += parts[p]

        def q_step(qi, carry):
            qbase = pl.multiple_of(qi * TILE, TILE)
            q2 = _sb_stack(q_ref[pl.ds(qbase, TILE), :])
            do2 = _sb_stack(do_ref[pl.ds(qbase, TILE), :])

            def block_rl(first, w, c_rep):
                ls2, a, c_rep = _sb_block(q2, k_ref, first, w, _sb_edge_masks(qi, first, w), c_rep, after_ones)
                rows = pl.ds(pl.multiple_of(first * TILE, TILE), w * TILE)
                e = jnp.concatenate(_sb_pair_dot(do2, v_ref[rows, :], NT, True), axis=0) * a
                beta = jnp.exp2(ls2)
                for j in range(w):
                    e_buf[first + j] = e[:, j * TILE:(j + 1) * TILE]
                    b_buf[first + j] = beta[:, j * TILE:(j + 1) * TILE]
                add_cols(dv_acc, rows, _sb_pair_dot(a.astype(bf16), do2, TN, False))
                return c_rep

            zero = jnp.zeros((SB_ROWS, TILE), f32)
            _sb_walk(qi, block_rl, zero, True)

            def block_lr(first, w, st):
                e_rep, dq_acc = st
                rows = pl.ds(pl.multiple_of(first * TILE, TILE), w * TILE)
                valids = _sb_edge_masks(qi, first, w)
                es = [e_buf[first + j] for j in range(w)]
                sums = _sb_sums(es[0] if w == 1 else jnp.concatenate(es, axis=1), w, before_ones)
                parts = []
                for j in range(w):
                    beta = b_buf[first + j]
                    parts.append(_sb_apply(valids[j], (es[j] - beta * (es[j] + e_rep + sums[j][:, :TILE])) * scale))
                    e_rep = e_rep + sums[j][:, TILE:]
                dz = (parts[0] if w == 1 else jnp.concatenate(parts, axis=1)).astype(bf16)
                add_cols(dk_acc, rows, _sb_pair_dot(dz, q2, TN, False))
                return e_rep, dq_acc + jnp.concatenate(_sb_pair_dot(dz, k_ref[rows, :], NN, True), axis=0)

            _, dq = _sb_walk(qi, block_lr, (zero, zero), False)
            dq_ref[pl.ds(qbase, TILE), :] = _sb_unstack(dq).astype(bf16)
            return carry

        lax.fori_loop(0, nt, q_step, 0)
        dk_ref[...] = dk_acc[...].astype(bf16)
        dv_ref[...] = dv_acc[...].astype(bf16)
        _side_finish(side, side_refs, (ngrp,))

    wide = SB_PAIRS * TILE
    s_in, s_in_specs, s_out, s_out_specs, s_scratch = _side_io(side)
    out = pl.pallas_call(
        body, out_shape=(S((lp, sbw), bf16),) * 3 + tuple(s_out), grid=(ngrp,),
        in_specs=[_sb_cols(lp, 0, True), _sb_cols(lp, ngrp, True), _sb_cols(lp, 2 * ngrp, True), _sb_cols(lp, 0, True), *s_in_specs],
        out_specs=(_sb_cols(lp, 0, True),) * 3 + tuple(s_out_specs),
        scratch_shapes=[pltpu.VMEM((lp, wide), f32), pltpu.VMEM((lp, wide), f32),
                        pltpu.VMEM((nt, SB_ROWS, TILE), f32), pltpu.VMEM((nt, SB_ROWS, TILE), f32), *s_scratch],
        name="sb_bwd" if side is None else "sb_bwd_comm",
        compiler_params=_cp("parallel" if side is None else "arbitrary"))(proj, proj, proj, dout, *s_in)
    return out[0], out[1], out[2], list(out[3:])


def _place():
    x, y, c = lax.axis_index("x"), lax.axis_index("y"), lax.axis_index("c")
    chips = [(1 - x, y), (x, 1 - y), (1 - x, 1 - y)]
    return x, y, c, chips


def _two_level_gather_body(rows, x_ref, send_sems, recv_sems, local_sem):
    x, y, c, chips = _place()
    me, sibling = (x, y, c), (x, y, 1 - c)

    def copy(k, block, to, src=None):
        return pltpu.make_async_remote_copy(
            src_ref=rows(*block) if src is None else src, dst_ref=rows(*block),
            send_sem=send_sems.at[k], recv_sem=recv_sems.at[k], device_id=to, device_id_type=MESH)

    mine = pltpu.make_async_copy(x_ref, rows(*me), local_sem)
    mine.start()
    first = [copy(0, me, sibling, src=x_ref)]
    first += [copy(1 + j, me, (*chip, c), src=x_ref) for j, chip in enumerate(chips)]
    for cp in first:
        cp.start()
    passed = [copy(4 + j, (*chip, c), sibling) for j, chip in enumerate(chips)]
    for j, chip in enumerate(chips):
        copy(1 + j, (*chip, c), me).wait_recv()
        passed[j].start()
    copy(0, sibling, me).wait_recv()
    for j, chip in enumerate(chips):
        copy(4 + j, (*chip, 1 - c), me).wait_recv()
    for cp in first + passed:
        cp.wait_send()
    mine.wait()


_GATHER_SEMS = [pltpu.SemaphoreType.DMA((7,)), pltpu.SemaphoreType.DMA((7,)), pltpu.SemaphoreType.DMA]


class _Side:
    def __init__(self, inputs, out_shapes, nsem, start, finish):
        self.inputs, self.out_shapes, self.start, self.finish = list(inputs), list(out_shapes), start, finish
        self.scratch = [pltpu.SemaphoreType.DMA((nsem,)), pltpu.SemaphoreType.DMA((nsem,)), pltpu.SemaphoreType.DMA((len(inputs),))]


def _side_io(side):
    if side is None:
        return [], [], [], [], []
    return side.inputs, [ANY] * len(side.inputs), side.out_shapes, [ANY] * len(side.out_shapes), side.scratch


def _side_split(side, refs, n_in, n_out, n_scratch):
    if side is None:
        return refs, None
    si, so = len(side.inputs), len(side.out_shapes)
    a, b, c, d = n_in, n_in + si, n_in + si + n_out, n_in + si + n_out + so
    return refs[:a] + refs[b:c] + refs[d:d + n_scratch], (refs[a:b], refs[c:d], refs[d + n_scratch:])


def _side_first_last(grid):
    first = last = None
    for k, n in enumerate(grid):
        i = pl.program_id(k)
        first = (i == 0) if first is None else first & (i == 0)
        last = (i == n - 1) if last is None else last & (i == n - 1)
    return first, last


def _side_start(side, side_refs, grid):
    if side is not None:
        pl.when(_side_first_last(grid)[0])(lambda: side.start(*side_refs))


def _side_finish(side, side_refs, grid):
    if side is not None:
        pl.when(_side_first_last(grid)[1])(lambda: side.finish(*side_refs))


def _run_side(side, name):
    def body(*refs):
        _, side_refs = _side_split(side, refs, 0, 0, 0)
        side.start(*side_refs)
        side.finish(*side_refs)

    inputs, in_specs, out_shapes, out_specs, scratch = _side_io(side)
    return pl.pallas_call(body, out_shape=tuple(out_shapes), in_specs=in_specs, out_specs=tuple(out_specs),
                          scratch_shapes=scratch, name=name)(*inputs)


def _gather_side(shards, layers):
    def plan(in_refs, out_refs, sems, starting):
        send_sems, recv_sems, local_sems = sems
        x, y, c, chips = _place()
        me, sibling = (x, y, c), (x, y, 1 - c)
        jobs = []
        for a, (x_ref, o_ref) in enumerate(zip(in_refs, out_refs)):
            r = x_ref.shape[1]
            src = x_ref.at[layers[a]]

            def rows(px, py, pc, o_ref=o_ref, r=r):
                return o_ref.at[0, pl.ds(pl.multiple_of((4 * px + 2 * py + pc) * r, 16), r), :]

            def copy(k, block, to, from_shard=False, a=a, rows=rows, src=src):
                return pltpu.make_async_remote_copy(
                    src_ref=src if from_shard else rows(*block), dst_ref=rows(*block),
                    send_sem=send_sems.at[7 * a + k], recv_sem=recv_sems.at[7 * a + k], device_id=to, device_id_type=MESH)

            job = dict(
                mine=pltpu.make_async_copy(src, rows(*me), local_sems.at[a]),
                first=[copy(0, me, sibling, True)] + [copy(1 + j, me, (*chip, c), True) for j, chip in enumerate(chips)])
            if not starting:
                job.update(
                    passed=[copy(4 + j, (*chip, c), sibling) for j, chip in enumerate(chips)],
                    from_chips=[copy(1 + j, (*chip, c), me) for j, chip in enumerate(chips)],
                    from_sibling=[copy(0, sibling, me)] + [copy(4 + j, (*chip, 1 - c), me) for j, chip in enumerate(chips)])
            jobs.append(job)
        return jobs

    def start(*refs):
        for job in plan(*refs, starting=True):
            job["mine"].start()
            for cp in job["first"]:
                cp.start()

    def finish(*refs):
        jobs = plan(*refs, starting=False)
        for job in jobs:
            for arrived, onward in zip(job["from_chips"], job["passed"]):
                arrived.wait_recv()
                onward.start()
        for job in jobs:
            for cp in job["from_sibling"]:
                cp.wait_recv()
            for cp in job["first"] + job["passed"]:
                cp.wait_send()
            job["mine"].wait()

    shapes = [S((1, NDEV * s.shape[1], s.shape[2]), s.dtype) for s in shards]
    return _Side(shards, shapes, 7 * len(shards), start, finish)


def _swap_side(gs):
    def plan(in_refs, out_refs, sems):
        send_sems, recv_sems, _ = sems
        x, y, c, chips = _place()
        copies = []
        for a, (g_ref, o_ref) in enumerate(zip(in_refs, out_refs)):
            r = g_ref.shape[1] // NDEV
            for j, (px, py) in enumerate([(x, y)] + chips):
                d = 4 * px + 2 * py + (1 - c)
                copies.append(pltpu.make_async_remote_copy(
                    src_ref=g_ref.at[:, pl.ds(pl.multiple_of(d * r, 8), r), :], dst_ref=o_ref.at[j],
                    send_sem=send_sems.at[4 * a + j], recv_sem=recv_sems.at[4 * a + j], device_id=(x, y, 1 - c), device_id_type=MESH))
        return copies

    return _exchange_side(gs, [S((4, g.shape[0], g.shape[1] // NDEV, g.shape[2]), f32) for g in gs], 4 * len(gs), plan)


def _ici_side(sends):
    def plan(in_refs, out_refs, sems):
        send_sems, recv_sems, _ = sems
        x, y, c, chips = _place()
        return [pltpu.make_async_remote_copy(
            src_ref=s_ref.at[j], dst_ref=o_ref.at[j], send_sem=send_sems.at[3 * a + j], recv_sem=recv_sems.at[3 * a + j],
            device_id=(*chip, c), device_id_type=MESH)
            for a, (s_ref, o_ref) in enumerate(zip(in_refs, out_refs)) for j, chip in enumerate(chips)]

    return _exchange_side(sends, [S(s.shape, s.dtype) for s in sends], 3 * len(sends), plan)


def _exchange_side(inputs, shapes, nsem, plan):
    def start(*refs):
        for cp in plan(*refs):
            cp.start()

    def finish(*refs):
        copies = plan(*refs)
        for cp in copies:
            cp.wait_recv()
        for cp in copies:
            cp.wait_send()

    return _Side(inputs, shapes, nsem, start, finish)


def _ag_small(shard, name):
    r, w = shard.shape

    def body(x_ref, o_ref, send_sems, recv_sems, local_sem):
        def rows(px, py, pc):
            return o_ref.at[pl.ds(pl.multiple_of((4 * px + 2 * py + pc) * r, 8), r), :]
        _two_level_gather_body(rows, x_ref, send_sems, recv_sems, local_sem)

    vmem = BS(memory_space=pltpu.VMEM)
    return pl.pallas_call(
        body, out_shape=S((NDEV * r, w), shard.dtype), in_specs=[vmem], out_specs=vmem,
        scratch_shapes=_GATHER_SEMS, name=name)(shard)


def _rel_ids():
    x, y, c = lax.axis_index("x"), lax.axis_index("y"), lax.axis_index("c")
    rel = [(x, y), (1 - x, y), (x, 1 - y), (1 - x, 1 - y)]
    return jnp.stack([4 * px + 2 * py + c for px, py in rel]).astype(jnp.int32)


def _rs_partial(g, sib, ids, name):
    ns, rows8, cdim = g.shape
    r = rows8 // NDEV
    tr = r // 2
    nb = r // tr

    def own_body(ids_ref, g_ref, s_ref, o_ref):
        o_ref[...] = g_ref[...] + s_ref[...]

    own = pl.pallas_call(
        own_body, out_shape=S((ns, r, cdim), f32),
        grid_spec=pltpu.PrefetchScalarGridSpec(
            num_scalar_prefetch=1, grid=(ns, nb),
            in_specs=[BS((None, tr, cdim), lambda s, t, ids_ref: (s, ids_ref[0] * nb + t, 0)),
                      BS((None, None, tr, cdim), lambda s, t, ids_ref: (0, s, t, 0))],
            out_specs=BS((None, tr, cdim), lambda s, t, ids_ref: (s, t, 0))),
        name=name + "_own", compiler_params=_cp("parallel", "parallel"))(ids, g, sib)

    def send_body(ids_ref, g_ref, s_ref, o_ref):
        o_ref[...] = (g_ref[...] + s_ref[...]).astype(bf16)

    send = pl.pallas_call(
        send_body, out_shape=S((3, ns, r, cdim), bf16),
        grid_spec=pltpu.PrefetchScalarGridSpec(
            num_scalar_prefetch=1, grid=(3, ns, nb),
            in_specs=[BS((None, tr, cdim), lambda j, s, t, ids_ref: (s, ids_ref[j + 1] * nb + t, 0)),
                      BS((None, None, tr, cdim), lambda j, s, t, ids_ref: (j + 1, s, t, 0))],
            out_specs=BS((None, None, tr, cdim), lambda j, s, t, ids_ref: (j, s, t, 0))),
        name=name + "_send", compiler_params=_cp("parallel", "parallel", "parallel"))(ids, g, sib)
    return own, send


def _prep(w, name):
    nl, r, cdim = w.shape
    blk = BS((None, r, cdim), lambda l: (l, 0, 0))

    def body(w_ref, o_ref):
        o_ref[...] = w_ref[...].astype(bf16)

    return pl.pallas_call(body, out_shape=S(w.shape, bf16), grid=(nl,), in_specs=[blk], out_specs=blk,
                          name=name, compiler_params=_cp("parallel"))(w)


def _adam_math(w, g, m, v):
    m2 = ADAM_B1 * m + (1.0 - ADAM_B1) * g
    v2 = ADAM_B2 * v + (1.0 - ADAM_B2) * (g * g)
    m_hat = m2 / (1.0 - ADAM_B1 ** ADAM_STEP)
    v_hat = v2 / (1.0 - ADAM_B2 ** ADAM_STEP)
    delta = -ADAM_LR * (m_hat / (jnp.sqrt(v_hat) + ADAM_EPS) + ADAM_WD * w)
    return delta, m2, v2


def _adam_big(w, m, v, own, arrived, layer, slot, name, prev=None):
    nl, a, b = w.shape
    ta = _pick(a, (a // 4, a // 2, a)) if (a // 4) % 16 == 0 else a
    w_blk = BS((None, ta, b), lambda t: (layer, t, 0))
    own_blk = BS((None, ta, b), lambda t: (slot, t, 0))
    arr_blk = [BS((None, None, ta, b), lambda t, j=j: (j, slot, t, 0)) for j in range(3)]

    def body(*refs):
        w_ref, m_ref, v_ref, own_ref, a0_ref, a1_ref, a2_ref = refs[:7]
        g_ref, d_ref, m2_ref, v2_ref = refs[-4:]
        g = ((own_ref[...] + a0_ref[...].astype(f32)) + a1_ref[...].astype(f32)) + a2_ref[...].astype(f32)
        delta, m2, v2 = _adam_math(w_ref[...], g, m_ref[...], v_ref[...])
        g_ref[...] = g
        d_ref[...] = delta
        m2_ref[...] = m2
        v2_ref[...] = v2

    threaded = () if prev is None else tuple(prev)
    return pl.pallas_call(
        body, out_shape=(S(w.shape, f32),) * 4, grid=(a // ta,),
        in_specs=[w_blk, w_blk, w_blk, own_blk] + arr_blk + [ANY] * len(threaded), out_specs=(w_blk,) * 4,
        input_output_aliases={7 + k: k for k in range(len(threaded))},
        name=name, compiler_params=_cp("parallel"))(w, m, v, own, arrived, arrived, arrived, *threaded)


def _adam_small(gathered, w, m, v):
    r, wd = w.shape

    def body(g_ref, w_ref, m_ref, v_ref, gs_ref, d_ref, m2_ref, v2_ref):
        g = g_ref[0:r, :]
        for dev in range(1, NDEV):
            g = g + g_ref[dev * r:(dev + 1) * r, :]
        delta, m2, v2 = _adam_math(w_ref[...], g, m_ref[...], v_ref[...])
        gs_ref[...] = g
        d_ref[...] = delta
        m2_ref[...] = m2
        v2_ref[...] = v2

    return pl.pallas_call(body, out_shape=(S((r, wd), f32),) * 4, name="adam_small", compiler_params=_cp())(gathered, w, m, v)


def _pack(arrays, width):
    flat = jnp.concatenate([a.reshape(-1) for a in arrays])
    rows = -(-flat.shape[0] // (8 * width)) * 8
    return jnp.pad(flat, (0, rows * width - flat.shape[0])).reshape(rows, width)


def _unpack(packed, shapes):
    flat = packed.reshape(-1)
    out, off = [], 0
    for shp in shapes:
        n = 1
        for s in shp:
            n *= s
        out.append(flat[off:off + n].reshape(shp))
        off += n
    return out


def kernel(x, meta_tokens, attn_norm_g, w_in, q_norm_g, k_norm_g, attn_sinks, swa_out_g, sb_out_g, w_o, ffn_norm_g, w_gate, w_up, w_down, loss_target, m_meta_tokens, m_attn_norm_g, m_w_in, m_q_norm_g, m_k_norm_g, m_attn_sinks, m_swa_out_g, m_sb_out_g, m_w_o, m_ffn_norm_g, m_w_gate, m_w_up, m_w_down, v_meta_tokens, v_attn_norm_g, v_w_in, v_q_norm_g, v_k_norm_g, v_attn_sinks, v_swa_out_g, v_sb_out_g, v_w_o, v_ffn_norm_g, v_w_gate, v_w_up, v_w_down):
    nl, d = attn_norm_g.shape
    x2, target = x[0], loss_target[0]
    me = 4 * lax.axis_index("x") + 2 * lax.axis_index("y") + lax.axis_index("c")
    ids = _rel_ids()

    meta_all = _ag_small(meta_tokens, "ag_meta")
    meta_full = meta_all.reshape(NDEV, N_META, -1).transpose(1, 0, 2).reshape(N_META, d)
    def t(a):
        return jnp.swapaxes(a, 1, 2)

    w_in_v, w_gate_v, w_up_v = t(w_in), t(w_gate), t(w_up)
    shards = [_prep(w_in_v, "prep_w_in"), _prep(w_o, "prep_w_o"), _prep(w_gate_v, "prep_w_gate"), _prep(w_up_v, "prep_w_up"),
              _prep(w_down, "prep_w_down")]
    attn_w = [_run_side(_gather_side(shards[:2], [0, 0]), "ag_attn_layer0")] + [None] * (nl - 1)

    qg2 = jnp.tile(q_norm_g, (1, TILE // HEAD_DIM))
    kg2 = jnp.tile(k_norm_g, (1, TILE // HEAD_DIM))
    out_g = jnp.concatenate([swa_out_g, sb_out_g], axis=1)
    swa_cols = (SWA_HEADS + 2 * SWA_KV_HEADS) * HEAD_DIM

    h = _embed(x2, meta_full)
    saved, weights = [], []
    for l in range(nl):
        w_in_t, w_o_f = attn_w[l]
        hn = _rmsnorm_fwd([h], attn_norm_g[l:l + 1], "attn_norm")
        proj = _mm(hn, w_in_t, 0, "nt", "proj_swa", cols=(0, swa_cols))
        proj_sb = _mm(hn, w_in_t, 0, "nt", "proj_sb", out_dtype=bf16, cols=(swa_cols, 3 * SB_HEADS * HEAD_DIM))
        out_a = _swa_fwd(proj, qg2[l:l + 1], kg2[l:l + 1], attn_sinks[l])
        if l + 1 < nl:
            out_b, gathered = _sb_fwd(proj_sb, _gather_side(shards[2:] + shards[:2], [l, l, l, l + 1, l + 1]))
            attn_w[l + 1] = gathered[3:]
        else:
            out_b, gathered = _sb_fwd(proj_sb, _gather_side(shards[2:], [l, l, l]))
        w_gate_t, w_up_t, w_down_f = gathered[:3]
        weights.append((w_in_t, w_o_f, w_gate_t, w_up_t, w_down_f))
        mixed = _rmsnorm_fwd([out_a, out_b], out_g[l:l + 1], "out_norm")
        h_mid = _mm(mixed, w_o_f, 0, "nn", "attn_out", res=h)
        hn2 = _rmsnorm_fwd([h_mid], ffn_norm_g[l:l + 1], "ffn_norm")
        g, u, act = _ffn_up(hn2, w_gate_t, w_up_t)
        h_out = _mm(act, w_down_f, 0, "nn", "ffn_down", res=h_mid)
        saved.append((h, hn, proj, proj_sb, out_a, out_b, mixed, h_mid, hn2, g, u, act))
        h = h_out

    loss_tile, dh = _loss_head(h, target)
    loss = lax.psum(loss_tile[0, 0], AXES)

    small = {k: [None] * nl for k in ("attn", "ffn", "out", "q", "k", "sink")}
    pending = None
    red_attn = [None] * nl
    red_ff = [None] * nl

    def attn_partial_sums(slabs, sibs):
        return [_rs_partial(gg, sib, ids, name) for gg, sib, name in zip(slabs, sibs, ("rs_part_w_in", "rs_part_w_o"))]

    for l in reversed(range(nl)):
        w_in_t, w_o_f, w_gate_t, w_up_t, w_down_f = weights[l]
        h_in, hn, proj, proj_sb, out_a, out_b, mixed, h_mid, hn2, g, u, act = saved[l]
        dg, du, sibs = _ffn_down_bwd(dh, w_down_f, g, u, None if pending is None else _swap_side(pending))
        parts = [] if pending is None else attn_partial_sums(pending, sibs)
        g_ff = _mm_tn(act, dh, None, 2, 3, "grad_w_down")
        g_ff = _mm_tn(dg, hn2, g_ff, 0, 3, "grad_w_gate")
        g_ff = _mm_tn(du, hn2, g_ff, 1, 3, "grad_w_up")
        d_gate, (sib_ff,) = _mm(dg, w_gate_t, 0, "nn", "d_ffn_in_gate", side=_swap_side([g_ff]))
        part_ff = _rs_partial(g_ff, sib_ff, ids, "rs_part_w_ff")
        dhn2 = _mm(du, w_up_t, 0, "nn", "d_ffn_in_up", res=d_gate)
        (dh_mid,), small["ffn"][l] = _rmsnorm_bwd(dhn2, [h_mid], ffn_norm_g[l:l + 1], dh, "ffn_norm_bwd")
        dmixed = _mm(dh_mid, w_o_f, 0, "nt", "d_mixed")
        g_o = _mm_tn(mixed, dh_mid, None, 0, 1, "grad_w_o")
        (dout_a, dout_b), small["out"][l] = _rmsnorm_bwd(dmixed, [out_a, out_b], out_g[l:l + 1], None, "out_norm_bwd", bf16)
        dq_b, dk_b, dv_b, arrived = _sb_bwd(proj_sb, dout_b, _ici_side([p[1] for p in parts] + [part_ff[1]]))
        if parts:
            red_attn[l + 1] = ([p[0] for p in parts], arrived[:2])
        red_ff[l] = (part_ff[0], arrived[-1])
        dq_a, dk_a, dv_a, dqg, dkg, dsk = _swa_bwd(proj, dout_a, qg2[l:l + 1], kg2[l:l + 1], attn_sinks[l])
        small["q"][l] = dqg[0, :HEAD_DIM] + dqg[0, HEAD_DIM:]
        small["k"][l] = dkg[0, :HEAD_DIM] + dkg[0, HEAD_DIM:]
        small["sink"][l] = dsk[0, :SWA_HEADS]
        dproj = jnp.concatenate([dq_a, dk_a, dv_a, dq_b, dk_b, dv_b], axis=1)
        dhn = _mm(dproj, w_in_t, 0, "nn", "d_attn_in")
        g_in = _mm_tn(dproj, hn, None, 0, 1, "grad_w_in")
        (dh,), small["attn"][l] = _rmsnorm_bwd(dhn, [h_in], attn_norm_g[l:l + 1], dh_mid, "attn_norm_bwd")
        pending = [g_in, g_o]

    parts = attn_partial_sums(pending, _run_side(_swap_side(pending), "rs_swap_attn_layer0"))
    red_attn[0] = ([p[0] for p in parts], list(_run_side(_ici_side([p[1] for p in parts]), "rs_ici_attn_layer0")))

    grad_x = dh[TILE:][None]
    d_meta = dh[PAD:TILE]

    qw = SWA_HEADS * HEAD_DIM
    d_out = jnp.concatenate(small["out"], axis=0)
    small_grads = [jnp.concatenate(small["attn"], axis=0), jnp.stack(small["q"]), jnp.stack(small["k"]), jnp.stack(small["sink"]),
                   d_out[:, :qw], d_out[:, qw:], jnp.concatenate(small["ffn"], axis=0), d_meta]
    col0 = me * meta_tokens.shape[1]

    def widen(a):
        return lax.dynamic_update_slice(jnp.zeros((N_META, d), f32), a, (0, col0))

    small_w = [attn_norm_g, q_norm_g, k_norm_g, attn_sinks, swa_out_g, sb_out_g, ffn_norm_g]
    small_m = [m_attn_norm_g, m_q_norm_g, m_k_norm_g, m_attn_sinks, m_swa_out_g, m_sb_out_g, m_ffn_norm_g]
    small_v = [v_attn_norm_g, v_q_norm_g, v_k_norm_g, v_attn_sinks, v_swa_out_g, v_sb_out_g, v_ffn_norm_g]
    shapes = [a.shape for a in small_w] + [(N_META, d)]
    packed = _adam_small(_ag_small(_pack(small_grads, d), "ag_small_grads"),
                         _pack(small_w + [widen(meta_tokens)], d), _pack(small_m + [widen(m_meta_tokens)], d),
                         _pack(small_v + [widen(v_meta_tokens)], d))
    small_out = []
    for p in packed:
        parts = _unpack(p, shapes)
        parts[-1] = lax.dynamic_slice(parts[-1], (0, col0), meta_tokens.shape)
        small_out.append(parts)

    big = dict.fromkeys(("w_in", "w_o", "w_gate", "w_up", "w_down"))
    for l in range(nl):
        (own_in, own_o), (arr_in, arr_o) = red_attn[l]
        own_ff, arr_ff = red_ff[l]
        big["w_in"] = _adam_big(w_in_v, t(m_w_in), t(v_w_in), own_in, arr_in, l, 0, "adam_w_in", big["w_in"])
        big["w_o"] = _adam_big(w_o, m_w_o, v_w_o, own_o, arr_o, l, 0, "adam_w_o", big["w_o"])
        big["w_gate"] = _adam_big(w_gate_v, t(m_w_gate), t(v_w_gate), own_ff, arr_ff, l, 0, "adam_w_gate", big["w_gate"])
        big["w_up"] = _adam_big(w_up_v, t(m_w_up), t(v_w_up), own_ff, arr_ff, l, 1, "adam_w_up", big["w_up"])
        big["w_down"] = _adam_big(w_down, m_w_down, v_w_down, own_ff, arr_ff, l, 2, "adam_w_down", big["w_down"])
    for name in ("w_in", "w_gate", "w_up"):
        big[name] = [t(a) for a in big[name]]

    def group(k):
        sm = small_out[k]
        return [sm[7], sm[0], big["w_in"][k], sm[1], sm[2], sm[3], sm[4], sm[5], big["w_o"][k], sm[6],
                big["w_gate"][k], big["w_up"][k], big["w_down"][k]]

    return (loss, grad_x, *group(0), *group(1), *group(2), *group(3))
```

```python
import jax
import jax.numpy as jnp
from jax import lax
from jax.experimental import pallas as pl
from jax.experimental.pallas import tpu as pltpu

f32, bf16 = jnp.float32, jnp.bfloat16
S = jax.ShapeDtypeStruct
BS = pl.BlockSpec

N_META = 16
TILE = 128
PAD = TILE - N_META
HEAD_DIM = 64
SWA_HEADS = 16
SWA_KV_HEADS = 4
SB_HEADS = 16
EPS = 1e-6
NEG = -1e30
ADAM_LR, ADAM_B1, ADAM_B2, ADAM_EPS, ADAM_WD, ADAM_STEP = 0.001, 0.9, 0.999, 1e-08, 0.01, 10
AXES = ("x", "y", "c")
NDEV = 8
VMEM_LIMIT_V7X = 56 * 1024 * 1024
MM_RESIDENT_K = 2048
TN_VMEM_BUDGET = 46 * 1024 * 1024
MESH = pl.DeviceIdType.MESH
ANY = pl.BlockSpec(memory_space=pl.ANY)

NN = ((1,), (0,))
NT = ((1,), (1,))
TN = ((0,), (0,))


def _cp(*sem):
    return pltpu.CompilerParams(dimension_semantics=sem if sem else None, vmem_limit_bytes=VMEM_LIMIT_V7X)


def _pick(n, cands):
    for c in cands:
        if n % c == 0:
            return c
    raise ValueError(f"no tile for {n} in {cands}")


def _dot(a, b, dims):
    return lax.dot_general(a, b, (dims, ((), ())), preferred_element_type=f32)


def _embed(x, meta_full):
    seq, d = x.shape
    meta_pad = jnp.pad(meta_full, ((PAD, 0), (0, 0)))

    def body(x_ref, m_ref, o_ref):
        i = pl.program_id(0)

        @pl.when(i == 0)
        def _():
            o_ref[...] = m_ref[...]

        @pl.when(i > 0)
        def _():
            o_ref[...] = x_ref[...]

    return pl.pallas_call(
        body, out_shape=S((TILE + seq, d), f32), grid=(1 + seq // TILE,),
        in_specs=[BS((TILE, d), lambda i: (jnp.maximum(i - 1, 0), 0)), BS((TILE, d), lambda i: (0, 0))],
        out_specs=BS((TILE, d), lambda i: (i, 0)), name="embed", compiler_params=_cp("arbitrary"))(x, meta_pad)


def _rmsnorm_fwd(xs, gain, name):
    lp = xs[0].shape[0]
    widths = [x.shape[1] for x in xs]
    w = sum(widths)
    tr = _pick(lp, (384, 256, 128))
    n = len(xs)

    def body(*refs):
        g_ref, o_ref = refs[n], refs[n + 1]
        off = 0
        for k in range(n):
            sl = slice(off, off + widths[k])
            off += widths[k]
            xv = refs[k][...]
            r = lax.rsqrt(jnp.mean(xv * xv, axis=-1, keepdims=True) + EPS)
            o_ref[:, sl] = ((xv * r) * g_ref[:, sl]).astype(bf16)

    return pl.pallas_call(
        body, out_shape=S((lp, w), bf16), grid=(lp // tr,),
        in_specs=[BS((tr, wk), lambda i: (i, 0)) for wk in widths] + [BS((1, w), lambda i: (0, 0))],
        out_specs=BS((tr, w), lambda i: (i, 0)), name=name, compiler_params=_cp("parallel"))(*xs, gain)


def _rmsnorm_bwd(dy, xs, gain, res, name, out_dtype=f32):
    lp = xs[0].shape[0]
    widths = [x.shape[1] for x in xs]
    w = sum(widths)
    tr = _pick(lp, (384, 256, 128))
    n = len(xs)
    has_res = res is not None
    assert not has_res or n == 1

    def body(*refs):
        dy_ref, g_ref = refs[0], refs[1 + n]
        res_ref = refs[2 + n] if has_res else None
        dx_refs = refs[2 + n + has_res:2 + 2 * n + has_res]
        dg_ref = refs[2 + 2 * n + has_res]

        @pl.when(pl.program_id(0) == 0)
        def _():
            dg_ref[...] = jnp.zeros_like(dg_ref)

        off = 0
        for k in range(n):
            sl = slice(off, off + widths[k])
            off += widths[k]
            xv = refs[1 + k][...]
            dyv = dy_ref[:, sl]
            r = lax.rsqrt(jnp.mean(xv * xv, axis=-1, keepdims=True) + EPS)
            xh = xv * r
            dg_ref[:, sl] += jnp.sum(dyv * xh, axis=0, keepdims=True)
            dxh = dyv * g_ref[:, sl]
            dx = r * (dxh - xh * jnp.mean(dxh * xh, axis=-1, keepdims=True))
            if has_res:
                dx = dx + res_ref[...]
                refs[-1][...] = dx.astype(bf16)
            dx_refs[k][...] = dx.astype(out_dtype)

    rows = [BS((tr, wk), lambda i: (i, 0)) for wk in widths]
    full = BS((tr, w), lambda i: (i, 0))
    vec = BS((1, w), lambda i: (0, 0))
    args = (dy, *xs, gain) + ((res,) if has_res else ())
    out = pl.pallas_call(
        body, out_shape=tuple(S((lp, wk), out_dtype) for wk in widths) + (S((1, w), f32),) + ((S((lp, w), bf16),) if has_res else ()),
        grid=(lp // tr,),
        in_specs=[full] + rows + [vec] + ([full] if has_res else []), out_specs=tuple(rows) + (vec,) + ((full,) if has_res else ()),
        name=name, compiler_params=_cp("arbitrary"))(*args)
    return (list(out[:n]), out[n], out[n + 1]) if has_res else (list(out[:n]), out[n])


def _loss_head(h, target):
    lp, d = h.shape

    def body(h_ref, t_ref, loss_ref, dh_ref, dhb_ref):
        i = pl.program_id(0)

        @pl.when(i == 0)
        def _():
            loss_ref[...] = jnp.zeros_like(loss_ref)
            dh_ref[...] = jnp.zeros_like(dh_ref)
            dhb_ref[...] = jnp.zeros_like(dhb_ref)

        @pl.when(i > 0)
        def _():
            e = h_ref[...] - t_ref[...]
            dh = e * (1.0 / d)
            dh_ref[...] = dh
            dhb_ref[...] = dh.astype(bf16)
            loss_ref[...] += 0.5 * jnp.sum(jnp.sum(e * e, axis=-1, keepdims=True) * (1.0 / d))

    tile = BS((TILE, d), lambda i: (i, 0))
    return pl.pallas_call(
        body, out_shape=(S((8, 128), f32), S((lp, d), f32), S((lp, d), bf16)), grid=(lp // TILE,),
        in_specs=[tile, BS((TILE, d), lambda i: (jnp.maximum(i - 1, 0), 0))],
        out_specs=(BS((8, 128), lambda i: (0, 0)), tile, tile),
        name="loss_head", compiler_params=_cp("arbitrary"))(h, target)


def _mm(a, b, layer, mode, name, out_dtype=f32, res=None, cols=None, side=None):
    m, k = a.shape
    n = b.shape[2] if mode == "nn" else b.shape[1]
    c0 = 0
    if cols is not None:
        assert mode == "nt"
        c0, n = cols
    tn = _pick(n, (512, 384, 256, 128))
    while c0 % tn:
        tn -= TILE
    assert n % tn == 0
    joff = c0 // tn
    rows_outer = k <= MM_RESIDENT_K
    tm = _pick(m, (1408, 384, 256, 128) if rows_outer else (384, 256, 128))
    dims = NN if mode == "nn" else NT
    has_res = res is not None

    grid = (m // tm, n // tn) if rows_outer else (n // tn, m // tm)

    def body(*refs):
        refs, side_refs = _side_split(side, refs, 2 + has_res, 1, 0)
        _side_start(side, side_refs, grid)
        if has_res:
            a_ref, b_ref, r_ref, o_ref = refs
        else:
            a_ref, b_ref, o_ref = refs
        acc = _dot(a_ref[...].astype(bf16), b_ref[...], dims)
        if has_res:
            acc = acc + r_ref[...]
        o_ref[...] = acc.astype(out_dtype)
        _side_finish(side, side_refs, grid)

    def ij(f):
        return (lambda i, j: f(i, j)) if rows_outer else (lambda j, i: f(i, j))

    if mode == "nn":
        b_spec = BS((None, k, tn), ij(lambda i, j: (layer, 0, j)))
    else:
        b_spec = BS((None, tn, k), ij(lambda i, j: (layer, joff + j, 0)))
    tile = BS((tm, tn), ij(lambda i, j: (i, j)))
    args = (a, b) + ((res,) if has_res else ())
    s_in, s_in_specs, s_out, s_out_specs, s_scratch = _side_io(side)
    sem = ("parallel", "parallel") if side is None else ("arbitrary", "arbitrary")
    out = pl.pallas_call(
        body, out_shape=(S((m, n), out_dtype), *s_out), grid=grid,
        in_specs=[BS((tm, k), ij(lambda i, j: (i, 0))), b_spec] + ([tile] if has_res else []) + s_in_specs,
        out_specs=(tile, *s_out_specs), scratch_shapes=s_scratch,
        name=name if side is None else name + "_comm", compiler_params=_cp(*sem))(*args, *s_in)
    return out[0] if side is None else (out[0], list(out[1:]))


def _mm_tn(a, b, gbuf, slot, nslots, name):
    t, m = a.shape
    n = b.shape[1]
    tk = _pick(t, (1408, 1024, 512, 384, 256, 128))
    nk = t // tk

    def vmem(rows):
        return 2 * (rows * n * 4 + tk * rows * a.dtype.itemsize + tk * n * b.dtype.itemsize)

    tm = next(c for c in (1536, 1408, 1024, 512, 384, 256, 128) if m % c == 0 and (vmem(c) <= TN_VMEM_BUDGET or c == 128))

    def body(*refs):
        a_ref, b_ref, o_ref = refs[0], refs[1], refs[-1]
        kk = pl.program_id(1)

        @pl.when(kk == 0)
        def _():
            o_ref[...] = jnp.zeros_like(o_ref)

        o_ref[...] += _dot(a_ref[...].astype(bf16), b_ref[...].astype(bf16), TN)

    in_specs = [BS((tk, tm), lambda i, kk: (kk, i)), BS((tk, n), lambda i, kk: (kk, 0))]
    args = (a, b)
    aliases = {}
    if gbuf is not None:
        in_specs.append(ANY)
        args = args + (gbuf,)
        aliases = {2: 0}
    return pl.pallas_call(
        body, out_shape=S((nslots, m, n), f32), grid=(m // tm, nk), in_specs=in_specs,
        out_specs=BS((None, tm, n), lambda i, kk: (slot, i, 0)), input_output_aliases=aliases,
        name=name, compiler_params=_cp("parallel", "arbitrary"))(*args)


def _ffn_up(hn, wg, wu):
    m, k = hn.shape
    n = wg.shape[1]
    tm = _pick(m, (1408, 384, 256, 128))
    tn = _pick(n, (512, 384, 256, 128))

    def body(a_ref, wg_ref, wu_ref, g_ref, u_ref, act_ref):
        a = a_ref[...]
        g = _dot(a, wg_ref[...], NT)
        u = _dot(a, wu_ref[...], NT)
        g_ref[...] = g
        u_ref[...] = u
        act_ref[...] = (g * jax.nn.sigmoid(g) * u).astype(bf16)

    tile = BS((tm, tn), lambda i, j: (i, j))
    return pl.pallas_call(
        body, out_shape=(S((m, n), f32), S((m, n), f32), S((m, n), bf16)), grid=(m // tm, n // tn),
        in_specs=[BS((tm, k), lambda i, j: (i, 0)), BS((None, tn, k), lambda i, j: (0, j, 0)),
                  BS((None, tn, k), lambda i, j: (0, j, 0))],
        out_specs=(tile, tile, tile), name="ffn_up", compiler_params=_cp("parallel", "parallel"))(hn, wg, wu)


def _ffn_down_bwd(dh, wd, g, u, side=None):
    m, k = dh.shape
    n = wd.shape[1]
    tm = _pick(m, (1408, 384, 256, 128))
    tn = _pick(n, (512, 384, 256, 128))
    grid = (m // tm, n // tn)

    def body(*refs):
        (a_ref, wd_ref, g_ref, u_ref, dg_ref, du_ref), side_refs = _side_split(side, refs, 4, 2, 0)
        _side_start(side, side_refs, grid)
        dact = _dot(a_ref[...].astype(bf16), wd_ref[...], NT)
        gv = g_ref[...]
        sg = jax.nn.sigmoid(gv)
        dg_ref[...] = (dact * u_ref[...] * (sg * (1.0 + gv * (1.0 - sg)))).astype(bf16)
        du_ref[...] = (dact * (gv * sg)).astype(bf16)
        _side_finish(side, side_refs, grid)

    tile = BS((tm, tn), lambda i, j: (i, j))
    s_in, s_in_specs, s_out, s_out_specs, s_scratch = _side_io(side)
    sem = ("parallel", "parallel") if side is None else ("arbitrary", "arbitrary")
    out = pl.pallas_call(
        body, out_shape=(S((m, n), bf16), S((m, n), bf16), *s_out), grid=grid,
        in_specs=[BS((tm, k), lambda i, j: (i, 0)), BS((None, tn, k), lambda i, j: (0, j, 0)), tile, tile, *s_in_specs],
        out_specs=(tile, tile, *s_out_specs), scratch_shapes=s_scratch,
        name="ffn_down_bwd" if side is None else "ffn_down_bwd_comm", compiler_params=_cp(*sem))(dh, wd, g, u, *s_in)
    return out[0], out[1], list(out[2:])


def _lane_lo():
    return lax.broadcasted_iota(jnp.int32, (1, TILE), 1) < HEAD_DIM


def _head_norm(x, g2):
    lo = _lane_lo()
    x2 = x * x
    s_lo = jnp.sum(jnp.where(lo, x2, 0.0), axis=-1, keepdims=True)
    s_hi = jnp.sum(jnp.where(lo, 0.0, x2), axis=-1, keepdims=True)
    r = jnp.where(lo, lax.rsqrt(s_lo * (1.0 / HEAD_DIM) + EPS), lax.rsqrt(s_hi * (1.0 / HEAD_DIM) + EPS))
    xh = x * r
    return xh * g2, xh, r


def _head_norm_bwd(dy, xh, r, g2):
    lo = _lane_lo()
    dgain = jnp.sum(dy * xh, axis=0, keepdims=True)
    dxh = dy * g2
    t = dxh * xh
    t_lo = jnp.sum(jnp.where(lo, t, 0.0), axis=-1, keepdims=True)
    t_hi = jnp.sum(jnp.where(lo, 0.0, t), axis=-1, keepdims=True)
    mt = jnp.where(lo, t_lo, t_hi) * (1.0 / HEAD_DIM)
    return r * (dxh - xh * mt), dgain


def _swa_slopes():
    return [2.0 ** (-8.0 * (h + 1) / SWA_HEADS) for h in range(SWA_HEADS)]


def _swa_masks(i):
    rows = lax.broadcasted_iota(jnp.int32, (TILE, TILE), 0)
    cols = lax.broadcasted_iota(jnp.int32, (TILE, TILE), 1)
    r_pos = i * TILE + rows
    prev = jnp.maximum(i - 1, 0)
    out = []
    for c, base in enumerate((0, prev * TILE, i * TILE)):
        s_pos = base + cols
        dist = r_pos - s_pos
        if c == 0:
            ok = (i >= 1) & (s_pos >= PAD)
        elif c == 1:
            ok = (i >= 2) & (dist < TILE)
        else:
            ok = (dist >= 0) & (s_pos >= PAD)
        out.append((ok, dist.astype(f32)))
    return out, prev


def _swa_geometry(i, group):
    masks, prev = _swa_masks(i)
    bases = (0, pl.multiple_of(prev * TILE, TILE), pl.multiple_of(i * TILE, TILE))
    ok = jnp.concatenate([m[0] for m in masks], axis=1)
    dist = jnp.concatenate([m[1] for m in masks], axis=1)
    return bases, jnp.concatenate([ok] * group, axis=0), jnp.concatenate([dist] * group, axis=0)


def _swa_col(values):
    return jnp.concatenate([jnp.full((TILE, 1), v, f32) for v in values], axis=0)


def _swa_keys(k_ref, v_ref, kg_ref, bases, kvh):
    kcols = slice((kvh // 2) * TILE, (kvh // 2 + 1) * TILE)
    ksel = _lane_lo() if kvh % 2 == 0 else jnp.logical_not(_lane_lo())
    kcat = jnp.concatenate([_head_norm(k_ref[pl.ds(b, TILE), kcols], kg_ref[...])[0] for b in bases], axis=0).astype(bf16)
    vcat = jnp.concatenate([jnp.where(ksel, v_ref[pl.ds(b, TILE), kcols], 0.0) for b in bases], axis=0).astype(bf16)
    return kcat, vcat, ksel


def _swa_stack(blocks, kvh, group, ksel):
    parts = []
    for gi in range(group):
        same = ((kvh * group + gi) % 2) == (kvh % 2)
        parts.append(jnp.where(ksel, blocks[gi] if same else pltpu.roll(blocks[gi], HEAD_DIM, 1), 0.0))
    return jnp.concatenate(parts, axis=0)


def _swa_unstack_add(acc, stacked, kvh, group):
    for gi in range(group):
        h = kvh * group + gi
        part = stacked[gi * TILE:(gi + 1) * TILE]
        acc[h // 2] = acc[h // 2] + (part if (h % 2) == (kvh % 2) else pltpu.roll(part, HEAD_DIM, 1))


def _swa_probs(q4, kcat, ok_g, dist_g, slope_col, sink_col):
    logits = _dot(q4, kcat, NT) * (HEAD_DIM ** -0.5) + jnp.where(ok_g, -slope_col * dist_g, NEG)
    mx = jnp.maximum(jnp.max(logits, axis=-1, keepdims=True), sink_col)
    return jnp.exp(logits - mx), jnp.exp(sink_col - mx)


def _swa_specs(lp, pw):
    qw, kvw = SWA_HEADS * HEAD_DIM, SWA_KV_HEADS * HEAD_DIM
    kidx = qw // kvw
    return (BS((TILE, qw), lambda i: (i, 0)), BS((lp, kvw), lambda i: (0, kidx)), BS((lp, kvw), lambda i: (0, kidx + 1)),
            BS((1, TILE), lambda i: (0, 0)))


def _swa_fwd(proj, qg2, kg2, sinks):
    lp, pw = proj.shape
    qw = SWA_HEADS * HEAD_DIM
    group = SWA_HEADS // SWA_KV_HEADS
    slopes = _swa_slopes()

    def body(sink_ref, q_ref, k_ref, v_ref, qg_ref, kg_ref, o_ref):
        i = pl.program_id(0)
        bases, ok_g, dist_g = _swa_geometry(i, group)
        qn = [_head_norm(q_ref[:, b * TILE:(b + 1) * TILE], qg_ref[...])[0] for b in range(qw // TILE)]
        acc = [jnp.zeros((TILE, TILE), f32) for _ in range(qw // TILE)]
        for kvh in range(SWA_KV_HEADS):
            kcat, vcat, ksel = _swa_keys(k_ref, v_ref, kg_ref, bases, kvh)
            q4 = _swa_stack([qn[(kvh * group + gi) // 2] for gi in range(group)], kvh, group, ksel).astype(bf16)
            sink_col = _swa_col([sink_ref[kvh * group + gi] for gi in range(group)])
            p, e_sink = _swa_probs(q4, kcat, ok_g, dist_g, _swa_col(slopes[kvh * group:(kvh + 1) * group]), sink_col)
            den = e_sink + jnp.sum(p, axis=-1, keepdims=True)
            o = _dot(p.astype(bf16), vcat, NN) / den
            _swa_unstack_add(acc, o, kvh, group)
        valid = (i * TILE + lax.broadcasted_iota(jnp.int32, (TILE, 1), 0)) >= PAD
        for b in range(qw // TILE):
            o_ref[:, b * TILE:(b + 1) * TILE] = jnp.where(valid, acc[b], 0.0)

    q_spec, k_spec, v_spec, vec = _swa_specs(lp, pw)
    return pl.pallas_call(
        body, out_shape=S((lp, qw), f32), grid=(lp // TILE,),
        in_specs=[BS(memory_space=pltpu.SMEM), q_spec, k_spec, v_spec, vec, vec],
        out_specs=BS((TILE, qw), lambda i: (i, 0)), name="swa_fwd", compiler_params=_cp("arbitrary"))(
            sinks, proj, proj, proj, qg2, kg2)


def _swa_bwd(proj, dout, qg2, kg2, sinks):
    lp, pw = proj.shape
    qw, kvw = SWA_HEADS * HEAD_DIM, SWA_KV_HEADS * HEAD_DIM
    group = SWA_HEADS // SWA_KV_HEADS
    scale = HEAD_DIM ** -0.5
    slopes = _swa_slopes()
    nt = lp // TILE

    def body(sink_ref, q_ref, k_ref, v_ref, do_ref, qg_ref, kg_ref, dq_ref, dk_ref, dv_ref, dqg_ref, dkg_ref, ds_ref,
             dkn_acc, dv_acc):
        i = pl.program_id(0)
        lo = _lane_lo()
        lane = lax.broadcasted_iota(jnp.int32, (1, TILE), 1)

        @pl.when(i == 0)
        def _():
            dkn_acc[...] = jnp.zeros_like(dkn_acc)
            dv_acc[...] = jnp.zeros_like(dv_acc)
            dqg_ref[...] = jnp.zeros_like(dqg_ref)
            dkg_ref[...] = jnp.zeros_like(dkg_ref)
            ds_ref[...] = jnp.zeros_like(ds_ref)

        bases, ok_g, dist_g = _swa_geometry(i, group)
        qnorm = [_head_norm(q_ref[:, b * TILE:(b + 1) * TILE], qg_ref[...]) for b in range(qw // TILE)]
        dqn = [jnp.zeros((TILE, TILE), f32) for _ in range(qw // TILE)]
        for kvh in range(SWA_KV_HEADS):
            kcols = slice((kvh // 2) * TILE, (kvh // 2 + 1) * TILE)
            heads = [kvh * group + gi for gi in range(group)]
            kcat, vcat, ksel = _swa_keys(k_ref, v_ref, kg_ref, bases, kvh)
            q4 = _swa_stack([qnorm[h // 2][0] for h in heads], kvh, group, ksel).astype(bf16)
            do4 = _swa_stack([do_ref[:, (h // 2) * TILE:(h // 2 + 1) * TILE].astype(f32) for h in heads], kvh, group, ksel).astype(bf16)
            sink_col = _swa_col([sink_ref[h] for h in heads])
            p, e_sink = _swa_probs(q4, kcat, ok_g, dist_g, _swa_col([slopes[h] for h in heads]), sink_col)
            inv = 1.0 / (e_sink + jnp.sum(p, axis=-1, keepdims=True))
            p = p * inv
            dp = _dot(do4, vcat, NT)
            dsum = jnp.sum(p * dp, axis=-1, keepdims=True)
            d_sink = e_sink * inv * dsum
            for gi, h in enumerate(heads):
                ds_ref[...] += jnp.where(lane == h, -jnp.sum(d_sink[gi * TILE:(gi + 1) * TILE]), 0.0)
            dsc = (p * (dp - dsum) * scale).astype(bf16)
            _swa_unstack_add(dqn, jnp.where(ksel, _dot(dsc, kcat, NN), 0.0), kvh, group)
            dkn = _dot(dsc, q4, TN)
            dvv = _dot(p.astype(bf16), do4, TN)
            for c in range(3):
                dkn_acc[pl.ds(bases[c], TILE), kcols] += dkn[c * TILE:(c + 1) * TILE]
                dv_acc[pl.ds(bases[c], TILE), kcols] += dvv[c * TILE:(c + 1) * TILE]
        for b in range(qw // TILE):
            _, xh, r = qnorm[b]
            dq, dgain = _head_norm_bwd(dqn[b], xh, r, qg_ref[...])
            dq_ref[:, b * TILE:(b + 1) * TILE] = dq.astype(bf16)
            dqg_ref[...] += dgain

        @pl.when(i == nt - 1)
        def _():
            dv_ref[...] = dv_acc[...].astype(bf16)

            def tile_step(t, carry):
                base = pl.multiple_of(t * TILE, TILE)
                for kb in range(kvw // TILE):
                    kcols = slice(kb * TILE, (kb + 1) * TILE)
                    _, xh, r = _head_norm(k_ref[pl.ds(base, TILE), kcols], kg_ref[...])
                    dk, dgain = _head_norm_bwd(dkn_acc[pl.ds(base, TILE), kcols], xh, r, kg_ref[...])
                    dk_ref[pl.ds(base, TILE), kcols] = dk.astype(bf16)
                    dkg_ref[...] += dgain
                return carry

            lax.fori_loop(0, nt, tile_step, 0)

    q_spec, k_spec, v_spec, vec = _swa_specs(lp, pw)
    whole = BS((lp, kvw), lambda i: (0, 0))
    return pl.pallas_call(
        body, out_shape=(S((lp, qw), bf16), S((lp, kvw), bf16), S((lp, kvw), bf16), S((1, TILE), f32), S((1, TILE), f32), S((1, TILE), f32)),
        grid=(nt,),
        in_specs=[BS(memory_space=pltpu.SMEM), q_spec, k_spec, v_spec, BS((TILE, qw), lambda i: (i, 0)), vec, vec],
        out_specs=(BS((TILE, qw), lambda i: (i, 0)), whole, whole, vec, vec, vec),
        scratch_shapes=[pltpu.VMEM((lp, kvw), f32), pltpu.VMEM((lp, kvw), f32)],
        name="swa_bwd", compiler_params=_cp("arbitrary"))(sinks, proj, proj, proj, dout, qg2, kg2)


LOG2E = 1.4426950408889634
SB_PAIRS = 2
ROWS2 = 2 * TILE
SB_ROWS = SB_PAIRS * ROWS2


def _sb_stack(x):
    lo = _lane_lo()
    zero = jnp.zeros((TILE, TILE), x.dtype)
    parts = []
    for p in range(SB_PAIRS):
        xp = x[:, p * TILE:(p + 1) * TILE]
        parts += [jnp.where(lo, xp, zero), jnp.where(lo, zero, xp)]
    return jnp.concatenate(parts, axis=0)


def _sb_unstack(x):
    lo = _lane_lo()
    parts = [jnp.where(lo, x[p * ROWS2:p * ROWS2 + TILE], x[p * ROWS2 + TILE:(p + 1) * ROWS2]) for p in range(SB_PAIRS)]
    return parts[0] if SB_PAIRS == 1 else jnp.concatenate(parts, axis=1)


def _sb_pair_dot(a, b, dims, b_lanes):
    parts = []
    for p in range(SB_PAIRS):
        bp = b[:, p * TILE:(p + 1) * TILE] if b_lanes else b[p * ROWS2:(p + 1) * ROWS2]
        parts.append(_dot(a[p * ROWS2:(p + 1) * ROWS2], bp, dims))
    return parts


def _sb_sum_matrix(after):
    rows = lax.broadcasted_iota(jnp.int32, (ROWS2, ROWS2), 0) & (TILE - 1)
    cols = lax.broadcasted_iota(jnp.int32, (ROWS2, ROWS2), 1)
    tri = (rows > cols) if after else (rows < cols)
    return (tri | (cols >= TILE)).astype(bf16)


def _sb_sums(x, w, sum_mat):
    hi = x.astype(bf16)
    lo = (x - hi.astype(f32)).astype(bf16)
    return [_dot(jnp.concatenate([hi[:, j * TILE:(j + 1) * TILE], lo[:, j * TILE:(j + 1) * TILE]], axis=1), sum_mat, NN)
            for j in range(w)]


def _sb_edge_masks(qi, first, w):
    rows = lax.broadcasted_iota(jnp.int32, (SB_ROWS, TILE), 0) & (TILE - 1)
    cols = lax.broadcasted_iota(jnp.int32, (SB_ROWS, TILE), 1)
    not_pad = first * TILE + cols >= PAD
    before_query = (first + w - 1 - qi) * TILE + cols < rows
    if w == 1:
        return [not_pad & before_query]
    return [not_pad] + [None] * (w - 2) + [before_query]


def _sb_apply(valid, x):
    return x if valid is None else jnp.where(valid, x, 0.0)


def _sb_block(q2, k_ref, first, w, valids, c_rep, after_ones):
    kwin = k_ref[pl.ds(pl.multiple_of(first * TILE, TILE), w * TILE), :]
    z2 = jnp.concatenate(_sb_pair_dot(q2, kwin, NT, True), axis=0) * (HEAD_DIM ** -0.5 * LOG2E)
    ls2 = jnp.minimum(z2, 0.0) - jnp.log2(1.0 + jnp.exp2(jnp.minimum(z2, -z2)))
    m2 = ls2 - z2
    if w == 1:
        m2 = _sb_apply(valids[0], m2)
    else:
        m2 = jnp.concatenate([_sb_apply(valids[j], m2[:, j * TILE:(j + 1) * TILE]) for j in range(w)], axis=1)
    sums = _sb_sums(m2, w, after_ones)
    parts = [None] * w
    for j in reversed(range(w)):
        parts[j] = _sb_apply(valids[j], jnp.exp2(ls2[:, j * TILE:(j + 1) * TILE] + sums[j][:, :TILE] + c_rep))
        c_rep = c_rep + sums[j][:, TILE:]
    return ls2, (parts[0] if w == 1 else jnp.concatenate(parts, axis=1)), c_rep


SB_BLOCKS = (4, 2, 1)
SB_DEAD = -160.0


def _sb_alive(c_rep):
    return jnp.max(c_rep) > SB_DEAD


def _sb_walk_down(qi, block, state, c_of):
    n = qi + 1
    big = SB_BLOCKS[0]

    def more(carry):
        t, st = carry
        return (t < n // big) & _sb_alive(c_of(st))

    def step(carry):
        t, st = carry
        return t + 1, block(n - big * (t + 1), big, st)

    visited, state = lax.while_loop(more, step, (jnp.int32(0), state))
    counts = [visited]
    for s in SB_BLOCKS[1:]:
        cnt = jnp.where(_sb_alive(c_of(state)), (n % (2 * s)) // s, 0)
        state = lax.fori_loop(0, cnt, lambda t, st, s=s: block(n % s, s, st), state)
        counts.append(cnt)
    return state, counts


def _sb_walk_up(qi, block, state, counts):
    n = qi + 1
    big = SB_BLOCKS[0]
    for s, cnt in reversed(list(zip(SB_BLOCKS[1:], counts[1:]))):
        state = lax.fori_loop(0, cnt, lambda t, st, s=s: block(n % s, s, st), state)
    visited = counts[0]
    return lax.fori_loop(0, visited, lambda i, st: block(n - big * (visited - i), big, st), state)


def _sb_cols(lp, off, single=False):
    mode = dict(pipeline_mode=pl.Buffered(1)) if single else {}
    return BS((lp, SB_PAIRS * TILE), lambda g: (0, off + g), **mode)


def _sb_fwd(proj, side=None):
    lp = proj.shape[0]
    sbw = SB_HEADS * HEAD_DIM
    ngrp = sbw // (SB_PAIRS * TILE)
    nt = lp // TILE

    def body(*refs):
        (q_ref, k_ref, v_ref, o_ref), side_refs = _side_split(side, refs, 3, 1, 0)
        _side_start(side, side_refs, (ngrp,))
        after_ones = _sb_sum_matrix(True)

        def q_step(qi, carry):
            qbase = pl.multiple_of(qi * TILE, TILE)
            q2 = _sb_stack(q_ref[pl.ds(qbase, TILE), :])

            def block(first, w, st):
                c_rep, acc = st
                _, a, c_rep = _sb_block(q2, k_ref, first, w, _sb_edge_masks(qi, first, w), c_rep, after_ones)
                vwin = v_ref[pl.ds(pl.multiple_of(first * TILE, TILE), w * TILE), :]
                return c_rep, acc + jnp.concatenate(_sb_pair_dot(a.astype(bf16), vwin, NN, True), axis=0)

            zero = jnp.zeros((SB_ROWS, TILE), f32)
            (_, acc), _ = _sb_walk_down(qi, block, (zero, zero), lambda st: st[0])
            o_ref[pl.ds(qbase, TILE), :] = _sb_unstack(acc)
            return carry

        lax.fori_loop(0, nt, q_step, 0)
        _side_finish(side, side_refs, (ngrp,))

    s_in, s_in_specs, s_out, s_out_specs, s_scratch = _side_io(side)
    out = pl.pallas_call(
        body, out_shape=(S((lp, sbw), f32), *s_out), grid=(ngrp,),
        in_specs=[_sb_cols(lp, 0), _sb_cols(lp, ngrp), _sb_cols(lp, 2 * ngrp), *s_in_specs],
        out_specs=(_sb_cols(lp, 0), *s_out_specs), scratch_shapes=s_scratch,
        name="sb_fwd" if side is None else "sb_fwd_comm",
        compiler_params=_cp("parallel" if side is None else "arbitrary"))(proj, proj, proj, *s_in)
    return out[0], list(out[1:])


def _sb_bwd(proj, dout, side=None):
    lp = proj.shape[0]
    sbw = SB_HEADS * HEAD_DIM
    ngrp = sbw // (SB_PAIRS * TILE)
    nt = lp // TILE
    scale = HEAD_DIM ** -0.5

    def body(*refs):
        (q_ref, k_ref, v_ref, do_ref, dq_ref, dk_ref, dv_ref, dk_acc, dv_acc, e_buf, b_buf), side_refs = _side_split(side, refs, 4, 3, 4)
        _side_start(side, side_refs, (ngrp,))
        dk_acc[...] = jnp.zeros_like(dk_acc)
        dv_acc[...] = jnp.zeros_like(dv_acc)
        after_ones = _sb_sum_matrix(True)
        before_ones = _sb_sum_matrix(False)

        def add_cols(acc_ref, rows, parts):
            for p in range(SB_PAIRS):
                acc_ref[rows, p * TILE:(p + 1) * TILE] += parts[p]

        def q_step(qi, carry):
            qbase = pl.multiple_of(qi * TILE, TILE)
            q2 = _sb_stack(q_ref[pl.ds(qbase, TILE), :])
            do2 = _sb_stack(do_ref[pl.ds(qbase, TILE), :])

            def block_rl(first, w, c_rep):
                ls2, a, c_rep = _sb_block(q2, k_ref, first, w, _sb_edge_masks(qi, first, w), c_rep, after_ones)
                rows = pl.ds(pl.multiple_of(first * TILE, TILE), w * TILE)
                e = jnp.concatenate(_sb_pair_dot(do2, v_ref[rows, :], NT, True), axis=0) * a
                beta = jnp.exp2(ls2)
                for j in range(w):
                    e_buf[first + j] = e[:, j * TILE:(j + 1) * TILE]
                    b_buf[first + j] = beta[:, j * TILE:(j + 1) * TILE]
                add_cols(dv_acc, rows, _sb_pair_dot(a.astype(bf16), do2, TN, False))
                return c_rep

            zero = jnp.zeros((SB_ROWS, TILE), f32)
            _, counts = _sb_walk_down(qi, block_rl, zero, lambda c_rep: c_rep)

            def block_lr(first, w, st):
                e_rep, dq_acc = st
                rows = pl.ds(pl.multiple_of(first * TILE, TILE), w * TILE)
                valids = _sb_edge_masks(qi, first, w)
                es = [e_buf[first + j] for j in range(w)]
                sums = _sb_sums(es[0] if w == 1 else jnp.concatenate(es, axis=1), w, before_ones)
                parts = []
                for j in range(w):
                    beta = b_buf[first + j]
                    parts.append(_sb_apply(valids[j], (es[j] - beta * (es[j] + e_rep + sums[j][:, :TILE])) * scale))
                    e_rep = e_rep + sums[j][:, TILE:]
                dz = (parts[0] if w == 1 else jnp.concatenate(parts, axis=1)).astype(bf16)
                add_cols(dk_acc, rows, _sb_pair_dot(dz, q2, TN, False))
                return e_rep, dq_acc + jnp.concatenate(_sb_pair_dot(dz, k_ref[rows, :], NN, True), axis=0)

            _, dq = _sb_walk_up(qi, block_lr, (zero, zero), counts)
            dq_ref[pl.ds(qbase, TILE), :] = _sb_unstack(dq).astype(bf16)
            return carry

        lax.fori_loop(0, nt, q_step, 0)
        dk_ref[...] = dk_acc[...].astype(bf16)
        dv_ref[...] = dv_acc[...].astype(bf16)
        _side_finish(side, side_refs, (ngrp,))

    wide = SB_PAIRS * TILE
    s_in, s_in_specs, s_out, s_out_specs, s_scratch = _side_io(side)
    out = pl.pallas_call(
        body, out_shape=(S((lp, sbw), bf16),) * 3 + tuple(s_out), grid=(ngrp,),
        in_specs=[_sb_cols(lp, 0, True), _sb_cols(lp, ngrp, True), _sb_cols(lp, 2 * ngrp, True), _sb_cols(lp, 0, True), *s_in_specs],
        out_specs=(_sb_cols(lp, 0, True),) * 3 + tuple(s_out_specs),
        scratch_shapes=[pltpu.VMEM((lp, wide), f32), pltpu.VMEM((lp, wide), f32),
                        pltpu.VMEM((nt, SB_ROWS, TILE), f32), pltpu.VMEM((nt, SB_ROWS, TILE), f32), *s_scratch],
        name="sb_bwd" if side is None else "sb_bwd_comm",
        compiler_params=_cp("parallel" if side is None else "arbitrary"))(proj, proj, proj, dout, *s_in)
    return out[0], out[1], out[2], list(out[3:])


def _place():
    x, y, c = lax.axis_index("x"), lax.axis_index("y"), lax.axis_index("c")
    chips = [(1 - x, y), (x, 1 - y), (1 - x, 1 - y)]
    return x, y, c, chips


def _two_level_gather_body(rows, x_ref, send_sems, recv_sems, local_sem):
    x, y, c, chips = _place()
    me, sibling = (x, y, c), (x, y, 1 - c)

    def copy(k, block, to, src=None):
        return pltpu.make_async_remote_copy(
            src_ref=rows(*block) if src is None else src, dst_ref=rows(*block),
            send_sem=send_sems.at[k], recv_sem=recv_sems.at[k], device_id=to, device_id_type=MESH)

    mine = pltpu.make_async_copy(x_ref, rows(*me), local_sem)
    mine.start()
    first = [copy(0, me, sibling, src=x_ref)]
    first += [copy(1 + j, me, (*chip, c), src=x_ref) for j, chip in enumerate(chips)]
    for cp in first:
        cp.start()
    passed = [copy(4 + j, (*chip, c), sibling) for j, chip in enumerate(chips)]
    for j, chip in enumerate(chips):
        copy(1 + j, (*chip, c), me).wait_recv()
        passed[j].start()
    copy(0, sibling, me).wait_recv()
    for j, chip in enumerate(chips):
        copy(4 + j, (*chip, 1 - c), me).wait_recv()
    for cp in first + passed:
        cp.wait_send()
    mine.wait()


_GATHER_SEMS = [pltpu.SemaphoreType.DMA((7,)), pltpu.SemaphoreType.DMA((7,)), pltpu.SemaphoreType.DMA]


class _Side:
    def __init__(self, inputs, out_shapes, nsem, start, finish):
        self.inputs, self.out_shapes, self.start, self.finish = list(inputs), list(out_shapes), start, finish
        self.scratch = [pltpu.SemaphoreType.DMA((nsem,)), pltpu.SemaphoreType.DMA((nsem,)), pltpu.SemaphoreType.DMA((len(inputs),))]


def _side_io(side):
    if side is None:
        return [], [], [], [], []
    return side.inputs, [ANY] * len(side.inputs), side.out_shapes, [ANY] * len(side.out_shapes), side.scratch


def _side_split(side, refs, n_in, n_out, n_scratch):
    if side is None:
        return refs, None
    si, so = len(side.inputs), len(side.out_shapes)
    a, b, c, d = n_in, n_in + si, n_in + si + n_out, n_in + si + n_out + so
    return refs[:a] + refs[b:c] + refs[d:d + n_scratch], (refs[a:b], refs[c:d], refs[d + n_scratch:])


def _side_first_last(grid):
    first = last = None
    for k, n in enumerate(grid):
        i = pl.program_id(k)
        first = (i == 0) if first is None else first & (i == 0)
        last = (i == n - 1) if last is None else last & (i == n - 1)
    return first, last


def _side_start(side, side_refs, grid):
    if side is not None:
        pl.when(_side_first_last(grid)[0])(lambda: side.start(*side_refs))


def _side_finish(side, side_refs, grid):
    if side is not None:
        pl.when(_side_first_last(grid)[1])(lambda: side.finish(*side_refs))


def _run_side(side, name):
    def body(*refs):
        _, side_refs = _side_split(side, refs, 0, 0, 0)
        side.start(*side_refs)
        side.finish(*side_refs)

    inputs, in_specs, out_shapes, out_specs, scratch = _side_io(side)
    return pl.pallas_call(body, out_shape=tuple(out_shapes), in_specs=in_specs, out_specs=tuple(out_specs),
                          scratch_shapes=scratch, name=name)(*inputs)


def _gather_side(shards, layers):
    def plan(in_refs, out_refs, sems, starting):
        send_sems, recv_sems, local_sems = sems
        x, y, c, chips = _place()
        me, sibling = (x, y, c), (x, y, 1 - c)
        jobs = []
        for a, (x_ref, o_ref) in enumerate(zip(in_refs, out_refs)):
            r = x_ref.shape[1]
            src = x_ref.at[layers[a]]

            def rows(px, py, pc, o_ref=o_ref, r=r):
                return o_ref.at[0, pl.ds(pl.multiple_of((4 * px + 2 * py + pc) * r, 16), r), :]

            def copy(k, block, to, from_shard=False, a=a, rows=rows, src=src):
                return pltpu.make_async_remote_copy(
                    src_ref=src if from_shard else rows(*block), dst_ref=rows(*block),
                    send_sem=send_sems.at[7 * a + k], recv_sem=recv_sems.at[7 * a + k], device_id=to, device_id_type=MESH)

            job = dict(
                mine=pltpu.make_async_copy(src, rows(*me), local_sems.at[a]),
                first=[copy(0, me, sibling, True)] + [copy(1 + j, me, (*chip, c), True) for j, chip in enumerate(chips)])
            if not starting:
                job.update(
                    passed=[copy(4 + j, (*chip, c), sibling) for j, chip in enumerate(chips)],
                    from_chips=[copy(1 + j, (*chip, c), me) for j, chip in enumerate(chips)],
                    from_sibling=[copy(0, sibling, me)] + [copy(4 + j, (*chip, 1 - c), me) for j, chip in enumerate(chips)])
            jobs.append(job)
        return jobs

    def start(*refs):
        for job in plan(*refs, starting=True):
            job["mine"].start()
            for cp in job["first"]:
                cp.start()

    def finish(*refs):
        jobs = plan(*refs, starting=False)
        for job in jobs:
            for arrived, onward in zip(job["from_chips"], job["passed"]):
                arrived.wait_recv()
                onward.start()
        for job in jobs:
            for cp in job["from_sibling"]:
                cp.wait_recv()
            for cp in job["first"] + job["passed"]:
                cp.wait_send()
            job["mine"].wait()

    shapes = [S((1, NDEV * s.shape[1], s.shape[2]), s.dtype) for s in shards]
    return _Side(shards, shapes, 7 * len(shards), start, finish)


def _swap_side(gs):
    def plan(in_refs, out_refs, sems):
        send_sems, recv_sems, _ = sems
        x, y, c, chips = _place()
        copies = []
        for a, (g_ref, o_ref) in enumerate(zip(in_refs, out_refs)):
            r = g_ref.shape[1] // NDEV
            for j, (px, py) in enumerate([(x, y)] + chips):
                d = 4 * px + 2 * py + (1 - c)
                copies.append(pltpu.make_async_remote_copy(
                    src_ref=g_ref.at[:, pl.ds(pl.multiple_of(d * r, 8), r), :], dst_ref=o_ref.at[j],
                    send_sem=send_sems.at[4 * a + j], recv_sem=recv_sems.at[4 * a + j], device_id=(x, y, 1 - c), device_id_type=MESH))
        return copies

    return _exchange_side(gs, [S((4, g.shape[0], g.shape[1] // NDEV, g.shape[2]), f32) for g in gs], 4 * len(gs), plan)


def _ici_side(sends):
    def plan(in_refs, out_refs, sems):
        send_sems, recv_sems, _ = sems
        x, y, c, chips = _place()
        return [pltpu.make_async_remote_copy(
            src_ref=s_ref.at[j], dst_ref=o_ref.at[j], send_sem=send_sems.at[3 * a + j], recv_sem=recv_sems.at[3 * a + j],
            device_id=(*chip, c), device_id_type=MESH)
            for a, (s_ref, o_ref) in enumerate(zip(in_refs, out_refs)) for j, chip in enumerate(chips)]

    return _exchange_side(sends, [S(s.shape, s.dtype) for s in sends], 3 * len(sends), plan)


def _exchange_side(inputs, shapes, nsem, plan):
    def start(*refs):
        for cp in plan(*refs):
            cp.start()

    def finish(*refs):
        copies = plan(*refs)
        for cp in copies:
            cp.wait_recv()
        for cp in copies:
            cp.wait_send()

    return _Side(inputs, shapes, nsem, start, finish)


def _ag_small(shard, name):
    r, w = shard.shape

    def body(x_ref, o_ref, send_sems, recv_sems, local_sem):
        def rows(px, py, pc):
            return o_ref.at[pl.ds(pl.multiple_of((4 * px + 2 * py + pc) * r, 8), r), :]
        _two_level_gather_body(rows, x_ref, send_sems, recv_sems, local_sem)

    vmem = BS(memory_space=pltpu.VMEM)
    return pl.pallas_call(
        body, out_shape=S((NDEV * r, w), shard.dtype), in_specs=[vmem], out_specs=vmem,
        scratch_shapes=_GATHER_SEMS, name=name)(shard)


def _rel_ids():
    x, y, c = lax.axis_index("x"), lax.axis_index("y"), lax.axis_index("c")
    rel = [(x, y), (1 - x, y), (x, 1 - y), (1 - x, 1 - y)]
    return jnp.stack([4 * px + 2 * py + c for px, py in rel]).astype(jnp.int32)


def _rs_partial(g, sib, ids, name):
    ns, rows8, cdim = g.shape
    r = rows8 // NDEV
    tr = r // 2
    nb = r // tr

    def own_body(ids_ref, g_ref, s_ref, o_ref):
        o_ref[...] = g_ref[...] + s_ref[...]

    own = pl.pallas_call(
        own_body, out_shape=S((ns, r, cdim), f32),
        grid_spec=pltpu.PrefetchScalarGridSpec(
            num_scalar_prefetch=1, grid=(ns, nb),
            in_specs=[BS((None, tr, cdim), lambda s, t, ids_ref: (s, ids_ref[0] * nb + t, 0)),
                      BS((None, None, tr, cdim), lambda s, t, ids_ref: (0, s, t, 0))],
            out_specs=BS((None, tr, cdim), lambda s, t, ids_ref: (s, t, 0))),
        name=name + "_own", compiler_params=_cp("parallel", "parallel"))(ids, g, sib)

    def send_body(ids_ref, g_ref, s_ref, o_ref):
        o_ref[...] = (g_ref[...] + s_ref[...]).astype(bf16)

    send = pl.pallas_call(
        send_body, out_shape=S((3, ns, r, cdim), bf16),
        grid_spec=pltpu.PrefetchScalarGridSpec(
            num_scalar_prefetch=1, grid=(3, ns, nb),
            in_specs=[BS((None, tr, cdim), lambda j, s, t, ids_ref: (s, ids_ref[j + 1] * nb + t, 0)),
                      BS((None, None, tr, cdim), lambda j, s, t, ids_ref: (j + 1, s, t, 0))],
            out_specs=BS((None, None, tr, cdim), lambda j, s, t, ids_ref: (j, s, t, 0))),
        name=name + "_send", compiler_params=_cp("parallel", "parallel", "parallel"))(ids, g, sib)
    return own, send


def _prep(w, name):
    nl, r, cdim = w.shape
    blk = BS((None, r, cdim), lambda l: (l, 0, 0))

    def body(w_ref, o_ref):
        o_ref[...] = w_ref[...].astype(bf16)

    return pl.pallas_call(body, out_shape=S(w.shape, bf16), grid=(nl,), in_specs=[blk], out_specs=blk,
                          name=name, compiler_params=_cp("parallel"))(w)


def _adam_math(w, g, m, v):
    m2 = ADAM_B1 * m + (1.0 - ADAM_B1) * g
    v2 = ADAM_B2 * v + (1.0 - ADAM_B2) * (g * g)
    m_hat = m2 / (1.0 - ADAM_B1 ** ADAM_STEP)
    v_hat = v2 / (1.0 - ADAM_B2 ** ADAM_STEP)
    delta = -ADAM_LR * (m_hat / (jnp.sqrt(v_hat) + ADAM_EPS) + ADAM_WD * w)
    return delta, m2, v2


def _adam_big(w, m, v, own, arrived, layer, slot, name, prev=None):
    nl, a, b = w.shape
    ta = _pick(a, (a // 4, a // 2, a)) if (a // 4) % 16 == 0 else a
    w_blk = BS((None, ta, b), lambda t: (layer, t, 0))
    own_blk = BS((None, ta, b), lambda t: (slot, t, 0))
    arr_blk = [BS((None, None, ta, b), lambda t, j=j: (j, slot, t, 0)) for j in range(3)]

    def body(*refs):
        w_ref, m_ref, v_ref, own_ref, a0_ref, a1_ref, a2_ref = refs[:7]
        g_ref, d_ref, m2_ref, v2_ref = refs[-4:]
        g = ((own_ref[...] + a0_ref[...].astype(f32)) + a1_ref[...].astype(f32)) + a2_ref[...].astype(f32)
        delta, m2, v2 = _adam_math(w_ref[...], g, m_ref[...], v_ref[...])
        g_ref[...] = g
        d_ref[...] = delta
        m2_ref[...] = m2
        v2_ref[...] = v2

    threaded = () if prev is None else tuple(prev)
    return pl.pallas_call(
        body, out_shape=(S(w.shape, f32),) * 4, grid=(a // ta,),
        in_specs=[w_blk, w_blk, w_blk, own_blk] + arr_blk + [ANY] * len(threaded), out_specs=(w_blk,) * 4,
        input_output_aliases={7 + k: k for k in range(len(threaded))},
        name=name, compiler_params=_cp("parallel"))(w, m, v, own, arrived, arrived, arrived, *threaded)


def _adam_small(gathered, w, m, v):
    r, wd = w.shape

    def body(g_ref, w_ref, m_ref, v_ref, gs_ref, d_ref, m2_ref, v2_ref):
        g = g_ref[0:r, :]
        for dev in range(1, NDEV):
            g = g + g_ref[dev * r:(dev + 1) * r, :]
        delta, m2, v2 = _adam_math(w_ref[...], g, m_ref[...], v_ref[...])
        gs_ref[...] = g
        d_ref[...] = delta
        m2_ref[...] = m2
        v2_ref[...] = v2

    return pl.pallas_call(body, out_shape=(S((r, wd), f32),) * 4, name="adam_small", compiler_params=_cp())(gathered, w, m, v)


def _pack(arrays, width):
    flat = jnp.concatenate([a.reshape(-1) for a in arrays])
    rows = -(-flat.shape[0] // (8 * width)) * 8
    return jnp.pad(flat, (0, rows * width - flat.shape[0])).reshape(rows, width)


def _unpack(packed, shapes):
    flat = packed.reshape(-1)
    out, off = [], 0
    for shp in shapes:
        n = 1
        for s in shp:
            n *= s
        out.append(flat[off:off + n].reshape(shp))
        off += n
    return out


def kernel(x, meta_tokens, attn_norm_g, w_in, q_norm_g, k_norm_g, attn_sinks, swa_out_g, sb_out_g, w_o, ffn_norm_g, w_gate, w_up, w_down, loss_target, m_meta_tokens, m_attn_norm_g, m_w_in, m_q_norm_g, m_k_norm_g, m_attn_sinks, m_swa_out_g, m_sb_out_g, m_w_o, m_ffn_norm_g, m_w_gate, m_w_up, m_w_down, v_meta_tokens, v_attn_norm_g, v_w_in, v_q_norm_g, v_k_norm_g, v_attn_sinks, v_swa_out_g, v_sb_out_g, v_w_o, v_ffn_norm_g, v_w_gate, v_w_up, v_w_down):
    nl, d = attn_norm_g.shape
    x2, target = x[0], loss_target[0]
    me = 4 * lax.axis_index("x") + 2 * lax.axis_index("y") + lax.axis_index("c")
    ids = _rel_ids()

    meta_all = _ag_small(meta_tokens, "ag_meta")
    meta_full = meta_all.reshape(NDEV, N_META, -1).transpose(1, 0, 2).reshape(N_META, d)
    def t(a):
        return jnp.swapaxes(a, 1, 2)

    w_in_v, w_gate_v, w_up_v = t(w_in), t(w_gate), t(w_up)
    shards = [_prep(w_in_v, "prep_w_in"), _prep(w_o, "prep_w_o"), _prep(w_gate_v, "prep_w_gate"), _prep(w_up_v, "prep_w_up"),
              _prep(w_down, "prep_w_down")]
    attn_w = [_run_side(_gather_side(shards[:2], [0, 0]), "ag_attn_layer0")] + [None] * (nl - 1)

    qg2 = jnp.tile(q_norm_g, (1, TILE // HEAD_DIM))
    kg2 = jnp.tile(k_norm_g, (1, TILE // HEAD_DIM))
    out_g = jnp.concatenate([swa_out_g, sb_out_g], axis=1)
    swa_cols = (SWA_HEADS + 2 * SWA_KV_HEADS) * HEAD_DIM

    h = _embed(x2, meta_full)
    saved, weights = [], []
    for l in range(nl):
        w_in_t, w_o_f = attn_w[l]
        hn = _rmsnorm_fwd([h], attn_norm_g[l:l + 1], "attn_norm")
        proj = _mm(hn, w_in_t, 0, "nt", "proj_swa", cols=(0, swa_cols))
        proj_sb = _mm(hn, w_in_t, 0, "nt", "proj_sb", out_dtype=bf16, cols=(swa_cols, 3 * SB_HEADS * HEAD_DIM))
        out_a = _swa_fwd(proj, qg2[l:l + 1], kg2[l:l + 1], attn_sinks[l])
        if l + 1 < nl:
            out_b, gathered = _sb_fwd(proj_sb, _gather_side(shards[2:] + shards[:2], [l, l, l, l + 1, l + 1]))
            attn_w[l + 1] = gathered[3:]
        else:
            out_b, gathered = _sb_fwd(proj_sb, _gather_side(shards[2:], [l, l, l]))
        w_gate_t, w_up_t, w_down_f = gathered[:3]
        weights.append((w_in_t, w_o_f, w_gate_t, w_up_t, w_down_f))
        mixed = _rmsnorm_fwd([out_a, out_b], out_g[l:l + 1], "out_norm")
        h_mid = _mm(mixed, w_o_f, 0, "nn", "attn_out", res=h)
        hn2 = _rmsnorm_fwd([h_mid], ffn_norm_g[l:l + 1], "ffn_norm")
        g, u, act = _ffn_up(hn2, w_gate_t, w_up_t)
        h_out = _mm(act, w_down_f, 0, "nn", "ffn_down", res=h_mid)
        saved.append((h, hn, proj, proj_sb, out_a, out_b, mixed, h_mid, hn2, g, u, act))
        h = h_out

    loss_tile, dh, dh_b = _loss_head(h, target)
    loss = lax.psum(loss_tile[0, 0], AXES)

    small = {k: [None] * nl for k in ("attn", "ffn", "out", "q", "k", "sink")}
    pending = None
    red_attn = [None] * nl
    red_ff = [None] * nl

    def attn_partial_sums(slabs, sibs):
        return [_rs_partial(gg, sib, ids, name) for gg, sib, name in zip(slabs, sibs, ("rs_part_w_in", "rs_part_w_o"))]

    for l in reversed(range(nl)):
        w_in_t, w_o_f, w_gate_t, w_up_t, w_down_f = weights[l]
        h_in, hn, proj, proj_sb, out_a, out_b, mixed, h_mid, hn2, g, u, act = saved[l]
        dg, du, sibs = _ffn_down_bwd(dh_b, w_down_f, g, u, None if pending is None else _swap_side(pending))
        parts = [] if pending is None else attn_partial_sums(pending, sibs)
        g_ff = _mm_tn(act, dh_b, None, 2, 3, "grad_w_down")
        g_ff = _mm_tn(dg, hn2, g_ff, 0, 3, "grad_w_gate")
        g_ff = _mm_tn(du, hn2, g_ff, 1, 3, "grad_w_up")
        d_gate, (sib_ff,) = _mm(dg, w_gate_t, 0, "nn", "d_ffn_in_gate", side=_swap_side([g_ff]))
        part_ff = _rs_partial(g_ff, sib_ff, ids, "rs_part_w_ff")
        dhn2 = _mm(du, w_up_t, 0, "nn", "d_ffn_in_up", res=d_gate)
        (dh_mid,), small["ffn"][l], dh_mid_b = _rmsnorm_bwd(dhn2, [h_mid], ffn_norm_g[l:l + 1], dh, "ffn_norm_bwd")
        dmixed = _mm(dh_mid_b, w_o_f, 0, "nt", "d_mixed")
        g_o = _mm_tn(mixed, dh_mid_b, None, 0, 1, "grad_w_o")
        (dout_a, dout_b), small["out"][l] = _rmsnorm_bwd(dmixed, [out_a, out_b], out_g[l:l + 1], None, "out_norm_bwd", bf16)
        dq_b, dk_b, dv_b, arrived = _sb_bwd(proj_sb, dout_b, _ici_side([p[1] for p in parts] + [part_ff[1]]))
        if parts:
            red_attn[l + 1] = ([p[0] for p in parts], arrived[:2])
        red_ff[l] = (part_ff[0], arrived[-1])
        dq_a, dk_a, dv_a, dqg, dkg, dsk = _swa_bwd(proj, dout_a, qg2[l:l + 1], kg2[l:l + 1], attn_sinks[l])
        small["q"][l] = dqg[0, :HEAD_DIM] + dqg[0, HEAD_DIM:]
        small["k"][l] = dkg[0, :HEAD_DIM] + dkg[0, HEAD_DIM:]
        small["sink"][l] = dsk[0, :SWA_HEADS]
        dproj = jnp.concatenate([dq_a, dk_a, dv_a, dq_b, dk_b, dv_b], axis=1)
        dhn = _mm(dproj, w_in_t, 0, "nn", "d_attn_in")
        g_in = _mm_tn(dproj, hn, None, 0, 1, "grad_w_in")
        (dh,), small["attn"][l], dh_b = _rmsnorm_bwd(dhn, [h_in], attn_norm_g[l:l + 1], dh_mid, "attn_norm_bwd")
        pending = [g_in, g_o]

    parts = attn_partial_sums(pending, _run_side(_swap_side(pending), "rs_swap_attn_layer0"))
    red_attn[0] = ([p[0] for p in parts], list(_run_side(_ici_side([p[1] for p in parts]), "rs_ici_attn_layer0")))

    grad_x = dh[TILE:][None]
    d_meta = dh[PAD:TILE]

    qw = SWA_HEADS * HEAD_DIM
    d_out = jnp.concatenate(small["out"], axis=0)
    small_grads = [jnp.concatenate(small["attn"], axis=0), jnp.stack(small["q"]), jnp.stack(small["k"]), jnp.stack(small["sink"]),
                   d_out[:, :qw], d_out[:, qw:], jnp.concatenate(small["ffn"], axis=0), d_meta]
    col0 = me * meta_tokens.shape[1]

    def widen(a):
        return lax.dynamic_update_slice(jnp.zeros((N_META, d), f32), a, (0, col0))

    small_w = [attn_norm_g, q_norm_g, k_norm_g, attn_sinks, swa_out_g, sb_out_g, ffn_norm_g]
    small_m = [m_attn_norm_g, m_q_norm_g, m_k_norm_g, m_attn_sinks, m_swa_out_g, m_sb_out_g, m_ffn_norm_g]
    small_v = [v_attn_norm_g, v_q_norm_g, v_k_norm_g, v_attn_sinks, v_swa_out_g, v_sb_out_g, v_ffn_norm_g]
    shapes = [a.shape for a in small_w] + [(N_META, d)]
    packed = _adam_small(_ag_small(_pack(small_grads, d), "ag_small_grads"),
                         _pack(small_w + [widen(meta_tokens)], d), _pack(small_m + [widen(m_meta_tokens)], d),
                         _pack(small_v + [widen(v_meta_tokens)], d))
    small_out = []
    for p in packed:
        parts = _unpack(p, shapes)
        parts[-1] = lax.dynamic_slice(parts[-1], (0, col0), meta_tokens.shape)
        small_out.append(parts)

    big = dict.fromkeys(("w_in", "w_o", "w_gate", "w_up", "w_down"))
    for l in range(nl):
        (own_in, own_o), (arr_in, arr_o) = red_attn[l]
        own_ff, arr_ff = red_ff[l]
        big["w_in"] = _adam_big(w_in_v, t(m_w_in), t(v_w_in), own_in, arr_in, l, 0, "adam_w_in", big["w_in"])
        big["w_o"] = _adam_big(w_o, m_w_o, v_w_o, own_o, arr_o, l, 0, "adam_w_o", big["w_o"])
        big["w_gate"] = _adam_big(w_gate_v, t(m_w_gate), t(v_w_gate), own_ff, arr_ff, l, 0, "adam_w_gate", big["w_gate"])
        big["w_up"] = _adam_big(w_up_v, t(m_w_up), t(v_w_up), own_ff, arr_ff, l, 1, "adam_w_up", big["w_up"])
        big["w_down"] = _adam_big(w_down, m_w_down, v_w_down, own_ff, arr_ff, l, 2, "adam_w_down", big["w_down"])
    for name in ("w_in", "w_gate", "w_up"):
        big[name] = [t(a) for a in big[name]]

    def group(k):
        sm = small_out[k]
        return [sm[7], sm[0], big["w_in"][k], sm[1], sm[2], sm[3], sm[4], sm[5], big["w_o"][k], sm[6],
                big["w_gate"][k], big["w_up"][k], big["w_down"][k]]

    return (loss, grad_x, *group(0), *group(1), *group(2), *group(3))
```

```python
import jax
import jax.numpy as jnp
from jax import lax
from jax.experimental import pallas as pl
from jax.experimental.pallas import tpu as pltpu

f32, bf16 = jnp.float32, jnp.bfloat16
S = jax.ShapeDtypeStruct
BS = pl.BlockSpec

N_META = 16
TILE = 128
PAD = TILE - N_META
HEAD_DIM = 64
SWA_HEADS = 16
SWA_KV_HEADS = 4
SB_HEADS = 16
EPS = 1e-6
NEG = -1e30
ADAM_LR, ADAM_B1, ADAM_B2, ADAM_EPS, ADAM_WD, ADAM_STEP = 0.001, 0.9, 0.999, 1e-08, 0.01, 10
AXES = ("x", "y", "c")
NDEV = 8
VMEM_LIMIT_V7X = 56 * 1024 * 1024
MM_RESIDENT_K = 2048
TN_VMEM_BUDGET = 46 * 1024 * 1024
MESH = pl.DeviceIdType.MESH
ANY = pl.BlockSpec(memory_space=pl.ANY)

NN = ((1,), (0,))
NT = ((1,), (1,))
TN = ((0,), (0,))


def _cp(*sem):
    return pltpu.CompilerParams(dimension_semantics=sem if sem else None, vmem_limit_bytes=VMEM_LIMIT_V7X)


def _pick(n, cands):
    for c in cands:
        if n % c == 0:
            return c
    raise ValueError(f"no tile for {n} in {cands}")


def _dot(a, b, dims):
    return lax.dot_general(a, b, (dims, ((), ())), preferred_element_type=f32)


def _embed(x, meta_full):
    seq, d = x.shape
    meta_pad = jnp.pad(meta_full, ((PAD, 0), (0, 0)))

    def body(x_ref, m_ref, o_ref):
        i = pl.program_id(0)

        @pl.when(i == 0)
        def _():
            o_ref[...] = m_ref[...]

        @pl.when(i > 0)
        def _():
            o_ref[...] = x_ref[...]

    return pl.pallas_call(
        body, out_shape=S((TILE + seq, d), f32), grid=(1 + seq // TILE,),
        in_specs=[BS((TILE, d), lambda i: (jnp.maximum(i - 1, 0), 0)), BS((TILE, d), lambda i: (0, 0))],
        out_specs=BS((TILE, d), lambda i: (i, 0)), name="embed", compiler_params=_cp("arbitrary"))(x, meta_pad)


def _rmsnorm_fwd(xs, gain, name):
    lp = xs[0].shape[0]
    widths = [x.shape[1] for x in xs]
    w = sum(widths)
    tr = _pick(lp, (384, 256, 128))
    n = len(xs)

    def body(*refs):
        g_ref, o_ref = refs[n], refs[n + 1]
        off = 0
        for k in range(n):
            sl = slice(off, off + widths[k])
            off += widths[k]
            xv = refs[k][...]
            r = lax.rsqrt(jnp.mean(xv * xv, axis=-1, keepdims=True) + EPS)
            o_ref[:, sl] = ((xv * r) * g_ref[:, sl]).astype(bf16)

    return pl.pallas_call(
        body, out_shape=S((lp, w), bf16), grid=(lp // tr,),
        in_specs=[BS((tr, wk), lambda i: (i, 0)) for wk in widths] + [BS((1, w), lambda i: (0, 0))],
        out_specs=BS((tr, w), lambda i: (i, 0)), name=name, compiler_params=_cp("parallel"))(*xs, gain)


def _rmsnorm_bwd(dy, xs, gain, res, name, out_dtype=f32):
    lp = xs[0].shape[0]
    widths = [x.shape[1] for x in xs]
    w = sum(widths)
    tr = _pick(lp, (384, 256, 128))
    n = len(xs)
    has_res = res is not None
    assert not has_res or n == 1

    def body(*refs):
        dy_ref, g_ref = refs[0], refs[1 + n]
        res_ref = refs[2 + n] if has_res else None
        dx_refs = refs[2 + n + has_res:2 + 2 * n + has_res]
        dg_ref = refs[2 + 2 * n + has_res]

        @pl.when(pl.program_id(0) == 0)
        def _():
            dg_ref[...] = jnp.zeros_like(dg_ref)

        off = 0
        for k in range(n):
            sl = slice(off, off + widths[k])
            off += widths[k]
            xv = refs[1 + k][...]
            dyv = dy_ref[:, sl]
            r = lax.rsqrt(jnp.mean(xv * xv, axis=-1, keepdims=True) + EPS)
            xh = xv * r
            dg_ref[:, sl] += jnp.sum(dyv * xh, axis=0, keepdims=True)
            dxh = dyv * g_ref[:, sl]
            dx = r * (dxh - xh * jnp.mean(dxh * xh, axis=-1, keepdims=True))
            if has_res:
                dx = dx + res_ref[...]
                refs[-1][...] = dx.astype(bf16)
            dx_refs[k][...] = dx.astype(out_dtype)

    rows = [BS((tr, wk), lambda i: (i, 0)) for wk in widths]
    full = BS((tr, w), lambda i: (i, 0))
    vec = BS((1, w), lambda i: (0, 0))
    args = (dy, *xs, gain) + ((res,) if has_res else ())
    out = pl.pallas_call(
        body, out_shape=tuple(S((lp, wk), out_dtype) for wk in widths) + (S((1, w), f32),) + ((S((lp, w), bf16),) if has_res else ()),
        grid=(lp // tr,),
        in_specs=[full] + rows + [vec] + ([full] if has_res else []), out_specs=tuple(rows) + (vec,) + ((full,) if has_res else ()),
        name=name, compiler_params=_cp("arbitrary"))(*args)
    return (list(out[:n]), out[n], out[n + 1]) if has_res else (list(out[:n]), out[n])


def _loss_head(h, target):
    lp, d = h.shape

    def body(h_ref, t_ref, loss_ref, dh_ref, dhb_ref):
        i = pl.program_id(0)

        @pl.when(i == 0)
        def _():
            loss_ref[...] = jnp.zeros_like(loss_ref)
            dh_ref[...] = jnp.zeros_like(dh_ref)
            dhb_ref[...] = jnp.zeros_like(dhb_ref)

        @pl.when(i > 0)
        def _():
            e = h_ref[...] - t_ref[...]
            dh = e * (1.0 / d)
            dh_ref[...] = dh
            dhb_ref[...] = dh.astype(bf16)
            loss_ref[...] += 0.5 * jnp.sum(jnp.sum(e * e, axis=-1, keepdims=True) * (1.0 / d))

    tile = BS((TILE, d), lambda i: (i, 0))
    return pl.pallas_call(
        body, out_shape=(S((8, 128), f32), S((lp, d), f32), S((lp, d), bf16)), grid=(lp // TILE,),
        in_specs=[tile, BS((TILE, d), lambda i: (jnp.maximum(i - 1, 0), 0))],
        out_specs=(BS((8, 128), lambda i: (0, 0)), tile, tile),
        name="loss_head", compiler_params=_cp("arbitrary"))(h, target)


def _mm(a, b, layer, mode, name, out_dtype=f32, res=None, cols=None, side=None):
    m, k = a.shape
    n = b.shape[2] if mode == "nn" else b.shape[1]
    c0 = 0
    if cols is not None:
        assert mode == "nt"
        c0, n = cols
    tn = _pick(n, (512, 384, 256, 128))
    while c0 % tn:
        tn -= TILE
    assert n % tn == 0
    joff = c0 // tn
    rows_outer = k <= MM_RESIDENT_K
    tm = _pick(m, (1408, 384, 256, 128) if rows_outer else (384, 256, 128))
    dims = NN if mode == "nn" else NT
    has_res = res is not None

    grid = (m // tm, n // tn) if rows_outer else (n // tn, m // tm)

    def body(*refs):
        refs, side_refs = _side_split(side, refs, 2 + has_res, 1, 0)
        _side_start(side, side_refs, grid)
        if has_res:
            a_ref, b_ref, r_ref, o_ref = refs
        else:
            a_ref, b_ref, o_ref = refs
        acc = _dot(a_ref[...].astype(bf16), b_ref[...], dims)
        if has_res:
            acc = acc + r_ref[...]
        o_ref[...] = acc.astype(out_dtype)
        _side_finish(side, side_refs, grid)

    def ij(f):
        return (lambda i, j: f(i, j)) if rows_outer else (lambda j, i: f(i, j))

    if mode == "nn":
        b_spec = BS((None, k, tn), ij(lambda i, j: (layer, 0, j)))
    else:
        b_spec = BS((None, tn, k), ij(lambda i, j: (layer, joff + j, 0)))
    tile = BS((tm, tn), ij(lambda i, j: (i, j)))
    args = (a, b) + ((res,) if has_res else ())
    s_in, s_in_specs, s_out, s_out_specs, s_scratch = _side_io(side)
    sem = ("parallel", "parallel") if side is None else ("arbitrary", "arbitrary")
    out = pl.pallas_call(
        body, out_shape=(S((m, n), out_dtype), *s_out), grid=grid,
        in_specs=[BS((tm, k), ij(lambda i, j: (i, 0))), b_spec] + ([tile] if has_res else []) + s_in_specs,
        out_specs=(tile, *s_out_specs), scratch_shapes=s_scratch,
        name=name if side is None else name + "_comm", compiler_params=_cp(*sem))(*args, *s_in)
    return out[0] if side is None else (out[0], list(out[1:]))


def _mm_tn(a, b, gbuf, slot, nslots, name):
    t, m = a.shape
    n = b.shape[1]
    tk = _pick(t, (1408, 1024, 512, 384, 256, 128))
    nk = t // tk

    def vmem(rows):
        return 2 * (rows * n * 4 + tk * rows * a.dtype.itemsize + tk * n * b.dtype.itemsize)

    tm = next(c for c in (1536, 1408, 1024, 512, 384, 256, 128) if m % c == 0 and (vmem(c) <= TN_VMEM_BUDGET or c == 128))

    def body(*refs):
        a_ref, b_ref, o_ref = refs[0], refs[1], refs[-1]
        kk = pl.program_id(1)

        @pl.when(kk == 0)
        def _():
            o_ref[...] = jnp.zeros_like(o_ref)

        o_ref[...] += _dot(a_ref[...].astype(bf16), b_ref[...].astype(bf16), TN)

    in_specs = [BS((tk, tm), lambda i, kk: (kk, i)), BS((tk, n), lambda i, kk: (kk, 0))]
    args = (a, b)
    aliases = {}
    if gbuf is not None:
        in_specs.append(ANY)
        args = args + (gbuf,)
        aliases = {2: 0}
    return pl.pallas_call(
        body, out_shape=S((nslots, m, n), f32), grid=(m // tm, nk), in_specs=in_specs,
        out_specs=BS((None, tm, n), lambda i, kk: (slot, i, 0)), input_output_aliases=aliases,
        name=name, compiler_params=_cp("parallel", "arbitrary"))(*args)


def _ffn_up(hn, wg, wu, side=None):
    m, k = hn.shape
    n = wg.shape[1]
    tm = _pick(m, (1408, 384, 256, 128))
    tn = _pick(n, (512, 384, 256, 128))
    grid = (m // tm, n // tn)

    def body(*refs):
        (a_ref, wg_ref, wu_ref, g_ref, u_ref, act_ref), side_refs = _side_split(side, refs, 3, 3, 0)
        _side_start(side, side_refs, grid)
        a = a_ref[...]
        g = _dot(a, wg_ref[...], NT)
        u = _dot(a, wu_ref[...], NT)
        g_ref[...] = g
        u_ref[...] = u
        act_ref[...] = (g * jax.nn.sigmoid(g) * u).astype(bf16)
        _side_finish(side, side_refs, grid)

    tile = BS((tm, tn), lambda i, j: (i, j))
    s_in, s_in_specs, s_out, s_out_specs, s_scratch = _side_io(side)
    sem = ("parallel", "parallel") if side is None else ("arbitrary", "arbitrary")
    out = pl.pallas_call(
        body, out_shape=(S((m, n), f32), S((m, n), f32), S((m, n), bf16), *s_out), grid=grid,
        in_specs=[BS((tm, k), lambda i, j: (i, 0)), BS((None, tn, k), lambda i, j: (0, j, 0)),
                  BS((None, tn, k), lambda i, j: (0, j, 0)), *s_in_specs],
        out_specs=(tile, tile, tile, *s_out_specs), scratch_shapes=s_scratch,
        name="ffn_up" if side is None else "ffn_up_comm", compiler_params=_cp(*sem))(hn, wg, wu, *s_in)
    return out[0], out[1], out[2], list(out[3:])


def _ffn_down_bwd(dh, wd, g, u, side=None):
    m, k = dh.shape
    n = wd.shape[1]
    tm = _pick(m, (1408, 384, 256, 128))
    tn = _pick(n, (512, 384, 256, 128))
    grid = (m // tm, n // tn)

    def body(*refs):
        (a_ref, wd_ref, g_ref, u_ref, dg_ref, du_ref), side_refs = _side_split(side, refs, 4, 2, 0)
        _side_start(side, side_refs, grid)
        dact = _dot(a_ref[...].astype(bf16), wd_ref[...], NT)
        gv = g_ref[...]
        sg = jax.nn.sigmoid(gv)
        dg_ref[...] = (dact * u_ref[...] * (sg * (1.0 + gv * (1.0 - sg)))).astype(bf16)
        du_ref[...] = (dact * (gv * sg)).astype(bf16)
        _side_finish(side, side_refs, grid)

    tile = BS((tm, tn), lambda i, j: (i, j))
    s_in, s_in_specs, s_out, s_out_specs, s_scratch = _side_io(side)
    sem = ("parallel", "parallel") if side is None else ("arbitrary", "arbitrary")
    out = pl.pallas_call(
        body, out_shape=(S((m, n), bf16), S((m, n), bf16), *s_out), grid=grid,
        in_specs=[BS((tm, k), lambda i, j: (i, 0)), BS((None, tn, k), lambda i, j: (0, j, 0)), tile, tile, *s_in_specs],
        out_specs=(tile, tile, *s_out_specs), scratch_shapes=s_scratch,
        name="ffn_down_bwd" if side is None else "ffn_down_bwd_comm", compiler_params=_cp(*sem))(dh, wd, g, u, *s_in)
    return out[0], out[1], list(out[2:])


def _lane_lo():
    return lax.broadcasted_iota(jnp.int32, (1, TILE), 1) < HEAD_DIM


def _head_norm(x, g2):
    lo = _lane_lo()
    x2 = x * x
    s_lo = jnp.sum(jnp.where(lo, x2, 0.0), axis=-1, keepdims=True)
    s_hi = jnp.sum(jnp.where(lo, 0.0, x2), axis=-1, keepdims=True)
    r = jnp.where(lo, lax.rsqrt(s_lo * (1.0 / HEAD_DIM) + EPS), lax.rsqrt(s_hi * (1.0 / HEAD_DIM) + EPS))
    xh = x * r
    return xh * g2, xh, r


def _head_norm_bwd(dy, xh, r, g2):
    lo = _lane_lo()
    dgain = jnp.sum(dy * xh, axis=0, keepdims=True)
    dxh = dy * g2
    t = dxh * xh
    t_lo = jnp.sum(jnp.where(lo, t, 0.0), axis=-1, keepdims=True)
    t_hi = jnp.sum(jnp.where(lo, 0.0, t), axis=-1, keepdims=True)
    mt = jnp.where(lo, t_lo, t_hi) * (1.0 / HEAD_DIM)
    return r * (dxh - xh * mt), dgain


def _swa_slopes():
    return [2.0 ** (-8.0 * (h + 1) / SWA_HEADS) for h in range(SWA_HEADS)]


def _swa_masks(i):
    rows = lax.broadcasted_iota(jnp.int32, (TILE, TILE), 0)
    cols = lax.broadcasted_iota(jnp.int32, (TILE, TILE), 1)
    r_pos = i * TILE + rows
    prev = jnp.maximum(i - 1, 0)
    out = []
    for c, base in enumerate((0, prev * TILE, i * TILE)):
        s_pos = base + cols
        dist = r_pos - s_pos
        if c == 0:
            ok = (i >= 1) & (s_pos >= PAD)
        elif c == 1:
            ok = (i >= 2) & (dist < TILE)
        else:
            ok = (dist >= 0) & (s_pos >= PAD)
        out.append((ok, dist.astype(f32)))
    return out, prev


def _swa_geometry(i, group):
    masks, prev = _swa_masks(i)
    bases = (0, pl.multiple_of(prev * TILE, TILE), pl.multiple_of(i * TILE, TILE))
    ok = jnp.concatenate([m[0] for m in masks], axis=1)
    dist = jnp.concatenate([m[1] for m in masks], axis=1)
    return bases, jnp.concatenate([ok] * group, axis=0), jnp.concatenate([dist] * group, axis=0)


def _swa_col(values):
    return jnp.concatenate([jnp.full((TILE, 1), v, f32) for v in values], axis=0)


def _swa_keys(k_ref, v_ref, kg_ref, bases, kvh, normed):
    kb = kvh // 2
    kcols = slice(kb * TILE, (kb + 1) * TILE)
    ksel = _lane_lo() if kvh % 2 == 0 else jnp.logical_not(_lane_lo())
    if kb not in normed:
        normed[kb] = jnp.concatenate([_head_norm(k_ref[pl.ds(b, TILE), kcols], kg_ref[...])[0] for b in bases], axis=0).astype(bf16)
    vcat = jnp.concatenate([jnp.where(ksel, v_ref[pl.ds(b, TILE), kcols], 0.0) for b in bases], axis=0).astype(bf16)
    return normed[kb], vcat, ksel


def _swa_stack(blocks, kvh, group, ksel):
    parts = []
    for gi in range(group):
        same = ((kvh * group + gi) % 2) == (kvh % 2)
        parts.append(jnp.where(ksel, blocks[gi] if same else pltpu.roll(blocks[gi], HEAD_DIM, 1), 0.0))
    return jnp.concatenate(parts, axis=0)


def _swa_unstack_add(acc, stacked, kvh, group):
    for gi in range(group):
        h = kvh * group + gi
        part = stacked[gi * TILE:(gi + 1) * TILE]
        acc[h // 2] = acc[h // 2] + (part if (h % 2) == (kvh % 2) else pltpu.roll(part, HEAD_DIM, 1))


def _swa_probs(q4, kcat, ok_g, dist_g, slope_col, sink_col):
    logits = _dot(q4, kcat, NT) * (HEAD_DIM ** -0.5) + jnp.where(ok_g, -slope_col * dist_g, NEG)
    mx = jnp.maximum(jnp.max(logits, axis=-1, keepdims=True), sink_col)
    return jnp.exp(logits - mx), jnp.exp(sink_col - mx)


def _swa_specs(lp, pw):
    qw, kvw = SWA_HEADS * HEAD_DIM, SWA_KV_HEADS * HEAD_DIM
    kidx = qw // kvw
    return (BS((TILE, qw), lambda i: (i, 0)), BS((lp, kvw), lambda i: (0, kidx)), BS((lp, kvw), lambda i: (0, kidx + 1)),
            BS((1, TILE), lambda i: (0, 0)))


def _swa_fwd(proj, qg2, kg2, sinks):
    lp, pw = proj.shape
    qw = SWA_HEADS * HEAD_DIM
    group = SWA_HEADS // SWA_KV_HEADS
    slopes = _swa_slopes()

    def body(sink_ref, q_ref, k_ref, v_ref, qg_ref, kg_ref, o_ref):
        i = pl.program_id(0)
        bases, ok_g, dist_g = _swa_geometry(i, group)
        qn = [_head_norm(q_ref[:, b * TILE:(b + 1) * TILE], qg_ref[...])[0] for b in range(qw // TILE)]
        acc = [jnp.zeros((TILE, TILE), f32) for _ in range(qw // TILE)]
        normed = {}
        for kvh in range(SWA_KV_HEADS):
            kcat, vcat, ksel = _swa_keys(k_ref, v_ref, kg_ref, bases, kvh, normed)
            q4 = _swa_stack([qn[(kvh * group + gi) // 2] for gi in range(group)], kvh, group, ksel).astype(bf16)
            sink_col = _swa_col([sink_ref[kvh * group + gi] for gi in range(group)])
            p, e_sink = _swa_probs(q4, kcat, ok_g, dist_g, _swa_col(slopes[kvh * group:(kvh + 1) * group]), sink_col)
            den = e_sink + jnp.sum(p, axis=-1, keepdims=True)
            o = _dot(p.astype(bf16), vcat, NN) / den
            _swa_unstack_add(acc, o, kvh, group)
        valid = (i * TILE + lax.broadcasted_iota(jnp.int32, (TILE, 1), 0)) >= PAD
        for b in range(qw // TILE):
            o_ref[:, b * TILE:(b + 1) * TILE] = jnp.where(valid, acc[b], 0.0)

    q_spec, k_spec, v_spec, vec = _swa_specs(lp, pw)
    return pl.pallas_call(
        body, out_shape=S((lp, qw), f32), grid=(lp // TILE,),
        in_specs=[BS(memory_space=pltpu.SMEM), q_spec, k_spec, v_spec, vec, vec],
        out_specs=BS((TILE, qw), lambda i: (i, 0)), name="swa_fwd", compiler_params=_cp("arbitrary"))(
            sinks, proj, proj, proj, qg2, kg2)


def _swa_bwd(proj, dout, qg2, kg2, sinks):
    lp, pw = proj.shape
    qw, kvw = SWA_HEADS * HEAD_DIM, SWA_KV_HEADS * HEAD_DIM
    group = SWA_HEADS // SWA_KV_HEADS
    scale = HEAD_DIM ** -0.5
    slopes = _swa_slopes()
    nt = lp // TILE

    def body(sink_ref, q_ref, k_ref, v_ref, do_ref, qg_ref, kg_ref, dq_ref, dk_ref, dv_ref, dqg_ref, dkg_ref, ds_ref,
             dkn_acc, dv_acc):
        i = pl.program_id(0)
        lo = _lane_lo()
        lane = lax.broadcasted_iota(jnp.int32, (1, TILE), 1)

        @pl.when(i == 0)
        def _():
            dkn_acc[...] = jnp.zeros_like(dkn_acc)
            dv_acc[...] = jnp.zeros_like(dv_acc)
            dqg_ref[...] = jnp.zeros_like(dqg_ref)
            dkg_ref[...] = jnp.zeros_like(dkg_ref)
            ds_ref[...] = jnp.zeros_like(ds_ref)

        bases, ok_g, dist_g = _swa_geometry(i, group)
        qnorm = [_head_norm(q_ref[:, b * TILE:(b + 1) * TILE], qg_ref[...]) for b in range(qw // TILE)]
        dqn = [jnp.zeros((TILE, TILE), f32) for _ in range(qw // TILE)]
        normed = {}
        for kvh in range(SWA_KV_HEADS):
            kcols = slice((kvh // 2) * TILE, (kvh // 2 + 1) * TILE)
            heads = [kvh * group + gi for gi in range(group)]
            kcat, vcat, ksel = _swa_keys(k_ref, v_ref, kg_ref, bases, kvh, normed)
            q4 = _swa_stack([qnorm[h // 2][0] for h in heads], kvh, group, ksel).astype(bf16)
            do4 = _swa_stack([do_ref[:, (h // 2) * TILE:(h // 2 + 1) * TILE].astype(f32) for h in heads], kvh, group, ksel).astype(bf16)
            sink_col = _swa_col([sink_ref[h] for h in heads])
            p, e_sink = _swa_probs(q4, kcat, ok_g, dist_g, _swa_col([slopes[h] for h in heads]), sink_col)
            inv = 1.0 / (e_sink + jnp.sum(p, axis=-1, keepdims=True))
            p = p * inv
            dp = _dot(do4, vcat, NT)
            dsum = jnp.sum(p * dp, axis=-1, keepdims=True)
            d_sink = e_sink * inv * dsum
            for gi, h in enumerate(heads):
                ds_ref[...] += jnp.where(lane == h, -jnp.sum(d_sink[gi * TILE:(gi + 1) * TILE]), 0.0)
            dsc = (p * (dp - dsum) * scale).astype(bf16)
            _swa_unstack_add(dqn, jnp.where(ksel, _dot(dsc, kcat, NN), 0.0), kvh, group)
            dkn = _dot(dsc, q4, TN)
            dvv = _dot(p.astype(bf16), do4, TN)
            for c in range(3):
                dkn_acc[pl.ds(bases[c], TILE), kcols] += dkn[c * TILE:(c + 1) * TILE]
                dv_acc[pl.ds(bases[c], TILE), kcols] += dvv[c * TILE:(c + 1) * TILE]
        for b in range(qw // TILE):
            _, xh, r = qnorm[b]
            dq, dgain = _head_norm_bwd(dqn[b], xh, r, qg_ref[...])
            dq_ref[:, b * TILE:(b + 1) * TILE] = dq.astype(bf16)
            dqg_ref[...] += dgain

        @pl.when(i == nt - 1)
        def _():
            dv_ref[...] = dv_acc[...].astype(bf16)

            def tile_step(t, carry):
                base = pl.multiple_of(t * TILE, TILE)
                for kb in range(kvw // TILE):
                    kcols = slice(kb * TILE, (kb + 1) * TILE)
                    _, xh, r = _head_norm(k_ref[pl.ds(base, TILE), kcols], kg_ref[...])
                    dk, dgain = _head_norm_bwd(dkn_acc[pl.ds(base, TILE), kcols], xh, r, kg_ref[...])
                    dk_ref[pl.ds(base, TILE), kcols] = dk.astype(bf16)
                    dkg_ref[...] += dgain
                return carry

            lax.fori_loop(0, nt, tile_step, 0)

    q_spec, k_spec, v_spec, vec = _swa_specs(lp, pw)
    whole = BS((lp, kvw), lambda i: (0, 0))
    return pl.pallas_call(
        body, out_shape=(S((lp, qw), bf16), S((lp, kvw), bf16), S((lp, kvw), bf16), S((1, TILE), f32), S((1, TILE), f32), S((1, TILE), f32)),
        grid=(nt,),
        in_specs=[BS(memory_space=pltpu.SMEM), q_spec, k_spec, v_spec, BS((TILE, qw), lambda i: (i, 0)), vec, vec],
        out_specs=(BS((TILE, qw), lambda i: (i, 0)), whole, whole, vec, vec, vec),
        scratch_shapes=[pltpu.VMEM((lp, kvw), f32), pltpu.VMEM((lp, kvw), f32)],
        name="swa_bwd", compiler_params=_cp("arbitrary"))(sinks, proj, proj, proj, dout, qg2, kg2)


LOG2E = 1.4426950408889634
SB_PAIRS = 2
ROWS2 = 2 * TILE
SB_ROWS = SB_PAIRS * ROWS2


def _sb_stack(x):
    lo = _lane_lo()
    zero = jnp.zeros((TILE, TILE), x.dtype)
    parts = []
    for p in range(SB_PAIRS):
        xp = x[:, p * TILE:(p + 1) * TILE]
        parts += [jnp.where(lo, xp, zero), jnp.where(lo, zero, xp)]
    return jnp.concatenate(parts, axis=0)


def _sb_unstack(x):
    lo = _lane_lo()
    parts = [jnp.where(lo, x[p * ROWS2:p * ROWS2 + TILE], x[p * ROWS2 + TILE:(p + 1) * ROWS2]) for p in range(SB_PAIRS)]
    return parts[0] if SB_PAIRS == 1 else jnp.concatenate(parts, axis=1)


def _sb_pair_dot(a, b, dims, b_lanes):
    parts = []
    for p in range(SB_PAIRS):
        bp = b[:, p * TILE:(p + 1) * TILE] if b_lanes else b[p * ROWS2:(p + 1) * ROWS2]
        parts.append(_dot(a[p * ROWS2:(p + 1) * ROWS2], bp, dims))
    return parts


def _sb_sum_matrix(after):
    rows = lax.broadcasted_iota(jnp.int32, (ROWS2, ROWS2), 0) & (TILE - 1)
    cols = lax.broadcasted_iota(jnp.int32, (ROWS2, ROWS2), 1)
    tri = (rows > cols) if after else (rows < cols)
    return (tri | (cols >= TILE)).astype(bf16)


def _sb_sums(x, w, sum_mat):
    hi = x.astype(bf16)
    lo = (x - hi.astype(f32)).astype(bf16)
    return [_dot(jnp.concatenate([hi[:, j * TILE:(j + 1) * TILE], lo[:, j * TILE:(j + 1) * TILE]], axis=1), sum_mat, NN)
            for j in range(w)]


def _sb_edge_masks(qi, first, w):
    rows = lax.broadcasted_iota(jnp.int32, (SB_ROWS, TILE), 0) & (TILE - 1)
    cols = lax.broadcasted_iota(jnp.int32, (SB_ROWS, TILE), 1)
    not_pad = first * TILE + cols >= PAD
    before_query = (first + w - 1 - qi) * TILE + cols < rows
    if w == 1:
        return [not_pad & before_query]
    return [not_pad] + [None] * (w - 2) + [before_query]


def _sb_apply(valid, x):
    return x if valid is None else jnp.where(valid, x, 0.0)


def _sb_block(q2, k_ref, first, w, valids, c_rep, after_ones):
    kwin = k_ref[pl.ds(pl.multiple_of(first * TILE, TILE), w * TILE), :]
    z2 = jnp.concatenate(_sb_pair_dot(q2, kwin, NT, True), axis=0) * (HEAD_DIM ** -0.5 * LOG2E)
    ls2 = jnp.minimum(z2, 0.0) - jnp.log2(1.0 + jnp.exp2(jnp.minimum(z2, -z2)))
    m2 = ls2 - z2
    if w == 1:
        m2 = _sb_apply(valids[0], m2)
    else:
        m2 = jnp.concatenate([_sb_apply(valids[j], m2[:, j * TILE:(j + 1) * TILE]) for j in range(w)], axis=1)
    sums = _sb_sums(m2, w, after_ones)
    parts = [None] * w
    for j in reversed(range(w)):
        parts[j] = _sb_apply(valids[j], jnp.exp2(ls2[:, j * TILE:(j + 1) * TILE] + sums[j][:, :TILE] + c_rep))
        c_rep = c_rep + sums[j][:, TILE:]
    return ls2, (parts[0] if w == 1 else jnp.concatenate(parts, axis=1)), c_rep


SB_BLOCKS = (4, 2, 1)
SB_DEAD = -160.0


def _sb_alive(c_rep):
    return jnp.max(c_rep) > SB_DEAD


def _sb_walk_down(qi, block, state, c_of):
    n = qi + 1
    big = SB_BLOCKS[0]

    def more(carry):
        t, st = carry
        return (t < n // big) & _sb_alive(c_of(st))

    def step(carry):
        t, st = carry
        return t + 1, block(n - big * (t + 1), big, st)

    visited, state = lax.while_loop(more, step, (jnp.int32(0), state))
    counts = [visited]
    for s in SB_BLOCKS[1:]:
        cnt = jnp.where(_sb_alive(c_of(state)), (n % (2 * s)) // s, 0)
        state = lax.fori_loop(0, cnt, lambda t, st, s=s: block(n % s, s, st), state)
        counts.append(cnt)
    return state, counts


def _sb_walk_up(qi, block, state, counts):
    n = qi + 1
    big = SB_BLOCKS[0]
    for s, cnt in reversed(list(zip(SB_BLOCKS[1:], counts[1:]))):
        state = lax.fori_loop(0, cnt, lambda t, st, s=s: block(n % s, s, st), state)
    visited = counts[0]
    return lax.fori_loop(0, visited, lambda i, st: block(n - big * (visited - i), big, st), state)


def _sb_cols(lp, off, single=False):
    mode = dict(pipeline_mode=pl.Buffered(1)) if single else {}
    return BS((lp, SB_PAIRS * TILE), lambda g: (0, off + g), **mode)


def _sb_fwd(proj, side=None):
    lp = proj.shape[0]
    sbw = SB_HEADS * HEAD_DIM
    ngrp = sbw // (SB_PAIRS * TILE)
    nt = lp // TILE

    def body(*refs):
        (q_ref, k_ref, v_ref, o_ref), side_refs = _side_split(side, refs, 3, 1, 0)
        _side_start(side, side_refs, (ngrp,))
        after_ones = _sb_sum_matrix(True)

        def q_step(qi, carry):
            qbase = pl.multiple_of(qi * TILE, TILE)
            q2 = _sb_stack(q_ref[pl.ds(qbase, TILE), :])

            def block(first, w, st):
                c_rep, acc = st
                _, a, c_rep = _sb_block(q2, k_ref, first, w, _sb_edge_masks(qi, first, w), c_rep, after_ones)
                vwin = v_ref[pl.ds(pl.multiple_of(first * TILE, TILE), w * TILE), :]
                return c_rep, acc + jnp.concatenate(_sb_pair_dot(a.astype(bf16), vwin, NN, True), axis=0)

            zero = jnp.zeros((SB_ROWS, TILE), f32)
            (_, acc), _ = _sb_walk_down(qi, block, (zero, zero), lambda st: st[0])
            o_ref[pl.ds(qbase, TILE), :] = _sb_unstack(acc)
            return carry

        lax.fori_loop(0, nt, q_step, 0)
        _side_finish(side, side_refs, (ngrp,))

    s_in, s_in_specs, s_out, s_out_specs, s_scratch = _side_io(side)
    out = pl.pallas_call(
        body, out_shape=(S((lp, sbw), f32), *s_out), grid=(ngrp,),
        in_specs=[_sb_cols(lp, 0), _sb_cols(lp, ngrp), _sb_cols(lp, 2 * ngrp), *s_in_specs],
        out_specs=(_sb_cols(lp, 0), *s_out_specs), scratch_shapes=s_scratch,
        name="sb_fwd" if side is None else "sb_fwd_comm",
        compiler_params=_cp("parallel" if side is None else "arbitrary"))(proj, proj, proj, *s_in)
    return out[0], list(out[1:])


def _sb_bwd(proj, dout, side=None):
    lp = proj.shape[0]
    sbw = SB_HEADS * HEAD_DIM
    ngrp = sbw // (SB_PAIRS * TILE)
    nt = lp // TILE
    scale = HEAD_DIM ** -0.5

    def body(*refs):
        (q_ref, k_ref, v_ref, do_ref, dq_ref, dk_ref, dv_ref, dk_acc, dv_acc, e_buf, b_buf), side_refs = _side_split(side, refs, 4, 3, 4)
        _side_start(side, side_refs, (ngrp,))
        dk_acc[...] = jnp.zeros_like(dk_acc)
        dv_acc[...] = jnp.zeros_like(dv_acc)
        after_ones = _sb_sum_matrix(True)
        before_ones = _sb_sum_matrix(False)

        def add_cols(acc_ref, rows, parts):
            for p in range(SB_PAIRS):
                acc_ref[rows, p * TILE:(p + 1) * TILE] += parts[p]

        def q_step(qi, carry):
            qbase = pl.multiple_of(qi * TILE, TILE)
            q2 = _sb_stack(q_ref[pl.ds(qbase, TILE), :])
            do2 = _sb_stack(do_ref[pl.ds(qbase, TILE), :])

            def block_rl(first, w, c_rep):
                ls2, a, c_rep = _sb_block(q2, k_ref, first, w, _sb_edge_masks(qi, first, w), c_rep, after_ones)
                rows = pl.ds(pl.multiple_of(first * TILE, TILE), w * TILE)
                e = jnp.concatenate(_sb_pair_dot(do2, v_ref[rows, :], NT, True), axis=0) * a
                beta = jnp.exp2(ls2)
                for j in range(w):
                    e_buf[first + j] = e[:, j * TILE:(j + 1) * TILE]
                    b_buf[first + j] = beta[:, j * TILE:(j + 1) * TILE]
                add_cols(dv_acc, rows, _sb_pair_dot(a.astype(bf16), do2, TN, False))
                return c_rep

            zero = jnp.zeros((SB_ROWS, TILE), f32)
            _, counts = _sb_walk_down(qi, block_rl, zero, lambda c_rep: c_rep)

            def block_lr(first, w, st):
                e_rep, dq_acc = st
                rows = pl.ds(pl.multiple_of(first * TILE, TILE), w * TILE)
                valids = _sb_edge_masks(qi, first, w)
                es = [e_buf[first + j] for j in range(w)]
                sums = _sb_sums(es[0] if w == 1 else jnp.concatenate(es, axis=1), w, before_ones)
                parts = []
                for j in range(w):
                    beta = b_buf[first + j]
                    parts.append(_sb_apply(valids[j], (es[j] - beta * (es[j] + e_rep + sums[j][:, :TILE])) * scale))
                    e_rep = e_rep + sums[j][:, TILE:]
                dz = (parts[0] if w == 1 else jnp.concatenate(parts, axis=1)).astype(bf16)
                add_cols(dk_acc, rows, _sb_pair_dot(dz, q2, TN, False))
                return e_rep, dq_acc + jnp.concatenate(_sb_pair_dot(dz, k_ref[rows, :], NN, True), axis=0)

            _, dq = _sb_walk_up(qi, block_lr, (zero, zero), counts)
            dq_ref[pl.ds(qbase, TILE), :] = _sb_unstack(dq).astype(bf16)
            return carry

        lax.fori_loop(0, nt, q_step, 0)
        dk_ref[...] = dk_acc[...].astype(bf16)
        dv_ref[...] = dv_acc[...].astype(bf16)
        _side_finish(side, side_refs, (ngrp,))

    wide = SB_PAIRS * TILE
    s_in, s_in_specs, s_out, s_out_specs, s_scratch = _side_io(side)
    out = pl.pallas_call(
        body, out_shape=(S((lp, sbw), bf16),) * 3 + tuple(s_out), grid=(ngrp,),
        in_specs=[_sb_cols(lp, 0, True), _sb_cols(lp, ngrp, True), _sb_cols(lp, 2 * ngrp, True), _sb_cols(lp, 0, True), *s_in_specs],
        out_specs=(_sb_cols(lp, 0, True),) * 3 + tuple(s_out_specs),
        scratch_shapes=[pltpu.VMEM((lp, wide), f32), pltpu.VMEM((lp, wide), f32),
                        pltpu.VMEM((nt, SB_ROWS, TILE), f32), pltpu.VMEM((nt, SB_ROWS, TILE), f32), *s_scratch],
        name="sb_bwd" if side is None else "sb_bwd_comm",
        compiler_params=_cp("parallel" if side is None else "arbitrary"))(proj, proj, proj, dout, *s_in)
    return out[0], out[1], out[2], list(out[3:])


def _place():
    x, y, c = lax.axis_index("x"), lax.axis_index("y"), lax.axis_index("c")
    chips = [(1 - x, y), (x, 1 - y), (1 - x, 1 - y)]
    return x, y, c, chips


def _two_level_gather_body(rows, x_ref, send_sems, recv_sems, local_sem):
    x, y, c, chips = _place()
    me, sibling = (x, y, c), (x, y, 1 - c)

    def copy(k, block, to, src=None):
        return pltpu.make_async_remote_copy(
            src_ref=rows(*block) if src is None else src, dst_ref=rows(*block),
            send_sem=send_sems.at[k], recv_sem=recv_sems.at[k], device_id=to, device_id_type=MESH)

    mine = pltpu.make_async_copy(x_ref, rows(*me), local_sem)
    mine.start()
    first = [copy(0, me, sibling, src=x_ref)]
    first += [copy(1 + j, me, (*chip, c), src=x_ref) for j, chip in enumerate(chips)]
    for cp in first:
        cp.start()
    passed = [copy(4 + j, (*chip, c), sibling) for j, chip in enumerate(chips)]
    for j, chip in enumerate(chips):
        copy(1 + j, (*chip, c), me).wait_recv()
        passed[j].start()
    copy(0, sibling, me).wait_recv()
    for j, chip in enumerate(chips):
        copy(4 + j, (*chip, 1 - c), me).wait_recv()
    for cp in first + passed:
        cp.wait_send()
    mine.wait()


_GATHER_SEMS = [pltpu.SemaphoreType.DMA((7,)), pltpu.SemaphoreType.DMA((7,)), pltpu.SemaphoreType.DMA]


class _Side:
    def __init__(self, inputs, out_shapes, nsem, start, finish):
        self.inputs, self.out_shapes, self.start, self.finish = list(inputs), list(out_shapes), start, finish
        self.scratch = [pltpu.SemaphoreType.DMA((nsem,)), pltpu.SemaphoreType.DMA((nsem,)), pltpu.SemaphoreType.DMA((len(inputs),))]


def _side_io(side):
    if side is None:
        return [], [], [], [], []
    return side.inputs, [ANY] * len(side.inputs), side.out_shapes, [ANY] * len(side.out_shapes), side.scratch


def _side_split(side, refs, n_in, n_out, n_scratch):
    if side is None:
        return refs, None
    si, so = len(side.inputs), len(side.out_shapes)
    a, b, c, d = n_in, n_in + si, n_in + si + n_out, n_in + si + n_out + so
    return refs[:a] + refs[b:c] + refs[d:d + n_scratch], (refs[a:b], refs[c:d], refs[d + n_scratch:])


def _side_first_last(grid):
    first = last = None
    for k, n in enumerate(grid):
        i = pl.program_id(k)
        first = (i == 0) if first is None else first & (i == 0)
        last = (i == n - 1) if last is None else last & (i == n - 1)
    return first, last


def _side_start(side, side_refs, grid):
    if side is not None:
        pl.when(_side_first_last(grid)[0])(lambda: side.start(*side_refs))


def _side_finish(side, side_refs, grid):
    if side is not None:
        pl.when(_side_first_last(grid)[1])(lambda: side.finish(*side_refs))


def _run_side(side, name):
    def body(*refs):
        _, side_refs = _side_split(side, refs, 0, 0, 0)
        side.start(*side_refs)
        side.finish(*side_refs)

    inputs, in_specs, out_shapes, out_specs, scratch = _side_io(side)
    return pl.pallas_call(body, out_shape=tuple(out_shapes), in_specs=in_specs, out_specs=tuple(out_specs),
                          scratch_shapes=scratch, name=name)(*inputs)


def _gather_side(shards, layers):
    def plan(in_refs, out_refs, sems, starting):
        send_sems, recv_sems, local_sems = sems
        x, y, c, chips = _place()
        me, sibling = (x, y, c), (x, y, 1 - c)
        jobs = []
        for a, (x_ref, o_ref) in enumerate(zip(in_refs, out_refs)):
            r = x_ref.shape[1]
            src = x_ref.at[layers[a]]

            def rows(px, py, pc, o_ref=o_ref, r=r):
                return o_ref.at[0, pl.ds(pl.multiple_of((4 * px + 2 * py + pc) * r, 16), r), :]

            def copy(k, block, to, from_shard=False, a=a, rows=rows, src=src):
                return pltpu.make_async_remote_copy(
                    src_ref=src if from_shard else rows(*block), dst_ref=rows(*block),
                    send_sem=send_sems.at[7 * a + k], recv_sem=recv_sems.at[7 * a + k], device_id=to, device_id_type=MESH)

            job = dict(
                mine=pltpu.make_async_copy(src, rows(*me), local_sems.at[a]),
                first=[copy(0, me, sibling, True)] + [copy(1 + j, me, (*chip, c), True) for j, chip in enumerate(chips)])
            if not starting:
                job.update(
                    passed=[copy(4 + j, (*chip, c), sibling) for j, chip in enumerate(chips)],
                    from_chips=[copy(1 + j, (*chip, c), me) for j, chip in enumerate(chips)],
                    from_sibling=[copy(0, sibling, me)] + [copy(4 + j, (*chip, 1 - c), me) for j, chip in enumerate(chips)])
            jobs.append(job)
        return jobs

    def start(*refs):
        for job in plan(*refs, starting=True):
            job["mine"].start()
            for cp in job["first"]:
                cp.start()

    def finish(*refs):
        jobs = plan(*refs, starting=False)
        for job in jobs:
            for arrived, onward in zip(job["from_chips"], job["passed"]):
                arrived.wait_recv()
                onward.start()
        for job in jobs:
            for cp in job["from_sibling"]:
                cp.wait_recv()
            for cp in job["first"] + job["passed"]:
                cp.wait_send()
            job["mine"].wait()

    shapes = [S((1, NDEV * s.shape[1], s.shape[2]), s.dtype) for s in shards]
    return _Side(shards, shapes, 7 * len(shards), start, finish)


def _swap_side(gs):
    def plan(in_refs, out_refs, sems):
        send_sems, recv_sems, _ = sems
        x, y, c, chips = _place()
        copies = []
        for a, (g_ref, o_ref) in enumerate(zip(in_refs, out_refs)):
            r = g_ref.shape[1] // NDEV
            for j, (px, py) in enumerate([(x, y)] + chips):
                d = 4 * px + 2 * py + (1 - c)
                copies.append(pltpu.make_async_remote_copy(
                    src_ref=g_ref.at[:, pl.ds(pl.multiple_of(d * r, 8), r), :], dst_ref=o_ref.at[j],
                    send_sem=send_sems.at[4 * a + j], recv_sem=recv_sems.at[4 * a + j], device_id=(x, y, 1 - c), device_id_type=MESH))
        return copies

    return _exchange_side(gs, [S((4, g.shape[0], g.shape[1] // NDEV, g.shape[2]), f32) for g in gs], 4 * len(gs), plan)


def _ici_side(sends):
    def plan(in_refs, out_refs, sems):
        send_sems, recv_sems, _ = sems
        x, y, c, chips = _place()
        return [pltpu.make_async_remote_copy(
            src_ref=s_ref.at[j], dst_ref=o_ref.at[j], send_sem=send_sems.at[3 * a + j], recv_sem=recv_sems.at[3 * a + j],
            device_id=(*chip, c), device_id_type=MESH)
            for a, (s_ref, o_ref) in enumerate(zip(in_refs, out_refs)) for j, chip in enumerate(chips)]

    return _exchange_side(sends, [S(s.shape, s.dtype) for s in sends], 3 * len(sends), plan)


def _exchange_side(inputs, shapes, nsem, plan):
    def start(*refs):
        for cp in plan(*refs):
            cp.start()

    def finish(*refs):
        copies = plan(*refs)
        for cp in copies:
            cp.wait_recv()
        for cp in copies:
            cp.wait_send()

    return _Side(inputs, shapes, nsem, start, finish)


def _ag_small(shard, name):
    r, w = shard.shape

    def body(x_ref, o_ref, send_sems, recv_sems, local_sem):
        def rows(px, py, pc):
            return o_ref.at[pl.ds(pl.multiple_of((4 * px + 2 * py + pc) * r, 8), r), :]
        _two_level_gather_body(rows, x_ref, send_sems, recv_sems, local_sem)

    vmem = BS(memory_space=pltpu.VMEM)
    return pl.pallas_call(
        body, out_shape=S((NDEV * r, w), shard.dtype), in_specs=[vmem], out_specs=vmem,
        scratch_shapes=_GATHER_SEMS, name=name)(shard)


def _rel_ids():
    x, y, c = lax.axis_index("x"), lax.axis_index("y"), lax.axis_index("c")
    rel = [(x, y), (1 - x, y), (x, 1 - y), (1 - x, 1 - y)]
    return jnp.stack([4 * px + 2 * py + c for px, py in rel]).astype(jnp.int32)


def _rs_partial(g, sib, ids, name):
    ns, rows8, cdim = g.shape
    r = rows8 // NDEV
    tr = r // 2
    nb = r // tr

    def own_body(ids_ref, g_ref, s_ref, o_ref):
        o_ref[...] = g_ref[...] + s_ref[...]

    own = pl.pallas_call(
        own_body, out_shape=S((ns, r, cdim), f32),
        grid_spec=pltpu.PrefetchScalarGridSpec(
            num_scalar_prefetch=1, grid=(ns, nb),
            in_specs=[BS((None, tr, cdim), lambda s, t, ids_ref: (s, ids_ref[0] * nb + t, 0)),
                      BS((None, None, tr, cdim), lambda s, t, ids_ref: (0, s, t, 0))],
            out_specs=BS((None, tr, cdim), lambda s, t, ids_ref: (s, t, 0))),
        name=name + "_own", compiler_params=_cp("parallel", "parallel"))(ids, g, sib)

    def send_body(ids_ref, g_ref, s_ref, o_ref):
        o_ref[...] = (g_ref[...] + s_ref[...]).astype(bf16)

    send = pl.pallas_call(
        send_body, out_shape=S((3, ns, r, cdim), bf16),
        grid_spec=pltpu.PrefetchScalarGridSpec(
            num_scalar_prefetch=1, grid=(3, ns, nb),
            in_specs=[BS((None, tr, cdim), lambda j, s, t, ids_ref: (s, ids_ref[j + 1] * nb + t, 0)),
                      BS((None, None, tr, cdim), lambda j, s, t, ids_ref: (j + 1, s, t, 0))],
            out_specs=BS((None, None, tr, cdim), lambda j, s, t, ids_ref: (j, s, t, 0))),
        name=name + "_send", compiler_params=_cp("parallel", "parallel", "parallel"))(ids, g, sib)
    return own, send


def _prep(w, name):
    nl, r, cdim = w.shape
    blk = BS((None, r, cdim), lambda l: (l, 0, 0))

    def body(w_ref, o_ref):
        o_ref[...] = w_ref[...].astype(bf16)

    return pl.pallas_call(body, out_shape=S(w.shape, bf16), grid=(nl,), in_specs=[blk], out_specs=blk,
                          name=name, compiler_params=_cp("parallel"))(w)


def _adam_math(w, g, m, v):
    m2 = ADAM_B1 * m + (1.0 - ADAM_B1) * g
    v2 = ADAM_B2 * v + (1.0 - ADAM_B2) * (g * g)
    m_hat = m2 / (1.0 - ADAM_B1 ** ADAM_STEP)
    v_hat = v2 / (1.0 - ADAM_B2 ** ADAM_STEP)
    delta = -ADAM_LR * (m_hat / (jnp.sqrt(v_hat) + ADAM_EPS) + ADAM_WD * w)
    return delta, m2, v2


def _adam_big(w, m, v, own, arrived, layer, slot, name, prev=None):
    nl, a, b = w.shape
    ta = _pick(a, (a // 4, a // 2, a)) if (a // 4) % 16 == 0 else a
    w_blk = BS((None, ta, b), lambda t: (layer, t, 0))
    own_blk = BS((None, ta, b), lambda t: (slot, t, 0))
    arr_blk = [BS((None, None, ta, b), lambda t, j=j: (j, slot, t, 0)) for j in range(3)]

    def body(*refs):
        w_ref, m_ref, v_ref, own_ref, a0_ref, a1_ref, a2_ref = refs[:7]
        g_ref, d_ref, m2_ref, v2_ref = refs[-4:]
        g = ((own_ref[...] + a0_ref[...].astype(f32)) + a1_ref[...].astype(f32)) + a2_ref[...].astype(f32)
        delta, m2, v2 = _adam_math(w_ref[...], g, m_ref[...], v_ref[...])
        g_ref[...] = g
        d_ref[...] = delta
        m2_ref[...] = m2
        v2_ref[...] = v2

    threaded = () if prev is None else tuple(prev)
    return pl.pallas_call(
        body, out_shape=(S(w.shape, f32),) * 4, grid=(a // ta,),
        in_specs=[w_blk, w_blk, w_blk, own_blk] + arr_blk + [ANY] * len(threaded), out_specs=(w_blk,) * 4,
        input_output_aliases={7 + k: k for k in range(len(threaded))},
        name=name, compiler_params=_cp("parallel"))(w, m, v, own, arrived, arrived, arrived, *threaded)


def _adam_small(gathered, w, m, v):
    r, wd = w.shape

    def body(g_ref, w_ref, m_ref, v_ref, gs_ref, d_ref, m2_ref, v2_ref):
        g = g_ref[0:r, :]
        for dev in range(1, NDEV):
            g = g + g_ref[dev * r:(dev + 1) * r, :]
        delta, m2, v2 = _adam_math(w_ref[...], g, m_ref[...], v_ref[...])
        gs_ref[...] = g
        d_ref[...] = delta
        m2_ref[...] = m2
        v2_ref[...] = v2

    return pl.pallas_call(body, out_shape=(S((r, wd), f32),) * 4, name="adam_small", compiler_params=_cp())(gathered, w, m, v)


def _pack(arrays, width):
    flat = jnp.concatenate([a.reshape(-1) for a in arrays])
    rows = -(-flat.shape[0] // (8 * width)) * 8
    return jnp.pad(flat, (0, rows * width - flat.shape[0])).reshape(rows, width)


def _unpack(packed, shapes):
    flat = packed.reshape(-1)
    out, off = [], 0
    for shp in shapes:
        n = 1
        for s in shp:
            n *= s
        out.append(flat[off:off + n].reshape(shp))
        off += n
    return out


def kernel(x, meta_tokens, attn_norm_g, w_in, q_norm_g, k_norm_g, attn_sinks, swa_out_g, sb_out_g, w_o, ffn_norm_g, w_gate, w_up, w_down, loss_target, m_meta_tokens, m_attn_norm_g, m_w_in, m_q_norm_g, m_k_norm_g, m_attn_sinks, m_swa_out_g, m_sb_out_g, m_w_o, m_ffn_norm_g, m_w_gate, m_w_up, m_w_down, v_meta_tokens, v_attn_norm_g, v_w_in, v_q_norm_g, v_k_norm_g, v_attn_sinks, v_swa_out_g, v_sb_out_g, v_w_o, v_ffn_norm_g, v_w_gate, v_w_up, v_w_down):
    nl, d = attn_norm_g.shape
    x2, target = x[0], loss_target[0]
    me = 4 * lax.axis_index("x") + 2 * lax.axis_index("y") + lax.axis_index("c")
    ids = _rel_ids()

    meta_all = _ag_small(meta_tokens, "ag_meta")
    meta_full = meta_all.reshape(NDEV, N_META, -1).transpose(1, 0, 2).reshape(N_META, d)
    def t(a):
        return jnp.swapaxes(a, 1, 2)

    w_in_v, w_gate_v, w_up_v = t(w_in), t(w_gate), t(w_up)
    shards = [_prep(w_in_v, "prep_w_in"), _prep(w_o, "prep_w_o"), _prep(w_gate_v, "prep_w_gate"), _prep(w_up_v, "prep_w_up"),
              _prep(w_down, "prep_w_down")]
    attn_w = [_run_side(_gather_side(shards[:2], [0, 0]), "ag_attn_layer0")] + [None] * (nl - 1)
    ffn_w = [None] * nl

    qg2 = jnp.tile(q_norm_g, (1, TILE // HEAD_DIM))
    kg2 = jnp.tile(k_norm_g, (1, TILE // HEAD_DIM))
    out_g = jnp.concatenate([swa_out_g, sb_out_g], axis=1)
    swa_cols = (SWA_HEADS + 2 * SWA_KV_HEADS) * HEAD_DIM

    h = _embed(x2, meta_full)
    saved, weights = [], []
    for l in range(nl):
        w_in_t, w_o_f = attn_w[l]
        hn = _rmsnorm_fwd([h], attn_norm_g[l:l + 1], "attn_norm")
        proj = _mm(hn, w_in_t, 0, "nt", "proj_swa", cols=(0, swa_cols))
        proj_sb = _mm(hn, w_in_t, 0, "nt", "proj_sb", out_dtype=bf16, cols=(swa_cols, 3 * SB_HEADS * HEAD_DIM))
        out_a = _swa_fwd(proj, qg2[l:l + 1], kg2[l:l + 1], attn_sinks[l])
        nxt = l + 1 < nl
        riding = ([(shards[0], l + 1), (shards[1], l + 1)] if nxt else []) + ([(s, 0) for s in shards[2:]] if l == 0 else [])
        out_b, gathered = _sb_fwd(proj_sb, _gather_side([s for s, _ in riding], [k for _, k in riding]) if riding else None)
        if nxt:
            attn_w[l + 1] = gathered[:2]
        if l == 0:
            ffn_w[0] = gathered[-3:]
        w_gate_t, w_up_t, w_down_f = ffn_w[l]
        weights.append((w_in_t, w_o_f, w_gate_t, w_up_t, w_down_f))
        mixed = _rmsnorm_fwd([out_a, out_b], out_g[l:l + 1], "out_norm")
        h_mid = _mm(mixed, w_o_f, 0, "nn", "attn_out", res=h)
        hn2 = _rmsnorm_fwd([h_mid], ffn_norm_g[l:l + 1], "ffn_norm")
        if nxt:
            g, u, act, gate_up = _ffn_up(hn2, w_gate_t, w_up_t, _gather_side(shards[2:4], [l + 1, l + 1]))
            h_out, down = _mm(act, w_down_f, 0, "nn", "ffn_down", res=h_mid, side=_gather_side(shards[4:], [l + 1]))
            ffn_w[l + 1] = gate_up + down
        else:
            g, u, act, _ = _ffn_up(hn2, w_gate_t, w_up_t)
            h_out = _mm(act, w_down_f, 0, "nn", "ffn_down", res=h_mid)
        saved.append((h, hn, proj, proj_sb, out_a, out_b, mixed, h_mid, hn2, g, u, act))
        h = h_out

    loss_tile, dh, dh_b = _loss_head(h, target)
    loss = lax.psum(loss_tile[0, 0], AXES)

    small = {k: [None] * nl for k in ("attn", "ffn", "out", "q", "k", "sink")}
    pending = None
    red_attn = [None] * nl
    red_ff = [None] * nl

    def attn_partial_sums(slabs, sibs):
        return [_rs_partial(gg, sib, ids, name) for gg, sib, name in zip(slabs, sibs, ("rs_part_w_in", "rs_part_w_o"))]

    for l in reversed(range(nl)):
        w_in_t, w_o_f, w_gate_t, w_up_t, w_down_f = weights[l]
        h_in, hn, proj, proj_sb, out_a, out_b, mixed, h_mid, hn2, g, u, act = saved[l]
        dg, du, sibs = _ffn_down_bwd(dh_b, w_down_f, g, u, None if pending is None else _swap_side(pending))
        parts = [] if pending is None else attn_partial_sums(pending, sibs)
        g_ff = _mm_tn(act, dh_b, None, 2, 3, "grad_w_down")
        g_ff = _mm_tn(dg, hn2, g_ff, 0, 3, "grad_w_gate")
        g_ff = _mm_tn(du, hn2, g_ff, 1, 3, "grad_w_up")
        d_gate, (sib_ff,) = _mm(dg, w_gate_t, 0, "nn", "d_ffn_in_gate", side=_swap_side([g_ff]))
        part_ff = _rs_partial(g_ff, sib_ff, ids, "rs_part_w_ff")
        dhn2 = _mm(du, w_up_t, 0, "nn", "d_ffn_in_up", res=d_gate)
        (dh_mid,), small["ffn"][l], dh_mid_b = _rmsnorm_bwd(dhn2, [h_mid], ffn_norm_g[l:l + 1], dh, "ffn_norm_bwd")
        dmixed = _mm(dh_mid_b, w_o_f, 0, "nt", "d_mixed")
        g_o = _mm_tn(mixed, dh_mid_b, None, 0, 1, "grad_w_o")
        (dout_a, dout_b), small["out"][l] = _rmsnorm_bwd(dmixed, [out_a, out_b], out_g[l:l + 1], None, "out_norm_bwd", bf16)
        dq_b, dk_b, dv_b, (arrived_ff,) = _sb_bwd(proj_sb, dout_b, _ici_side([part_ff[1]]))
        red_ff[l] = (part_ff[0], arrived_ff)
        dq_a, dk_a, dv_a, dqg, dkg, dsk = _swa_bwd(proj, dout_a, qg2[l:l + 1], kg2[l:l + 1], attn_sinks[l])
        small["q"][l] = dqg[0, :HEAD_DIM] + dqg[0, HEAD_DIM:]
        small["k"][l] = dkg[0, :HEAD_DIM] + dkg[0, HEAD_DIM:]
        small["sink"][l] = dsk[0, :SWA_HEADS]
        dproj = jnp.concatenate([dq_a, dk_a, dv_a, dq_b, dk_b, dv_b], axis=1)
        if parts:
            dhn, arrived = _mm(dproj, w_in_t, 0, "nn", "d_attn_in", side=_ici_side([p[1] for p in parts]))
            red_attn[l + 1] = ([p[0] for p in parts], arrived)
        else:
            dhn = _mm(dproj, w_in_t, 0, "nn", "d_attn_in")
        g_in = _mm_tn(dproj, hn, None, 0, 1, "grad_w_in")
        (dh,), small["attn"][l], dh_b = _rmsnorm_bwd(dhn, [h_in], attn_norm_g[l:l + 1], dh_mid, "attn_norm_bwd")
        pending = [g_in, g_o]

    parts = attn_partial_sums(pending, _run_side(_swap_side(pending), "rs_swap_attn_layer0"))
    red_attn[0] = ([p[0] for p in parts], list(_run_side(_ici_side([p[1] for p in parts]), "rs_ici_attn_layer0")))

    grad_x = dh[TILE:][None]
    d_meta = dh[PAD:TILE]

    qw = SWA_HEADS * HEAD_DIM
    d_out = jnp.concatenate(small["out"], axis=0)
    small_grads = [jnp.concatenate(small["attn"], axis=0), jnp.stack(small["q"]), jnp.stack(small["k"]), jnp.stack(small["sink"]),
                   d_out[:, :qw], d_out[:, qw:], jnp.concatenate(small["ffn"], axis=0), d_meta]
    col0 = me * meta_tokens.shape[1]

    def widen(a):
        return lax.dynamic_update_slice(jnp.zeros((N_META, d), f32), a, (0, col0))

    small_w = [attn_norm_g, q_norm_g, k_norm_g, attn_sinks, swa_out_g, sb_out_g, ffn_norm_g]
    small_m = [m_attn_norm_g, m_q_norm_g, m_k_norm_g, m_attn_sinks, m_swa_out_g, m_sb_out_g, m_ffn_norm_g]
    small_v = [v_attn_norm_g, v_q_norm_g, v_k_norm_g, v_attn_sinks, v_swa_out_g, v_sb_out_g, v_ffn_norm_g]
    shapes = [a.shape for a in small_w] + [(N_META, d)]
    packed = _adam_small(_ag_small(_pack(small_grads, d), "ag_small_grads"),
                         _pack(small_w + [widen(meta_tokens)], d), _pack(small_m + [widen(m_meta_tokens)], d),
                         _pack(small_v + [widen(v_meta_tokens)], d))
    small_out = []
    for p in packed:
        parts = _unpack(p, shapes)
        parts[-1] = lax.dynamic_slice(parts[-1], (0, col0), meta_tokens.shape)
        small_out.append(parts)

    big = dict.fromkeys(("w_in", "w_o", "w_gate", "w_up", "w_down"))
    for l in range(nl):
        (own_in, own_o), (arr_in, arr_o) = red_attn[l]
        own_ff, arr_ff = red_ff[l]
        big["w_in"] = _adam_big(w_in_v, t(m_w_in), t(v_w_in), own_in, arr_in, l, 0, "adam_w_in", big["w_in"])
        big["w_o"] = _adam_big(w_o, m_w_o, v_w_o, own_o, arr_o, l, 0, "adam_w_o", big["w_o"])
        big["w_gate"] = _adam_big(w_gate_v, t(m_w_gate), t(v_w_gate), own_ff, arr_ff, l, 0, "adam_w_gate", big["w_gate"])
        big["w_up"] = _adam_big(w_up_v, t(m_w_up), t(v_w_up), own_ff, arr_ff, l, 1, "adam_w_up", big["w_up"])
        big["w_down"] = _adam_big(w_down, m_w_down, v_w_down, own_ff, arr_ff, l, 2, "adam_w_down", big["w_down"])
    for name in ("w_in", "w_gate", "w_up"):
        big[name] = [t(a) for a in big[name]]

    def group(k):
        sm = small_out[k]
        return [sm[7], sm[0], big["w_in"][k], sm[1], sm[2], sm[3], sm[4], sm[5], big["w_o"][k], sm[6],
                big["w_gate"][k], big["w_up"][k], big["w_down"][k]]

    return (loss, grad_x, *group(0), *group(1), *group(2), *group(3))
```

```python
import jax
import jax.numpy as jnp
from jax import lax
from jax.experimental import pallas as pl
from jax.experimental.pallas import tpu as pltpu

f32, bf16 = jnp.float32, jnp.bfloat16
S = jax.ShapeDtypeStruct
BS = pl.BlockSpec

N_META = 16
TILE = 128
PAD = TILE - N_META
HEAD_DIM = 64
SWA_HEADS = 16
SWA_KV_HEADS = 4
SB_HEADS = 16
EPS = 1e-6
NEG = -1e30
ADAM_LR, ADAM_B1, ADAM_B2, ADAM_EPS, ADAM_WD, ADAM_STEP = 0.001, 0.9, 0.999, 1e-08, 0.01, 10
AXES = ("x", "y", "c")
NDEV = 8
VMEM_LIMIT_V7X = 56 * 1024 * 1024
MM_RESIDENT_K = 2048
TN_VMEM_BUDGET = 46 * 1024 * 1024
MESH = pl.DeviceIdType.MESH
ANY = pl.BlockSpec(memory_space=pl.ANY)

NN = ((1,), (0,))
NT = ((1,), (1,))
TN = ((0,), (0,))


def _cp(*sem):
    return pltpu.CompilerParams(dimension_semantics=sem if sem else None, vmem_limit_bytes=VMEM_LIMIT_V7X)


def _pick(n, cands):
    for c in cands:
        if n % c == 0:
            return c
    raise ValueError(f"no tile for {n} in {cands}")


def _dot(a, b, dims):
    return lax.dot_general(a, b, (dims, ((), ())), preferred_element_type=f32)


def _embed(x, meta_full):
    seq, d = x.shape
    meta_pad = jnp.pad(meta_full, ((PAD, 0), (0, 0)))

    def body(x_ref, m_ref, o_ref):
        i = pl.program_id(0)

        @pl.when(i == 0)
        def _():
            o_ref[...] = m_ref[...]

        @pl.when(i > 0)
        def _():
            o_ref[...] = x_ref[...]

    return pl.pallas_call(
        body, out_shape=S((TILE + seq, d), f32), grid=(1 + seq // TILE,),
        in_specs=[BS((TILE, d), lambda i: (jnp.maximum(i - 1, 0), 0)), BS((TILE, d), lambda i: (0, 0))],
        out_specs=BS((TILE, d), lambda i: (i, 0)), name="embed", compiler_params=_cp("arbitrary"))(x, meta_pad)


def _rmsnorm_fwd(xs, gain, name):
    lp = xs[0].shape[0]
    widths = [x.shape[1] for x in xs]
    w = sum(widths)
    tr = _pick(lp, (384, 256, 128))
    n = len(xs)

    def body(*refs):
        g_ref, o_ref = refs[n], refs[n + 1]
        off = 0
        for k in range(n):
            sl = slice(off, off + widths[k])
            off += widths[k]
            xv = refs[k][...]
            r = lax.rsqrt(jnp.mean(xv * xv, axis=-1, keepdims=True) + EPS)
            o_ref[:, sl] = ((xv * r) * g_ref[:, sl]).astype(bf16)

    return pl.pallas_call(
        body, out_shape=S((lp, w), bf16), grid=(lp // tr,),
        in_specs=[BS((tr, wk), lambda i: (i, 0)) for wk in widths] + [BS((1, w), lambda i: (0, 0))],
        out_specs=BS((tr, w), lambda i: (i, 0)), name=name, compiler_params=_cp("parallel"))(*xs, gain)


def _rmsnorm_bwd(dy, xs, gain, res, name, out_dtype=f32):
    lp = xs[0].shape[0]
    widths = [x.shape[1] for x in xs]
    w = sum(widths)
    tr = _pick(lp, (384, 256, 128))
    n = len(xs)
    has_res = res is not None
    assert not has_res or n == 1

    def body(*refs):
        dy_ref, g_ref = refs[0], refs[1 + n]
        res_ref = refs[2 + n] if has_res else None
        dx_refs = refs[2 + n + has_res:2 + 2 * n + has_res]
        dg_ref = refs[2 + 2 * n + has_res]

        @pl.when(pl.program_id(0) == 0)
        def _():
            dg_ref[...] = jnp.zeros_like(dg_ref)

        off = 0
        for k in range(n):
            sl = slice(off, off + widths[k])
            off += widths[k]
            xv = refs[1 + k][...]
            dyv = dy_ref[:, sl]
            r = lax.rsqrt(jnp.mean(xv * xv, axis=-1, keepdims=True) + EPS)
            xh = xv * r
            dg_ref[:, sl] += jnp.sum(dyv * xh, axis=0, keepdims=True)
            dxh = dyv * g_ref[:, sl]
            dx = r * (dxh - xh * jnp.mean(dxh * xh, axis=-1, keepdims=True))
            if has_res:
                dx = dx + res_ref[...]
                refs[-1][...] = dx.astype(bf16)
            dx_refs[k][...] = dx.astype(out_dtype)

    rows = [BS((tr, wk), lambda i: (i, 0)) for wk in widths]
    full = BS((tr, w), lambda i: (i, 0))
    vec = BS((1, w), lambda i: (0, 0))
    args = (dy, *xs, gain) + ((res,) if has_res else ())
    out = pl.pallas_call(
        body, out_shape=tuple(S((lp, wk), out_dtype) for wk in widths) + (S((1, w), f32),) + ((S((lp, w), bf16),) if has_res else ()),
        grid=(lp // tr,),
        in_specs=[full] + rows + [vec] + ([full] if has_res else []), out_specs=tuple(rows) + (vec,) + ((full,) if has_res else ()),
        name=name, compiler_params=_cp("arbitrary"))(*args)
    return (list(out[:n]), out[n], out[n + 1]) if has_res else (list(out[:n]), out[n])


def _loss_head(h, target):
    lp, d = h.shape

    def body(h_ref, t_ref, loss_ref, dh_ref, dhb_ref):
        i = pl.program_id(0)

        @pl.when(i == 0)
        def _():
            loss_ref[...] = jnp.zeros_like(loss_ref)
            dh_ref[...] = jnp.zeros_like(dh_ref)
            dhb_ref[...] = jnp.zeros_like(dhb_ref)

        @pl.when(i > 0)
        def _():
            e = h_ref[...] - t_ref[...]
            dh = e * (1.0 / d)
            dh_ref[...] = dh
            dhb_ref[...] = dh.astype(bf16)
            loss_ref[...] += 0.5 * jnp.sum(jnp.sum(e * e, axis=-1, keepdims=True) * (1.0 / d))

    tile = BS((TILE, d), lambda i: (i, 0))
    return pl.pallas_call(
        body, out_shape=(S((8, 128), f32), S((lp, d), f32), S((lp, d), bf16)), grid=(lp // TILE,),
        in_specs=[tile, BS((TILE, d), lambda i: (jnp.maximum(i - 1, 0), 0))],
        out_specs=(BS((8, 128), lambda i: (0, 0)), tile, tile),
        name="loss_head", compiler_params=_cp("arbitrary"))(h, target)


def _mm(a, b, layer, mode, name, out_dtype=f32, res=None, cols=None, side=None):
    m, k = a.shape
    n = b.shape[2] if mode == "nn" else b.shape[1]
    c0 = 0
    if cols is not None:
        assert mode == "nt"
        c0, n = cols
    tn = _pick(n, (512, 384, 256, 128))
    while c0 % tn:
        tn -= TILE
    assert n % tn == 0
    joff = c0 // tn
    single_a = k > MM_RESIDENT_K and a.dtype == bf16 and m % 1408 == 0
    rows_outer = k <= MM_RESIDENT_K or single_a
    tm = _pick(m, (1408, 384, 256, 128) if rows_outer else (384, 256, 128))
    dims = NN if mode == "nn" else NT
    has_res = res is not None

    grid = (m // tm, n // tn) if rows_outer else (n // tn, m // tm)

    def body(*refs):
        refs, side_refs = _side_split(side, refs, 2 + has_res, 1, 0)
        _side_start(side, side_refs, grid)
        if has_res:
            a_ref, b_ref, r_ref, o_ref = refs
        else:
            a_ref, b_ref, o_ref = refs
        acc = _dot(a_ref[...].astype(bf16), b_ref[...], dims)
        if has_res:
            acc = acc + r_ref[...]
        o_ref[...] = acc.astype(out_dtype)
        _side_finish(side, side_refs, grid)

    def ij(f):
        return (lambda i, j: f(i, j)) if rows_outer else (lambda j, i: f(i, j))

    if mode == "nn":
        b_spec = BS((None, k, tn), ij(lambda i, j: (layer, 0, j)))
    else:
        b_spec = BS((None, tn, k), ij(lambda i, j: (layer, joff + j, 0)))
    tile = BS((tm, tn), ij(lambda i, j: (i, j)))
    args = (a, b) + ((res,) if has_res else ())
    s_in, s_in_specs, s_out, s_out_specs, s_scratch = _side_io(side)
    sem = ("parallel", "parallel") if side is None else ("arbitrary", "arbitrary")
    out = pl.pallas_call(
        body, out_shape=(S((m, n), out_dtype), *s_out), grid=grid,
        in_specs=[BS((tm, k), ij(lambda i, j: (i, 0)), **(dict(pipeline_mode=pl.Buffered(1)) if single_a else {})), b_spec]
        + ([tile] if has_res else []) + s_in_specs,
        out_specs=(tile, *s_out_specs), scratch_shapes=s_scratch,
        name=name if side is None else name + "_comm", compiler_params=_cp(*sem))(*args, *s_in)
    return out[0] if side is None else (out[0], list(out[1:]))


def _mm_tn(a, b, gbuf, slot, nslots, name):
    t, m = a.shape
    n = b.shape[1]
    tk = _pick(t, (1408, 1024, 512, 384, 256, 128))
    nk = t // tk

    def vmem(rows):
        return 2 * (rows * n * 4 + tk * rows * a.dtype.itemsize + tk * n * b.dtype.itemsize)

    tm = next(c for c in (1536, 1408, 1024, 512, 384, 256, 128) if m % c == 0 and (vmem(c) <= TN_VMEM_BUDGET or c == 128))

    def body(*refs):
        a_ref, b_ref, o_ref = refs[0], refs[1], refs[-1]
        kk = pl.program_id(1)

        @pl.when(kk == 0)
        def _():
            o_ref[...] = jnp.zeros_like(o_ref)

        o_ref[...] += _dot(a_ref[...].astype(bf16), b_ref[...].astype(bf16), TN)

    in_specs = [BS((tk, tm), lambda i, kk: (kk, i)), BS((tk, n), lambda i, kk: (kk, 0))]
    args = (a, b)
    aliases = {}
    if gbuf is not None:
        in_specs.append(ANY)
        args = args + (gbuf,)
        aliases = {2: 0}
    return pl.pallas_call(
        body, out_shape=S((nslots, m, n), f32), grid=(m // tm, nk), in_specs=in_specs,
        out_specs=BS((None, tm, n), lambda i, kk: (slot, i, 0)), input_output_aliases=aliases,
        name=name, compiler_params=_cp("parallel", "arbitrary"))(*args)


def _ffn_up(hn, wg, wu, side=None):
    m, k = hn.shape
    n = wg.shape[1]
    tm = _pick(m, (1408, 384, 256, 128))
    tn = _pick(n, (512, 384, 256, 128))
    grid = (m // tm, n // tn)

    def body(*refs):
        (a_ref, wg_ref, wu_ref, g_ref, u_ref, act_ref), side_refs = _side_split(side, refs, 3, 3, 0)
        _side_start(side, side_refs, grid)
        a = a_ref[...]
        g = _dot(a, wg_ref[...], NT)
        u = _dot(a, wu_ref[...], NT)
        g_ref[...] = g
        u_ref[...] = u
        act_ref[...] = (g * jax.nn.sigmoid(g) * u).astype(bf16)
        _side_finish(side, side_refs, grid)

    tile = BS((tm, tn), lambda i, j: (i, j))
    s_in, s_in_specs, s_out, s_out_specs, s_scratch = _side_io(side)
    sem = ("parallel", "parallel") if side is None else ("arbitrary", "arbitrary")
    out = pl.pallas_call(
        body, out_shape=(S((m, n), f32), S((m, n), f32), S((m, n), bf16), *s_out), grid=grid,
        in_specs=[BS((tm, k), lambda i, j: (i, 0)), BS((None, tn, k), lambda i, j: (0, j, 0)),
                  BS((None, tn, k), lambda i, j: (0, j, 0)), *s_in_specs],
        out_specs=(tile, tile, tile, *s_out_specs), scratch_shapes=s_scratch,
        name="ffn_up" if side is None else "ffn_up_comm", compiler_params=_cp(*sem))(hn, wg, wu, *s_in)
    return out[0], out[1], out[2], list(out[3:])


def _ffn_down_bwd(dh, wd, g, u, side=None):
    m, k = dh.shape
    n = wd.shape[1]
    tm = _pick(m, (1408, 384, 256, 128))
    tn = _pick(n, (512, 384, 256, 128))
    grid = (m // tm, n // tn)

    def body(*refs):
        (a_ref, wd_ref, g_ref, u_ref, dg_ref, du_ref), side_refs = _side_split(side, refs, 4, 2, 0)
        _side_start(side, side_refs, grid)
        dact = _dot(a_ref[...].astype(bf16), wd_ref[...], NT)
        gv = g_ref[...]
        sg = jax.nn.sigmoid(gv)
        dg_ref[...] = (dact * u_ref[...] * (sg * (1.0 + gv * (1.0 - sg)))).astype(bf16)
        du_ref[...] = (dact * (gv * sg)).astype(bf16)
        _side_finish(side, side_refs, grid)

    tile = BS((tm, tn), lambda i, j: (i, j))
    s_in, s_in_specs, s_out, s_out_specs, s_scratch = _side_io(side)
    sem = ("parallel", "parallel") if side is None else ("arbitrary", "arbitrary")
    out = pl.pallas_call(
        body, out_shape=(S((m, n), bf16), S((m, n), bf16), *s_out), grid=grid,
        in_specs=[BS((tm, k), lambda i, j: (i, 0)), BS((None, tn, k), lambda i, j: (0, j, 0)), tile, tile, *s_in_specs],
        out_specs=(tile, tile, *s_out_specs), scratch_shapes=s_scratch,
        name="ffn_down_bwd" if side is None else "ffn_down_bwd_comm", compiler_params=_cp(*sem))(dh, wd, g, u, *s_in)
    return out[0], out[1], list(out[2:])


def _lane_lo():
    return lax.broadcasted_iota(jnp.int32, (1, TILE), 1) < HEAD_DIM


def _head_norm(x, g2):
    lo = _lane_lo()
    x2 = x * x
    s_lo = jnp.sum(jnp.where(lo, x2, 0.0), axis=-1, keepdims=True)
    s_hi = jnp.sum(jnp.where(lo, 0.0, x2), axis=-1, keepdims=True)
    r = jnp.where(lo, lax.rsqrt(s_lo * (1.0 / HEAD_DIM) + EPS), lax.rsqrt(s_hi * (1.0 / HEAD_DIM) + EPS))
    xh = x * r
    return xh * g2, xh, r


def _head_norm_bwd(dy, xh, r, g2):
    lo = _lane_lo()
    dgain = jnp.sum(dy * xh, axis=0, keepdims=True)
    dxh = dy * g2
    t = dxh * xh
    t_lo = jnp.sum(jnp.where(lo, t, 0.0), axis=-1, keepdims=True)
    t_hi = jnp.sum(jnp.where(lo, 0.0, t), axis=-1, keepdims=True)
    mt = jnp.where(lo, t_lo, t_hi) * (1.0 / HEAD_DIM)
    return r * (dxh - xh * mt), dgain


def _swa_slopes():
    return [2.0 ** (-8.0 * (h + 1) / SWA_HEADS) for h in range(SWA_HEADS)]


def _swa_masks(i):
    rows = lax.broadcasted_iota(jnp.int32, (TILE, TILE), 0)
    cols = lax.broadcasted_iota(jnp.int32, (TILE, TILE), 1)
    r_pos = i * TILE + rows
    prev = jnp.maximum(i - 1, 0)
    out = []
    for c, base in enumerate((0, prev * TILE, i * TILE)):
        s_pos = base + cols
        dist = r_pos - s_pos
        if c == 0:
            ok = (i >= 1) & (s_pos >= PAD)
        elif c == 1:
            ok = (i >= 2) & (dist < TILE)
        else:
            ok = (dist >= 0) & (s_pos >= PAD)
        out.append((ok, dist.astype(f32)))
    return out, prev


def _swa_geometry(i, group):
    masks, prev = _swa_masks(i)
    bases = (0, pl.multiple_of(prev * TILE, TILE), pl.multiple_of(i * TILE, TILE))
    ok = jnp.concatenate([m[0] for m in masks], axis=1)
    dist = jnp.concatenate([m[1] for m in masks], axis=1)
    return bases, jnp.concatenate([ok] * group, axis=0), jnp.concatenate([dist] * group, axis=0)


def _swa_col(values):
    return jnp.concatenate([jnp.full((TILE, 1), v, f32) for v in values], axis=0)


def _swa_keys(k_ref, v_ref, kg_ref, bases, kvh):
    kcols = slice((kvh // 2) * TILE, (kvh // 2 + 1) * TILE)
    ksel = _lane_lo() if kvh % 2 == 0 else jnp.logical_not(_lane_lo())
    kcat = jnp.concatenate([_head_norm(k_ref[pl.ds(b, TILE), kcols], kg_ref[...])[0] for b in bases], axis=0).astype(bf16)
    vcat = jnp.concatenate([jnp.where(ksel, v_ref[pl.ds(b, TILE), kcols], 0.0) for b in bases], axis=0).astype(bf16)
    return kcat, vcat, ksel


def _swa_stack(blocks, kvh, group, ksel):
    parts = []
    for gi in range(group):
        same = ((kvh * group + gi) % 2) == (kvh % 2)
        parts.append(jnp.where(ksel, blocks[gi] if same else pltpu.roll(blocks[gi], HEAD_DIM, 1), 0.0))
    return jnp.concatenate(parts, axis=0)


def _swa_unstack_add(acc, stacked, kvh, group):
    for gi in range(group):
        h = kvh * group + gi
        part = stacked[gi * TILE:(gi + 1) * TILE]
        acc[h // 2] = acc[h // 2] + (part if (h % 2) == (kvh % 2) else pltpu.roll(part, HEAD_DIM, 1))


def _swa_probs(q4, kcat, ok_g, dist_g, slope_col, sink_col):
    logits = _dot(q4, kcat, NT) * (HEAD_DIM ** -0.5) + jnp.where(ok_g, -slope_col * dist_g, NEG)
    mx = jnp.maximum(jnp.max(logits, axis=-1, keepdims=True), sink_col)
    return jnp.exp(logits - mx), jnp.exp(sink_col - mx)


def _swa_specs(lp, pw):
    qw, kvw = SWA_HEADS * HEAD_DIM, SWA_KV_HEADS * HEAD_DIM
    kidx = qw // kvw
    return (BS((TILE, qw), lambda i: (i, 0)), BS((lp, kvw), lambda i: (0, kidx)), BS((lp, kvw), lambda i: (0, kidx + 1)),
            BS((1, TILE), lambda i: (0, 0)))


def _swa_fwd(proj, qg2, kg2, sinks):
    lp, pw = proj.shape
    qw = SWA_HEADS * HEAD_DIM
    group = SWA_HEADS // SWA_KV_HEADS
    slopes = _swa_slopes()

    def body(sink_ref, q_ref, k_ref, v_ref, qg_ref, kg_ref, o_ref):
        i = pl.program_id(0)
        bases, ok_g, dist_g = _swa_geometry(i, group)
        qn = [_head_norm(q_ref[:, b * TILE:(b + 1) * TILE], qg_ref[...])[0] for b in range(qw // TILE)]
        acc = [jnp.zeros((TILE, TILE), f32) for _ in range(qw // TILE)]
        for kvh in range(SWA_KV_HEADS):
            kcat, vcat, ksel = _swa_keys(k_ref, v_ref, kg_ref, bases, kvh)
            q4 = _swa_stack([qn[(kvh * group + gi) // 2] for gi in range(group)], kvh, group, ksel).astype(bf16)
            sink_col = _swa_col([sink_ref[kvh * group + gi] for gi in range(group)])
            p, e_sink = _swa_probs(q4, kcat, ok_g, dist_g, _swa_col(slopes[kvh * group:(kvh + 1) * group]), sink_col)
            den = e_sink + jnp.sum(p, axis=-1, keepdims=True)
            o = _dot(p.astype(bf16), vcat, NN) / den
            _swa_unstack_add(acc, o, kvh, group)
        valid = (i * TILE + lax.broadcasted_iota(jnp.int32, (TILE, 1), 0)) >= PAD
        for b in range(qw // TILE):
            o_ref[:, b * TILE:(b + 1) * TILE] = jnp.where(valid, acc[b], 0.0)

    q_spec, k_spec, v_spec, vec = _swa_specs(lp, pw)
    return pl.pallas_call(
        body, out_shape=S((lp, qw), f32), grid=(lp // TILE,),
        in_specs=[BS(memory_space=pltpu.SMEM), q_spec, k_spec, v_spec, vec, vec],
        out_specs=BS((TILE, qw), lambda i: (i, 0)), name="swa_fwd", compiler_params=_cp("arbitrary"))(
            sinks, proj, proj, proj, qg2, kg2)


def _swa_bwd(proj, dout, qg2, kg2, sinks):
    lp, pw = proj.shape
    qw, kvw = SWA_HEADS * HEAD_DIM, SWA_KV_HEADS * HEAD_DIM
    group = SWA_HEADS // SWA_KV_HEADS
    scale = HEAD_DIM ** -0.5
    slopes = _swa_slopes()
    nt = lp // TILE

    def body(sink_ref, q_ref, k_ref, v_ref, do_ref, qg_ref, kg_ref, dq_ref, dk_ref, dv_ref, dqg_ref, dkg_ref, ds_ref,
             dkn_acc, dv_acc):
        i = pl.program_id(0)
        lo = _lane_lo()
        lane = lax.broadcasted_iota(jnp.int32, (1, TILE), 1)

        @pl.when(i == 0)
        def _():
            dkn_acc[...] = jnp.zeros_like(dkn_acc)
            dv_acc[...] = jnp.zeros_like(dv_acc)
            dqg_ref[...] = jnp.zeros_like(dqg_ref)
            dkg_ref[...] = jnp.zeros_like(dkg_ref)
            ds_ref[...] = jnp.zeros_like(ds_ref)

        bases, ok_g, dist_g = _swa_geometry(i, group)
        qnorm = [_head_norm(q_ref[:, b * TILE:(b + 1) * TILE], qg_ref[...]) for b in range(qw // TILE)]
        dqn = [jnp.zeros((TILE, TILE), f32) for _ in range(qw // TILE)]
        for kvh in range(SWA_KV_HEADS):
            kcols = slice((kvh // 2) * TILE, (kvh // 2 + 1) * TILE)
            heads = [kvh * group + gi for gi in range(group)]
            kcat, vcat, ksel = _swa_keys(k_ref, v_ref, kg_ref, bases, kvh)
            q4 = _swa_stack([qnorm[h // 2][0] for h in heads], kvh, group, ksel).astype(bf16)
            do4 = _swa_stack([do_ref[:, (h // 2) * TILE:(h // 2 + 1) * TILE].astype(f32) for h in heads], kvh, group, ksel).astype(bf16)
            sink_col = _swa_col([sink_ref[h] for h in heads])
            p, e_sink = _swa_probs(q4, kcat, ok_g, dist_g, _swa_col([slopes[h] for h in heads]), sink_col)
            inv = 1.0 / (e_sink + jnp.sum(p, axis=-1, keepdims=True))
            p = p * inv
            dp = _dot(do4, vcat, NT)
            dsum = jnp.sum(p * dp, axis=-1, keepdims=True)
            d_sink = e_sink * inv * dsum
            for gi, h in enumerate(heads):
                ds_ref[...] += jnp.where(lane == h, -jnp.sum(d_sink[gi * TILE:(gi + 1) * TILE]), 0.0)
            dsc = (p * (dp - dsum) * scale).astype(bf16)
            _swa_unstack_add(dqn, jnp.where(ksel, _dot(dsc, kcat, NN), 0.0), kvh, group)
            dkn = _dot(dsc, q4, TN)
            dvv = _dot(p.astype(bf16), do4, TN)
            for c in range(3):
                dkn_acc[pl.ds(bases[c], TILE), kcols] += dkn[c * TILE:(c + 1) * TILE]
                dv_acc[pl.ds(bases[c], TILE), kcols] += dvv[c * TILE:(c + 1) * TILE]
        for b in range(qw // TILE):
            _, xh, r = qnorm[b]
            dq, dgain = _head_norm_bwd(dqn[b], xh, r, qg_ref[...])
            dq_ref[:, b * TILE:(b + 1) * TILE] = dq.astype(bf16)
            dqg_ref[...] += dgain

        @pl.when(i == nt - 1)
        def _():
            dv_ref[...] = dv_acc[...].astype(bf16)

            def tile_step(t, carry):
                base = pl.multiple_of(t * TILE, TILE)
                for kb in range(kvw // TILE):
                    kcols = slice(kb * TILE, (kb + 1) * TILE)
                    _, xh, r = _head_norm(k_ref[pl.ds(base, TILE), kcols], kg_ref[...])
                    dk, dgain = _head_norm_bwd(dkn_acc[pl.ds(base, TILE), kcols], xh, r, kg_ref[...])
                    dk_ref[pl.ds(base, TILE), kcols] = dk.astype(bf16)
                    dkg_ref[...] += dgain
                return carry

            lax.fori_loop(0, nt, tile_step, 0)

    q_spec, k_spec, v_spec, vec = _swa_specs(lp, pw)
    whole = BS((lp, kvw), lambda i: (0, 0))
    return pl.pallas_call(
        body, out_shape=(S((lp, qw), bf16), S((lp, kvw), bf16), S((lp, kvw), bf16), S((1, TILE), f32), S((1, TILE), f32), S((1, TILE), f32)),
        grid=(nt,),
        in_specs=[BS(memory_space=pltpu.SMEM), q_spec, k_spec, v_spec, BS((TILE, qw), lambda i: (i, 0)), vec, vec],
        out_specs=(BS((TILE, qw), lambda i: (i, 0)), whole, whole, vec, vec, vec),
        scratch_shapes=[pltpu.VMEM((lp, kvw), f32), pltpu.VMEM((lp, kvw), f32)],
        name="swa_bwd", compiler_params=_cp("arbitrary"))(sinks, proj, proj, proj, dout, qg2, kg2)


LOG2E = 1.4426950408889634
SB_PAIRS = 2
ROWS2 = 2 * TILE
SB_ROWS = SB_PAIRS * ROWS2


def _sb_stack(x):
    lo = _lane_lo()
    zero = jnp.zeros((TILE, TILE), x.dtype)
    parts = []
    for p in range(SB_PAIRS):
        xp = x[:, p * TILE:(p + 1) * TILE]
        parts += [jnp.where(lo, xp, zero), jnp.where(lo, zero, xp)]
    return jnp.concatenate(parts, axis=0)


def _sb_unstack(x):
    lo = _lane_lo()
    parts = [jnp.where(lo, x[p * ROWS2:p * ROWS2 + TILE], x[p * ROWS2 + TILE:(p + 1) * ROWS2]) for p in range(SB_PAIRS)]
    return parts[0] if SB_PAIRS == 1 else jnp.concatenate(parts, axis=1)


def _sb_pair_dot(a, b, dims, b_lanes):
    parts = []
    for p in range(SB_PAIRS):
        bp = b[:, p * TILE:(p + 1) * TILE] if b_lanes else b[p * ROWS2:(p + 1) * ROWS2]
        parts.append(_dot(a[p * ROWS2:(p + 1) * ROWS2], bp, dims))
    return parts


def _sb_sum_matrix(after):
    rows = lax.broadcasted_iota(jnp.int32, (ROWS2, ROWS2), 0) & (TILE - 1)
    cols = lax.broadcasted_iota(jnp.int32, (ROWS2, ROWS2), 1)
    tri = (rows > cols) if after else (rows < cols)
    return (tri | (cols >= TILE)).astype(bf16)


def _sb_sums(x, w, sum_mat):
    hi = x.astype(bf16)
    lo = (x - hi.astype(f32)).astype(bf16)
    return [_dot(jnp.concatenate([hi[:, j * TILE:(j + 1) * TILE], lo[:, j * TILE:(j + 1) * TILE]], axis=1), sum_mat, NN)
            for j in range(w)]


def _sb_edge_masks(qi, first, w):
    rows = lax.broadcasted_iota(jnp.int32, (SB_ROWS, TILE), 0) & (TILE - 1)
    cols = lax.broadcasted_iota(jnp.int32, (SB_ROWS, TILE), 1)
    not_pad = first * TILE + cols >= PAD
    before_query = (first + w - 1 - qi) * TILE + cols < rows
    if w == 1:
        return [not_pad & before_query]
    return [not_pad] + [None] * (w - 2) + [before_query]


def _sb_apply(valid, x):
    return x if valid is None else jnp.where(valid, x, 0.0)


def _sb_block(q2, k_ref, first, w, valids, c_rep, after_ones):
    kwin = k_ref[pl.ds(pl.multiple_of(first * TILE, TILE), w * TILE), :]
    z2 = jnp.concatenate(_sb_pair_dot(q2, kwin, NT, True), axis=0) * (HEAD_DIM ** -0.5 * LOG2E)
    ls2 = jnp.minimum(z2, 0.0) - jnp.log2(1.0 + jnp.exp2(jnp.minimum(z2, -z2)))
    m2 = ls2 - z2
    if w == 1:
        m2 = _sb_apply(valids[0], m2)
    else:
        m2 = jnp.concatenate([_sb_apply(valids[j], m2[:, j * TILE:(j + 1) * TILE]) for j in range(w)], axis=1)
    sums = _sb_sums(m2, w, after_ones)
    parts = [None] * w
    for j in reversed(range(w)):
        parts[j] = _sb_apply(valids[j], jnp.exp2(ls2[:, j * TILE:(j + 1) * TILE] + sums[j][:, :TILE] + c_rep))
        c_rep = c_rep + sums[j][:, TILE:]
    return ls2, (parts[0] if w == 1 else jnp.concatenate(parts, axis=1)), c_rep


SB_BLOCKS = (4, 2, 1)
SB_DEAD = -160.0


def _sb_alive(c_rep):
    return jnp.max(c_rep) > SB_DEAD


def _sb_walk_down(qi, block, state, c_of):
    n = qi + 1
    big = SB_BLOCKS[0]

    def more(carry):
        t, st = carry
        return (t < n // big) & _sb_alive(c_of(st))

    def step(carry):
        t, st = carry
        return t + 1, block(n - big * (t + 1), big, st)

    visited, state = lax.while_loop(more, step, (jnp.int32(0), state))
    counts = [visited]
    for s in SB_BLOCKS[1:]:
        cnt = jnp.where(_sb_alive(c_of(state)), (n % (2 * s)) // s, 0)
        state = lax.fori_loop(0, cnt, lambda t, st, s=s: block(n % s, s, st), state)
        counts.append(cnt)
    return state, counts


def _sb_walk_up(qi, block, state, counts):
    n = qi + 1
    big = SB_BLOCKS[0]
    for s, cnt in reversed(list(zip(SB_BLOCKS[1:], counts[1:]))):
        state = lax.fori_loop(0, cnt, lambda t, st, s=s: block(n % s, s, st), state)
    visited = counts[0]
    return lax.fori_loop(0, visited, lambda i, st: block(n - big * (visited - i), big, st), state)


def _sb_cols(lp, off, single=False):
    mode = dict(pipeline_mode=pl.Buffered(1)) if single else {}
    return BS((lp, SB_PAIRS * TILE), lambda g: (0, off + g), **mode)


def _sb_fwd(proj, side=None):
    lp = proj.shape[0]
    sbw = SB_HEADS * HEAD_DIM
    ngrp = sbw // (SB_PAIRS * TILE)
    nt = lp // TILE

    def body(*refs):
        (q_ref, k_ref, v_ref, o_ref), side_refs = _side_split(side, refs, 3, 1, 0)
        _side_start(side, side_refs, (ngrp,))
        after_ones = _sb_sum_matrix(True)

        def q_step(qi, carry):
            qbase = pl.multiple_of(qi * TILE, TILE)
            q2 = _sb_stack(q_ref[pl.ds(qbase, TILE), :])

            def block(first, w, st):
                c_rep, acc = st
                _, a, c_rep = _sb_block(q2, k_ref, first, w, _sb_edge_masks(qi, first, w), c_rep, after_ones)
                vwin = v_ref[pl.ds(pl.multiple_of(first * TILE, TILE), w * TILE), :]
                return c_rep, acc + jnp.concatenate(_sb_pair_dot(a.astype(bf16), vwin, NN, True), axis=0)

            zero = jnp.zeros((SB_ROWS, TILE), f32)
            (_, acc), _ = _sb_walk_down(qi, block, (zero, zero), lambda st: st[0])
            o_ref[pl.ds(qbase, TILE), :] = _sb_unstack(acc)
            return carry

        lax.fori_loop(0, nt, q_step, 0)
        _side_finish(side, side_refs, (ngrp,))

    s_in, s_in_specs, s_out, s_out_specs, s_scratch = _side_io(side)
    out = pl.pallas_call(
        body, out_shape=(S((lp, sbw), f32), *s_out), grid=(ngrp,),
        in_specs=[_sb_cols(lp, 0), _sb_cols(lp, ngrp), _sb_cols(lp, 2 * ngrp), *s_in_specs],
        out_specs=(_sb_cols(lp, 0), *s_out_specs), scratch_shapes=s_scratch,
        name="sb_fwd" if side is None else "sb_fwd_comm",
        compiler_params=_cp("parallel" if side is None else "arbitrary"))(proj, proj, proj, *s_in)
    return out[0], list(out[1:])


def _sb_bwd(proj, dout, side=None):
    lp = proj.shape[0]
    sbw = SB_HEADS * HEAD_DIM
    ngrp = sbw // (SB_PAIRS * TILE)
    nt = lp // TILE
    scale = HEAD_DIM ** -0.5

    def body(*refs):
        (q_ref, k_ref, v_ref, do_ref, dq_ref, dk_ref, dv_ref, dk_acc, dv_acc, e_buf, b_buf), side_refs = _side_split(side, refs, 4, 3, 4)
        _side_start(side, side_refs, (ngrp,))
        dk_acc[...] = jnp.zeros_like(dk_acc)
        dv_acc[...] = jnp.zeros_like(dv_acc)
        after_ones = _sb_sum_matrix(True)
        before_ones = _sb_sum_matrix(False)

        def add_cols(acc_ref, rows, parts):
            for p in range(SB_PAIRS):
                acc_ref[rows, p * TILE:(p + 1) * TILE] += parts[p]

        def q_step(qi, carry):
            qbase = pl.multiple_of(qi * TILE, TILE)
            q2 = _sb_stack(q_ref[pl.ds(qbase, TILE), :])
            do2 = _sb_stack(do_ref[pl.ds(qbase, TILE), :])

            def block_rl(first, w, c_rep):
                ls2, a, c_rep = _sb_block(q2, k_ref, first, w, _sb_edge_masks(qi, first, w), c_rep, after_ones)
                rows = pl.ds(pl.multiple_of(first * TILE, TILE), w * TILE)
                e = jnp.concatenate(_sb_pair_dot(do2, v_ref[rows, :], NT, True), axis=0) * a
                beta = jnp.exp2(ls2)
                for j in range(w):
                    e_buf[first + j] = e[:, j * TILE:(j + 1) * TILE]
                    b_buf[first + j] = beta[:, j * TILE:(j + 1) * TILE]
                add_cols(dv_acc, rows, _sb_pair_dot(a.astype(bf16), do2, TN, False))
                return c_rep

            zero = jnp.zeros((SB_ROWS, TILE), f32)
            _, counts = _sb_walk_down(qi, block_rl, zero, lambda c_rep: c_rep)

            def block_lr(first, w, st):
                e_rep, dq_acc = st
                rows = pl.ds(pl.multiple_of(first * TILE, TILE), w * TILE)
                valids = _sb_edge_masks(qi, first, w)
                es = [e_buf[first + j] for j in range(w)]
                sums = _sb_sums(es[0] if w == 1 else jnp.concatenate(es, axis=1), w, before_ones)
                parts = []
                for j in range(w):
                    beta = b_buf[first + j]
                    parts.append(_sb_apply(valids[j], (es[j] - beta * (es[j] + e_rep + sums[j][:, :TILE])) * scale))
                    e_rep = e_rep + sums[j][:, TILE:]
                dz = (parts[0] if w == 1 else jnp.concatenate(parts, axis=1)).astype(bf16)
                add_cols(dk_acc, rows, _sb_pair_dot(dz, q2, TN, False))
                return e_rep, dq_acc + jnp.concatenate(_sb_pair_dot(dz, k_ref[rows, :], NN, True), axis=0)

            _, dq = _sb_walk_up(qi, block_lr, (zero, zero), counts)
            dq_ref[pl.ds(qbase, TILE), :] = _sb_unstack(dq).astype(bf16)
            return carry

        lax.fori_loop(0, nt, q_step, 0)
        dk_ref[...] = dk_acc[...].astype(bf16)
        dv_ref[...] = dv_acc[...].astype(bf16)
        _side_finish(side, side_refs, (ngrp,))

    wide = SB_PAIRS * TILE
    s_in, s_in_specs, s_out, s_out_specs, s_scratch = _side_io(side)
    out = pl.pallas_call(
        body, out_shape=(S((lp, sbw), bf16),) * 3 + tuple(s_out), grid=(ngrp,),
        in_specs=[_sb_cols(lp, 0, True), _sb_cols(lp, ngrp, True), _sb_cols(lp, 2 * ngrp, True), _sb_cols(lp, 0, True), *s_in_specs],
        out_specs=(_sb_cols(lp, 0, True),) * 3 + tuple(s_out_specs),
        scratch_shapes=[pltpu.VMEM((lp, wide), f32), pltpu.VMEM((lp, wide), f32),
                        pltpu.VMEM((nt, SB_ROWS, TILE), f32), pltpu.VMEM((nt, SB_ROWS, TILE), f32), *s_scratch],
        name="sb_bwd" if side is None else "sb_bwd_comm",
        compiler_params=_cp("parallel" if side is None else "arbitrary"))(proj, proj, proj, dout, *s_in)
    return out[0], out[1], out[2], list(out[3:])


def _place():
    x, y, c = lax.axis_index("x"), lax.axis_index("y"), lax.axis_index("c")
    chips = [(1 - x, y), (x, 1 - y), (1 - x, 1 - y)]
    return x, y, c, chips


def _two_level_gather_body(rows, x_ref, send_sems, recv_sems, local_sem):
    x, y, c, chips = _place()
    me, sibling = (x, y, c), (x, y, 1 - c)

    def copy(k, block, to, src=None):
        return pltpu.make_async_remote_copy(
            src_ref=rows(*block) if src is None else src, dst_ref=rows(*block),
            send_sem=send_sems.at[k], recv_sem=recv_sems.at[k], device_id=to, device_id_type=MESH)

    mine = pltpu.make_async_copy(x_ref, rows(*me), local_sem)
    mine.start()
    first = [copy(0, me, sibling, src=x_ref)]
    first += [copy(1 + j, me, (*chip, c), src=x_ref) for j, chip in enumerate(chips)]
    for cp in first:
        cp.start()
    passed = [copy(4 + j, (*chip, c), sibling) for j, chip in enumerate(chips)]
    for j, chip in enumerate(chips):
        copy(1 + j, (*chip, c), me).wait_recv()
        passed[j].start()
    copy(0, sibling, me).wait_recv()
    for j, chip in enumerate(chips):
        copy(4 + j, (*chip, 1 - c), me).wait_recv()
    for cp in first + passed:
        cp.wait_send()
    mine.wait()


_GATHER_SEMS = [pltpu.SemaphoreType.DMA((7,)), pltpu.SemaphoreType.DMA((7,)), pltpu.SemaphoreType.DMA]


class _Side:
    def __init__(self, inputs, out_shapes, nsem, start, finish):
        self.inputs, self.out_shapes, self.start, self.finish = list(inputs), list(out_shapes), start, finish
        self.scratch = [pltpu.SemaphoreType.DMA((nsem,)), pltpu.SemaphoreType.DMA((nsem,)), pltpu.SemaphoreType.DMA((len(inputs),))]


def _side_io(side):
    if side is None:
        return [], [], [], [], []
    return side.inputs, [ANY] * len(side.inputs), side.out_shapes, [ANY] * len(side.out_shapes), side.scratch


def _side_split(side, refs, n_in, n_out, n_scratch):
    if side is None:
        return refs, None
    si, so = len(side.inputs), len(side.out_shapes)
    a, b, c, d = n_in, n_in + si, n_in + si + n_out, n_in + si + n_out + so
    return refs[:a] + refs[b:c] + refs[d:d + n_scratch], (refs[a:b], refs[c:d], refs[d + n_scratch:])


def _side_first_last(grid):
    first = last = None
    for k, n in enumerate(grid):
        i = pl.program_id(k)
        first = (i == 0) if first is None else first & (i == 0)
        last = (i == n - 1) if last is None else last & (i == n - 1)
    return first, last


def _side_start(side, side_refs, grid):
    if side is not None:
        pl.when(_side_first_last(grid)[0])(lambda: side.start(*side_refs))


def _side_finish(side, side_refs, grid):
    if side is not None:
        pl.when(_side_first_last(grid)[1])(lambda: side.finish(*side_refs))


def _run_side(side, name):
    def body(*refs):
        _, side_refs = _side_split(side, refs, 0, 0, 0)
        side.start(*side_refs)
        side.finish(*side_refs)

    inputs, in_specs, out_shapes, out_specs, scratch = _side_io(side)
    return pl.pallas_call(body, out_shape=tuple(out_shapes), in_specs=in_specs, out_specs=tuple(out_specs),
                          scratch_shapes=scratch, name=name)(*inputs)


def _gather_side(shards, layers):
    def plan(in_refs, out_refs, sems, starting):
        send_sems, recv_sems, local_sems = sems
        x, y, c, chips = _place()
        me, sibling = (x, y, c), (x, y, 1 - c)
        jobs = []
        for a, (x_ref, o_ref) in enumerate(zip(in_refs, out_refs)):
            r = x_ref.shape[1]
            src = x_ref.at[layers[a]]

            def rows(px, py, pc, o_ref=o_ref, r=r):
                return o_ref.at[0, pl.ds(pl.multiple_of((4 * px + 2 * py + pc) * r, 16), r), :]

            def copy(k, block, to, from_shard=False, a=a, rows=rows, src=src):
                return pltpu.make_async_remote_copy(
                    src_ref=src if from_shard else rows(*block), dst_ref=rows(*block),
                    send_sem=send_sems.at[7 * a + k], recv_sem=recv_sems.at[7 * a + k], device_id=to, device_id_type=MESH)

            job = dict(
                mine=pltpu.make_async_copy(src, rows(*me), local_sems.at[a]),
                first=[copy(0, me, sibling, True)] + [copy(1 + j, me, (*chip, c), True) for j, chip in enumerate(chips)])
            if not starting:
                job.update(
                    passed=[copy(4 + j, (*chip, c), sibling) for j, chip in enumerate(chips)],
                    from_chips=[copy(1 + j, (*chip, c), me) for j, chip in enumerate(chips)],
                    from_sibling=[copy(0, sibling, me)] + [copy(4 + j, (*chip, 1 - c), me) for j, chip in enumerate(chips)])
            jobs.append(job)
        return jobs

    def start(*refs):
        for job in plan(*refs, starting=True):
            job["mine"].start()
            for cp in job["first"]:
                cp.start()

    def finish(*refs):
        jobs = plan(*refs, starting=False)
        for job in jobs:
            for arrived, onward in zip(job["from_chips"], job["passed"]):
                arrived.wait_recv()
                onward.start()
        for job in jobs:
            for cp in job["from_sibling"]:
                cp.wait_recv()
            for cp in job["first"] + job["passed"]:
                cp.wait_send()
            job["mine"].wait()

    shapes = [S((1, NDEV * s.shape[1], s.shape[2]), s.dtype) for s in shards]
    return _Side(shards, shapes, 7 * len(shards), start, finish)


def _swap_side(gs):
    def plan(in_refs, out_refs, sems):
        send_sems, recv_sems, _ = sems
        x, y, c, chips = _place()
        copies = []
        for a, (g_ref, o_ref) in enumerate(zip(in_refs, out_refs)):
            r = g_ref.shape[1] // NDEV
            for j, (px, py) in enumerate([(x, y)] + chips):
                d = 4 * px + 2 * py + (1 - c)
                copies.append(pltpu.make_async_remote_copy(
                    src_ref=g_ref.at[:, pl.ds(pl.multiple_of(d * r, 8), r), :], dst_ref=o_ref.at[j],
                    send_sem=send_sems.at[4 * a + j], recv_sem=recv_sems.at[4 * a + j], device_id=(x, y, 1 - c), device_id_type=MESH))
        return copies

    return _exchange_side(gs, [S((4, g.shape[0], g.shape[1] // NDEV, g.shape[2]), f32) for g in gs], 4 * len(gs), plan)


def _ici_side(sends):
    def plan(in_refs, out_refs, sems):
        send_sems, recv_sems, _ = sems
        x, y, c, chips = _place()
        return [pltpu.make_async_remote_copy(
            src_ref=s_ref.at[j], dst_ref=o_ref.at[j], send_sem=send_sems.at[3 * a + j], recv_sem=recv_sems.at[3 * a + j],
            device_id=(*chip, c), device_id_type=MESH)
            for a, (s_ref, o_ref) in enumerate(zip(in_refs, out_refs)) for j, chip in enumerate(chips)]

    return _exchange_side(sends, [S(s.shape, s.dtype) for s in sends], 3 * len(sends), plan)


def _exchange_side(inputs, shapes, nsem, plan):
    def start(*refs):
        for cp in plan(*refs):
            cp.start()

    def finish(*refs):
        copies = plan(*refs)
        for cp in copies:
            cp.wait_recv()
        for cp in copies:
            cp.wait_send()

    return _Side(inputs, shapes, nsem, start, finish)


def _ag_small(shard, name):
    r, w = shard.shape

    def body(x_ref, o_ref, send_sems, recv_sems, local_sem):
        def rows(px, py, pc):
            return o_ref.at[pl.ds(pl.multiple_of((4 * px + 2 * py + pc) * r, 8), r), :]
        _two_level_gather_body(rows, x_ref, send_sems, recv_sems, local_sem)

    vmem = BS(memory_space=pltpu.VMEM)
    return pl.pallas_call(
        body, out_shape=S((NDEV * r, w), shard.dtype), in_specs=[vmem], out_specs=vmem,
        scratch_shapes=_GATHER_SEMS, name=name)(shard)


def _rel_ids():
    x, y, c = lax.axis_index("x"), lax.axis_index("y"), lax.axis_index("c")
    rel = [(x, y), (1 - x, y), (x, 1 - y), (1 - x, 1 - y)]
    return jnp.stack([4 * px + 2 * py + c for px, py in rel]).astype(jnp.int32)


def _rs_partial(g, sib, ids, name):
    ns, rows8, cdim = g.shape
    r = rows8 // NDEV
    tr = r // 2
    nb = r // tr

    def own_body(ids_ref, g_ref, s_ref, o_ref):
        o_ref[...] = g_ref[...] + s_ref[...]

    own = pl.pallas_call(
        own_body, out_shape=S((ns, r, cdim), f32),
        grid_spec=pltpu.PrefetchScalarGridSpec(
            num_scalar_prefetch=1, grid=(ns, nb),
            in_specs=[BS((None, tr, cdim), lambda s, t, ids_ref: (s, ids_ref[0] * nb + t, 0)),
                      BS((None, None, tr, cdim), lambda s, t, ids_ref: (0, s, t, 0))],
            out_specs=BS((None, tr, cdim), lambda s, t, ids_ref: (s, t, 0))),
        name=name + "_own", compiler_params=_cp("parallel", "parallel"))(ids, g, sib)

    def send_body(ids_ref, g_ref, s_ref, o_ref):
        o_ref[...] = (g_ref[...] + s_ref[...]).astype(bf16)

    send = pl.pallas_call(
        send_body, out_shape=S((3, ns, r, cdim), bf16),
        grid_spec=pltpu.PrefetchScalarGridSpec(
            num_scalar_prefetch=1, grid=(3, ns, nb),
            in_specs=[BS((None, tr, cdim), lambda j, s, t, ids_ref: (s, ids_ref[j + 1] * nb + t, 0)),
                      BS((None, None, tr, cdim), lambda j, s, t, ids_ref: (j + 1, s, t, 0))],
            out_specs=BS((None, None, tr, cdim), lambda j, s, t, ids_ref: (j, s, t, 0))),
        name=name + "_send", compiler_params=_cp("parallel", "parallel", "parallel"))(ids, g, sib)
    return own, send


def _prep(w, name):
    nl, r, cdim = w.shape
    blk = BS((None, r, cdim), lambda l: (l, 0, 0))

    def body(w_ref, o_ref):
        o_ref[...] = w_ref[...].astype(bf16)

    return pl.pallas_call(body, out_shape=S(w.shape, bf16), grid=(nl,), in_specs=[blk], out_specs=blk,
                          name=name, compiler_params=_cp("parallel"))(w)


def _adam_math(w, g, m, v):
    m2 = ADAM_B1 * m + (1.0 - ADAM_B1) * g
    v2 = ADAM_B2 * v + (1.0 - ADAM_B2) * (g * g)
    m_hat = m2 / (1.0 - ADAM_B1 ** ADAM_STEP)
    v_hat = v2 / (1.0 - ADAM_B2 ** ADAM_STEP)
    delta = -ADAM_LR * (m_hat / (jnp.sqrt(v_hat) + ADAM_EPS) + ADAM_WD * w)
    return delta, m2, v2


def _adam_big(w, m, v, own, arrived, layer, slot, name, prev=None):
    nl, a, b = w.shape
    ta = _pick(a, (a // 4, a // 2, a)) if (a // 4) % 16 == 0 else a
    w_blk = BS((None, ta, b), lambda t: (layer, t, 0))
    own_blk = BS((None, ta, b), lambda t: (slot, t, 0))
    arr_blk = [BS((None, None, ta, b), lambda t, j=j: (j, slot, t, 0)) for j in range(3)]

    def body(*refs):
        w_ref, m_ref, v_ref, own_ref, a0_ref, a1_ref, a2_ref = refs[:7]
        g_ref, d_ref, m2_ref, v2_ref = refs[-4:]
        g = ((own_ref[...] + a0_ref[...].astype(f32)) + a1_ref[...].astype(f32)) + a2_ref[...].astype(f32)
        delta, m2, v2 = _adam_math(w_ref[...], g, m_ref[...], v_ref[...])
        g_ref[...] = g
        d_ref[...] = delta
        m2_ref[...] = m2
        v2_ref[...] = v2

    threaded = () if prev is None else tuple(prev)
    return pl.pallas_call(
        body, out_shape=(S(w.shape, f32),) * 4, grid=(a // ta,),
        in_specs=[w_blk, w_blk, w_blk, own_blk] + arr_blk + [ANY] * len(threaded), out_specs=(w_blk,) * 4,
        input_output_aliases={7 + k: k for k in range(len(threaded))},
        name=name, compiler_params=_cp("parallel"))(w, m, v, own, arrived, arrived, arrived, *threaded)


def _adam_small(gathered, w, m, v):
    r, wd = w.shape

    def body(g_ref, w_ref, m_ref, v_ref, gs_ref, d_ref, m2_ref, v2_ref):
        g = g_ref[0:r, :]
        for dev in range(1, NDEV):
            g = g + g_ref[dev * r:(dev + 1) * r, :]
        delta, m2, v2 = _adam_math(w_ref[...], g, m_ref[...], v_ref[...])
        gs_ref[...] = g
        d_ref[...] = delta
        m2_ref[...] = m2
        v2_ref[...] = v2

    return pl.pallas_call(body, out_shape=(S((r, wd), f32),) * 4, name="adam_small", compiler_params=_cp())(gathered, w, m, v)


def _pack(arrays, width):
    flat = jnp.concatenate([a.reshape(-1) for a in arrays])
    rows = -(-flat.shape[0] // (8 * width)) * 8
    return jnp.pad(flat, (0, rows * width - flat.shape[0])).reshape(rows, width)


def _unpack(packed, shapes):
    flat = packed.reshape(-1)
    out, off = [], 0
    for shp in shapes:
        n = 1
        for s in shp:
            n *= s
        out.append(flat[off:off + n].reshape(shp))
        off += n
    return out


def kernel(x, meta_tokens, attn_norm_g, w_in, q_norm_g, k_norm_g, attn_sinks, swa_out_g, sb_out_g, w_o, ffn_norm_g, w_gate, w_up, w_down, loss_target, m_meta_tokens, m_attn_norm_g, m_w_in, m_q_norm_g, m_k_norm_g, m_attn_sinks, m_swa_out_g, m_sb_out_g, m_w_o, m_ffn_norm_g, m_w_gate, m_w_up, m_w_down, v_meta_tokens, v_attn_norm_g, v_w_in, v_q_norm_g, v_k_norm_g, v_attn_sinks, v_swa_out_g, v_sb_out_g, v_w_o, v_ffn_norm_g, v_w_gate, v_w_up, v_w_down):
    nl, d = attn_norm_g.shape
    x2, target = x[0], loss_target[0]
    me = 4 * lax.axis_index("x") + 2 * lax.axis_index("y") + lax.axis_index("c")
    ids = _rel_ids()

    meta_all = _ag_small(meta_tokens, "ag_meta")
    meta_full = meta_all.reshape(NDEV, N_META, -1).transpose(1, 0, 2).reshape(N_META, d)
    def t(a):
        return jnp.swapaxes(a, 1, 2)

    w_in_v, w_gate_v, w_up_v = t(w_in), t(w_gate), t(w_up)
    shards = [_prep(w_in_v, "prep_w_in"), _prep(w_o, "prep_w_o"), _prep(w_gate_v, "prep_w_gate"), _prep(w_up_v, "prep_w_up"),
              _prep(w_down, "prep_w_down")]
    attn_w = [_run_side(_gather_side(shards[:2], [0, 0]), "ag_attn_layer0")] + [None] * (nl - 1)
    ffn_w = [None] * nl

    qg2 = jnp.tile(q_norm_g, (1, TILE // HEAD_DIM))
    kg2 = jnp.tile(k_norm_g, (1, TILE // HEAD_DIM))
    out_g = jnp.concatenate([swa_out_g, sb_out_g], axis=1)
    swa_cols = (SWA_HEADS + 2 * SWA_KV_HEADS) * HEAD_DIM

    h = _embed(x2, meta_full)
    saved, weights = [], []
    for l in range(nl):
        w_in_t, w_o_f = attn_w[l]
        hn = _rmsnorm_fwd([h], attn_norm_g[l:l + 1], "attn_norm")
        proj = _mm(hn, w_in_t, 0, "nt", "proj_swa", cols=(0, swa_cols))
        proj_sb = _mm(hn, w_in_t, 0, "nt", "proj_sb", out_dtype=bf16, cols=(swa_cols, 3 * SB_HEADS * HEAD_DIM))
        out_a = _swa_fwd(proj, qg2[l:l + 1], kg2[l:l + 1], attn_sinks[l])
        nxt = l + 1 < nl
        riding = ([(shards[0], l + 1), (shards[1], l + 1)] if nxt else []) + ([(s, 0) for s in shards[2:]] if l == 0 else [])
        out_b, gathered = _sb_fwd(proj_sb, _gather_side([s for s, _ in riding], [k for _, k in riding]) if riding else None)
        if nxt:
            attn_w[l + 1] = gathered[:2]
        if l == 0:
            ffn_w[0] = gathered[-3:]
        w_gate_t, w_up_t, w_down_f = ffn_w[l]
        weights.append((w_in_t, w_o_f, w_gate_t, w_up_t, w_down_f))
        mixed = _rmsnorm_fwd([out_a, out_b], out_g[l:l + 1], "out_norm")
        h_mid = _mm(mixed, w_o_f, 0, "nn", "attn_out", res=h)
        hn2 = _rmsnorm_fwd([h_mid], ffn_norm_g[l:l + 1], "ffn_norm")
        if nxt:
            g, u, act, gate_up = _ffn_up(hn2, w_gate_t, w_up_t, _gather_side(shards[2:4], [l + 1, l + 1]))
            h_out, down = _mm(act, w_down_f, 0, "nn", "ffn_down", res=h_mid, side=_gather_side(shards[4:], [l + 1]))
            ffn_w[l + 1] = gate_up + down
        else:
            g, u, act, _ = _ffn_up(hn2, w_gate_t, w_up_t)
            h_out = _mm(act, w_down_f, 0, "nn", "ffn_down", res=h_mid)
        saved.append((h, hn, proj, proj_sb, out_a, out_b, mixed, h_mid, hn2, g, u, act))
        h = h_out

    loss_tile, dh, dh_b = _loss_head(h, target)
    loss = lax.psum(loss_tile[0, 0], AXES)

    small = {k: [None] * nl for k in ("attn", "ffn", "out", "q", "k", "sink")}
    pending = None
    red_attn = [None] * nl
    red_ff = [None] * nl

    def attn_partial_sums(slabs, sibs):
        return [_rs_partial(gg, sib, ids, name) for gg, sib, name in zip(slabs, sibs, ("rs_part_w_in", "rs_part_w_o"))]

    for l in reversed(range(nl)):
        w_in_t, w_o_f, w_gate_t, w_up_t, w_down_f = weights[l]
        h_in, hn, proj, proj_sb, out_a, out_b, mixed, h_mid, hn2, g, u, act = saved[l]
        dg, du, sibs = _ffn_down_bwd(dh_b, w_down_f, g, u, None if pending is None else _swap_side(pending))
        parts = [] if pending is None else attn_partial_sums(pending, sibs)
        g_ff = _mm_tn(act, dh_b, None, 2, 3, "grad_w_down")
        g_ff = _mm_tn(dg, hn2, g_ff, 0, 3, "grad_w_gate")
        g_ff = _mm_tn(du, hn2, g_ff, 1, 3, "grad_w_up")
        d_gate, (sib_ff,) = _mm(dg, w_gate_t, 0, "nn", "d_ffn_in_gate", side=_swap_side([g_ff]))
        part_ff = _rs_partial(g_ff, sib_ff, ids, "rs_part_w_ff")
        dhn2 = _mm(du, w_up_t, 0, "nn", "d_ffn_in_up", res=d_gate)
        (dh_mid,), small["ffn"][l], dh_mid_b = _rmsnorm_bwd(dhn2, [h_mid], ffn_norm_g[l:l + 1], dh, "ffn_norm_bwd")
        dmixed = _mm(dh_mid_b, w_o_f, 0, "nt", "d_mixed")
        g_o = _mm_tn(mixed, dh_mid_b, None, 0, 1, "grad_w_o")
        (dout_a, dout_b), small["out"][l] = _rmsnorm_bwd(dmixed, [out_a, out_b], out_g[l:l + 1], None, "out_norm_bwd", bf16)
        dq_b, dk_b, dv_b, (arrived_ff,) = _sb_bwd(proj_sb, dout_b, _ici_side([part_ff[1]]))
        red_ff[l] = (part_ff[0], arrived_ff)
        dq_a, dk_a, dv_a, dqg, dkg, dsk = _swa_bwd(proj, dout_a, qg2[l:l + 1], kg2[l:l + 1], attn_sinks[l])
        small["q"][l] = dqg[0, :HEAD_DIM] + dqg[0, HEAD_DIM:]
        small["k"][l] = dkg[0, :HEAD_DIM] + dkg[0, HEAD_DIM:]
        small["sink"][l] = dsk[0, :SWA_HEADS]
        dproj = jnp.concatenate([dq_a, dk_a, dv_a, dq_b, dk_b, dv_b], axis=1)
        if parts:
            dhn, arrived = _mm(dproj, w_in_t, 0, "nn", "d_attn_in", side=_ici_side([p[1] for p in parts]))
            red_attn[l + 1] = ([p[0] for p in parts], arrived)
        else:
            dhn = _mm(dproj, w_in_t, 0, "nn", "d_attn_in")
        g_in = _mm_tn(dproj, hn, None, 0, 1, "grad_w_in")
        (dh,), small["attn"][l], dh_b = _rmsnorm_bwd(dhn, [h_in], attn_norm_g[l:l + 1], dh_mid, "attn_norm_bwd")
        pending = [g_in, g_o]

    parts = attn_partial_sums(pending, _run_side(_swap_side(pending), "rs_swap_attn_layer0"))
    red_attn[0] = ([p[0] for p in parts], list(_run_side(_ici_side([p[1] for p in parts]), "rs_ici_attn_layer0")))

    grad_x = dh[TILE:][None]
    d_meta = dh[PAD:TILE]

    qw = SWA_HEADS * HEAD_DIM
    d_out = jnp.concatenate(small["out"], axis=0)
    small_grads = [jnp.concatenate(small["attn"], axis=0), jnp.stack(small["q"]), jnp.stack(small["k"]), jnp.stack(small["sink"]),
                   d_out[:, :qw], d_out[:, qw:], jnp.concatenate(small["ffn"], axis=0), d_meta]
    col0 = me * meta_tokens.shape[1]

    def widen(a):
        return lax.dynamic_update_slice(jnp.zeros((N_META, d), f32), a, (0, col0))

    small_w = [attn_norm_g, q_norm_g, k_norm_g, attn_sinks, swa_out_g, sb_out_g, ffn_norm_g]
    small_m = [m_attn_norm_g, m_q_norm_g, m_k_norm_g, m_attn_sinks, m_swa_out_g, m_sb_out_g, m_ffn_norm_g]
    small_v = [v_attn_norm_g, v_q_norm_g, v_k_norm_g, v_attn_sinks, v_swa_out_g, v_sb_out_g, v_ffn_norm_g]
    shapes = [a.shape for a in small_w] + [(N_META, d)]
    packed = _adam_small(_ag_small(_pack(small_grads, d), "ag_small_grads"),
                         _pack(small_w + [widen(meta_tokens)], d), _pack(small_m + [widen(m_meta_tokens)], d),
                         _pack(small_v + [widen(v_meta_tokens)], d))
    small_out = []
    for p in packed:
        parts = _unpack(p, shapes)
        parts[-1] = lax.dynamic_slice(parts[-1], (0, col0), meta_tokens.shape)
        small_out.append(parts)

    big = dict.fromkeys(("w_in", "w_o", "w_gate", "w_up", "w_down"))
    for l in range(nl):
        (own_in, own_o), (arr_in, arr_o) = red_attn[l]
        own_ff, arr_ff = red_ff[l]
        big["w_in"] = _adam_big(w_in_v, t(m_w_in), t(v_w_in), own_in, arr_in, l, 0, "adam_w_in", big["w_in"])
        big["w_o"] = _adam_big(w_o, m_w_o, v_w_o, own_o, arr_o, l, 0, "adam_w_o", big["w_o"])
        big["w_gate"] = _adam_big(w_gate_v, t(m_w_gate), t(v_w_gate), own_ff, arr_ff, l, 0, "adam_w_gate", big["w_gate"])
        big["w_up"] = _adam_big(w_up_v, t(m_w_up), t(v_w_up), own_ff, arr_ff, l, 1, "adam_w_up", big["w_up"])
        big["w_down"] = _adam_big(w_down, m_w_down, v_w_down, own_ff, arr_ff, l, 2, "adam_w_down", big["w_down"])
    for name in ("w_in", "w_gate", "w_up"):
        big[name] = [t(a) for a in big[name]]

    def group(k):
        sm = small_out[k]
        return [sm[7], sm[0], big["w_in"][k], sm[1], sm[2], sm[3], sm[4], sm[5], big["w_o"][k], sm[6],
                big["w_gate"][k], big["w_up"][k], big["w_down"][k]]

    return (loss, grad_x, *group(0), *group(1), *group(2), *group(3))
```

```python
import jax
import jax.numpy as jnp
from jax import lax
from jax.experimental import pallas as pl
from jax.experimental.pallas import tpu as pltpu

f32, bf16 = jnp.float32, jnp.bfloat16
S = jax.ShapeDtypeStruct
BS = pl.BlockSpec

N_META = 16
TILE = 128
PAD = TILE - N_META
HEAD_DIM = 64
SWA_HEADS = 16
SWA_KV_HEADS = 4
SB_HEADS = 16
EPS = 1e-6
NEG = -1e30
ADAM_LR, ADAM_B1, ADAM_B2, ADAM_EPS, ADAM_WD, ADAM_STEP = 0.001, 0.9, 0.999, 1e-08, 0.01, 10
AXES = ("x", "y", "c")
NDEV = 8
VMEM_LIMIT_V7X = 56 * 1024 * 1024
MM_RESIDENT_K = 2048
TN_VMEM_BUDGET = 46 * 1024 * 1024
MESH = pl.DeviceIdType.MESH
ANY = pl.BlockSpec(memory_space=pl.ANY)

NN = ((1,), (0,))
NT = ((1,), (1,))
TN = ((0,), (0,))


def _cp(*sem):
    return pltpu.CompilerParams(dimension_semantics=sem if sem else None, vmem_limit_bytes=VMEM_LIMIT_V7X)


def _pick(n, cands):
    for c in cands:
        if n % c == 0:
            return c
    raise ValueError(f"no tile for {n} in {cands}")


def _dot(a, b, dims):
    return lax.dot_general(a, b, (dims, ((), ())), preferred_element_type=f32)


def _embed(x, meta_full):
    seq, d = x.shape
    meta_pad = jnp.pad(meta_full, ((PAD, 0), (0, 0)))

    def body(x_ref, m_ref, o_ref):
        i = pl.program_id(0)

        @pl.when(i == 0)
        def _():
            o_ref[...] = m_ref[...]

        @pl.when(i > 0)
        def _():
            o_ref[...] = x_ref[...]

    return pl.pallas_call(
        body, out_shape=S((TILE + seq, d), f32), grid=(1 + seq // TILE,),
        in_specs=[BS((TILE, d), lambda i: (jnp.maximum(i - 1, 0), 0)), BS((TILE, d), lambda i: (0, 0))],
        out_specs=BS((TILE, d), lambda i: (i, 0)), name="embed", compiler_params=_cp("arbitrary"))(x, meta_pad)


def _rmsnorm_fwd(xs, gain, name):
    lp = xs[0].shape[0]
    widths = [x.shape[1] for x in xs]
    w = sum(widths)
    tr = _pick(lp, (384, 256, 128))
    n = len(xs)

    def body(*refs):
        g_ref, o_ref = refs[n], refs[n + 1]
        off = 0
        for k in range(n):
            sl = slice(off, off + widths[k])
            off += widths[k]
            xv = refs[k][...]
            r = lax.rsqrt(jnp.mean(xv * xv, axis=-1, keepdims=True) + EPS)
            o_ref[:, sl] = ((xv * r) * g_ref[:, sl]).astype(bf16)

    return pl.pallas_call(
        body, out_shape=S((lp, w), bf16), grid=(lp // tr,),
        in_specs=[BS((tr, wk), lambda i: (i, 0)) for wk in widths] + [BS((1, w), lambda i: (0, 0))],
        out_specs=BS((tr, w), lambda i: (i, 0)), name=name, compiler_params=_cp("parallel"))(*xs, gain)


def _rmsnorm_bwd(dy, xs, gain, res, name, out_dtype=f32):
    lp = xs[0].shape[0]
    widths = [x.shape[1] for x in xs]
    w = sum(widths)
    tr = _pick(lp, (384, 256, 128))
    n = len(xs)
    has_res = res is not None
    assert not has_res or n == 1

    def body(*refs):
        dy_ref, g_ref = refs[0], refs[1 + n]
        res_ref = refs[2 + n] if has_res else None
        dx_refs = refs[2 + n + has_res:2 + 2 * n + has_res]
        dg_ref = refs[2 + 2 * n + has_res]

        @pl.when(pl.program_id(0) == 0)
        def _():
            dg_ref[...] = jnp.zeros_like(dg_ref)

        off = 0
        for k in range(n):
            sl = slice(off, off + widths[k])
            off += widths[k]
            xv = refs[1 + k][...]
            dyv = dy_ref[:, sl]
            r = lax.rsqrt(jnp.mean(xv * xv, axis=-1, keepdims=True) + EPS)
            xh = xv * r
            dg_ref[:, sl] += jnp.sum(dyv * xh, axis=0, keepdims=True)
            dxh = dyv * g_ref[:, sl]
            dx = r * (dxh - xh * jnp.mean(dxh * xh, axis=-1, keepdims=True))
            if has_res:
                dx = dx + res_ref[...]
                refs[-1][...] = dx.astype(bf16)
            dx_refs[k][...] = dx.astype(out_dtype)

    rows = [BS((tr, wk), lambda i: (i, 0)) for wk in widths]
    full = BS((tr, w), lambda i: (i, 0))
    vec = BS((1, w), lambda i: (0, 0))
    args = (dy, *xs, gain) + ((res,) if has_res else ())
    out = pl.pallas_call(
        body, out_shape=tuple(S((lp, wk), out_dtype) for wk in widths) + (S((1, w), f32),) + ((S((lp, w), bf16),) if has_res else ()),
        grid=(lp // tr,),
        in_specs=[full] + rows + [vec] + ([full] if has_res else []), out_specs=tuple(rows) + (vec,) + ((full,) if has_res else ()),
        name=name, compiler_params=_cp("arbitrary"))(*args)
    return (list(out[:n]), out[n], out[n + 1]) if has_res else (list(out[:n]), out[n])


def _loss_head(h, target):
    lp, d = h.shape

    def body(h_ref, t_ref, loss_ref, dh_ref, dhb_ref):
        i = pl.program_id(0)

        @pl.when(i == 0)
        def _():
            loss_ref[...] = jnp.zeros_like(loss_ref)
            dh_ref[...] = jnp.zeros_like(dh_ref)
            dhb_ref[...] = jnp.zeros_like(dhb_ref)

        @pl.when(i > 0)
        def _():
            e = h_ref[...] - t_ref[...]
            dh = e * (1.0 / d)
            dh_ref[...] = dh
            dhb_ref[...] = dh.astype(bf16)
            loss_ref[...] += 0.5 * jnp.sum(jnp.sum(e * e, axis=-1, keepdims=True) * (1.0 / d))

    tile = BS((TILE, d), lambda i: (i, 0))
    return pl.pallas_call(
        body, out_shape=(S((8, 128), f32), S((lp, d), f32), S((lp, d), bf16)), grid=(lp // TILE,),
        in_specs=[tile, BS((TILE, d), lambda i: (jnp.maximum(i - 1, 0), 0))],
        out_specs=(BS((8, 128), lambda i: (0, 0)), tile, tile),
        name="loss_head", compiler_params=_cp("arbitrary"))(h, target)


def _mm(a, b, layer, mode, name, out_dtype=f32, res=None, cols=None, side=None):
    m, k = a.shape
    n = b.shape[2] if mode == "nn" else b.shape[1]
    c0 = 0
    if cols is not None:
        assert mode == "nt"
        c0, n = cols
    tn = _pick(n, (512, 384, 256, 128))
    while c0 % tn:
        tn -= TILE
    assert n % tn == 0
    joff = c0 // tn
    single_a = k > MM_RESIDENT_K and a.dtype == bf16 and m % 1408 == 0
    rows_outer = k <= MM_RESIDENT_K or single_a
    tm = _pick(m, (1408, 384, 256, 128) if rows_outer else (384, 256, 128))
    dims = NN if mode == "nn" else NT
    has_res = res is not None

    grid = (m // tm, n // tn) if rows_outer else (n // tn, m // tm)

    def body(*refs):
        refs, side_refs = _side_split(side, refs, 2 + has_res, 1, 0)
        _side_start(side, side_refs, grid)
        if has_res:
            a_ref, b_ref, r_ref, o_ref = refs
        else:
            a_ref, b_ref, o_ref = refs
        acc = _dot(a_ref[...].astype(bf16), b_ref[...], dims)
        if has_res:
            acc = acc + r_ref[...]
        o_ref[...] = acc.astype(out_dtype)
        _side_finish(side, side_refs, grid)

    def ij(f):
        return (lambda i, j: f(i, j)) if rows_outer else (lambda j, i: f(i, j))

    if mode == "nn":
        b_spec = BS((None, k, tn), ij(lambda i, j: (layer, 0, j)))
    else:
        b_spec = BS((None, tn, k), ij(lambda i, j: (layer, joff + j, 0)))
    tile = BS((tm, tn), ij(lambda i, j: (i, j)))
    args = (a, b) + ((res,) if has_res else ())
    s_in, s_in_specs, s_out, s_out_specs, s_scratch = _side_io(side)
    sem = ("parallel", "parallel") if side is None else ("arbitrary", "arbitrary")
    out = pl.pallas_call(
        body, out_shape=(S((m, n), out_dtype), *s_out), grid=grid,
        in_specs=[BS((tm, k), ij(lambda i, j: (i, 0)), **(dict(pipeline_mode=pl.Buffered(1)) if single_a else {})), b_spec]
        + ([tile] if has_res else []) + s_in_specs,
        out_specs=(tile, *s_out_specs), scratch_shapes=s_scratch,
        name=name if side is None else name + "_comm", compiler_params=_cp(*sem))(*args, *s_in)
    return out[0] if side is None else (out[0], list(out[1:]))


def _mm_tn(a, b, gbuf, slot, nslots, name):
    t, m = a.shape
    n = b.shape[1]
    tk = _pick(t, (1408, 1024, 512, 384, 256, 128))
    nk = t // tk

    def vmem(rows):
        return 2 * (rows * n * 4 + tk * rows * a.dtype.itemsize + tk * n * b.dtype.itemsize)

    tm = next(c for c in (1536, 1408, 1024, 512, 384, 256, 128) if m % c == 0 and (vmem(c) <= TN_VMEM_BUDGET or c == 128))

    def body(*refs):
        a_ref, b_ref, o_ref = refs[0], refs[1], refs[-1]
        kk = pl.program_id(1)

        @pl.when(kk == 0)
        def _():
            o_ref[...] = jnp.zeros_like(o_ref)

        o_ref[...] += _dot(a_ref[...].astype(bf16), b_ref[...].astype(bf16), TN)

    in_specs = [BS((tk, tm), lambda i, kk: (kk, i)), BS((tk, n), lambda i, kk: (kk, 0))]
    args = (a, b)
    aliases = {}
    if gbuf is not None:
        in_specs.append(ANY)
        args = args + (gbuf,)
        aliases = {2: 0}
    return pl.pallas_call(
        body, out_shape=S((nslots, m, n), f32), grid=(m // tm, nk), in_specs=in_specs,
        out_specs=BS((None, tm, n), lambda i, kk: (slot, i, 0)), input_output_aliases=aliases,
        name=name, compiler_params=_cp("parallel", "arbitrary"))(*args)


def _ffn_up(hn, wg, wu, side=None):
    m, k = hn.shape
    n = wg.shape[1]
    tm = _pick(m, (1408, 384, 256, 128))
    tn = _pick(n, (512, 384, 256, 128))
    grid = (m // tm, n // tn)

    def body(*refs):
        (a_ref, wg_ref, wu_ref, g_ref, u_ref, act_ref), side_refs = _side_split(side, refs, 3, 3, 0)
        _side_start(side, side_refs, grid)
        a = a_ref[...]
        g = _dot(a, wg_ref[...], NT)
        u = _dot(a, wu_ref[...], NT)
        g_ref[...] = g
        u_ref[...] = u
        act_ref[...] = (g * jax.nn.sigmoid(g) * u).astype(bf16)
        _side_finish(side, side_refs, grid)

    tile = BS((tm, tn), lambda i, j: (i, j))
    s_in, s_in_specs, s_out, s_out_specs, s_scratch = _side_io(side)
    sem = ("parallel", "parallel") if side is None else ("arbitrary", "arbitrary")
    out = pl.pallas_call(
        body, out_shape=(S((m, n), f32), S((m, n), f32), S((m, n), bf16), *s_out), grid=grid,
        in_specs=[BS((tm, k), lambda i, j: (i, 0)), BS((None, tn, k), lambda i, j: (0, j, 0)),
                  BS((None, tn, k), lambda i, j: (0, j, 0)), *s_in_specs],
        out_specs=(tile, tile, tile, *s_out_specs), scratch_shapes=s_scratch,
        name="ffn_up" if side is None else "ffn_up_comm", compiler_params=_cp(*sem))(hn, wg, wu, *s_in)
    return out[0], out[1], out[2], list(out[3:])


def _ffn_down_bwd(dh, wd, g, u, side=None):
    m, k = dh.shape
    n = wd.shape[1]
    tm = _pick(m, (1408, 384, 256, 128))
    tn = _pick(n, (512, 384, 256, 128))
    grid = (m // tm, n // tn)

    def body(*refs):
        (a_ref, wd_ref, g_ref, u_ref, dg_ref, du_ref), side_refs = _side_split(side, refs, 4, 2, 0)
        _side_start(side, side_refs, grid)
        dact = _dot(a_ref[...].astype(bf16), wd_ref[...], NT)
        gv = g_ref[...]
        sg = jax.nn.sigmoid(gv)
        dg_ref[...] = (dact * u_ref[...] * (sg * (1.0 + gv * (1.0 - sg)))).astype(bf16)
        du_ref[...] = (dact * (gv * sg)).astype(bf16)
        _side_finish(side, side_refs, grid)

    tile = BS((tm, tn), lambda i, j: (i, j))
    s_in, s_in_specs, s_out, s_out_specs, s_scratch = _side_io(side)
    sem = ("parallel", "parallel") if side is None else ("arbitrary", "arbitrary")
    out = pl.pallas_call(
        body, out_shape=(S((m, n), bf16), S((m, n), bf16), *s_out), grid=grid,
        in_specs=[BS((tm, k), lambda i, j: (i, 0)), BS((None, tn, k), lambda i, j: (0, j, 0)), tile, tile, *s_in_specs],
        out_specs=(tile, tile, *s_out_specs), scratch_shapes=s_scratch,
        name="ffn_down_bwd" if side is None else "ffn_down_bwd_comm", compiler_params=_cp(*sem))(dh, wd, g, u, *s_in)
    return out[0], out[1], list(out[2:])


def _lane_lo():
    return lax.broadcasted_iota(jnp.int32, (1, TILE), 1) < HEAD_DIM


def _head_norm(x, g2):
    lo = _lane_lo()
    x2 = x * x
    s_lo = jnp.sum(jnp.where(lo, x2, 0.0), axis=-1, keepdims=True)
    s_hi = jnp.sum(jnp.where(lo, 0.0, x2), axis=-1, keepdims=True)
    r = jnp.where(lo, lax.rsqrt(s_lo * (1.0 / HEAD_DIM) + EPS), lax.rsqrt(s_hi * (1.0 / HEAD_DIM) + EPS))
    xh = x * r
    return xh * g2, xh, r


def _head_norm_bwd(dy, xh, r, g2):
    lo = _lane_lo()
    dgain = jnp.sum(dy * xh, axis=0, keepdims=True)
    dxh = dy * g2
    t = dxh * xh
    t_lo = jnp.sum(jnp.where(lo, t, 0.0), axis=-1, keepdims=True)
    t_hi = jnp.sum(jnp.where(lo, 0.0, t), axis=-1, keepdims=True)
    mt = jnp.where(lo, t_lo, t_hi) * (1.0 / HEAD_DIM)
    return r * (dxh - xh * mt), dgain


def _swa_slopes():
    return [2.0 ** (-8.0 * (h + 1) / SWA_HEADS) for h in range(SWA_HEADS)]


def _swa_masks(i):
    rows = lax.broadcasted_iota(jnp.int32, (TILE, TILE), 0)
    cols = lax.broadcasted_iota(jnp.int32, (TILE, TILE), 1)
    r_pos = i * TILE + rows
    prev = jnp.maximum(i - 1, 0)
    out = []
    for c, base in enumerate((0, prev * TILE, i * TILE)):
        s_pos = base + cols
        dist = r_pos - s_pos
        if c == 0:
            ok = (i >= 1) & (s_pos >= PAD)
        elif c == 1:
            ok = (i >= 2) & (dist < TILE)
        else:
            ok = (dist >= 0) & (s_pos >= PAD)
        out.append((ok, dist.astype(f32)))
    return out, prev


def _swa_geometry(i, group):
    masks, prev = _swa_masks(i)
    bases = (0, pl.multiple_of(prev * TILE, TILE), pl.multiple_of(i * TILE, TILE))
    ok = jnp.concatenate([m[0] for m in masks], axis=1)
    dist = jnp.concatenate([m[1] for m in masks], axis=1)
    return bases, jnp.concatenate([ok] * group, axis=0), jnp.concatenate([dist] * group, axis=0)


def _swa_col(values):
    return jnp.concatenate([jnp.full((TILE, 1), v, f32) for v in values], axis=0)


def _swa_keys(k_ref, v_ref, kg_ref, bases, kvh):
    kcols = slice((kvh // 2) * TILE, (kvh // 2 + 1) * TILE)
    ksel = _lane_lo() if kvh % 2 == 0 else jnp.logical_not(_lane_lo())
    kcat = jnp.concatenate([_head_norm(k_ref[pl.ds(b, TILE), kcols], kg_ref[...])[0] for b in bases], axis=0).astype(bf16)
    vcat = jnp.concatenate([jnp.where(ksel, v_ref[pl.ds(b, TILE), kcols], 0.0) for b in bases], axis=0).astype(bf16)
    return kcat, vcat, ksel


def _swa_stack(blocks, kvh, group, ksel):
    parts = []
    for gi in range(group):
        same = ((kvh * group + gi) % 2) == (kvh % 2)
        parts.append(jnp.where(ksel, blocks[gi] if same else pltpu.roll(blocks[gi], HEAD_DIM, 1), 0.0))
    return jnp.concatenate(parts, axis=0)


def _swa_unstack_add(acc, stacked, kvh, group):
    for gi in range(group):
        h = kvh * group + gi
        part = stacked[gi * TILE:(gi + 1) * TILE]
        acc[h // 2] = acc[h // 2] + (part if (h % 2) == (kvh % 2) else pltpu.roll(part, HEAD_DIM, 1))


def _swa_probs(q4, kcat, ok_g, dist_g, slope_col, sink_col):
    logits = _dot(q4, kcat, NT) * (HEAD_DIM ** -0.5) + jnp.where(ok_g, -slope_col * dist_g, NEG)
    mx = jnp.maximum(jnp.max(logits, axis=-1, keepdims=True), sink_col)
    return jnp.exp(logits - mx), jnp.exp(sink_col - mx)


def _swa_specs(lp, pw):
    qw, kvw = SWA_HEADS * HEAD_DIM, SWA_KV_HEADS * HEAD_DIM
    kidx = qw // kvw
    return (BS((TILE, qw), lambda i: (i, 0)), BS((lp, kvw), lambda i: (0, kidx)), BS((lp, kvw), lambda i: (0, kidx + 1)),
            BS((1, TILE), lambda i: (0, 0)))


def _swa_fwd(proj, qg2, kg2, sinks, side=None):
    lp, pw = proj.shape
    qw = SWA_HEADS * HEAD_DIM
    group = SWA_HEADS // SWA_KV_HEADS
    slopes = _swa_slopes()
    grid = (lp // TILE,)

    def body(*refs):
        (sink_ref, q_ref, k_ref, v_ref, qg_ref, kg_ref, o_ref), side_refs = _side_split(side, refs, 6, 1, 0)
        _side_start(side, side_refs, grid)
        i = pl.program_id(0)
        bases, ok_g, dist_g = _swa_geometry(i, group)
        qn = [_head_norm(q_ref[:, b * TILE:(b + 1) * TILE], qg_ref[...])[0] for b in range(qw // TILE)]
        acc = [jnp.zeros((TILE, TILE), f32) for _ in range(qw // TILE)]
        for kvh in range(SWA_KV_HEADS):
            kcat, vcat, ksel = _swa_keys(k_ref, v_ref, kg_ref, bases, kvh)
            q4 = _swa_stack([qn[(kvh * group + gi) // 2] for gi in range(group)], kvh, group, ksel).astype(bf16)
            sink_col = _swa_col([sink_ref[kvh * group + gi] for gi in range(group)])
            p, e_sink = _swa_probs(q4, kcat, ok_g, dist_g, _swa_col(slopes[kvh * group:(kvh + 1) * group]), sink_col)
            den = e_sink + jnp.sum(p, axis=-1, keepdims=True)
            o = _dot(p.astype(bf16), vcat, NN) / den
            _swa_unstack_add(acc, o, kvh, group)
        valid = (i * TILE + lax.broadcasted_iota(jnp.int32, (TILE, 1), 0)) >= PAD
        for b in range(qw // TILE):
            o_ref[:, b * TILE:(b + 1) * TILE] = jnp.where(valid, acc[b], 0.0)
        _side_finish(side, side_refs, grid)

    q_spec, k_spec, v_spec, vec = _swa_specs(lp, pw)
    s_in, s_in_specs, s_out, s_out_specs, s_scratch = _side_io(side)
    out = pl.pallas_call(
        body, out_shape=(S((lp, qw), f32), *s_out), grid=grid,
        in_specs=[BS(memory_space=pltpu.SMEM), q_spec, k_spec, v_spec, vec, vec, *s_in_specs],
        out_specs=(BS((TILE, qw), lambda i: (i, 0)), *s_out_specs), scratch_shapes=s_scratch,
        name="swa_fwd" if side is None else "swa_fwd_comm", compiler_params=_cp("arbitrary"))(
            sinks, proj, proj, proj, qg2, kg2, *s_in)
    return out[0], list(out[1:])


def _swa_bwd(proj, dout, qg2, kg2, sinks, side=None):
    lp, pw = proj.shape
    qw, kvw = SWA_HEADS * HEAD_DIM, SWA_KV_HEADS * HEAD_DIM
    group = SWA_HEADS // SWA_KV_HEADS
    scale = HEAD_DIM ** -0.5
    slopes = _swa_slopes()
    nt = lp // TILE

    def body(*refs):
        (sink_ref, q_ref, k_ref, v_ref, do_ref, qg_ref, kg_ref, dq_ref, dk_ref, dv_ref, dqg_ref, dkg_ref, ds_ref,
         dkn_acc, dv_acc), side_refs = _side_split(side, refs, 7, 6, 2)
        _side_start(side, side_refs, (nt,))
        i = pl.program_id(0)
        lo = _lane_lo()
        lane = lax.broadcasted_iota(jnp.int32, (1, TILE), 1)

        @pl.when(i == 0)
        def _():
            dkn_acc[...] = jnp.zeros_like(dkn_acc)
            dv_acc[...] = jnp.zeros_like(dv_acc)
            dqg_ref[...] = jnp.zeros_like(dqg_ref)
            dkg_ref[...] = jnp.zeros_like(dkg_ref)
            ds_ref[...] = jnp.zeros_like(ds_ref)

        bases, ok_g, dist_g = _swa_geometry(i, group)
        qnorm = [_head_norm(q_ref[:, b * TILE:(b + 1) * TILE], qg_ref[...]) for b in range(qw // TILE)]
        dqn = [jnp.zeros((TILE, TILE), f32) for _ in range(qw // TILE)]
        for kvh in range(SWA_KV_HEADS):
            kcols = slice((kvh // 2) * TILE, (kvh // 2 + 1) * TILE)
            heads = [kvh * group + gi for gi in range(group)]
            kcat, vcat, ksel = _swa_keys(k_ref, v_ref, kg_ref, bases, kvh)
            q4 = _swa_stack([qnorm[h // 2][0] for h in heads], kvh, group, ksel).astype(bf16)
            do4 = _swa_stack([do_ref[:, (h // 2) * TILE:(h // 2 + 1) * TILE].astype(f32) for h in heads], kvh, group, ksel).astype(bf16)
            sink_col = _swa_col([sink_ref[h] for h in heads])
            p, e_sink = _swa_probs(q4, kcat, ok_g, dist_g, _swa_col([slopes[h] for h in heads]), sink_col)
            inv = 1.0 / (e_sink + jnp.sum(p, axis=-1, keepdims=True))
            p = p * inv
            dp = _dot(do4, vcat, NT)
            dsum = jnp.sum(p * dp, axis=-1, keepdims=True)
            d_sink = e_sink * inv * dsum
            for gi, h in enumerate(heads):
                ds_ref[...] += jnp.where(lane == h, -jnp.sum(d_sink[gi * TILE:(gi + 1) * TILE]), 0.0)
            dsc = (p * (dp - dsum) * scale).astype(bf16)
            _swa_unstack_add(dqn, jnp.where(ksel, _dot(dsc, kcat, NN), 0.0), kvh, group)
            dkn = _dot(dsc, q4, TN)
            dvv = _dot(p.astype(bf16), do4, TN)
            for c in range(3):
                dkn_acc[pl.ds(bases[c], TILE), kcols] += dkn[c * TILE:(c + 1) * TILE]
                dv_acc[pl.ds(bases[c], TILE), kcols] += dvv[c * TILE:(c + 1) * TILE]
        for b in range(qw // TILE):
            _, xh, r = qnorm[b]
            dq, dgain = _head_norm_bwd(dqn[b], xh, r, qg_ref[...])
            dq_ref[:, b * TILE:(b + 1) * TILE] = dq.astype(bf16)
            dqg_ref[...] += dgain

        @pl.when(i == nt - 1)
        def _():
            dv_ref[...] = dv_acc[...].astype(bf16)

            def tile_step(t, carry):
                base = pl.multiple_of(t * TILE, TILE)
                for kb in range(kvw // TILE):
                    kcols = slice(kb * TILE, (kb + 1) * TILE)
                    _, xh, r = _head_norm(k_ref[pl.ds(base, TILE), kcols], kg_ref[...])
                    dk, dgain = _head_norm_bwd(dkn_acc[pl.ds(base, TILE), kcols], xh, r, kg_ref[...])
                    dk_ref[pl.ds(base, TILE), kcols] = dk.astype(bf16)
                    dkg_ref[...] += dgain
                return carry

            lax.fori_loop(0, nt, tile_step, 0)

        _side_finish(side, side_refs, (nt,))

    q_spec, k_spec, v_spec, vec = _swa_specs(lp, pw)
    whole = BS((lp, kvw), lambda i: (0, 0))
    s_in, s_in_specs, s_out, s_out_specs, s_scratch = _side_io(side)
    out = pl.pallas_call(
        body, out_shape=(S((lp, qw), bf16), S((lp, kvw), bf16), S((lp, kvw), bf16), S((1, TILE), f32), S((1, TILE), f32), S((1, TILE), f32),
                         *s_out),
        grid=(nt,),
        in_specs=[BS(memory_space=pltpu.SMEM), q_spec, k_spec, v_spec, BS((TILE, qw), lambda i: (i, 0)), vec, vec, *s_in_specs],
        out_specs=(BS((TILE, qw), lambda i: (i, 0)), whole, whole, vec, vec, vec, *s_out_specs),
        scratch_shapes=[pltpu.VMEM((lp, kvw), f32), pltpu.VMEM((lp, kvw), f32), *s_scratch],
        name="swa_bwd" if side is None else "swa_bwd_comm", compiler_params=_cp("arbitrary"))(
            sinks, proj, proj, proj, dout, qg2, kg2, *s_in)
    return (*out[:6], list(out[6:]))


LOG2E = 1.4426950408889634
SB_PAIRS = 2
ROWS2 = 2 * TILE
SB_ROWS = SB_PAIRS * ROWS2


def _sb_stack(x):
    lo = _lane_lo()
    zero = jnp.zeros((TILE, TILE), x.dtype)
    parts = []
    for p in range(SB_PAIRS):
        xp = x[:, p * TILE:(p + 1) * TILE]
        parts += [jnp.where(lo, xp, zero), jnp.where(lo, zero, xp)]
    return jnp.concatenate(parts, axis=0)


def _sb_unstack(x):
    lo = _lane_lo()
    parts = [jnp.where(lo, x[p * ROWS2:p * ROWS2 + TILE], x[p * ROWS2 + TILE:(p + 1) * ROWS2]) for p in range(SB_PAIRS)]
    return parts[0] if SB_PAIRS == 1 else jnp.concatenate(parts, axis=1)


def _sb_pair_dot(a, b, dims, b_lanes):
    parts = []
    for p in range(SB_PAIRS):
        bp = b[:, p * TILE:(p + 1) * TILE] if b_lanes else b[p * ROWS2:(p + 1) * ROWS2]
        parts.append(_dot(a[p * ROWS2:(p + 1) * ROWS2], bp, dims))
    return parts


def _sb_sum_matrix(after):
    rows = lax.broadcasted_iota(jnp.int32, (ROWS2, ROWS2), 0) & (TILE - 1)
    cols = lax.broadcasted_iota(jnp.int32, (ROWS2, ROWS2), 1)
    tri = (rows > cols) if after else (rows < cols)
    return (tri | (cols >= TILE)).astype(bf16)


def _sb_sums(x, w, sum_mat):
    hi = x.astype(bf16)
    lo = (x - hi.astype(f32)).astype(bf16)
    return [_dot(jnp.concatenate([hi[:, j * TILE:(j + 1) * TILE], lo[:, j * TILE:(j + 1) * TILE]], axis=1), sum_mat, NN)
            for j in range(w)]


def _sb_edge_masks(qi, first, w):
    rows = lax.broadcasted_iota(jnp.int32, (SB_ROWS, TILE), 0) & (TILE - 1)
    cols = lax.broadcasted_iota(jnp.int32, (SB_ROWS, TILE), 1)
    not_pad = first * TILE + cols >= PAD
    before_query = (first + w - 1 - qi) * TILE + cols < rows
    if w == 1:
        return [not_pad & before_query]
    return [not_pad] + [None] * (w - 2) + [before_query]


def _sb_apply(valid, x):
    return x if valid is None else jnp.where(valid, x, 0.0)


def _sb_block(q2, k_ref, first, w, valids, c_rep, after_ones):
    kwin = k_ref[pl.ds(pl.multiple_of(first * TILE, TILE), w * TILE), :]
    z2 = jnp.concatenate(_sb_pair_dot(q2, kwin, NT, True), axis=0) * (HEAD_DIM ** -0.5 * LOG2E)
    ls2 = jnp.minimum(z2, 0.0) - jnp.log2(1.0 + jnp.exp2(jnp.minimum(z2, -z2)))
    m2 = ls2 - z2
    if w == 1:
        m2 = _sb_apply(valids[0], m2)
    else:
        m2 = jnp.concatenate([_sb_apply(valids[j], m2[:, j * TILE:(j + 1) * TILE]) for j in range(w)], axis=1)
    sums = _sb_sums(m2, w, after_ones)
    parts = [None] * w
    for j in reversed(range(w)):
        parts[j] = _sb_apply(valids[j], jnp.exp2(ls2[:, j * TILE:(j + 1) * TILE] + sums[j][:, :TILE] + c_rep))
        c_rep = c_rep + sums[j][:, TILE:]
    return ls2, (parts[0] if w == 1 else jnp.concatenate(parts, axis=1)), c_rep


SB_BLOCKS = (4, 2, 1)
SB_DEAD = -160.0


def _sb_alive(c_rep):
    return jnp.max(c_rep) > SB_DEAD


def _sb_walk_down(qi, block, state, c_of):
    n = qi + 1
    big = SB_BLOCKS[0]

    def more(carry):
        t, st = carry
        return (t < n // big) & _sb_alive(c_of(st))

    def step(carry):
        t, st = carry
        return t + 1, block(n - big * (t + 1), big, st)

    visited, state = lax.while_loop(more, step, (jnp.int32(0), state))
    counts = [visited]
    for s in SB_BLOCKS[1:]:
        cnt = jnp.where(_sb_alive(c_of(state)), (n % (2 * s)) // s, 0)
        state = lax.fori_loop(0, cnt, lambda t, st, s=s: block(n % s, s, st), state)
        counts.append(cnt)
    return state, counts


def _sb_walk_up(qi, block, state, counts):
    n = qi + 1
    big = SB_BLOCKS[0]
    for s, cnt in reversed(list(zip(SB_BLOCKS[1:], counts[1:]))):
        state = lax.fori_loop(0, cnt, lambda t, st, s=s: block(n % s, s, st), state)
    visited = counts[0]
    return lax.fori_loop(0, visited, lambda i, st: block(n - big * (visited - i), big, st), state)


def _sb_cols(lp, off, single=False):
    mode = dict(pipeline_mode=pl.Buffered(1)) if single else {}
    return BS((lp, SB_PAIRS * TILE), lambda g: (0, off + g), **mode)


def _sb_fwd(proj, side=None):
    lp = proj.shape[0]
    sbw = SB_HEADS * HEAD_DIM
    ngrp = sbw // (SB_PAIRS * TILE)
    nt = lp // TILE

    def body(*refs):
        (q_ref, k_ref, v_ref, o_ref), side_refs = _side_split(side, refs, 3, 1, 0)
        _side_start(side, side_refs, (ngrp,))
        after_ones = _sb_sum_matrix(True)

        def q_step(qi, carry):
            qbase = pl.multiple_of(qi * TILE, TILE)
            q2 = _sb_stack(q_ref[pl.ds(qbase, TILE), :])

            def block(first, w, st):
                c_rep, acc = st
                _, a, c_rep = _sb_block(q2, k_ref, first, w, _sb_edge_masks(qi, first, w), c_rep, after_ones)
                vwin = v_ref[pl.ds(pl.multiple_of(first * TILE, TILE), w * TILE), :]
                return c_rep, acc + jnp.concatenate(_sb_pair_dot(a.astype(bf16), vwin, NN, True), axis=0)

            zero = jnp.zeros((SB_ROWS, TILE), f32)
            (_, acc), _ = _sb_walk_down(qi, block, (zero, zero), lambda st: st[0])
            o_ref[pl.ds(qbase, TILE), :] = _sb_unstack(acc)
            return carry

        lax.fori_loop(0, nt, q_step, 0)
        _side_finish(side, side_refs, (ngrp,))

    s_in, s_in_specs, s_out, s_out_specs, s_scratch = _side_io(side)
    out = pl.pallas_call(
        body, out_shape=(S((lp, sbw), f32), *s_out), grid=(ngrp,),
        in_specs=[_sb_cols(lp, 0), _sb_cols(lp, ngrp), _sb_cols(lp, 2 * ngrp), *s_in_specs],
        out_specs=(_sb_cols(lp, 0), *s_out_specs), scratch_shapes=s_scratch,
        name="sb_fwd" if side is None else "sb_fwd_comm",
        compiler_params=_cp("parallel" if side is None else "arbitrary"))(proj, proj, proj, *s_in)
    return out[0], list(out[1:])


def _sb_bwd(proj, dout, side=None):
    lp = proj.shape[0]
    sbw = SB_HEADS * HEAD_DIM
    ngrp = sbw // (SB_PAIRS * TILE)
    nt = lp // TILE
    scale = HEAD_DIM ** -0.5

    def body(*refs):
        (q_ref, k_ref, v_ref, do_ref, dq_ref, dk_ref, dv_ref, dk_acc, dv_acc, e_buf, b_buf), side_refs = _side_split(side, refs, 4, 3, 4)
        _side_start(side, side_refs, (ngrp,))
        dk_acc[...] = jnp.zeros_like(dk_acc)
        dv_acc[...] = jnp.zeros_like(dv_acc)
        after_ones = _sb_sum_matrix(True)
        before_ones = _sb_sum_matrix(False)

        def add_cols(acc_ref, rows, parts):
            for p in range(SB_PAIRS):
                acc_ref[rows, p * TILE:(p + 1) * TILE] += parts[p]

        def q_step(qi, carry):
            qbase = pl.multiple_of(qi * TILE, TILE)
            q2 = _sb_stack(q_ref[pl.ds(qbase, TILE), :])
            do2 = _sb_stack(do_ref[pl.ds(qbase, TILE), :])

            def block_rl(first, w, c_rep):
                ls2, a, c_rep = _sb_block(q2, k_ref, first, w, _sb_edge_masks(qi, first, w), c_rep, after_ones)
                rows = pl.ds(pl.multiple_of(first * TILE, TILE), w * TILE)
                e = jnp.concatenate(_sb_pair_dot(do2, v_ref[rows, :], NT, True), axis=0) * a
                beta = jnp.exp2(ls2)
                for j in range(w):
                    e_buf[first + j] = e[:, j * TILE:(j + 1) * TILE]
                    b_buf[first + j] = beta[:, j * TILE:(j + 1) * TILE]
                add_cols(dv_acc, rows, _sb_pair_dot(a.astype(bf16), do2, TN, False))
                return c_rep

            zero = jnp.zeros((SB_ROWS, TILE), f32)
            _, counts = _sb_walk_down(qi, block_rl, zero, lambda c_rep: c_rep)

            def block_lr(first, w, st):
                e_rep, dq_acc = st
                rows = pl.ds(pl.multiple_of(first * TILE, TILE), w * TILE)
                valids = _sb_edge_masks(qi, first, w)
                es = [e_buf[first + j] for j in range(w)]
                sums = _sb_sums(es[0] if w == 1 else jnp.concatenate(es, axis=1), w, before_ones)
                parts = []
                for j in range(w):
                    beta = b_buf[first + j]
                    parts.append(_sb_apply(valids[j], (es[j] - beta * (es[j] + e_rep + sums[j][:, :TILE])) * scale))
                    e_rep = e_rep + sums[j][:, TILE:]
                dz = (parts[0] if w == 1 else jnp.concatenate(parts, axis=1)).astype(bf16)
                add_cols(dk_acc, rows, _sb_pair_dot(dz, q2, TN, False))
                return e_rep, dq_acc + jnp.concatenate(_sb_pair_dot(dz, k_ref[rows, :], NN, True), axis=0)

            _, dq = _sb_walk_up(qi, block_lr, (zero, zero), counts)
            dq_ref[pl.ds(qbase, TILE), :] = _sb_unstack(dq).astype(bf16)
            return carry

        lax.fori_loop(0, nt, q_step, 0)
        dk_ref[...] = dk_acc[...].astype(bf16)
        dv_ref[...] = dv_acc[...].astype(bf16)
        _side_finish(side, side_refs, (ngrp,))

    wide = SB_PAIRS * TILE
    s_in, s_in_specs, s_out, s_out_specs, s_scratch = _side_io(side)
    out = pl.pallas_call(
        body, out_shape=(S((lp, sbw), bf16),) * 3 + tuple(s_out), grid=(ngrp,),
        in_specs=[_sb_cols(lp, 0, True), _sb_cols(lp, ngrp, True), _sb_cols(lp, 2 * ngrp, True), _sb_cols(lp, 0, True), *s_in_specs],
        out_specs=(_sb_cols(lp, 0, True),) * 3 + tuple(s_out_specs),
        scratch_shapes=[pltpu.VMEM((lp, wide), f32), pltpu.VMEM((lp, wide), f32),
                        pltpu.VMEM((nt, SB_ROWS, TILE), f32), pltpu.VMEM((nt, SB_ROWS, TILE), f32), *s_scratch],
        name="sb_bwd" if side is None else "sb_bwd_comm",
        compiler_params=_cp("parallel" if side is None else "arbitrary"))(proj, proj, proj, dout, *s_in)
    return out[0], out[1], out[2], list(out[3:])


def _place():
    x, y, c = lax.axis_index("x"), lax.axis_index("y"), lax.axis_index("c")
    chips = [(1 - x, y), (x, 1 - y), (1 - x, 1 - y)]
    return x, y, c, chips


def _two_level_gather_body(rows, x_ref, send_sems, recv_sems, local_sem):
    x, y, c, chips = _place()
    me, sibling = (x, y, c), (x, y, 1 - c)

    def copy(k, block, to, src=None):
        return pltpu.make_async_remote_copy(
            src_ref=rows(*block) if src is None else src, dst_ref=rows(*block),
            send_sem=send_sems.at[k], recv_sem=recv_sems.at[k], device_id=to, device_id_type=MESH)

    mine = pltpu.make_async_copy(x_ref, rows(*me), local_sem)
    mine.start()
    first = [copy(0, me, sibling, src=x_ref)]
    first += [copy(1 + j, me, (*chip, c), src=x_ref) for j, chip in enumerate(chips)]
    for cp in first:
        cp.start()
    passed = [copy(4 + j, (*chip, c), sibling) for j, chip in enumerate(chips)]
    for j, chip in enumerate(chips):
        copy(1 + j, (*chip, c), me).wait_recv()
        passed[j].start()
    copy(0, sibling, me).wait_recv()
    for j, chip in enumerate(chips):
        copy(4 + j, (*chip, 1 - c), me).wait_recv()
    for cp in first + passed:
        cp.wait_send()
    mine.wait()


_GATHER_SEMS = [pltpu.SemaphoreType.DMA((7,)), pltpu.SemaphoreType.DMA((7,)), pltpu.SemaphoreType.DMA]


class _Side:
    def __init__(self, inputs, out_shapes, nsem, start, finish):
        self.inputs, self.out_shapes, self.start, self.finish = list(inputs), list(out_shapes), start, finish
        self.scratch = [pltpu.SemaphoreType.DMA((nsem,)), pltpu.SemaphoreType.DMA((nsem,)), pltpu.SemaphoreType.DMA((len(inputs),))]


def _side_io(side):
    if side is None:
        return [], [], [], [], []
    return side.inputs, [ANY] * len(side.inputs), side.out_shapes, [ANY] * len(side.out_shapes), side.scratch


def _side_split(side, refs, n_in, n_out, n_scratch):
    if side is None:
        return refs, None
    si, so = len(side.inputs), len(side.out_shapes)
    a, b, c, d = n_in, n_in + si, n_in + si + n_out, n_in + si + n_out + so
    return refs[:a] + refs[b:c] + refs[d:d + n_scratch], (refs[a:b], refs[c:d], refs[d + n_scratch:])


def _side_first_last(grid):
    first = last = None
    for k, n in enumerate(grid):
        i = pl.program_id(k)
        first = (i == 0) if first is None else first & (i == 0)
        last = (i == n - 1) if last is None else last & (i == n - 1)
    return first, last


def _side_start(side, side_refs, grid):
    if side is not None:
        pl.when(_side_first_last(grid)[0])(lambda: side.start(*side_refs))


def _side_finish(side, side_refs, grid):
    if side is not None:
        pl.when(_side_first_last(grid)[1])(lambda: side.finish(*side_refs))


def _run_side(side, name):
    def body(*refs):
        _, side_refs = _side_split(side, refs, 0, 0, 0)
        side.start(*side_refs)
        side.finish(*side_refs)

    inputs, in_specs, out_shapes, out_specs, scratch = _side_io(side)
    return pl.pallas_call(body, out_shape=tuple(out_shapes), in_specs=in_specs, out_specs=tuple(out_specs),
                          scratch_shapes=scratch, name=name)(*inputs)


def _gather_side(shards, layers):
    def plan(in_refs, out_refs, sems, starting):
        send_sems, recv_sems, local_sems = sems
        x, y, c, chips = _place()
        me, sibling = (x, y, c), (x, y, 1 - c)
        jobs = []
        for a, (x_ref, o_ref) in enumerate(zip(in_refs, out_refs)):
            r = x_ref.shape[1]
            src = x_ref.at[layers[a]]

            def rows(px, py, pc, o_ref=o_ref, r=r):
                return o_ref.at[0, pl.ds(pl.multiple_of((4 * px + 2 * py + pc) * r, 16), r), :]

            def copy(k, block, to, from_shard=False, a=a, rows=rows, src=src):
                return pltpu.make_async_remote_copy(
                    src_ref=src if from_shard else rows(*block), dst_ref=rows(*block),
                    send_sem=send_sems.at[7 * a + k], recv_sem=recv_sems.at[7 * a + k], device_id=to, device_id_type=MESH)

            job = dict(
                mine=pltpu.make_async_copy(src, rows(*me), local_sems.at[a]),
                first=[copy(0, me, sibling, True)] + [copy(1 + j, me, (*chip, c), True) for j, chip in enumerate(chips)])
            if not starting:
                job.update(
                    passed=[copy(4 + j, (*chip, c), sibling) for j, chip in enumerate(chips)],
                    from_chips=[copy(1 + j, (*chip, c), me) for j, chip in enumerate(chips)],
                    from_sibling=[copy(0, sibling, me)] + [copy(4 + j, (*chip, 1 - c), me) for j, chip in enumerate(chips)])
            jobs.append(job)
        return jobs

    def start(*refs):
        for job in plan(*refs, starting=True):
            job["mine"].start()
            for cp in job["first"]:
                cp.start()

    def finish(*refs):
        jobs = plan(*refs, starting=False)
        for job in jobs:
            for arrived, onward in zip(job["from_chips"], job["passed"]):
                arrived.wait_recv()
                onward.start()
        for job in jobs:
            for cp in job["from_sibling"]:
                cp.wait_recv()
            for cp in job["first"] + job["passed"]:
                cp.wait_send()
            job["mine"].wait()

    shapes = [S((1, NDEV * s.shape[1], s.shape[2]), s.dtype) for s in shards]
    return _Side(shards, shapes, 7 * len(shards), start, finish)


def _swap_side(gs):
    def plan(in_refs, out_refs, sems):
        send_sems, recv_sems, _ = sems
        x, y, c, chips = _place()
        copies = []
        for a, (g_ref, o_ref) in enumerate(zip(in_refs, out_refs)):
            r = g_ref.shape[1] // NDEV
            for j, (px, py) in enumerate([(x, y)] + chips):
                d = 4 * px + 2 * py + (1 - c)
                copies.append(pltpu.make_async_remote_copy(
                    src_ref=g_ref.at[:, pl.ds(pl.multiple_of(d * r, 8), r), :], dst_ref=o_ref.at[j],
                    send_sem=send_sems.at[4 * a + j], recv_sem=recv_sems.at[4 * a + j], device_id=(x, y, 1 - c), device_id_type=MESH))
        return copies

    return _exchange_side(gs, [S((4, g.shape[0], g.shape[1] // NDEV, g.shape[2]), f32) for g in gs], 4 * len(gs), plan)


def _ici_side(sends):
    def plan(in_refs, out_refs, sems):
        send_sems, recv_sems, _ = sems
        x, y, c, chips = _place()
        return [pltpu.make_async_remote_copy(
            src_ref=s_ref.at[j], dst_ref=o_ref.at[j], send_sem=send_sems.at[3 * a + j], recv_sem=recv_sems.at[3 * a + j],
            device_id=(*chip, c), device_id_type=MESH)
            for a, (s_ref, o_ref) in enumerate(zip(in_refs, out_refs)) for j, chip in enumerate(chips)]

    return _exchange_side(sends, [S(s.shape, s.dtype) for s in sends], 3 * len(sends), plan)


def _exchange_side(inputs, shapes, nsem, plan):
    def start(*refs):
        for cp in plan(*refs):
            cp.start()

    def finish(*refs):
        copies = plan(*refs)
        for cp in copies:
            cp.wait_recv()
        for cp in copies:
            cp.wait_send()

    return _Side(inputs, shapes, nsem, start, finish)


def _ag_small(shard, name):
    r, w = shard.shape

    def body(x_ref, o_ref, send_sems, recv_sems, local_sem):
        def rows(px, py, pc):
            return o_ref.at[pl.ds(pl.multiple_of((4 * px + 2 * py + pc) * r, 8), r), :]
        _two_level_gather_body(rows, x_ref, send_sems, recv_sems, local_sem)

    vmem = BS(memory_space=pltpu.VMEM)
    return pl.pallas_call(
        body, out_shape=S((NDEV * r, w), shard.dtype), in_specs=[vmem], out_specs=vmem,
        scratch_shapes=_GATHER_SEMS, name=name)(shard)


def _rel_ids():
    x, y, c = lax.axis_index("x"), lax.axis_index("y"), lax.axis_index("c")
    rel = [(x, y), (1 - x, y), (x, 1 - y), (1 - x, 1 - y)]
    return jnp.stack([4 * px + 2 * py + c for px, py in rel]).astype(jnp.int32)


def _rs_partial(g, sib, ids, name):
    ns, rows8, cdim = g.shape
    r = rows8 // NDEV
    tr = r // 2
    nb = r // tr

    def own_body(ids_ref, g_ref, s_ref, o_ref):
        o_ref[...] = g_ref[...] + s_ref[...]

    own = pl.pallas_call(
        own_body, out_shape=S((ns, r, cdim), f32),
        grid_spec=pltpu.PrefetchScalarGridSpec(
            num_scalar_prefetch=1, grid=(ns, nb),
            in_specs=[BS((None, tr, cdim), lambda s, t, ids_ref: (s, ids_ref[0] * nb + t, 0)),
                      BS((None, None, tr, cdim), lambda s, t, ids_ref: (0, s, t, 0))],
            out_specs=BS((None, tr, cdim), lambda s, t, ids_ref: (s, t, 0))),
        name=name + "_own", compiler_params=_cp("parallel", "parallel"))(ids, g, sib)

    def send_body(ids_ref, g_ref, s_ref, o_ref):
        o_ref[...] = (g_ref[...] + s_ref[...]).astype(bf16)

    send = pl.pallas_call(
        send_body, out_shape=S((3, ns, r, cdim), bf16),
        grid_spec=pltpu.PrefetchScalarGridSpec(
            num_scalar_prefetch=1, grid=(3, ns, nb),
            in_specs=[BS((None, tr, cdim), lambda j, s, t, ids_ref: (s, ids_ref[j + 1] * nb + t, 0)),
                      BS((None, None, tr, cdim), lambda j, s, t, ids_ref: (j + 1, s, t, 0))],
            out_specs=BS((None, None, tr, cdim), lambda j, s, t, ids_ref: (j, s, t, 0))),
        name=name + "_send", compiler_params=_cp("parallel", "parallel", "parallel"))(ids, g, sib)
    return own, send


def _prep(w, name):
    nl, r, cdim = w.shape
    blk = BS((None, r, cdim), lambda l: (l, 0, 0))

    def body(w_ref, o_ref):
        o_ref[...] = w_ref[...].astype(bf16)

    return pl.pallas_call(body, out_shape=S(w.shape, bf16), grid=(nl,), in_specs=[blk], out_specs=blk,
                          name=name, compiler_params=_cp("parallel"))(w)


def _adam_math(w, g, m, v):
    m2 = ADAM_B1 * m + (1.0 - ADAM_B1) * g
    v2 = ADAM_B2 * v + (1.0 - ADAM_B2) * (g * g)
    m_hat = m2 / (1.0 - ADAM_B1 ** ADAM_STEP)
    v_hat = v2 / (1.0 - ADAM_B2 ** ADAM_STEP)
    delta = -ADAM_LR * (m_hat / (jnp.sqrt(v_hat) + ADAM_EPS) + ADAM_WD * w)
    return delta, m2, v2


def _adam_big(w, m, v, own, arrived, layer, slot, name, prev=None):
    nl, a, b = w.shape
    ta = _pick(a, (a // 4, a // 2, a)) if (a // 4) % 16 == 0 else a
    w_blk = BS((None, ta, b), lambda t: (layer, t, 0))
    own_blk = BS((None, ta, b), lambda t: (slot, t, 0))
    arr_blk = [BS((None, None, ta, b), lambda t, j=j: (j, slot, t, 0)) for j in range(3)]

    def body(*refs):
        w_ref, m_ref, v_ref, own_ref, a0_ref, a1_ref, a2_ref = refs[:7]
        g_ref, d_ref, m2_ref, v2_ref = refs[-4:]
        g = ((own_ref[...] + a0_ref[...].astype(f32)) + a1_ref[...].astype(f32)) + a2_ref[...].astype(f32)
        delta, m2, v2 = _adam_math(w_ref[...], g, m_ref[...], v_ref[...])
        g_ref[...] = g
        d_ref[...] = delta
        m2_ref[...] = m2
        v2_ref[...] = v2

    threaded = () if prev is None else tuple(prev)
    return pl.pallas_call(
        body, out_shape=(S(w.shape, f32),) * 4, grid=(a // ta,),
        in_specs=[w_blk, w_blk, w_blk, own_blk] + arr_blk + [ANY] * len(threaded), out_specs=(w_blk,) * 4,
        input_output_aliases={7 + k: k for k in range(len(threaded))},
        name=name, compiler_params=_cp("parallel"))(w, m, v, own, arrived, arrived, arrived, *threaded)


def _adam_small(gathered, w, m, v):
    r, wd = w.shape

    def body(g_ref, w_ref, m_ref, v_ref, gs_ref, d_ref, m2_ref, v2_ref):
        g = g_ref[0:r, :]
        for dev in range(1, NDEV):
            g = g + g_ref[dev * r:(dev + 1) * r, :]
        delta, m2, v2 = _adam_math(w_ref[...], g, m_ref[...], v_ref[...])
        gs_ref[...] = g
        d_ref[...] = delta
        m2_ref[...] = m2
        v2_ref[...] = v2

    return pl.pallas_call(body, out_shape=(S((r, wd), f32),) * 4, name="adam_small", compiler_params=_cp())(gathered, w, m, v)


def _pack(arrays, width):
    flat = jnp.concatenate([a.reshape(-1) for a in arrays])
    rows = -(-flat.shape[0] // (8 * width)) * 8
    return jnp.pad(flat, (0, rows * width - flat.shape[0])).reshape(rows, width)


def _unpack(packed, shapes):
    flat = packed.reshape(-1)
    out, off = [], 0
    for shp in shapes:
        n = 1
        for s in shp:
            n *= s
        out.append(flat[off:off + n].reshape(shp))
        off += n
    return out


def kernel(x, meta_tokens, attn_norm_g, w_in, q_norm_g, k_norm_g, attn_sinks, swa_out_g, sb_out_g, w_o, ffn_norm_g, w_gate, w_up, w_down, loss_target, m_meta_tokens, m_attn_norm_g, m_w_in, m_q_norm_g, m_k_norm_g, m_attn_sinks, m_swa_out_g, m_sb_out_g, m_w_o, m_ffn_norm_g, m_w_gate, m_w_up, m_w_down, v_meta_tokens, v_attn_norm_g, v_w_in, v_q_norm_g, v_k_norm_g, v_attn_sinks, v_swa_out_g, v_sb_out_g, v_w_o, v_ffn_norm_g, v_w_gate, v_w_up, v_w_down):
    nl, d = attn_norm_g.shape
    x2, target = x[0], loss_target[0]
    me = 4 * lax.axis_index("x") + 2 * lax.axis_index("y") + lax.axis_index("c")
    ids = _rel_ids()

    meta_all = _ag_small(meta_tokens, "ag_meta")
    meta_full = meta_all.reshape(NDEV, N_META, -1).transpose(1, 0, 2).reshape(N_META, d)
    def t(a):
        return jnp.swapaxes(a, 1, 2)

    w_in_v, w_gate_v, w_up_v = t(w_in), t(w_gate), t(w_up)
    shards = [_prep(w_in_v, "prep_w_in"), _prep(w_o, "prep_w_o"), _prep(w_gate_v, "prep_w_gate"), _prep(w_up_v, "prep_w_up"),
              _prep(w_down, "prep_w_down")]
    attn_w = [_run_side(_gather_side(shards[:2], [0, 0]), "ag_attn_layer0")] + [None] * (nl - 1)
    ffn_w = [None] * nl

    qg2 = jnp.tile(q_norm_g, (1, TILE // HEAD_DIM))
    kg2 = jnp.tile(k_norm_g, (1, TILE // HEAD_DIM))
    out_g = jnp.concatenate([swa_out_g, sb_out_g], axis=1)
    swa_cols = (SWA_HEADS + 2 * SWA_KV_HEADS) * HEAD_DIM

    h = _embed(x2, meta_full)
    saved, weights = [], []
    for l in range(nl):
        w_in_t, w_o_f = attn_w[l]
        hn = _rmsnorm_fwd([h], attn_norm_g[l:l + 1], "attn_norm")
        nxt = l + 1 < nl
        sb_cols = (swa_cols, 3 * SB_HEADS * HEAD_DIM)
        proj = _mm(hn, w_in_t, 0, "nt", "proj_swa", cols=(0, swa_cols))
        if l == 0:
            proj_sb, (gate0,) = _mm(hn, w_in_t, 0, "nt", "proj_sb", out_dtype=bf16, cols=sb_cols, side=_gather_side(shards[2:3], [0]))
            out_a, (up0,) = _swa_fwd(proj, qg2[l:l + 1], kg2[l:l + 1], attn_sinks[l], _gather_side(shards[3:4], [0]))
            riding = [(shards[4], 0)] + ([(shards[0], 1), (shards[1], 1)] if nxt else [])
            out_b, gathered = _sb_fwd(proj_sb, _gather_side([s for s, _ in riding], [k for _, k in riding]))
            ffn_w[0] = [gate0, up0, gathered[0]]
            if nxt:
                attn_w[1] = gathered[1:]
        else:
            proj_sb = _mm(hn, w_in_t, 0, "nt", "proj_sb", out_dtype=bf16, cols=sb_cols)
            out_a, up_next = _swa_fwd(proj, qg2[l:l + 1], kg2[l:l + 1], attn_sinks[l], _gather_side(shards[3:4], [l + 1]) if nxt else None)
            out_b, attn_next = _sb_fwd(proj_sb, _gather_side(shards[:2], [l + 1, l + 1]) if nxt else None)
            if nxt:
                attn_w[l + 1] = attn_next
        w_gate_t, w_up_t, w_down_f = ffn_w[l]
        weights.append((w_in_t, w_o_f, w_gate_t, w_up_t, w_down_f))
        mixed = _rmsnorm_fwd([out_a, out_b], out_g[l:l + 1], "out_norm")
        h_mid = _mm(mixed, w_o_f, 0, "nn", "attn_out", res=h)
        hn2 = _rmsnorm_fwd([h_mid], ffn_norm_g[l:l + 1], "ffn_norm")
        if nxt:
            ahead = shards[2:4] if l == 0 else shards[2:3]
            g, u, act, gate_up = _ffn_up(hn2, w_gate_t, w_up_t, _gather_side(ahead, [l + 1] * len(ahead)))
            h_out, down = _mm(act, w_down_f, 0, "nn", "ffn_down", res=h_mid, side=_gather_side(shards[4:], [l + 1]))
            ffn_w[l + 1] = (gate_up if l == 0 else gate_up + up_next) + down
        else:
            g, u, act, _ = _ffn_up(hn2, w_gate_t, w_up_t)
            h_out = _mm(act, w_down_f, 0, "nn", "ffn_down", res=h_mid)
        saved.append((h, hn, proj, proj_sb, out_a, out_b, mixed, h_mid, hn2, g, u, act))
        h = h_out

    loss_tile, dh, dh_b = _loss_head(h, target)
    loss = lax.psum(loss_tile[0, 0], AXES)

    small = {k: [None] * nl for k in ("attn", "ffn", "out", "q", "k", "sink")}
    pending = None
    red_attn = [None] * nl
    red_ff = [None] * nl

    def attn_partial_sums(slabs, sibs):
        return [_rs_partial(gg, sib, ids, name) for gg, sib, name in zip(slabs, sibs, ("rs_part_w_in", "rs_part_w_o"))]

    for l in reversed(range(nl)):
        w_in_t, w_o_f, w_gate_t, w_up_t, w_down_f = weights[l]
        h_in, hn, proj, proj_sb, out_a, out_b, mixed, h_mid, hn2, g, u, act = saved[l]
        dg, du, sibs = _ffn_down_bwd(dh_b, w_down_f, g, u, None if pending is None else _swap_side(pending))
        parts = [] if pending is None else attn_partial_sums(pending, sibs)
        g_ff = _mm_tn(act, dh_b, None, 2, 3, "grad_w_down")
        g_ff = _mm_tn(dg, hn2, g_ff, 0, 3, "grad_w_gate")
        g_ff = _mm_tn(du, hn2, g_ff, 1, 3, "grad_w_up")
        d_gate, (sib_ff,) = _mm(dg, w_gate_t, 0, "nn", "d_ffn_in_gate", side=_swap_side([g_ff]))
        part_ff = _rs_partial(g_ff, sib_ff, ids, "rs_part_w_ff")
        dhn2 = _mm(du, w_up_t, 0, "nn", "d_ffn_in_up", res=d_gate)
        (dh_mid,), small["ffn"][l], dh_mid_b = _rmsnorm_bwd(dhn2, [h_mid], ffn_norm_g[l:l + 1], dh, "ffn_norm_bwd")
        dmixed = _mm(dh_mid_b, w_o_f, 0, "nt", "d_mixed")
        g_o = _mm_tn(mixed, dh_mid_b, None, 0, 1, "grad_w_o")
        (dout_a, dout_b), small["out"][l] = _rmsnorm_bwd(dmixed, [out_a, out_b], out_g[l:l + 1], None, "out_norm_bwd", bf16)
        dq_b, dk_b, dv_b, (arrived_ff,) = _sb_bwd(proj_sb, dout_b, _ici_side([part_ff[1]]))
        red_ff[l] = (part_ff[0], arrived_ff)
        dq_a, dk_a, dv_a, dqg, dkg, dsk, arrived = _swa_bwd(proj, dout_a, qg2[l:l + 1], kg2[l:l + 1], attn_sinks[l],
                                                            _ici_side([p[1] for p in parts]) if parts else None)
        if parts:
            red_attn[l + 1] = ([p[0] for p in parts], arrived)
        small["q"][l] = dqg[0, :HEAD_DIM] + dqg[0, HEAD_DIM:]
        small["k"][l] = dkg[0, :HEAD_DIM] + dkg[0, HEAD_DIM:]
        small["sink"][l] = dsk[0, :SWA_HEADS]
        dproj = jnp.concatenate([dq_a, dk_a, dv_a, dq_b, dk_b, dv_b], axis=1)
        dhn = _mm(dproj, w_in_t, 0, "nn", "d_attn_in")
        g_in = _mm_tn(dproj, hn, None, 0, 1, "grad_w_in")
        (dh,), small["attn"][l], dh_b = _rmsnorm_bwd(dhn, [h_in], attn_norm_g[l:l + 1], dh_mid, "attn_norm_bwd")
        pending = [g_in, g_o]

    parts = attn_partial_sums(pending, _run_side(_swap_side(pending), "rs_swap_attn_layer0"))
    red_attn[0] = ([p[0] for p in parts], list(_run_side(_ici_side([p[1] for p in parts]), "rs_ici_attn_layer0")))

    grad_x = dh[TILE:][None]
    d_meta = dh[PAD:TILE]

    qw = SWA_HEADS * HEAD_DIM
    d_out = jnp.concatenate(small["out"], axis=0)
    small_grads = [jnp.concatenate(small["attn"], axis=0), jnp.stack(small["q"]), jnp.stack(small["k"]), jnp.stack(small["sink"]),
                   d_out[:, :qw], d_out[:, qw:], jnp.concatenate(small["ffn"], axis=0), d_meta]
    col0 = me * meta_tokens.shape[1]

    def widen(a):
        return lax.dynamic_update_slice(jnp.zeros((N_META, d), f32), a, (0, col0))

    small_w = [attn_norm_g, q_norm_g, k_norm_g, attn_sinks, swa_out_g, sb_out_g, ffn_norm_g]
    small_m = [m_attn_norm_g, m_q_norm_g, m_k_norm_g, m_attn_sinks, m_swa_out_g, m_sb_out_g, m_ffn_norm_g]
    small_v = [v_attn_norm_g, v_q_norm_g, v_k_norm_g, v_attn_sinks, v_swa_out_g, v_sb_out_g, v_ffn_norm_g]
    shapes = [a.shape for a in small_w] + [(N_META, d)]
    packed = _adam_small(_ag_small(_pack(small_grads, d), "ag_small_grads"),
                         _pack(small_w + [widen(meta_tokens)], d), _pack(small_m + [widen(m_meta_tokens)], d),
                         _pack(small_v + [widen(v_meta_tokens)], d))
    small_out = []
    for p in packed:
        parts = _unpack(p, shapes)
        parts[-1] = lax.dynamic_slice(parts[-1], (0, col0), meta_tokens.shape)
        small_out.append(parts)

    big = dict.fromkeys(("w_in", "w_o", "w_gate", "w_up", "w_down"))
    for l in range(nl):
        (own_in, own_o), (arr_in, arr_o) = red_attn[l]
        own_ff, arr_ff = red_ff[l]
        big["w_in"] = _adam_big(w_in_v, t(m_w_in), t(v_w_in), own_in, arr_in, l, 0, "adam_w_in", big["w_in"])
        big["w_o"] = _adam_big(w_o, m_w_o, v_w_o, own_o, arr_o, l, 0, "adam_w_o", big["w_o"])
        big["w_gate"] = _adam_big(w_gate_v, t(m_w_gate), t(v_w_gate), own_ff, arr_ff, l, 0, "adam_w_gate", big["w_gate"])
        big["w_up"] = _adam_big(w_up_v, t(m_w_up), t(v_w_up), own_ff, arr_ff, l, 1, "adam_w_up", big["w_up"])
        big["w_down"] = _adam_big(w_down, m_w_down, v_w_down, own_ff, arr_ff, l, 2, "adam_w_down", big["w_down"])
    for name in ("w_in", "w_gate", "w_up"):
        big[name] = [t(a) for a in big[name]]

    def group(k):
        sm = small_out[k]
        return [sm[7], sm[0], big["w_in"][k], sm[1], sm[2], sm[3], sm[4], sm[5], big["w_o"][k], sm[6],
                big["w_gate"][k], big["w_up"][k], big["w_down"][k]]

    return (loss, grad_x, *group(0), *group(1), *group(2), *group(3))
```

```python
import jax
import jax.numpy as jnp
from jax import lax
from jax.experimental import pallas as pl
from jax.experimental.pallas import tpu as pltpu

f32, bf16 = jnp.float32, jnp.bfloat16
S = jax.ShapeDtypeStruct
BS = pl.BlockSpec

N_META = 16
TILE = 128
PAD = TILE - N_META
HEAD_DIM = 64
SWA_HEADS = 16
SWA_KV_HEADS = 4
SB_HEADS = 16
EPS = 1e-6
NEG = -1e30
ADAM_LR, ADAM_B1, ADAM_B2, ADAM_EPS, ADAM_WD, ADAM_STEP = 0.001, 0.9, 0.999, 1e-08, 0.01, 10
AXES = ("x", "y", "c")
NDEV = 8
VMEM_LIMIT_V7X = 56 * 1024 * 1024
MM_RESIDENT_K = 2048
TN_VMEM_BUDGET = 46 * 1024 * 1024
MESH = pl.DeviceIdType.MESH
ANY = pl.BlockSpec(memory_space=pl.ANY)

NN = ((1,), (0,))
NT = ((1,), (1,))
TN = ((0,), (0,))


def _cp(*sem):
    return pltpu.CompilerParams(dimension_semantics=sem if sem else None, vmem_limit_bytes=VMEM_LIMIT_V7X)


def _pick(n, cands):
    for c in cands:
        if n % c == 0:
            return c
    raise ValueError(f"no tile for {n} in {cands}")


def _dot(a, b, dims):
    return lax.dot_general(a, b, (dims, ((), ())), preferred_element_type=f32)


def _embed(x, meta_full):
    seq, d = x.shape
    meta_pad = jnp.pad(meta_full, ((PAD, 0), (0, 0)))

    def body(x_ref, m_ref, o_ref):
        i = pl.program_id(0)

        @pl.when(i == 0)
        def _():
            o_ref[...] = m_ref[...]

        @pl.when(i > 0)
        def _():
            o_ref[...] = x_ref[...]

    return pl.pallas_call(
        body, out_shape=S((TILE + seq, d), f32), grid=(1 + seq // TILE,),
        in_specs=[BS((TILE, d), lambda i: (jnp.maximum(i - 1, 0), 0)), BS((TILE, d), lambda i: (0, 0))],
        out_specs=BS((TILE, d), lambda i: (i, 0)), name="embed", compiler_params=_cp("arbitrary"))(x, meta_pad)


def _rmsnorm_fwd(xs, gain, name):
    lp = xs[0].shape[0]
    widths = [x.shape[1] for x in xs]
    w = sum(widths)
    tr = _pick(lp, (384, 256, 128))
    n = len(xs)

    def body(*refs):
        g_ref, o_ref = refs[n], refs[n + 1]
        off = 0
        for k in range(n):
            sl = slice(off, off + widths[k])
            off += widths[k]
            xv = refs[k][...]
            r = lax.rsqrt(jnp.mean(xv * xv, axis=-1, keepdims=True) + EPS)
            o_ref[:, sl] = ((xv * r) * g_ref[:, sl]).astype(bf16)

    return pl.pallas_call(
        body, out_shape=S((lp, w), bf16), grid=(lp // tr,),
        in_specs=[BS((tr, wk), lambda i: (i, 0)) for wk in widths] + [BS((1, w), lambda i: (0, 0))],
        out_specs=BS((tr, w), lambda i: (i, 0)), name=name, compiler_params=_cp("parallel"))(*xs, gain)


def _rmsnorm_bwd(dy, xs, gain, res, name, out_dtype=f32, layer=0, nl=1, slab=None):
    lp = xs[0].shape[0]
    widths = [x.shape[1] for x in xs]
    w = sum(widths)
    tr = _pick(lp, (384, 256, 128))
    n = len(xs)
    has_res = res is not None
    has_slab = slab is not None
    assert not has_res or n == 1

    def body(*refs):
        dy_ref, g_ref = refs[0], refs[1 + n]
        res_ref = refs[2 + n] if has_res else None
        first_out = 2 + n + has_res + has_slab
        dx_refs = refs[first_out:first_out + n]
        dg_ref = refs[first_out + n]

        @pl.when(pl.program_id(0) == 0)
        def _():
            dg_ref[...] = jnp.zeros_like(dg_ref)

        off = 0
        for k in range(n):
            sl = slice(off, off + widths[k])
            off += widths[k]
            xv = refs[1 + k][...]
            dyv = dy_ref[:, sl]
            r = lax.rsqrt(jnp.mean(xv * xv, axis=-1, keepdims=True) + EPS)
            xh = xv * r
            dg_ref[:, sl] += jnp.sum(dyv * xh, axis=0, keepdims=True)
            dxh = dyv * g_ref[:, sl]
            dx = r * (dxh - xh * jnp.mean(dxh * xh, axis=-1, keepdims=True))
            if has_res:
                dx = dx + res_ref[...]
                refs[-1][...] = dx.astype(bf16)
            dx_refs[k][...] = dx.astype(out_dtype)

    rows = [BS((tr, wk), lambda i: (i, 0)) for wk in widths]
    full = BS((tr, w), lambda i: (i, 0))
    vec = BS((1, w), lambda i: (0, 0))
    args = (dy, *xs, gain) + ((res,) if has_res else ()) + ((slab,) if has_slab else ())
    row_of_slab = BS((None, 1, w), lambda i: (layer, 0, 0))
    out = pl.pallas_call(
        body, out_shape=tuple(S((lp, wk), out_dtype) for wk in widths) + (S((nl, 1, w), f32),) + ((S((lp, w), bf16),) if has_res else ()),
        grid=(lp // tr,),
        in_specs=[full] + rows + [vec] + ([full] if has_res else []) + ([ANY] if has_slab else []),
        out_specs=tuple(rows) + (row_of_slab,) + ((full,) if has_res else ()),
        input_output_aliases={len(args) - 1: n} if has_slab else {},
        name=name, compiler_params=_cp("arbitrary"))(*args)
    return (list(out[:n]), out[n], out[n + 1]) if has_res else (list(out[:n]), out[n])


def _loss_head(h, target):
    lp, d = h.shape

    def body(h_ref, t_ref, loss_ref, dh_ref, dhb_ref):
        i = pl.program_id(0)

        @pl.when(i == 0)
        def _():
            loss_ref[...] = jnp.zeros_like(loss_ref)
            dh_ref[...] = jnp.zeros_like(dh_ref)
            dhb_ref[...] = jnp.zeros_like(dhb_ref)

        @pl.when(i > 0)
        def _():
            e = h_ref[...] - t_ref[...]
            dh = e * (1.0 / d)
            dh_ref[...] = dh
            dhb_ref[...] = dh.astype(bf16)
            loss_ref[...] += 0.5 * jnp.sum(jnp.sum(e * e, axis=-1, keepdims=True) * (1.0 / d))

    tile = BS((TILE, d), lambda i: (i, 0))
    return pl.pallas_call(
        body, out_shape=(S((8, 128), f32), S((lp, d), f32), S((lp, d), bf16)), grid=(lp // TILE,),
        in_specs=[tile, BS((TILE, d), lambda i: (jnp.maximum(i - 1, 0), 0))],
        out_specs=(BS((8, 128), lambda i: (0, 0)), tile, tile),
        name="loss_head", compiler_params=_cp("arbitrary"))(h, target)


def _mm(a, b, layer, mode, name, out_dtype=f32, res=None, cols=None, side=None):
    m, k = a.shape
    n = b.shape[2] if mode == "nn" else b.shape[1]
    c0 = 0
    if cols is not None:
        assert mode == "nt"
        c0, n = cols
    tn = _pick(n, (512, 384, 256, 128))
    while c0 % tn:
        tn -= TILE
    assert n % tn == 0
    joff = c0 // tn
    single_a = k > MM_RESIDENT_K and a.dtype == bf16 and m % 1408 == 0
    rows_outer = k <= MM_RESIDENT_K or single_a
    tm = _pick(m, (1408, 384, 256, 128) if rows_outer else (384, 256, 128))
    dims = NN if mode == "nn" else NT
    has_res = res is not None

    grid = (m // tm, n // tn) if rows_outer else (n // tn, m // tm)

    def body(*refs):
        refs, side_refs = _side_split(side, refs, 2 + has_res, 1, 0)
        _side_start(side, side_refs, grid)
        if has_res:
            a_ref, b_ref, r_ref, o_ref = refs
        else:
            a_ref, b_ref, o_ref = refs
        acc = _dot(a_ref[...].astype(bf16), b_ref[...], dims)
        if has_res:
            acc = acc + r_ref[...]
        o_ref[...] = acc.astype(out_dtype)
        _side_finish(side, side_refs, grid)

    def ij(f):
        return (lambda i, j: f(i, j)) if rows_outer else (lambda j, i: f(i, j))

    if mode == "nn":
        b_spec = BS((None, k, tn), ij(lambda i, j: (layer, 0, j)))
    else:
        b_spec = BS((None, tn, k), ij(lambda i, j: (layer, joff + j, 0)))
    tile = BS((tm, tn), ij(lambda i, j: (i, j)))
    args = (a, b) + ((res,) if has_res else ())
    s_in, s_in_specs, s_out, s_out_specs, s_scratch = _side_io(side)
    sem = ("parallel", "parallel") if side is None else ("arbitrary", "arbitrary")
    out = pl.pallas_call(
        body, out_shape=(S((m, n), out_dtype), *s_out), grid=grid,
        in_specs=[BS((tm, k), ij(lambda i, j: (i, 0)), **(dict(pipeline_mode=pl.Buffered(1)) if single_a else {})), b_spec]
        + ([tile] if has_res else []) + s_in_specs,
        out_specs=(tile, *s_out_specs), scratch_shapes=s_scratch,
        name=name if side is None else name + "_comm", compiler_params=_cp(*sem))(*args, *s_in)
    return out[0] if side is None else (out[0], list(out[1:]))


def _mm_tn(a, b, gbuf, slot, nslots, name):
    t, m = a.shape
    n = b.shape[1]
    tk = _pick(t, (1408, 1024, 512, 384, 256, 128))
    nk = t // tk

    def vmem(rows):
        return 2 * (rows * n * 4 + tk * rows * a.dtype.itemsize + tk * n * b.dtype.itemsize)

    tm = next(c for c in (1536, 1408, 1024, 512, 384, 256, 128) if m % c == 0 and (vmem(c) <= TN_VMEM_BUDGET or c == 128))

    def body(*refs):
        a_ref, b_ref, o_ref = refs[0], refs[1], refs[-1]
        kk = pl.program_id(1)

        @pl.when(kk == 0)
        def _():
            o_ref[...] = jnp.zeros_like(o_ref)

        o_ref[...] += _dot(a_ref[...].astype(bf16), b_ref[...].astype(bf16), TN)

    in_specs = [BS((tk, tm), lambda i, kk: (kk, i)), BS((tk, n), lambda i, kk: (kk, 0))]
    args = (a, b)
    aliases = {}
    if gbuf is not None:
        in_specs.append(ANY)
        args = args + (gbuf,)
        aliases = {2: 0}
    return pl.pallas_call(
        body, out_shape=S((nslots, m, n), f32), grid=(m // tm, nk), in_specs=in_specs,
        out_specs=BS((None, tm, n), lambda i, kk: (slot, i, 0)), input_output_aliases=aliases,
        name=name, compiler_params=_cp("parallel", "arbitrary"))(*args)


def _ffn_up(hn, wg, wu, side=None):
    m, k = hn.shape
    n = wg.shape[1]
    tm = _pick(m, (1408, 384, 256, 128))
    tn = _pick(n, (512, 384, 256, 128))
    grid = (m // tm, n // tn)

    def body(*refs):
        (a_ref, wg_ref, wu_ref, g_ref, u_ref, act_ref), side_refs = _side_split(side, refs, 3, 3, 0)
        _side_start(side, side_refs, grid)
        a = a_ref[...]
        g = _dot(a, wg_ref[...], NT)
        u = _dot(a, wu_ref[...], NT)
        g_ref[...] = g
        u_ref[...] = u
        act_ref[...] = (g * jax.nn.sigmoid(g) * u).astype(bf16)
        _side_finish(side, side_refs, grid)

    tile = BS((tm, tn), lambda i, j: (i, j))
    s_in, s_in_specs, s_out, s_out_specs, s_scratch = _side_io(side)
    sem = ("parallel", "parallel") if side is None else ("arbitrary", "arbitrary")
    out = pl.pallas_call(
        body, out_shape=(S((m, n), f32), S((m, n), f32), S((m, n), bf16), *s_out), grid=grid,
        in_specs=[BS((tm, k), lambda i, j: (i, 0)), BS((None, tn, k), lambda i, j: (0, j, 0)),
                  BS((None, tn, k), lambda i, j: (0, j, 0)), *s_in_specs],
        out_specs=(tile, tile, tile, *s_out_specs), scratch_shapes=s_scratch,
        name="ffn_up" if side is None else "ffn_up_comm", compiler_params=_cp(*sem))(hn, wg, wu, *s_in)
    return out[0], out[1], out[2], list(out[3:])


def _ffn_down_bwd(dh, wd, g, u, side=None):
    m, k = dh.shape
    n = wd.shape[1]
    tm = _pick(m, (1408, 384, 256, 128))
    tn = _pick(n, (512, 384, 256, 128))
    grid = (m // tm, n // tn)

    def body(*refs):
        (a_ref, wd_ref, g_ref, u_ref, dg_ref, du_ref), side_refs = _side_split(side, refs, 4, 2, 0)
        _side_start(side, side_refs, grid)
        dact = _dot(a_ref[...].astype(bf16), wd_ref[...], NT)
        gv = g_ref[...]
        sg = jax.nn.sigmoid(gv)
        dg_ref[...] = (dact * u_ref[...] * (sg * (1.0 + gv * (1.0 - sg)))).astype(bf16)
        du_ref[...] = (dact * (gv * sg)).astype(bf16)
        _side_finish(side, side_refs, grid)

    tile = BS((tm, tn), lambda i, j: (i, j))
    s_in, s_in_specs, s_out, s_out_specs, s_scratch = _side_io(side)
    sem = ("parallel", "parallel") if side is None else ("arbitrary", "arbitrary")
    out = pl.pallas_call(
        body, out_shape=(S((m, n), bf16), S((m, n), bf16), *s_out), grid=grid,
        in_specs=[BS((tm, k), lambda i, j: (i, 0)), BS((None, tn, k), lambda i, j: (0, j, 0)), tile, tile, *s_in_specs],
        out_specs=(tile, tile, *s_out_specs), scratch_shapes=s_scratch,
        name="ffn_down_bwd" if side is None else "ffn_down_bwd_comm", compiler_params=_cp(*sem))(dh, wd, g, u, *s_in)
    return out[0], out[1], list(out[2:])


def _lane_lo():
    return lax.broadcasted_iota(jnp.int32, (1, TILE), 1) < HEAD_DIM


def _head_norm(x, g2):
    lo = _lane_lo()
    x2 = x * x
    s_lo = jnp.sum(jnp.where(lo, x2, 0.0), axis=-1, keepdims=True)
    s_hi = jnp.sum(jnp.where(lo, 0.0, x2), axis=-1, keepdims=True)
    r = jnp.where(lo, lax.rsqrt(s_lo * (1.0 / HEAD_DIM) + EPS), lax.rsqrt(s_hi * (1.0 / HEAD_DIM) + EPS))
    xh = x * r
    return xh * g2, xh, r


def _head_norm_bwd(dy, xh, r, g2):
    lo = _lane_lo()
    dgain = jnp.sum(dy * xh, axis=0, keepdims=True)
    dxh = dy * g2
    t = dxh * xh
    t_lo = jnp.sum(jnp.where(lo, t, 0.0), axis=-1, keepdims=True)
    t_hi = jnp.sum(jnp.where(lo, 0.0, t), axis=-1, keepdims=True)
    mt = jnp.where(lo, t_lo, t_hi) * (1.0 / HEAD_DIM)
    return r * (dxh - xh * mt), dgain


def _swa_slopes():
    return [2.0 ** (-8.0 * (h + 1) / SWA_HEADS) for h in range(SWA_HEADS)]


def _swa_masks(i):
    rows = lax.broadcasted_iota(jnp.int32, (TILE, TILE), 0)
    cols = lax.broadcasted_iota(jnp.int32, (TILE, TILE), 1)
    r_pos = i * TILE + rows
    prev = jnp.maximum(i - 1, 0)
    out = []
    for c, base in enumerate((0, prev * TILE, i * TILE)):
        s_pos = base + cols
        dist = r_pos - s_pos
        if c == 0:
            ok = (i >= 1) & (s_pos >= PAD)
        elif c == 1:
            ok = (i >= 2) & (dist < TILE)
        else:
            ok = (dist >= 0) & (s_pos >= PAD)
        out.append((ok, dist.astype(f32)))
    return out, prev


def _swa_geometry(i, group):
    masks, prev = _swa_masks(i)
    bases = (0, pl.multiple_of(prev * TILE, TILE), pl.multiple_of(i * TILE, TILE))
    ok = jnp.concatenate([m[0] for m in masks], axis=1)
    dist = jnp.concatenate([m[1] for m in masks], axis=1)
    return bases, jnp.concatenate([ok] * group, axis=0), jnp.concatenate([dist] * group, axis=0)


def _swa_col(values):
    return jnp.concatenate([jnp.full((TILE, 1), v, f32) for v in values], axis=0)


def _swa_keys(k_ref, v_ref, kg_ref, bases, kvh):
    kcols = slice((kvh // 2) * TILE, (kvh // 2 + 1) * TILE)
    ksel = _lane_lo() if kvh % 2 == 0 else jnp.logical_not(_lane_lo())
    kcat = jnp.concatenate([_head_norm(k_ref[pl.ds(b, TILE), kcols], kg_ref[...])[0] for b in bases], axis=0).astype(bf16)
    vcat = jnp.concatenate([jnp.where(ksel, v_ref[pl.ds(b, TILE), kcols], 0.0) for b in bases], axis=0).astype(bf16)
    return kcat, vcat, ksel


def _swa_stack(blocks, kvh, group, ksel):
    parts = []
    for gi in range(group):
        same = ((kvh * group + gi) % 2) == (kvh % 2)
        parts.append(jnp.where(ksel, blocks[gi] if same else pltpu.roll(blocks[gi], HEAD_DIM, 1), 0.0))
    return jnp.concatenate(parts, axis=0)


def _swa_unstack_add(acc, stacked, kvh, group):
    for gi in range(group):
        h = kvh * group + gi
        part = stacked[gi * TILE:(gi + 1) * TILE]
        acc[h // 2] = acc[h // 2] + (part if (h % 2) == (kvh % 2) else pltpu.roll(part, HEAD_DIM, 1))


def _swa_probs(q4, kcat, ok_g, dist_g, slope_col, sink_col):
    logits = _dot(q4, kcat, NT) * (HEAD_DIM ** -0.5) + jnp.where(ok_g, -slope_col * dist_g, NEG)
    mx = jnp.maximum(jnp.max(logits, axis=-1, keepdims=True), sink_col)
    return jnp.exp(logits - mx), jnp.exp(sink_col - mx)


def _swa_specs(lp, pw):
    qw, kvw = SWA_HEADS * HEAD_DIM, SWA_KV_HEADS * HEAD_DIM
    kidx = qw // kvw
    return (BS((TILE, qw), lambda i: (i, 0)), BS((lp, kvw), lambda i: (0, kidx)), BS((lp, kvw), lambda i: (0, kidx + 1)),
            BS((1, TILE), lambda i: (0, 0)))


def _swa_fwd(proj, qg2, kg2, sinks, side=None):
    lp, pw = proj.shape
    qw = SWA_HEADS * HEAD_DIM
    group = SWA_HEADS // SWA_KV_HEADS
    slopes = _swa_slopes()
    grid = (lp // TILE,)

    def body(*refs):
        (sink_ref, q_ref, k_ref, v_ref, qg_ref, kg_ref, o_ref), side_refs = _side_split(side, refs, 6, 1, 0)
        _side_start(side, side_refs, grid)
        i = pl.program_id(0)
        bases, ok_g, dist_g = _swa_geometry(i, group)
        qn = [_head_norm(q_ref[:, b * TILE:(b + 1) * TILE], qg_ref[...])[0] for b in range(qw // TILE)]
        acc = [jnp.zeros((TILE, TILE), f32) for _ in range(qw // TILE)]
        for kvh in range(SWA_KV_HEADS):
            kcat, vcat, ksel = _swa_keys(k_ref, v_ref, kg_ref, bases, kvh)
            q4 = _swa_stack([qn[(kvh * group + gi) // 2] for gi in range(group)], kvh, group, ksel).astype(bf16)
            sink_col = _swa_col([sink_ref[kvh * group + gi] for gi in range(group)])
            p, e_sink = _swa_probs(q4, kcat, ok_g, dist_g, _swa_col(slopes[kvh * group:(kvh + 1) * group]), sink_col)
            den = e_sink + jnp.sum(p, axis=-1, keepdims=True)
            o = _dot(p.astype(bf16), vcat, NN) / den
            _swa_unstack_add(acc, o, kvh, group)
        valid = (i * TILE + lax.broadcasted_iota(jnp.int32, (TILE, 1), 0)) >= PAD
        for b in range(qw // TILE):
            o_ref[:, b * TILE:(b + 1) * TILE] = jnp.where(valid, acc[b], 0.0)
        _side_finish(side, side_refs, grid)

    q_spec, k_spec, v_spec, vec = _swa_specs(lp, pw)
    s_in, s_in_specs, s_out, s_out_specs, s_scratch = _side_io(side)
    out = pl.pallas_call(
        body, out_shape=(S((lp, qw), f32), *s_out), grid=grid,
        in_specs=[BS(memory_space=pltpu.SMEM), q_spec, k_spec, v_spec, vec, vec, *s_in_specs],
        out_specs=(BS((TILE, qw), lambda i: (i, 0)), *s_out_specs), scratch_shapes=s_scratch,
        name="swa_fwd" if side is None else "swa_fwd_comm", compiler_params=_cp("arbitrary"))(
            sinks, proj, proj, proj, qg2, kg2, *s_in)
    return out[0], list(out[1:])


def _swa_bwd(proj, dout, qg2, kg2, sinks, side=None, layer=0, nl=1, slabs=None):
    lp, pw = proj.shape
    qw, kvw = SWA_HEADS * HEAD_DIM, SWA_KV_HEADS * HEAD_DIM
    group = SWA_HEADS // SWA_KV_HEADS
    scale = HEAD_DIM ** -0.5
    slopes = _swa_slopes()
    nt = lp // TILE
    threaded = () if slabs is None else tuple(slabs)
    n_in = 7 + len(threaded)

    def body(*refs):
        main, side_refs = _side_split(side, refs, n_in, 6, 2)
        sink_ref, q_ref, k_ref, v_ref, do_ref, qg_ref, kg_ref = main[:7]
        dq_ref, dk_ref, dv_ref, dqg_ref, dkg_ref, ds_ref, dkn_acc, dv_acc = main[n_in:]
        _side_start(side, side_refs, (nt,))
        i = pl.program_id(0)
        lo = _lane_lo()
        lane = lax.broadcasted_iota(jnp.int32, (1, TILE), 1)

        @pl.when(i == 0)
        def _():
            dkn_acc[...] = jnp.zeros_like(dkn_acc)
            dv_acc[...] = jnp.zeros_like(dv_acc)
            dqg_ref[...] = jnp.zeros_like(dqg_ref)
            dkg_ref[...] = jnp.zeros_like(dkg_ref)
            ds_ref[...] = jnp.zeros_like(ds_ref)

        bases, ok_g, dist_g = _swa_geometry(i, group)
        qnorm = [_head_norm(q_ref[:, b * TILE:(b + 1) * TILE], qg_ref[...]) for b in range(qw // TILE)]
        dqn = [jnp.zeros((TILE, TILE), f32) for _ in range(qw // TILE)]
        for kvh in range(SWA_KV_HEADS):
            kcols = slice((kvh // 2) * TILE, (kvh // 2 + 1) * TILE)
            heads = [kvh * group + gi for gi in range(group)]
            kcat, vcat, ksel = _swa_keys(k_ref, v_ref, kg_ref, bases, kvh)
            q4 = _swa_stack([qnorm[h // 2][0] for h in heads], kvh, group, ksel).astype(bf16)
            do4 = _swa_stack([do_ref[:, (h // 2) * TILE:(h // 2 + 1) * TILE].astype(f32) for h in heads], kvh, group, ksel).astype(bf16)
            sink_col = _swa_col([sink_ref[h] for h in heads])
            p, e_sink = _swa_probs(q4, kcat, ok_g, dist_g, _swa_col([slopes[h] for h in heads]), sink_col)
            inv = 1.0 / (e_sink + jnp.sum(p, axis=-1, keepdims=True))
            p = p * inv
            dp = _dot(do4, vcat, NT)
            dsum = jnp.sum(p * dp, axis=-1, keepdims=True)
            d_sink = e_sink * inv * dsum
            for gi, h in enumerate(heads):
                ds_ref[...] += jnp.where(lane == h, -jnp.sum(d_sink[gi * TILE:(gi + 1) * TILE]), 0.0)
            dsc = (p * (dp - dsum) * scale).astype(bf16)
            _swa_unstack_add(dqn, jnp.where(ksel, _dot(dsc, kcat, NN), 0.0), kvh, group)
            dkn = _dot(dsc, q4, TN)
            dvv = _dot(p.astype(bf16), do4, TN)
            for c in range(3):
                dkn_acc[pl.ds(bases[c], TILE), kcols] += dkn[c * TILE:(c + 1) * TILE]
                dv_acc[pl.ds(bases[c], TILE), kcols] += dvv[c * TILE:(c + 1) * TILE]
        for b in range(qw // TILE):
            _, xh, r = qnorm[b]
            dq, dgain = _head_norm_bwd(dqn[b], xh, r, qg_ref[...])
            dq_ref[:, b * TILE:(b + 1) * TILE] = dq.astype(bf16)
            dqg_ref[...] += dgain

        @pl.when(i == nt - 1)
        def _():
            dv_ref[...] = dv_acc[...].astype(bf16)

            def tile_step(t, carry):
                base = pl.multiple_of(t * TILE, TILE)
                for kb in range(kvw // TILE):
                    kcols = slice(kb * TILE, (kb + 1) * TILE)
                    _, xh, r = _head_norm(k_ref[pl.ds(base, TILE), kcols], kg_ref[...])
                    dk, dgain = _head_norm_bwd(dkn_acc[pl.ds(base, TILE), kcols], xh, r, kg_ref[...])
                    dk_ref[pl.ds(base, TILE), kcols] = dk.astype(bf16)
                    dkg_ref[...] += dgain
                return carry

            lax.fori_loop(0, nt, tile_step, 0)

        _side_finish(side, side_refs, (nt,))

    q_spec, k_spec, v_spec, vec = _swa_specs(lp, pw)
    whole = BS((lp, kvw), lambda i: (0, 0))
    s_in, s_in_specs, s_out, s_out_specs, s_scratch = _side_io(side)
    slab = S((nl, 1, TILE), f32)
    slab_row = BS((None, 1, TILE), lambda i: (layer, 0, 0))
    out = pl.pallas_call(
        body, out_shape=(S((lp, qw), bf16), S((lp, kvw), bf16), S((lp, kvw), bf16), slab, slab, slab, *s_out),
        grid=(nt,),
        in_specs=[BS(memory_space=pltpu.SMEM), q_spec, k_spec, v_spec, BS((TILE, qw), lambda i: (i, 0)), vec, vec,
                  *([ANY] * len(threaded)), *s_in_specs],
        out_specs=(BS((TILE, qw), lambda i: (i, 0)), whole, whole, slab_row, slab_row, slab_row, *s_out_specs),
        scratch_shapes=[pltpu.VMEM((lp, kvw), f32), pltpu.VMEM((lp, kvw), f32), *s_scratch],
        input_output_aliases={7 + k: 3 + k for k in range(len(threaded))},
        name="swa_bwd" if side is None else "swa_bwd_comm", compiler_params=_cp("arbitrary"))(
            sinks, proj, proj, proj, dout, qg2, kg2, *threaded, *s_in)
    return (*out[:3], list(out[3:6]), list(out[6:]))


LOG2E = 1.4426950408889634
SB_PAIRS = 2
ROWS2 = 2 * TILE
SB_ROWS = SB_PAIRS * ROWS2


def _sb_stack(x):
    lo = _lane_lo()
    zero = jnp.zeros((TILE, TILE), x.dtype)
    parts = []
    for p in range(SB_PAIRS):
        xp = x[:, p * TILE:(p + 1) * TILE]
        parts += [jnp.where(lo, xp, zero), jnp.where(lo, zero, xp)]
    return jnp.concatenate(parts, axis=0)


def _sb_unstack(x):
    lo = _lane_lo()
    parts = [jnp.where(lo, x[p * ROWS2:p * ROWS2 + TILE], x[p * ROWS2 + TILE:(p + 1) * ROWS2]) for p in range(SB_PAIRS)]
    return parts[0] if SB_PAIRS == 1 else jnp.concatenate(parts, axis=1)


def _sb_pair_dot(a, b, dims, b_lanes):
    parts = []
    for p in range(SB_PAIRS):
        bp = b[:, p * TILE:(p + 1) * TILE] if b_lanes else b[p * ROWS2:(p + 1) * ROWS2]
        parts.append(_dot(a[p * ROWS2:(p + 1) * ROWS2], bp, dims))
    return parts


def _sb_sum_matrix(after):
    rows = lax.broadcasted_iota(jnp.int32, (ROWS2, ROWS2), 0) & (TILE - 1)
    cols = lax.broadcasted_iota(jnp.int32, (ROWS2, ROWS2), 1)
    tri = (rows > cols) if after else (rows < cols)
    return (tri | (cols >= TILE)).astype(bf16)


def _sb_sums(x, w, sum_mat):
    hi = x.astype(bf16)
    lo = (x - hi.astype(f32)).astype(bf16)
    return [_dot(jnp.concatenate([hi[:, j * TILE:(j + 1) * TILE], lo[:, j * TILE:(j + 1) * TILE]], axis=1), sum_mat, NN)
            for j in range(w)]


def _sb_edge_masks(qi, first, w):
    rows = lax.broadcasted_iota(jnp.int32, (SB_ROWS, TILE), 0) & (TILE - 1)
    cols = lax.broadcasted_iota(jnp.int32, (SB_ROWS, TILE), 1)
    not_pad = first * TILE + cols >= PAD
    before_query = (first + w - 1 - qi) * TILE + cols < rows
    if w == 1:
        return [not_pad & before_query]
    return [not_pad] + [None] * (w - 2) + [before_query]


def _sb_apply(valid, x):
    return x if valid is None else jnp.where(valid, x, 0.0)


def _sb_block(q2, k_ref, first, w, valids, c_rep, after_ones):
    kwin = k_ref[pl.ds(pl.multiple_of(first * TILE, TILE), w * TILE), :]
    z2 = jnp.concatenate(_sb_pair_dot(q2, kwin, NT, True), axis=0) * (HEAD_DIM ** -0.5 * LOG2E)
    ls2 = jnp.minimum(z2, 0.0) - jnp.log2(1.0 + jnp.exp2(jnp.minimum(z2, -z2)))
    m2 = ls2 - z2
    if w == 1:
        m2 = _sb_apply(valids[0], m2)
    else:
        m2 = jnp.concatenate([_sb_apply(valids[j], m2[:, j * TILE:(j + 1) * TILE]) for j in range(w)], axis=1)
    sums = _sb_sums(m2, w, after_ones)
    parts = [None] * w
    for j in reversed(range(w)):
        parts[j] = _sb_apply(valids[j], jnp.exp2(ls2[:, j * TILE:(j + 1) * TILE] + sums[j][:, :TILE] + c_rep))
        c_rep = c_rep + sums[j][:, TILE:]
    return ls2, (parts[0] if w == 1 else jnp.concatenate(parts, axis=1)), c_rep


SB_BLOCKS = (4, 2, 1)
SB_DEAD = -160.0


def _sb_alive(c_rep):
    return jnp.max(c_rep) > SB_DEAD


def _sb_walk_down(qi, block, state, c_of):
    n = qi + 1
    big = SB_BLOCKS[0]

    def more(carry):
        t, st = carry
        return (t < n // big) & _sb_alive(c_of(st))

    def step(carry):
        t, st = carry
        return t + 1, block(n - big * (t + 1), big, st)

    visited, state = lax.while_loop(more, step, (jnp.int32(0), state))
    counts = [visited]
    for s in SB_BLOCKS[1:]:
        cnt = jnp.where(_sb_alive(c_of(state)), (n % (2 * s)) // s, 0)
        state = lax.fori_loop(0, cnt, lambda t, st, s=s: block(n % s, s, st), state)
        counts.append(cnt)
    return state, counts


def _sb_walk_up(qi, block, state, counts):
    n = qi + 1
    big = SB_BLOCKS[0]
    for s, cnt in reversed(list(zip(SB_BLOCKS[1:], counts[1:]))):
        state = lax.fori_loop(0, cnt, lambda t, st, s=s: block(n % s, s, st), state)
    visited = counts[0]
    return lax.fori_loop(0, visited, lambda i, st: block(n - big * (visited - i), big, st), state)


def _sb_cols(lp, off, single=False):
    mode = dict(pipeline_mode=pl.Buffered(1)) if single else {}
    return BS((lp, SB_PAIRS * TILE), lambda g: (0, off + g), **mode)


def _sb_fwd(proj, side=None):
    lp = proj.shape[0]
    sbw = SB_HEADS * HEAD_DIM
    ngrp = sbw // (SB_PAIRS * TILE)
    nt = lp // TILE

    def body(*refs):
        (q_ref, k_ref, v_ref, o_ref), side_refs = _side_split(side, refs, 3, 1, 0)
        _side_start(side, side_refs, (ngrp,))
        after_ones = _sb_sum_matrix(True)

        def q_step(qi, carry):
            qbase = pl.multiple_of(qi * TILE, TILE)
            q2 = _sb_stack(q_ref[pl.ds(qbase, TILE), :])

            def block(first, w, st):
                c_rep, acc = st
                _, a, c_rep = _sb_block(q2, k_ref, first, w, _sb_edge_masks(qi, first, w), c_rep, after_ones)
                vwin = v_ref[pl.ds(pl.multiple_of(first * TILE, TILE), w * TILE), :]
                return c_rep, acc + jnp.concatenate(_sb_pair_dot(a.astype(bf16), vwin, NN, True), axis=0)

            zero = jnp.zeros((SB_ROWS, TILE), f32)
            (_, acc), _ = _sb_walk_down(qi, block, (zero, zero), lambda st: st[0])
            o_ref[pl.ds(qbase, TILE), :] = _sb_unstack(acc)
            return carry

        lax.fori_loop(0, nt, q_step, 0)
        _side_finish(side, side_refs, (ngrp,))

    s_in, s_in_specs, s_out, s_out_specs, s_scratch = _side_io(side)
    out = pl.pallas_call(
        body, out_shape=(S((lp, sbw), f32), *s_out), grid=(ngrp,),
        in_specs=[_sb_cols(lp, 0), _sb_cols(lp, ngrp), _sb_cols(lp, 2 * ngrp), *s_in_specs],
        out_specs=(_sb_cols(lp, 0), *s_out_specs), scratch_shapes=s_scratch,
        name="sb_fwd" if side is None else "sb_fwd_comm",
        compiler_params=_cp("parallel" if side is None else "arbitrary"))(proj, proj, proj, *s_in)
    return out[0], list(out[1:])


def _sb_bwd(proj, dout, side=None):
    lp = proj.shape[0]
    sbw = SB_HEADS * HEAD_DIM
    ngrp = sbw // (SB_PAIRS * TILE)
    nt = lp // TILE
    scale = HEAD_DIM ** -0.5

    def body(*refs):
        (q_ref, k_ref, v_ref, do_ref, dq_ref, dk_ref, dv_ref, dk_acc, dv_acc, e_buf, b_buf), side_refs = _side_split(side, refs, 4, 3, 4)
        _side_start(side, side_refs, (ngrp,))
        dk_acc[...] = jnp.zeros_like(dk_acc)
        dv_acc[...] = jnp.zeros_like(dv_acc)
        after_ones = _sb_sum_matrix(True)
        before_ones = _sb_sum_matrix(False)

        def add_cols(acc_ref, rows, parts):
            for p in range(SB_PAIRS):
                acc_ref[rows, p * TILE:(p + 1) * TILE] += parts[p]

        def q_step(qi, carry):
            qbase = pl.multiple_of(qi * TILE, TILE)
            q2 = _sb_stack(q_ref[pl.ds(qbase, TILE), :])
            do2 = _sb_stack(do_ref[pl.ds(qbase, TILE), :])

            def block_rl(first, w, c_rep):
                ls2, a, c_rep = _sb_block(q2, k_ref, first, w, _sb_edge_masks(qi, first, w), c_rep, after_ones)
                rows = pl.ds(pl.multiple_of(first * TILE, TILE), w * TILE)
                e = jnp.concatenate(_sb_pair_dot(do2, v_ref[rows, :], NT, True), axis=0) * a
                beta = jnp.exp2(ls2)
                for j in range(w):
                    e_buf[first + j] = e[:, j * TILE:(j + 1) * TILE]
                    b_buf[first + j] = beta[:, j * TILE:(j + 1) * TILE]
                add_cols(dv_acc, rows, _sb_pair_dot(a.astype(bf16), do2, TN, False))
                return c_rep

            zero = jnp.zeros((SB_ROWS, TILE), f32)
            _, counts = _sb_walk_down(qi, block_rl, zero, lambda c_rep: c_rep)

            def block_lr(first, w, st):
                e_rep, dq_acc = st
                rows = pl.ds(pl.multiple_of(first * TILE, TILE), w * TILE)
                valids = _sb_edge_masks(qi, first, w)
                es = [e_buf[first + j] for j in range(w)]
                sums = _sb_sums(es[0] if w == 1 else jnp.concatenate(es, axis=1), w, before_ones)
                parts = []
                for j in range(w):
                    beta = b_buf[first + j]
                    parts.append(_sb_apply(valids[j], (es[j] - beta * (es[j] + e_rep + sums[j][:, :TILE])) * scale))
                    e_rep = e_rep + sums[j][:, TILE:]
                dz = (parts[0] if w == 1 else jnp.concatenate(parts, axis=1)).astype(bf16)
                add_cols(dk_acc, rows, _sb_pair_dot(dz, q2, TN, False))
                return e_rep, dq_acc + jnp.concatenate(_sb_pair_dot(dz, k_ref[rows, :], NN, True), axis=0)

            _, dq = _sb_walk_up(qi, block_lr, (zero, zero), counts)
            dq_ref[pl.ds(qbase, TILE), :] = _sb_unstack(dq).astype(bf16)
            return carry

        lax.fori_loop(0, nt, q_step, 0)
        dk_ref[...] = dk_acc[...].astype(bf16)
        dv_ref[...] = dv_acc[...].astype(bf16)
        _side_finish(side, side_refs, (ngrp,))

    wide = SB_PAIRS * TILE
    s_in, s_in_specs, s_out, s_out_specs, s_scratch = _side_io(side)
    out = pl.pallas_call(
        body, out_shape=(S((lp, sbw), bf16),) * 3 + tuple(s_out), grid=(ngrp,),
        in_specs=[_sb_cols(lp, 0, True), _sb_cols(lp, ngrp, True), _sb_cols(lp, 2 * ngrp, True), _sb_cols(lp, 0, True), *s_in_specs],
        out_specs=(_sb_cols(lp, 0, True),) * 3 + tuple(s_out_specs),
        scratch_shapes=[pltpu.VMEM((lp, wide), f32), pltpu.VMEM((lp, wide), f32),
                        pltpu.VMEM((nt, SB_ROWS, TILE), f32), pltpu.VMEM((nt, SB_ROWS, TILE), f32), *s_scratch],
        name="sb_bwd" if side is None else "sb_bwd_comm",
        compiler_params=_cp("parallel" if side is None else "arbitrary"))(proj, proj, proj, dout, *s_in)
    return out[0], out[1], out[2], list(out[3:])


def _place():
    x, y, c = lax.axis_index("x"), lax.axis_index("y"), lax.axis_index("c")
    chips = [(1 - x, y), (x, 1 - y), (1 - x, 1 - y)]
    return x, y, c, chips


def _two_level_gather_body(rows, x_ref, send_sems, recv_sems, local_sem):
    x, y, c, chips = _place()
    me, sibling = (x, y, c), (x, y, 1 - c)

    def copy(k, block, to, src=None):
        return pltpu.make_async_remote_copy(
            src_ref=rows(*block) if src is None else src, dst_ref=rows(*block),
            send_sem=send_sems.at[k], recv_sem=recv_sems.at[k], device_id=to, device_id_type=MESH)

    mine = pltpu.make_async_copy(x_ref, rows(*me), local_sem)
    mine.start()
    first = [copy(0, me, sibling, src=x_ref)]
    first += [copy(1 + j, me, (*chip, c), src=x_ref) for j, chip in enumerate(chips)]
    for cp in first:
        cp.start()
    passed = [copy(4 + j, (*chip, c), sibling) for j, chip in enumerate(chips)]
    for j, chip in enumerate(chips):
        copy(1 + j, (*chip, c), me).wait_recv()
        passed[j].start()
    copy(0, sibling, me).wait_recv()
    for j, chip in enumerate(chips):
        copy(4 + j, (*chip, 1 - c), me).wait_recv()
    for cp in first + passed:
        cp.wait_send()
    mine.wait()


_GATHER_SEMS = [pltpu.SemaphoreType.DMA((7,)), pltpu.SemaphoreType.DMA((7,)), pltpu.SemaphoreType.DMA]


class _Side:
    def __init__(self, inputs, out_shapes, nsem, start, finish):
        self.inputs, self.out_shapes, self.start, self.finish = list(inputs), list(out_shapes), start, finish
        self.scratch = [pltpu.SemaphoreType.DMA((nsem,)), pltpu.SemaphoreType.DMA((nsem,)), pltpu.SemaphoreType.DMA((len(inputs),))]


def _side_io(side):
    if side is None:
        return [], [], [], [], []
    return side.inputs, [ANY] * len(side.inputs), side.out_shapes, [ANY] * len(side.out_shapes), side.scratch


def _side_split(side, refs, n_in, n_out, n_scratch):
    if side is None:
        return refs, None
    si, so = len(side.inputs), len(side.out_shapes)
    a, b, c, d = n_in, n_in + si, n_in + si + n_out, n_in + si + n_out + so
    return refs[:a] + refs[b:c] + refs[d:d + n_scratch], (refs[a:b], refs[c:d], refs[d + n_scratch:])


def _side_first_last(grid):
    first = last = None
    for k, n in enumerate(grid):
        i = pl.program_id(k)
        first = (i == 0) if first is None else first & (i == 0)
        last = (i == n - 1) if last is None else last & (i == n - 1)
    return first, last


def _side_start(side, side_refs, grid):
    if side is not None:
        pl.when(_side_first_last(grid)[0])(lambda: side.start(*side_refs))


def _side_finish(side, side_refs, grid):
    if side is not None:
        pl.when(_side_first_last(grid)[1])(lambda: side.finish(*side_refs))


def _run_side(side, name):
    def body(*refs):
        _, side_refs = _side_split(side, refs, 0, 0, 0)
        side.start(*side_refs)
        side.finish(*side_refs)

    inputs, in_specs, out_shapes, out_specs, scratch = _side_io(side)
    return pl.pallas_call(body, out_shape=tuple(out_shapes), in_specs=in_specs, out_specs=tuple(out_specs),
                          scratch_shapes=scratch, name=name)(*inputs)


def _gather_side(shards, layers):
    def plan(in_refs, out_refs, sems, starting):
        send_sems, recv_sems, local_sems = sems
        x, y, c, chips = _place()
        me, sibling = (x, y, c), (x, y, 1 - c)
        jobs = []
        for a, (x_ref, o_ref) in enumerate(zip(in_refs, out_refs)):
            r = x_ref.shape[1]
            src = x_ref.at[layers[a]]

            def rows(px, py, pc, o_ref=o_ref, r=r):
                return o_ref.at[0, pl.ds(pl.multiple_of((4 * px + 2 * py + pc) * r, 16), r), :]

            def copy(k, block, to, from_shard=False, a=a, rows=rows, src=src):
                return pltpu.make_async_remote_copy(
                    src_ref=src if from_shard else rows(*block), dst_ref=rows(*block),
                    send_sem=send_sems.at[7 * a + k], recv_sem=recv_sems.at[7 * a + k], device_id=to, device_id_type=MESH)

            job = dict(
                mine=pltpu.make_async_copy(src, rows(*me), local_sems.at[a]),
                first=[copy(0, me, sibling, True)] + [copy(1 + j, me, (*chip, c), True) for j, chip in enumerate(chips)])
            if not starting:
                job.update(
                    passed=[copy(4 + j, (*chip, c), sibling) for j, chip in enumerate(chips)],
                    from_chips=[copy(1 + j, (*chip, c), me) for j, chip in enumerate(chips)],
                    from_sibling=[copy(0, sibling, me)] + [copy(4 + j, (*chip, 1 - c), me) for j, chip in enumerate(chips)])
            jobs.append(job)
        return jobs

    def start(*refs):
        for job in plan(*refs, starting=True):
            job["mine"].start()
            for cp in job["first"]:
                cp.start()

    def finish(*refs):
        jobs = plan(*refs, starting=False)
        for job in jobs:
            for arrived, onward in zip(job["from_chips"], job["passed"]):
                arrived.wait_recv()
                onward.start()
        for job in jobs:
            for cp in job["from_sibling"]:
                cp.wait_recv()
            for cp in job["first"] + job["passed"]:
                cp.wait_send()
            job["mine"].wait()

    shapes = [S((1, NDEV * s.shape[1], s.shape[2]), s.dtype) for s in shards]
    return _Side(shards, shapes, 7 * len(shards), start, finish)


def _swap_side(gs):
    def plan(in_refs, out_refs, sems):
        send_sems, recv_sems, _ = sems
        x, y, c, chips = _place()
        copies = []
        for a, (g_ref, o_ref) in enumerate(zip(in_refs, out_refs)):
            r = g_ref.shape[1] // NDEV
            for j, (px, py) in enumerate([(x, y)] + chips):
                d = 4 * px + 2 * py + (1 - c)
                copies.append(pltpu.make_async_remote_copy(
                    src_ref=g_ref.at[:, pl.ds(pl.multiple_of(d * r, 8), r), :], dst_ref=o_ref.at[j],
                    send_sem=send_sems.at[4 * a + j], recv_sem=recv_sems.at[4 * a + j], device_id=(x, y, 1 - c), device_id_type=MESH))
        return copies

    return _exchange_side(gs, [S((4, g.shape[0], g.shape[1] // NDEV, g.shape[2]), f32) for g in gs], 4 * len(gs), plan)


def _ici_side(sends):
    def plan(in_refs, out_refs, sems):
        send_sems, recv_sems, _ = sems
        x, y, c, chips = _place()
        return [pltpu.make_async_remote_copy(
            src_ref=s_ref.at[j], dst_ref=o_ref.at[j], send_sem=send_sems.at[3 * a + j], recv_sem=recv_sems.at[3 * a + j],
            device_id=(*chip, c), device_id_type=MESH)
            for a, (s_ref, o_ref) in enumerate(zip(in_refs, out_refs)) for j, chip in enumerate(chips)]

    return _exchange_side(sends, [S(s.shape, s.dtype) for s in sends], 3 * len(sends), plan)


def _exchange_side(inputs, shapes, nsem, plan):
    def start(*refs):
        for cp in plan(*refs):
            cp.start()

    def finish(*refs):
        copies = plan(*refs)
        for cp in copies:
            cp.wait_recv()
        for cp in copies:
            cp.wait_send()

    return _Side(inputs, shapes, nsem, start, finish)


def _ag_small(shard, name):
    r, w = shard.shape

    def body(x_ref, o_ref, send_sems, recv_sems, local_sem):
        def rows(px, py, pc):
            return o_ref.at[pl.ds(pl.multiple_of((4 * px + 2 * py + pc) * r, 8), r), :]
        _two_level_gather_body(rows, x_ref, send_sems, recv_sems, local_sem)

    vmem = BS(memory_space=pltpu.VMEM)
    return pl.pallas_call(
        body, out_shape=S((NDEV * r, w), shard.dtype), in_specs=[vmem], out_specs=vmem,
        scratch_shapes=_GATHER_SEMS, name=name)(shard)


def _rel_ids():
    x, y, c = lax.axis_index("x"), lax.axis_index("y"), lax.axis_index("c")
    rel = [(x, y), (1 - x, y), (x, 1 - y), (1 - x, 1 - y)]
    return jnp.stack([4 * px + 2 * py + c for px, py in rel]).astype(jnp.int32)


def _rs_partial(g, sib, ids, name):
    ns, rows8, cdim = g.shape
    r = rows8 // NDEV
    tr = r // 2
    nb = r // tr

    def own_body(ids_ref, g_ref, s_ref, o_ref):
        o_ref[...] = g_ref[...] + s_ref[...]

    own = pl.pallas_call(
        own_body, out_shape=S((ns, r, cdim), f32),
        grid_spec=pltpu.PrefetchScalarGridSpec(
            num_scalar_prefetch=1, grid=(ns, nb),
            in_specs=[BS((None, tr, cdim), lambda s, t, ids_ref: (s, ids_ref[0] * nb + t, 0)),
                      BS((None, None, tr, cdim), lambda s, t, ids_ref: (0, s, t, 0))],
            out_specs=BS((None, tr, cdim), lambda s, t, ids_ref: (s, t, 0))),
        name=name + "_own", compiler_params=_cp("parallel", "parallel"))(ids, g, sib)

    def send_body(ids_ref, g_ref, s_ref, o_ref):
        o_ref[...] = (g_ref[...] + s_ref[...]).astype(bf16)

    send = pl.pallas_call(
        send_body, out_shape=S((3, ns, r, cdim), bf16),
        grid_spec=pltpu.PrefetchScalarGridSpec(
            num_scalar_prefetch=1, grid=(3, ns, nb),
            in_specs=[BS((None, tr, cdim), lambda j, s, t, ids_ref: (s, ids_ref[j + 1] * nb + t, 0)),
                      BS((None, None, tr, cdim), lambda j, s, t, ids_ref: (j + 1, s, t, 0))],
            out_specs=BS((None, None, tr, cdim), lambda j, s, t, ids_ref: (j, s, t, 0))),
        name=name + "_send", compiler_params=_cp("parallel", "parallel", "parallel"))(ids, g, sib)
    return own, send


def _prep(w, name):
    nl, r, cdim = w.shape
    blk = BS((None, r, cdim), lambda l: (l, 0, 0))

    def body(w_ref, o_ref):
        o_ref[...] = w_ref[...].astype(bf16)

    return pl.pallas_call(body, out_shape=S(w.shape, bf16), grid=(nl,), in_specs=[blk], out_specs=blk,
                          name=name, compiler_params=_cp("parallel"))(w)


def _adam_math(w, g, m, v):
    m2 = ADAM_B1 * m + (1.0 - ADAM_B1) * g
    v2 = ADAM_B2 * v + (1.0 - ADAM_B2) * (g * g)
    m_hat = m2 / (1.0 - ADAM_B1 ** ADAM_STEP)
    v_hat = v2 / (1.0 - ADAM_B2 ** ADAM_STEP)
    delta = -ADAM_LR * (m_hat / (jnp.sqrt(v_hat) + ADAM_EPS) + ADAM_WD * w)
    return delta, m2, v2


def _adam_big(w, m, v, own, arrived, layer, slot, name, prev=None):
    nl, a, b = w.shape
    ta = _pick(a, (a // 4, a // 2, a)) if (a // 4) % 16 == 0 else a
    w_blk = BS((None, ta, b), lambda t: (layer, t, 0))
    own_blk = BS((None, ta, b), lambda t: (slot, t, 0))
    arr_blk = [BS((None, None, ta, b), lambda t, j=j: (j, slot, t, 0)) for j in range(3)]

    def body(*refs):
        w_ref, m_ref, v_ref, own_ref, a0_ref, a1_ref, a2_ref = refs[:7]
        g_ref, d_ref, m2_ref, v2_ref = refs[-4:]
        g = ((own_ref[...] + a0_ref[...].astype(f32)) + a1_ref[...].astype(f32)) + a2_ref[...].astype(f32)
        delta, m2, v2 = _adam_math(w_ref[...], g, m_ref[...], v_ref[...])
        g_ref[...] = g
        d_ref[...] = delta
        m2_ref[...] = m2
        v2_ref[...] = v2

    threaded = () if prev is None else tuple(prev)
    return pl.pallas_call(
        body, out_shape=(S(w.shape, f32),) * 4, grid=(a // ta,),
        in_specs=[w_blk, w_blk, w_blk, own_blk] + arr_blk + [ANY] * len(threaded), out_specs=(w_blk,) * 4,
        input_output_aliases={7 + k: k for k in range(len(threaded))},
        name=name, compiler_params=_cp("parallel"))(w, m, v, own, arrived, arrived, arrived, *threaded)


def _adam_small(gathered, w, m, v):
    r, wd = w.shape

    def body(g_ref, w_ref, m_ref, v_ref, gs_ref, d_ref, m2_ref, v2_ref):
        g = g_ref[0:r, :]
        for dev in range(1, NDEV):
            g = g + g_ref[dev * r:(dev + 1) * r, :]
        delta, m2, v2 = _adam_math(w_ref[...], g, m_ref[...], v_ref[...])
        gs_ref[...] = g
        d_ref[...] = delta
        m2_ref[...] = m2
        v2_ref[...] = v2

    return pl.pallas_call(body, out_shape=(S((r, wd), f32),) * 4, name="adam_small", compiler_params=_cp())(gathered, w, m, v)


def _pack(arrays, width):
    flat = jnp.concatenate([a.reshape(-1) for a in arrays])
    rows = -(-flat.shape[0] // (8 * width)) * 8
    return jnp.pad(flat, (0, rows * width - flat.shape[0])).reshape(rows, width)


def _unpack(packed, shapes):
    flat = packed.reshape(-1)
    out, off = [], 0
    for shp in shapes:
        n = 1
        for s in shp:
            n *= s
        out.append(flat[off:off + n].reshape(shp))
        off += n
    return out


def kernel(x, meta_tokens, attn_norm_g, w_in, q_norm_g, k_norm_g, attn_sinks, swa_out_g, sb_out_g, w_o, ffn_norm_g, w_gate, w_up, w_down, loss_target, m_meta_tokens, m_attn_norm_g, m_w_in, m_q_norm_g, m_k_norm_g, m_attn_sinks, m_swa_out_g, m_sb_out_g, m_w_o, m_ffn_norm_g, m_w_gate, m_w_up, m_w_down, v_meta_tokens, v_attn_norm_g, v_w_in, v_q_norm_g, v_k_norm_g, v_attn_sinks, v_swa_out_g, v_sb_out_g, v_w_o, v_ffn_norm_g, v_w_gate, v_w_up, v_w_down):
    nl, d = attn_norm_g.shape
    x2, target = x[0], loss_target[0]
    me = 4 * lax.axis_index("x") + 2 * lax.axis_index("y") + lax.axis_index("c")
    ids = _rel_ids()

    meta_all = _ag_small(meta_tokens, "ag_meta")
    meta_full = meta_all.reshape(NDEV, N_META, -1).transpose(1, 0, 2).reshape(N_META, d)
    def t(a):
        return jnp.swapaxes(a, 1, 2)

    w_in_v, w_gate_v, w_up_v = t(w_in), t(w_gate), t(w_up)
    shards = [_prep(w_in_v, "prep_w_in"), _prep(w_o, "prep_w_o"), _prep(w_gate_v, "prep_w_gate"), _prep(w_up_v, "prep_w_up"),
              _prep(w_down, "prep_w_down")]
    attn_w = [_run_side(_gather_side(shards[:2], [0, 0]), "ag_attn_layer0")] + [None] * (nl - 1)
    ffn_w = [None] * nl

    qg2 = jnp.tile(q_norm_g, (1, TILE // HEAD_DIM))
    kg2 = jnp.tile(k_norm_g, (1, TILE // HEAD_DIM))
    out_g = jnp.concatenate([swa_out_g, sb_out_g], axis=1)
    swa_cols = (SWA_HEADS + 2 * SWA_KV_HEADS) * HEAD_DIM

    h = _embed(x2, meta_full)
    saved, weights = [], []
    for l in range(nl):
        w_in_t, w_o_f = attn_w[l]
        hn = _rmsnorm_fwd([h], attn_norm_g[l:l + 1], "attn_norm")
        nxt = l + 1 < nl
        sb_cols = (swa_cols, 3 * SB_HEADS * HEAD_DIM)
        proj = _mm(hn, w_in_t, 0, "nt", "proj_swa", cols=(0, swa_cols))
        if l == 0:
            proj_sb, (gate0,) = _mm(hn, w_in_t, 0, "nt", "proj_sb", out_dtype=bf16, cols=sb_cols, side=_gather_side(shards[2:3], [0]))
            out_a, (up0,) = _swa_fwd(proj, qg2[l:l + 1], kg2[l:l + 1], attn_sinks[l], _gather_side(shards[3:4], [0]))
            riding = [(shards[4], 0)] + ([(shards[0], 1), (shards[1], 1)] if nxt else [])
            out_b, gathered = _sb_fwd(proj_sb, _gather_side([s for s, _ in riding], [k for _, k in riding]))
            ffn_w[0] = [gate0, up0, gathered[0]]
            if nxt:
                attn_w[1] = gathered[1:]
        else:
            proj_sb = _mm(hn, w_in_t, 0, "nt", "proj_sb", out_dtype=bf16, cols=sb_cols)
            out_a, up_next = _swa_fwd(proj, qg2[l:l + 1], kg2[l:l + 1], attn_sinks[l], _gather_side(shards[3:4], [l + 1]) if nxt else None)
            out_b, attn_next = _sb_fwd(proj_sb, _gather_side(shards[:2], [l + 1, l + 1]) if nxt else None)
            if nxt:
                attn_w[l + 1] = attn_next
        w_gate_t, w_up_t, w_down_f = ffn_w[l]
        weights.append((w_in_t, w_o_f, w_gate_t, w_up_t, w_down_f))
        mixed = _rmsnorm_fwd([out_a, out_b], out_g[l:l + 1], "out_norm")
        h_mid = _mm(mixed, w_o_f, 0, "nn", "attn_out", res=h)
        hn2 = _rmsnorm_fwd([h_mid], ffn_norm_g[l:l + 1], "ffn_norm")
        if nxt:
            ahead = shards[2:4] if l == 0 else shards[2:3]
            g, u, act, gate_up = _ffn_up(hn2, w_gate_t, w_up_t, _gather_side(ahead, [l + 1] * len(ahead)))
            h_out, down = _mm(act, w_down_f, 0, "nn", "ffn_down", res=h_mid, side=_gather_side(shards[4:], [l + 1]))
            ffn_w[l + 1] = (gate_up if l == 0 else gate_up + up_next) + down
        else:
            g, u, act, _ = _ffn_up(hn2, w_gate_t, w_up_t)
            h_out = _mm(act, w_down_f, 0, "nn", "ffn_down", res=h_mid)
        saved.append((h, hn, proj, proj_sb, out_a, out_b, mixed, h_mid, hn2, g, u, act))
        h = h_out

    loss_tile, dh, dh_b = _loss_head(h, target)
    loss = lax.psum(loss_tile[0, 0], AXES)

    small = dict.fromkeys(("attn", "ffn", "out", "swa"))
    pending = None
    red_attn = [None] * nl
    red_ff = [None] * nl

    def attn_partial_sums(slabs, sibs):
        return [_rs_partial(gg, sib, ids, name) for gg, sib, name in zip(slabs, sibs, ("rs_part_w_in", "rs_part_w_o"))]

    for l in reversed(range(nl)):
        w_in_t, w_o_f, w_gate_t, w_up_t, w_down_f = weights[l]
        h_in, hn, proj, proj_sb, out_a, out_b, mixed, h_mid, hn2, g, u, act = saved[l]
        dg, du, sibs = _ffn_down_bwd(dh_b, w_down_f, g, u, None if pending is None else _swap_side(pending))
        parts = [] if pending is None else attn_partial_sums(pending, sibs)
        g_ff = _mm_tn(act, dh_b, None, 2, 3, "grad_w_down")
        g_ff = _mm_tn(dg, hn2, g_ff, 0, 3, "grad_w_gate")
        g_ff = _mm_tn(du, hn2, g_ff, 1, 3, "grad_w_up")
        d_gate, (sib_ff,) = _mm(dg, w_gate_t, 0, "nn", "d_ffn_in_gate", side=_swap_side([g_ff]))
        part_ff = _rs_partial(g_ff, sib_ff, ids, "rs_part_w_ff")
        dhn2 = _mm(du, w_up_t, 0, "nn", "d_ffn_in_up", res=d_gate)
        (dh_mid,), small["ffn"], dh_mid_b = _rmsnorm_bwd(dhn2, [h_mid], ffn_norm_g[l:l + 1], dh, "ffn_norm_bwd",
                                                         layer=l, nl=nl, slab=small["ffn"])
        dmixed = _mm(dh_mid_b, w_o_f, 0, "nt", "d_mixed")
        g_o = _mm_tn(mixed, dh_mid_b, None, 0, 1, "grad_w_o")
        (dout_a, dout_b), small["out"] = _rmsnorm_bwd(dmixed, [out_a, out_b], out_g[l:l + 1], None, "out_norm_bwd", bf16,
                                                      layer=l, nl=nl, slab=small["out"])
        dq_b, dk_b, dv_b, (arrived_ff,) = _sb_bwd(proj_sb, dout_b, _ici_side([part_ff[1]]))
        red_ff[l] = (part_ff[0], arrived_ff)
        dq_a, dk_a, dv_a, small["swa"], arrived = _swa_bwd(proj, dout_a, qg2[l:l + 1], kg2[l:l + 1], attn_sinks[l],
                                                           _ici_side([p[1] for p in parts]) if parts else None,
                                                           layer=l, nl=nl, slabs=small["swa"])
        if parts:
            red_attn[l + 1] = ([p[0] for p in parts], arrived)
        dproj = jnp.concatenate([dq_a, dk_a, dv_a, dq_b, dk_b, dv_b], axis=1)
        dhn = _mm(dproj, w_in_t, 0, "nn", "d_attn_in")
        g_in = _mm_tn(dproj, hn, None, 0, 1, "grad_w_in")
        (dh,), small["attn"], dh_b = _rmsnorm_bwd(dhn, [h_in], attn_norm_g[l:l + 1], dh_mid, "attn_norm_bwd",
                                                  layer=l, nl=nl, slab=small["attn"])
        pending = [g_in, g_o]

    parts = attn_partial_sums(pending, _run_side(_swap_side(pending), "rs_swap_attn_layer0"))
    red_attn[0] = ([p[0] for p in parts], list(_run_side(_ici_side([p[1] for p in parts]), "rs_ici_attn_layer0")))

    grad_x = dh[TILE:][None]
    d_meta = dh[PAD:TILE]

    qw = SWA_HEADS * HEAD_DIM
    d_out = small["out"][:, 0]
    dqg, dkg, dsk = (a[:, 0] for a in small["swa"])
    small_grads = [small["attn"][:, 0], dqg[:, :HEAD_DIM] + dqg[:, HEAD_DIM:], dkg[:, :HEAD_DIM] + dkg[:, HEAD_DIM:],
                   dsk[:, :SWA_HEADS], d_out[:, :qw], d_out[:, qw:], small["ffn"][:, 0], d_meta]
    col0 = me * meta_tokens.shape[1]

    def widen(a):
        return lax.dynamic_update_slice(jnp.zeros((N_META, d), f32), a, (0, col0))

    small_w = [attn_norm_g, q_norm_g, k_norm_g, attn_sinks, swa_out_g, sb_out_g, ffn_norm_g]
    small_m = [m_attn_norm_g, m_q_norm_g, m_k_norm_g, m_attn_sinks, m_swa_out_g, m_sb_out_g, m_ffn_norm_g]
    small_v = [v_attn_norm_g, v_q_norm_g, v_k_norm_g, v_attn_sinks, v_swa_out_g, v_sb_out_g, v_ffn_norm_g]
    shapes = [a.shape for a in small_w] + [(N_META, d)]
    packed = _adam_small(_ag_small(_pack(small_grads, d), "ag_small_grads"),
                         _pack(small_w + [widen(meta_tokens)], d), _pack(small_m + [widen(m_meta_tokens)], d),
                         _pack(small_v + [widen(v_meta_tokens)], d))
    small_out = []
    for p in packed:
        parts = _unpack(p, shapes)
        parts[-1] = lax.dynamic_slice(parts[-1], (0, col0), meta_tokens.shape)
        small_out.append(parts)

    big = dict.fromkeys(("w_in", "w_o", "w_gate", "w_up", "w_down"))
    for l in range(nl):
        (own_in, own_o), (arr_in, arr_o) = red_attn[l]
        own_ff, arr_ff = red_ff[l]
        big["w_in"] = _adam_big(w_in_v, t(m_w_in), t(v_w_in), own_in, arr_in, l, 0, "adam_w_in", big["w_in"])
        big["w_o"] = _adam_big(w_o, m_w_o, v_w_o, own_o, arr_o, l, 0, "adam_w_o", big["w_o"])
        big["w_gate"] = _adam_big(w_gate_v, t(m_w_gate), t(v_w_gate), own_ff, arr_ff, l, 0, "adam_w_gate", big["w_gate"])
        big["w_up"] = _adam_big(w_up_v, t(m_w_up), t(v_w_up), own_ff, arr_ff, l, 1, "adam_w_up", big["w_up"])
        big["w_down"] = _adam_big(w_down, m_w_down, v_w_down, own_ff, arr_ff, l, 2, "adam_w_down", big["w_down"])
    for name in ("w_in", "w_gate", "w_up"):
        big[name] = [t(a) for a in big[name]]

    def group(k):
        sm = small_out[k]
        return [sm[7], sm[0], big["w_in"][k], sm[1], sm[2], sm[3], sm[4], sm[5], big["w_o"][k], sm[6],
                big["w_gate"][k], big["w_up"][k], big["w_down"][k]]

    return (loss, grad_x, *group(0), *group(1), *group(2), *group(3))
```

```python
import jax
import jax.numpy as jnp
from jax import lax
from jax.experimental import pallas as pl
from jax.experimental.pallas import tpu as pltpu

f32, bf16 = jnp.float32, jnp.bfloat16
S = jax.ShapeDtypeStruct
BS = pl.BlockSpec

N_META = 16
TILE = 128
PAD = TILE - N_META
HEAD_DIM = 64
SWA_HEADS = 16
SWA_KV_HEADS = 4
SB_HEADS = 16
EPS = 1e-6
NEG = -1e30
ADAM_LR, ADAM_B1, ADAM_B2, ADAM_EPS, ADAM_WD, ADAM_STEP = 0.001, 0.9, 0.999, 1e-08, 0.01, 10
AXES = ("x", "y", "c")
NDEV = 8
VMEM_LIMIT_V7X = 56 * 1024 * 1024
MM_RESIDENT_K = 2048
TN_VMEM_BUDGET = 46 * 1024 * 1024
MESH = pl.DeviceIdType.MESH
ANY = pl.BlockSpec(memory_space=pl.ANY)

NN = ((1,), (0,))
NT = ((1,), (1,))
TN = ((0,), (0,))


def _cp(*sem):
    return pltpu.CompilerParams(dimension_semantics=sem if sem else None, vmem_limit_bytes=VMEM_LIMIT_V7X)


def _pick(n, cands):
    for c in cands:
        if n % c == 0:
            return c
    raise ValueError(f"no tile for {n} in {cands}")


def _dot(a, b, dims):
    return lax.dot_general(a, b, (dims, ((), ())), preferred_element_type=f32)


def _embed(x, meta_full):
    seq, d = x.shape
    meta_pad = jnp.pad(meta_full, ((PAD, 0), (0, 0)))

    def body(x_ref, m_ref, o_ref):
        i = pl.program_id(0)

        @pl.when(i == 0)
        def _():
            o_ref[...] = m_ref[...]

        @pl.when(i > 0)
        def _():
            o_ref[...] = x_ref[...]

    return pl.pallas_call(
        body, out_shape=S((TILE + seq, d), f32), grid=(1 + seq // TILE,),
        in_specs=[BS((TILE, d), lambda i: (jnp.maximum(i - 1, 0), 0)), BS((TILE, d), lambda i: (0, 0))],
        out_specs=BS((TILE, d), lambda i: (i, 0)), name="embed", compiler_params=_cp("arbitrary"))(x, meta_pad)


def _rmsnorm_fwd(xs, gain, name):
    lp = xs[0].shape[0]
    widths = [x.shape[1] for x in xs]
    w = sum(widths)
    tr = _pick(lp, (384, 256, 128))
    n = len(xs)

    def body(*refs):
        g_ref, o_ref = refs[n], refs[n + 1]
        off = 0
        for k in range(n):
            sl = slice(off, off + widths[k])
            off += widths[k]
            xv = refs[k][...]
            r = lax.rsqrt(jnp.mean(xv * xv, axis=-1, keepdims=True) + EPS)
            o_ref[:, sl] = ((xv * r) * g_ref[:, sl]).astype(bf16)

    return pl.pallas_call(
        body, out_shape=S((lp, w), bf16), grid=(lp // tr,),
        in_specs=[BS((tr, wk), lambda i: (i, 0)) for wk in widths] + [BS((1, w), lambda i: (0, 0))],
        out_specs=BS((tr, w), lambda i: (i, 0)), name=name, compiler_params=_cp("parallel"))(*xs, gain)


def _rmsnorm_bwd(dy, xs, gain, res, name, out_dtype=f32):
    lp = xs[0].shape[0]
    widths = [x.shape[1] for x in xs]
    w = sum(widths)
    tr = _pick(lp, (384, 256, 128))
    n = len(xs)
    has_res = res is not None
    assert not has_res or n == 1

    def body(*refs):
        dy_ref, g_ref = refs[0], refs[1 + n]
        res_ref = refs[2 + n] if has_res else None
        dx_refs = refs[2 + n + has_res:2 + 2 * n + has_res]
        dg_ref = refs[2 + 2 * n + has_res]

        @pl.when(pl.program_id(0) == 0)
        def _():
            dg_ref[...] = jnp.zeros_like(dg_ref)

        off = 0
        for k in range(n):
            sl = slice(off, off + widths[k])
            off += widths[k]
            xv = refs[1 + k][...]
            dyv = dy_ref[:, sl]
            r = lax.rsqrt(jnp.mean(xv * xv, axis=-1, keepdims=True) + EPS)
            xh = xv * r
            dg_ref[:, sl] += jnp.sum(dyv * xh, axis=0, keepdims=True)
            dxh = dyv * g_ref[:, sl]
            dx = r * (dxh - xh * jnp.mean(dxh * xh, axis=-1, keepdims=True))
            if has_res:
                dx = dx + res_ref[...]
                refs[-1][...] = dx.astype(bf16)
            dx_refs[k][...] = dx.astype(out_dtype)

    rows = [BS((tr, wk), lambda i: (i, 0)) for wk in widths]
    full = BS((tr, w), lambda i: (i, 0))
    vec = BS((1, w), lambda i: (0, 0))
    args = (dy, *xs, gain) + ((res,) if has_res else ())
    out = pl.pallas_call(
        body, out_shape=tuple(S((lp, wk), out_dtype) for wk in widths) + (S((1, w), f32),) + ((S((lp, w), bf16),) if has_res else ()),
        grid=(lp // tr,),
        in_specs=[full] + rows + [vec] + ([full] if has_res else []), out_specs=tuple(rows) + (vec,) + ((full,) if has_res else ()),
        name=name, compiler_params=_cp("arbitrary"))(*args)
    return (list(out[:n]), out[n], out[n + 1]) if has_res else (list(out[:n]), out[n])


def _loss_head(h, target):
    lp, d = h.shape

    def body(h_ref, t_ref, loss_ref, dh_ref, dhb_ref):
        i = pl.program_id(0)

        @pl.when(i == 0)
        def _():
            loss_ref[...] = jnp.zeros_like(loss_ref)
            dh_ref[...] = jnp.zeros_like(dh_ref)
            dhb_ref[...] = jnp.zeros_like(dhb_ref)

        @pl.when(i > 0)
        def _():
            e = h_ref[...] - t_ref[...]
            dh = e * (1.0 / d)
            dh_ref[...] = dh
            dhb_ref[...] = dh.astype(bf16)
            loss_ref[...] += 0.5 * jnp.sum(jnp.sum(e * e, axis=-1, keepdims=True) * (1.0 / d))

    tile = BS((TILE, d), lambda i: (i, 0))
    return pl.pallas_call(
        body, out_shape=(S((8, 128), f32), S((lp, d), f32), S((lp, d), bf16)), grid=(lp // TILE,),
        in_specs=[tile, BS((TILE, d), lambda i: (jnp.maximum(i - 1, 0), 0))],
        out_specs=(BS((8, 128), lambda i: (0, 0)), tile, tile),
        name="loss_head", compiler_params=_cp("arbitrary"))(h, target)


def _mm(a, b, layer, mode, name, out_dtype=f32, res=None, cols=None, side=None):
    m, k = a.shape
    n = b.shape[2] if mode == "nn" else b.shape[1]
    c0 = 0
    if cols is not None:
        assert mode == "nt"
        c0, n = cols
    tn = _pick(n, (512, 384, 256, 128))
    while c0 % tn:
        tn -= TILE
    assert n % tn == 0
    joff = c0 // tn
    single_a = k > MM_RESIDENT_K and a.dtype == bf16 and m % 1408 == 0
    rows_outer = k <= MM_RESIDENT_K or single_a
    tm = _pick(m, (1408, 384, 256, 128) if rows_outer else (384, 256, 128))
    dims = NN if mode == "nn" else NT
    has_res = res is not None

    grid = (m // tm, n // tn) if rows_outer else (n // tn, m // tm)

    def body(*refs):
        refs, side_refs = _side_split(side, refs, 2 + has_res, 1, 0)
        _side_start(side, side_refs, grid)
        if has_res:
            a_ref, b_ref, r_ref, o_ref = refs
        else:
            a_ref, b_ref, o_ref = refs
        acc = _dot(a_ref[...].astype(bf16), b_ref[...], dims)
        if has_res:
            acc = acc + r_ref[...]
        o_ref[...] = acc.astype(out_dtype)
        _side_finish(side, side_refs, grid)

    def ij(f):
        return (lambda i, j: f(i, j)) if rows_outer else (lambda j, i: f(i, j))

    if mode == "nn":
        b_spec = BS((None, k, tn), ij(lambda i, j: (layer, 0, j)))
    else:
        b_spec = BS((None, tn, k), ij(lambda i, j: (layer, joff + j, 0)))
    tile = BS((tm, tn), ij(lambda i, j: (i, j)))
    args = (a, b) + ((res,) if has_res else ())
    s_in, s_in_specs, s_out, s_out_specs, s_scratch = _side_io(side)
    sem = ("parallel", "parallel") if side is None else ("arbitrary", "arbitrary")
    out = pl.pallas_call(
        body, out_shape=(S((m, n), out_dtype), *s_out), grid=grid,
        in_specs=[BS((tm, k), ij(lambda i, j: (i, 0)), **(dict(pipeline_mode=pl.Buffered(1)) if single_a else {})), b_spec]
        + ([tile] if has_res else []) + s_in_specs,
        out_specs=(tile, *s_out_specs), scratch_shapes=s_scratch,
        name=name if side is None else name + "_comm", compiler_params=_cp(*sem))(*args, *s_in)
    return out[0] if side is None else (out[0], list(out[1:]))


def _mm_tn(a, b, gbuf, slot, nslots, name):
    t, m = a.shape
    n = b.shape[1]
    tk = _pick(t, (1408, 1024, 512, 384, 256, 128))
    nk = t // tk

    def vmem(rows):
        return 2 * (rows * n * 4 + tk * rows * a.dtype.itemsize + tk * n * b.dtype.itemsize)

    tm = next(c for c in (1536, 1408, 1024, 512, 384, 256, 128) if m % c == 0 and (vmem(c) <= TN_VMEM_BUDGET or c == 128))

    def body(*refs):
        a_ref, b_ref, o_ref = refs[0], refs[1], refs[-1]
        kk = pl.program_id(1)

        @pl.when(kk == 0)
        def _():
            o_ref[...] = jnp.zeros_like(o_ref)

        o_ref[...] += _dot(a_ref[...].astype(bf16), b_ref[...].astype(bf16), TN)

    in_specs = [BS((tk, tm), lambda i, kk: (kk, i)), BS((tk, n), lambda i, kk: (kk, 0))]
    args = (a, b)
    aliases = {}
    if gbuf is not None:
        in_specs.append(ANY)
        args = args + (gbuf,)
        aliases = {2: 0}
    return pl.pallas_call(
        body, out_shape=S((nslots, m, n), f32), grid=(m // tm, nk), in_specs=in_specs,
        out_specs=BS((None, tm, n), lambda i, kk: (slot, i, 0)), input_output_aliases=aliases,
        name=name, compiler_params=_cp("parallel", "arbitrary"))(*args)


def _ffn_up(hn, wg, wu, side=None):
    m, k = hn.shape
    n = wg.shape[1]
    tm = _pick(m, (1408, 384, 256, 128))
    tn = _pick(n, (512, 384, 256, 128))
    grid = (m // tm, n // tn)

    def body(*refs):
        (a_ref, wg_ref, wu_ref, g_ref, u_ref, act_ref), side_refs = _side_split(side, refs, 3, 3, 0)
        _side_start(side, side_refs, grid)
        a = a_ref[...]
        g = _dot(a, wg_ref[...], NT)
        u = _dot(a, wu_ref[...], NT)
        g_ref[...] = g
        u_ref[...] = u
        act_ref[...] = (g * jax.nn.sigmoid(g) * u).astype(bf16)
        _side_finish(side, side_refs, grid)

    tile = BS((tm, tn), lambda i, j: (i, j))
    s_in, s_in_specs, s_out, s_out_specs, s_scratch = _side_io(side)
    sem = ("parallel", "parallel") if side is None else ("arbitrary", "arbitrary")
    out = pl.pallas_call(
        body, out_shape=(S((m, n), f32), S((m, n), f32), S((m, n), bf16), *s_out), grid=grid,
        in_specs=[BS((tm, k), lambda i, j: (i, 0)), BS((None, tn, k), lambda i, j: (0, j, 0)),
                  BS((None, tn, k), lambda i, j: (0, j, 0)), *s_in_specs],
        out_specs=(tile, tile, tile, *s_out_specs), scratch_shapes=s_scratch,
        name="ffn_up" if side is None else "ffn_up_comm", compiler_params=_cp(*sem))(hn, wg, wu, *s_in)
    return out[0], out[1], out[2], list(out[3:])


def _ffn_in_bwd(dg, du, wg, wu, side=None):
    m, k = dg.shape
    n = wg.shape[2]
    tm = _pick(m, (384, 256, 128))
    tn = _pick(n, (512, 384, 256, 128))
    grid = (n // tn, m // tm)

    def body(*refs):
        (dg_ref, du_ref, wg_ref, wu_ref, o_ref), side_refs = _side_split(side, refs, 4, 1, 0)
        _side_start(side, side_refs, grid)
        o_ref[...] = _dot(du_ref[...], wu_ref[...], NN) + _dot(dg_ref[...], wg_ref[...], NN)
        _side_finish(side, side_refs, grid)

    a_spec = BS((tm, k), lambda j, i: (i, 0))
    b_spec = BS((None, k, tn), lambda j, i: (0, 0, j))
    tile = BS((tm, tn), lambda j, i: (i, j))
    s_in, s_in_specs, s_out, s_out_specs, s_scratch = _side_io(side)
    sem = ("parallel", "parallel") if side is None else ("arbitrary", "arbitrary")
    out = pl.pallas_call(
        body, out_shape=(S((m, n), f32), *s_out), grid=grid,
        in_specs=[a_spec, a_spec, b_spec, b_spec, *s_in_specs], out_specs=(tile, *s_out_specs), scratch_shapes=s_scratch,
        name="d_ffn_in" if side is None else "d_ffn_in_comm", compiler_params=_cp(*sem))(dg, du, wg, wu, *s_in)
    return out[0], list(out[1:])


def _ffn_down_bwd(dh, wd, g, u, side=None):
    m, k = dh.shape
    n = wd.shape[1]
    tm = _pick(m, (1408, 384, 256, 128))
    tn = _pick(n, (512, 384, 256, 128))
    grid = (m // tm, n // tn)

    def body(*refs):
        (a_ref, wd_ref, g_ref, u_ref, dg_ref, du_ref), side_refs = _side_split(side, refs, 4, 2, 0)
        _side_start(side, side_refs, grid)
        dact = _dot(a_ref[...].astype(bf16), wd_ref[...], NT)
        gv = g_ref[...]
        sg = jax.nn.sigmoid(gv)
        dg_ref[...] = (dact * u_ref[...] * (sg * (1.0 + gv * (1.0 - sg)))).astype(bf16)
        du_ref[...] = (dact * (gv * sg)).astype(bf16)
        _side_finish(side, side_refs, grid)

    tile = BS((tm, tn), lambda i, j: (i, j))
    s_in, s_in_specs, s_out, s_out_specs, s_scratch = _side_io(side)
    sem = ("parallel", "parallel") if side is None else ("arbitrary", "arbitrary")
    out = pl.pallas_call(
        body, out_shape=(S((m, n), bf16), S((m, n), bf16), *s_out), grid=grid,
        in_specs=[BS((tm, k), lambda i, j: (i, 0)), BS((None, tn, k), lambda i, j: (0, j, 0)), tile, tile, *s_in_specs],
        out_specs=(tile, tile, *s_out_specs), scratch_shapes=s_scratch,
        name="ffn_down_bwd" if side is None else "ffn_down_bwd_comm", compiler_params=_cp(*sem))(dh, wd, g, u, *s_in)
    return out[0], out[1], list(out[2:])


def _lane_lo():
    return lax.broadcasted_iota(jnp.int32, (1, TILE), 1) < HEAD_DIM


def _head_norm(x, g2):
    lo = _lane_lo()
    x2 = x * x
    s_lo = jnp.sum(jnp.where(lo, x2, 0.0), axis=-1, keepdims=True)
    s_hi = jnp.sum(jnp.where(lo, 0.0, x2), axis=-1, keepdims=True)
    r = jnp.where(lo, lax.rsqrt(s_lo * (1.0 / HEAD_DIM) + EPS), lax.rsqrt(s_hi * (1.0 / HEAD_DIM) + EPS))
    xh = x * r
    return xh * g2, xh, r


def _head_norm_bwd(dy, xh, r, g2):
    lo = _lane_lo()
    dgain = jnp.sum(dy * xh, axis=0, keepdims=True)
    dxh = dy * g2
    t = dxh * xh
    t_lo = jnp.sum(jnp.where(lo, t, 0.0), axis=-1, keepdims=True)
    t_hi = jnp.sum(jnp.where(lo, 0.0, t), axis=-1, keepdims=True)
    mt = jnp.where(lo, t_lo, t_hi) * (1.0 / HEAD_DIM)
    return r * (dxh - xh * mt), dgain


def _swa_slopes():
    return [2.0 ** (-8.0 * (h + 1) / SWA_HEADS) for h in range(SWA_HEADS)]


def _swa_masks(i):
    rows = lax.broadcasted_iota(jnp.int32, (TILE, TILE), 0)
    cols = lax.broadcasted_iota(jnp.int32, (TILE, TILE), 1)
    r_pos = i * TILE + rows
    prev = jnp.maximum(i - 1, 0)
    out = []
    for c, base in enumerate((0, prev * TILE, i * TILE)):
        s_pos = base + cols
        dist = r_pos - s_pos
        if c == 0:
            ok = (i >= 1) & (s_pos >= PAD)
        elif c == 1:
            ok = (i >= 2) & (dist < TILE)
        else:
            ok = (dist >= 0) & (s_pos >= PAD)
        out.append((ok, dist.astype(f32)))
    return out, prev


def _swa_geometry(i, group):
    masks, prev = _swa_masks(i)
    bases = (0, pl.multiple_of(prev * TILE, TILE), pl.multiple_of(i * TILE, TILE))
    ok = jnp.concatenate([m[0] for m in masks], axis=1)
    dist = jnp.concatenate([m[1] for m in masks], axis=1)
    return bases, jnp.concatenate([ok] * group, axis=0), jnp.concatenate([dist] * group, axis=0)


def _swa_col(values):
    return jnp.concatenate([jnp.full((TILE, 1), v, f32) for v in values], axis=0)


def _swa_keys(k_ref, v_ref, kg_ref, bases, kvh):
    kcols = slice((kvh // 2) * TILE, (kvh // 2 + 1) * TILE)
    ksel = _lane_lo() if kvh % 2 == 0 else jnp.logical_not(_lane_lo())
    kcat = jnp.concatenate([_head_norm(k_ref[pl.ds(b, TILE), kcols], kg_ref[...])[0] for b in bases], axis=0).astype(bf16)
    vcat = jnp.concatenate([jnp.where(ksel, v_ref[pl.ds(b, TILE), kcols], 0.0) for b in bases], axis=0).astype(bf16)
    return kcat, vcat, ksel


def _swa_stack(blocks, kvh, group, ksel):
    parts = []
    for gi in range(group):
        same = ((kvh * group + gi) % 2) == (kvh % 2)
        parts.append(jnp.where(ksel, blocks[gi] if same else pltpu.roll(blocks[gi], HEAD_DIM, 1), 0.0))
    return jnp.concatenate(parts, axis=0)


def _swa_unstack_add(acc, stacked, kvh, group):
    for gi in range(group):
        h = kvh * group + gi
        part = stacked[gi * TILE:(gi + 1) * TILE]
        acc[h // 2] = acc[h // 2] + (part if (h % 2) == (kvh % 2) else pltpu.roll(part, HEAD_DIM, 1))


def _swa_probs(q4, kcat, ok_g, dist_g, slope_col, sink_col):
    logits = _dot(q4, kcat, NT) * (HEAD_DIM ** -0.5) + jnp.where(ok_g, -slope_col * dist_g, NEG)
    mx = jnp.maximum(jnp.max(logits, axis=-1, keepdims=True), sink_col)
    return jnp.exp(logits - mx), jnp.exp(sink_col - mx)


def _swa_specs(lp, pw):
    qw, kvw = SWA_HEADS * HEAD_DIM, SWA_KV_HEADS * HEAD_DIM
    kidx = qw // kvw
    return (BS((TILE, qw), lambda i: (i, 0)), BS((lp, kvw), lambda i: (0, kidx)), BS((lp, kvw), lambda i: (0, kidx + 1)),
            BS((1, TILE), lambda i: (0, 0)))


def _swa_fwd(proj, qg2, kg2, sinks, side=None):
    lp, pw = proj.shape
    qw = SWA_HEADS * HEAD_DIM
    group = SWA_HEADS // SWA_KV_HEADS
    slopes = _swa_slopes()
    grid = (lp // TILE,)

    def body(*refs):
        (sink_ref, q_ref, k_ref, v_ref, qg_ref, kg_ref, o_ref), side_refs = _side_split(side, refs, 6, 1, 0)
        _side_start(side, side_refs, grid)
        i = pl.program_id(0)
        bases, ok_g, dist_g = _swa_geometry(i, group)
        qn = [_head_norm(q_ref[:, b * TILE:(b + 1) * TILE], qg_ref[...])[0] for b in range(qw // TILE)]
        acc = [jnp.zeros((TILE, TILE), f32) for _ in range(qw // TILE)]
        for kvh in range(SWA_KV_HEADS):
            kcat, vcat, ksel = _swa_keys(k_ref, v_ref, kg_ref, bases, kvh)
            q4 = _swa_stack([qn[(kvh * group + gi) // 2] for gi in range(group)], kvh, group, ksel).astype(bf16)
            sink_col = _swa_col([sink_ref[kvh * group + gi] for gi in range(group)])
            p, e_sink = _swa_probs(q4, kcat, ok_g, dist_g, _swa_col(slopes[kvh * group:(kvh + 1) * group]), sink_col)
            den = e_sink + jnp.sum(p, axis=-1, keepdims=True)
            o = _dot(p.astype(bf16), vcat, NN) / den
            _swa_unstack_add(acc, o, kvh, group)
        valid = (i * TILE + lax.broadcasted_iota(jnp.int32, (TILE, 1), 0)) >= PAD
        for b in range(qw // TILE):
            o_ref[:, b * TILE:(b + 1) * TILE] = jnp.where(valid, acc[b], 0.0)
        _side_finish(side, side_refs, grid)

    q_spec, k_spec, v_spec, vec = _swa_specs(lp, pw)
    s_in, s_in_specs, s_out, s_out_specs, s_scratch = _side_io(side)
    out = pl.pallas_call(
        body, out_shape=(S((lp, qw), f32), *s_out), grid=grid,
        in_specs=[BS(memory_space=pltpu.SMEM), q_spec, k_spec, v_spec, vec, vec, *s_in_specs],
        out_specs=(BS((TILE, qw), lambda i: (i, 0)), *s_out_specs), scratch_shapes=s_scratch,
        name="swa_fwd" if side is None else "swa_fwd_comm", compiler_params=_cp("arbitrary"))(
            sinks, proj, proj, proj, qg2, kg2, *s_in)
    return out[0], list(out[1:])


def _swa_bwd(proj, dout, qg2, kg2, sinks, side=None):
    lp, pw = proj.shape
    qw, kvw = SWA_HEADS * HEAD_DIM, SWA_KV_HEADS * HEAD_DIM
    group = SWA_HEADS // SWA_KV_HEADS
    scale = HEAD_DIM ** -0.5
    slopes = _swa_slopes()
    nt = lp // TILE

    def body(*refs):
        (sink_ref, q_ref, k_ref, v_ref, do_ref, qg_ref, kg_ref, dq_ref, dk_ref, dv_ref, dqg_ref, dkg_ref, ds_ref,
         dkn_acc, dv_acc), side_refs = _side_split(side, refs, 7, 6, 2)
        _side_start(side, side_refs, (nt,))
        i = pl.program_id(0)
        lo = _lane_lo()
        lane = lax.broadcasted_iota(jnp.int32, (1, TILE), 1)

        @pl.when(i == 0)
        def _():
            dkn_acc[...] = jnp.zeros_like(dkn_acc)
            dv_acc[...] = jnp.zeros_like(dv_acc)
            dqg_ref[...] = jnp.zeros_like(dqg_ref)
            dkg_ref[...] = jnp.zeros_like(dkg_ref)
            ds_ref[...] = jnp.zeros_like(ds_ref)

        bases, ok_g, dist_g = _swa_geometry(i, group)
        qnorm = [_head_norm(q_ref[:, b * TILE:(b + 1) * TILE], qg_ref[...]) for b in range(qw // TILE)]
        dqn = [jnp.zeros((TILE, TILE), f32) for _ in range(qw // TILE)]
        for kvh in range(SWA_KV_HEADS):
            kcols = slice((kvh // 2) * TILE, (kvh // 2 + 1) * TILE)
            heads = [kvh * group + gi for gi in range(group)]
            kcat, vcat, ksel = _swa_keys(k_ref, v_ref, kg_ref, bases, kvh)
            q4 = _swa_stack([qnorm[h // 2][0] for h in heads], kvh, group, ksel).astype(bf16)
            do4 = _swa_stack([do_ref[:, (h // 2) * TILE:(h // 2 + 1) * TILE].astype(f32) for h in heads], kvh, group, ksel).astype(bf16)
            sink_col = _swa_col([sink_ref[h] for h in heads])
            p, e_sink = _swa_probs(q4, kcat, ok_g, dist_g, _swa_col([slopes[h] for h in heads]), sink_col)
            inv = 1.0 / (e_sink + jnp.sum(p, axis=-1, keepdims=True))
            p = p * inv
            dp = _dot(do4, vcat, NT)
            dsum = jnp.sum(p * dp, axis=-1, keepdims=True)
            d_sink = e_sink * inv * dsum
            for gi, h in enumerate(heads):
                ds_ref[...] += jnp.where(lane == h, -jnp.sum(d_sink[gi * TILE:(gi + 1) * TILE]), 0.0)
            dsc = (p * (dp - dsum) * scale).astype(bf16)
            _swa_unstack_add(dqn, jnp.where(ksel, _dot(dsc, kcat, NN), 0.0), kvh, group)
            dkn = _dot(dsc, q4, TN)
            dvv = _dot(p.astype(bf16), do4, TN)
            for c in range(3):
                dkn_acc[pl.ds(bases[c], TILE), kcols] += dkn[c * TILE:(c + 1) * TILE]
                dv_acc[pl.ds(bases[c], TILE), kcols] += dvv[c * TILE:(c + 1) * TILE]
        for b in range(qw // TILE):
            _, xh, r = qnorm[b]
            dq, dgain = _head_norm_bwd(dqn[b], xh, r, qg_ref[...])
            dq_ref[:, b * TILE:(b + 1) * TILE] = dq.astype(bf16)
            dqg_ref[...] += dgain

        @pl.when(i == nt - 1)
        def _():
            dv_ref[...] = dv_acc[...].astype(bf16)

            def tile_step(t, carry):
                base = pl.multiple_of(t * TILE, TILE)
                for kb in range(kvw // TILE):
                    kcols = slice(kb * TILE, (kb + 1) * TILE)
                    _, xh, r = _head_norm(k_ref[pl.ds(base, TILE), kcols], kg_ref[...])
                    dk, dgain = _head_norm_bwd(dkn_acc[pl.ds(base, TILE), kcols], xh, r, kg_ref[...])
                    dk_ref[pl.ds(base, TILE), kcols] = dk.astype(bf16)
                    dkg_ref[...] += dgain
                return carry

            lax.fori_loop(0, nt, tile_step, 0)

        _side_finish(side, side_refs, (nt,))

    q_spec, k_spec, v_spec, vec = _swa_specs(lp, pw)
    whole = BS((lp, kvw), lambda i: (0, 0))
    s_in, s_in_specs, s_out, s_out_specs, s_scratch = _side_io(side)
    out = pl.pallas_call(
        body, out_shape=(S((lp, qw), bf16), S((lp, kvw), bf16), S((lp, kvw), bf16), S((1, TILE), f32), S((1, TILE), f32), S((1, TILE), f32),
                         *s_out),
        grid=(nt,),
        in_specs=[BS(memory_space=pltpu.SMEM), q_spec, k_spec, v_spec, BS((TILE, qw), lambda i: (i, 0)), vec, vec, *s_in_specs],
        out_specs=(BS((TILE, qw), lambda i: (i, 0)), whole, whole, vec, vec, vec, *s_out_specs),
        scratch_shapes=[pltpu.VMEM((lp, kvw), f32), pltpu.VMEM((lp, kvw), f32), *s_scratch],
        name="swa_bwd" if side is None else "swa_bwd_comm", compiler_params=_cp("arbitrary"))(
            sinks, proj, proj, proj, dout, qg2, kg2, *s_in)
    return (*out[:6], list(out[6:]))


LOG2E = 1.4426950408889634
SB_PAIRS = 2
ROWS2 = 2 * TILE
SB_ROWS = SB_PAIRS * ROWS2


def _sb_stack(x):
    lo = _lane_lo()
    zero = jnp.zeros((TILE, TILE), x.dtype)
    parts = []
    for p in range(SB_PAIRS):
        xp = x[:, p * TILE:(p + 1) * TILE]
        parts += [jnp.where(lo, xp, zero), jnp.where(lo, zero, xp)]
    return jnp.concatenate(parts, axis=0)


def _sb_unstack(x):
    lo = _lane_lo()
    parts = [jnp.where(lo, x[p * ROWS2:p * ROWS2 + TILE], x[p * ROWS2 + TILE:(p + 1) * ROWS2]) for p in range(SB_PAIRS)]
    return parts[0] if SB_PAIRS == 1 else jnp.concatenate(parts, axis=1)


def _sb_pair_dot(a, b, dims, b_lanes):
    parts = []
    for p in range(SB_PAIRS):
        bp = b[:, p * TILE:(p + 1) * TILE] if b_lanes else b[p * ROWS2:(p + 1) * ROWS2]
        parts.append(_dot(a[p * ROWS2:(p + 1) * ROWS2], bp, dims))
    return parts


def _sb_sum_matrix(after):
    rows = lax.broadcasted_iota(jnp.int32, (ROWS2, ROWS2), 0) & (TILE - 1)
    cols = lax.broadcasted_iota(jnp.int32, (ROWS2, ROWS2), 1)
    tri = (rows > cols) if after else (rows < cols)
    return (tri | (cols >= TILE)).astype(bf16)


def _sb_sums(x, w, sum_mat):
    hi = x.astype(bf16)
    lo = (x - hi.astype(f32)).astype(bf16)
    return [_dot(jnp.concatenate([hi[:, j * TILE:(j + 1) * TILE], lo[:, j * TILE:(j + 1) * TILE]], axis=1), sum_mat, NN)
            for j in range(w)]


def _sb_edge_masks(qi, first, w):
    rows = lax.broadcasted_iota(jnp.int32, (SB_ROWS, TILE), 0) & (TILE - 1)
    cols = lax.broadcasted_iota(jnp.int32, (SB_ROWS, TILE), 1)
    not_pad = first * TILE + cols >= PAD
    before_query = (first + w - 1 - qi) * TILE + cols < rows
    if w == 1:
        return [not_pad & before_query]
    return [not_pad] + [None] * (w - 2) + [before_query]


def _sb_apply(valid, x):
    return x if valid is None else jnp.where(valid, x, 0.0)


def _sb_block(q2, k_ref, first, w, valids, c_rep, after_ones):
    kwin = k_ref[pl.ds(pl.multiple_of(first * TILE, TILE), w * TILE), :]
    z2 = jnp.concatenate(_sb_pair_dot(q2, kwin, NT, True), axis=0) * (HEAD_DIM ** -0.5 * LOG2E)
    ls2 = jnp.minimum(z2, 0.0) - jnp.log2(1.0 + jnp.exp2(jnp.minimum(z2, -z2)))
    m2 = ls2 - z2
    if w == 1:
        m2 = _sb_apply(valids[0], m2)
    else:
        m2 = jnp.concatenate([_sb_apply(valids[j], m2[:, j * TILE:(j + 1) * TILE]) for j in range(w)], axis=1)
    sums = _sb_sums(m2, w, after_ones)
    parts = [None] * w
    for j in reversed(range(w)):
        parts[j] = _sb_apply(valids[j], jnp.exp2(ls2[:, j * TILE:(j + 1) * TILE] + sums[j][:, :TILE] + c_rep))
        c_rep = c_rep + sums[j][:, TILE:]
    return ls2, (parts[0] if w == 1 else jnp.concatenate(parts, axis=1)), c_rep


SB_BLOCKS = (4, 2, 1)
SB_DEAD = -160.0


def _sb_alive(c_rep):
    return jnp.max(c_rep) > SB_DEAD


def _sb_walk_down(qi, block, state, c_of):
    n = qi + 1
    big = SB_BLOCKS[0]

    def more(carry):
        t, st = carry
        return (t < n // big) & _sb_alive(c_of(st))

    def step(carry):
        t, st = carry
        return t + 1, block(n - big * (t + 1), big, st)

    visited, state = lax.while_loop(more, step, (jnp.int32(0), state))
    counts = [visited]
    for s in SB_BLOCKS[1:]:
        cnt = jnp.where(_sb_alive(c_of(state)), (n % (2 * s)) // s, 0)
        state = lax.fori_loop(0, cnt, lambda t, st, s=s: block(n % s, s, st), state)
        counts.append(cnt)
    return state, counts


def _sb_walk_up(qi, block, state, counts):
    n = qi + 1
    big = SB_BLOCKS[0]
    for s, cnt in reversed(list(zip(SB_BLOCKS[1:], counts[1:]))):
        state = lax.fori_loop(0, cnt, lambda t, st, s=s: block(n % s, s, st), state)
    visited = counts[0]
    return lax.fori_loop(0, visited, lambda i, st: block(n - big * (visited - i), big, st), state)


def _sb_cols(lp, off, single=False):
    mode = dict(pipeline_mode=pl.Buffered(1)) if single else {}
    return BS((lp, SB_PAIRS * TILE), lambda g: (0, off + g), **mode)


def _sb_fwd(proj, side=None):
    lp = proj.shape[0]
    sbw = SB_HEADS * HEAD_DIM
    ngrp = sbw // (SB_PAIRS * TILE)
    nt = lp // TILE

    def body(*refs):
        (q_ref, k_ref, v_ref, o_ref), side_refs = _side_split(side, refs, 3, 1, 0)
        _side_start(side, side_refs, (ngrp,))
        after_ones = _sb_sum_matrix(True)

        def q_step(qi, carry):
            qbase = pl.multiple_of(qi * TILE, TILE)
            q2 = _sb_stack(q_ref[pl.ds(qbase, TILE), :])

            def block(first, w, st):
                c_rep, acc = st
                _, a, c_rep = _sb_block(q2, k_ref, first, w, _sb_edge_masks(qi, first, w), c_rep, after_ones)
                vwin = v_ref[pl.ds(pl.multiple_of(first * TILE, TILE), w * TILE), :]
                return c_rep, acc + jnp.concatenate(_sb_pair_dot(a.astype(bf16), vwin, NN, True), axis=0)

            zero = jnp.zeros((SB_ROWS, TILE), f32)
            (_, acc), _ = _sb_walk_down(qi, block, (zero, zero), lambda st: st[0])
            o_ref[pl.ds(qbase, TILE), :] = _sb_unstack(acc)
            return carry

        lax.fori_loop(0, nt, q_step, 0)
        _side_finish(side, side_refs, (ngrp,))

    s_in, s_in_specs, s_out, s_out_specs, s_scratch = _side_io(side)
    out = pl.pallas_call(
        body, out_shape=(S((lp, sbw), f32), *s_out), grid=(ngrp,),
        in_specs=[_sb_cols(lp, 0), _sb_cols(lp, ngrp), _sb_cols(lp, 2 * ngrp), *s_in_specs],
        out_specs=(_sb_cols(lp, 0), *s_out_specs), scratch_shapes=s_scratch,
        name="sb_fwd" if side is None else "sb_fwd_comm",
        compiler_params=_cp("parallel" if side is None else "arbitrary"))(proj, proj, proj, *s_in)
    return out[0], list(out[1:])


def _sb_bwd(proj, dout, side=None):
    lp = proj.shape[0]
    sbw = SB_HEADS * HEAD_DIM
    ngrp = sbw // (SB_PAIRS * TILE)
    nt = lp // TILE
    scale = HEAD_DIM ** -0.5

    def body(*refs):
        (q_ref, k_ref, v_ref, do_ref, dq_ref, dk_ref, dv_ref, dk_acc, dv_acc, e_buf, b_buf), side_refs = _side_split(side, refs, 4, 3, 4)
        _side_start(side, side_refs, (ngrp,))
        dk_acc[...] = jnp.zeros_like(dk_acc)
        dv_acc[...] = jnp.zeros_like(dv_acc)
        after_ones = _sb_sum_matrix(True)
        before_ones = _sb_sum_matrix(False)

        def add_cols(acc_ref, rows, parts):
            for p in range(SB_PAIRS):
                acc_ref[rows, p * TILE:(p + 1) * TILE] += parts[p]

        def q_step(qi, carry):
            qbase = pl.multiple_of(qi * TILE, TILE)
            q2 = _sb_stack(q_ref[pl.ds(qbase, TILE), :])
            do2 = _sb_stack(do_ref[pl.ds(qbase, TILE), :])

            def block_rl(first, w, c_rep):
                ls2, a, c_rep = _sb_block(q2, k_ref, first, w, _sb_edge_masks(qi, first, w), c_rep, after_ones)
                rows = pl.ds(pl.multiple_of(first * TILE, TILE), w * TILE)
                e = jnp.concatenate(_sb_pair_dot(do2, v_ref[rows, :], NT, True), axis=0) * a
                beta = jnp.exp2(ls2)
                for j in range(w):
                    e_buf[first + j] = e[:, j * TILE:(j + 1) * TILE]
                    b_buf[first + j] = beta[:, j * TILE:(j + 1) * TILE]
                add_cols(dv_acc, rows, _sb_pair_dot(a.astype(bf16), do2, TN, False))
                return c_rep

            zero = jnp.zeros((SB_ROWS, TILE), f32)
            _, counts = _sb_walk_down(qi, block_rl, zero, lambda c_rep: c_rep)

            def block_lr(first, w, st):
                e_rep, dq_acc = st
                rows = pl.ds(pl.multiple_of(first * TILE, TILE), w * TILE)
                valids = _sb_edge_masks(qi, first, w)
                es = [e_buf[first + j] for j in range(w)]
                sums = _sb_sums(es[0] if w == 1 else jnp.concatenate(es, axis=1), w, before_ones)
                parts = []
                for j in range(w):
                    beta = b_buf[first + j]
                    parts.append(_sb_apply(valids[j], (es[j] - beta * (es[j] + e_rep + sums[j][:, :TILE])) * scale))
                    e_rep = e_rep + sums[j][:, TILE:]
                dz = (parts[0] if w == 1 else jnp.concatenate(parts, axis=1)).astype(bf16)
                add_cols(dk_acc, rows, _sb_pair_dot(dz, q2, TN, False))
                return e_rep, dq_acc + jnp.concatenate(_sb_pair_dot(dz, k_ref[rows, :], NN, True), axis=0)

            _, dq = _sb_walk_up(qi, block_lr, (zero, zero), counts)
            dq_ref[pl.ds(qbase, TILE), :] = _sb_unstack(dq).astype(bf16)
            return carry

        lax.fori_loop(0, nt, q_step, 0)
        dk_ref[...] = dk_acc[...].astype(bf16)
        dv_ref[...] = dv_acc[...].astype(bf16)
        _side_finish(side, side_refs, (ngrp,))

    wide = SB_PAIRS * TILE
    s_in, s_in_specs, s_out, s_out_specs, s_scratch = _side_io(side)
    out = pl.pallas_call(
        body, out_shape=(S((lp, sbw), bf16),) * 3 + tuple(s_out), grid=(ngrp,),
        in_specs=[_sb_cols(lp, 0, True), _sb_cols(lp, ngrp, True), _sb_cols(lp, 2 * ngrp, True), _sb_cols(lp, 0, True), *s_in_specs],
        out_specs=(_sb_cols(lp, 0, True),) * 3 + tuple(s_out_specs),
        scratch_shapes=[pltpu.VMEM((lp, wide), f32), pltpu.VMEM((lp, wide), f32),
                        pltpu.VMEM((nt, SB_ROWS, TILE), f32), pltpu.VMEM((nt, SB_ROWS, TILE), f32), *s_scratch],
        name="sb_bwd" if side is None else "sb_bwd_comm",
        compiler_params=_cp("parallel" if side is None else "arbitrary"))(proj, proj, proj, dout, *s_in)
    return out[0], out[1], out[2], list(out[3:])


def _place():
    x, y, c = lax.axis_index("x"), lax.axis_index("y"), lax.axis_index("c")
    chips = [(1 - x, y), (x, 1 - y), (1 - x, 1 - y)]
    return x, y, c, chips


def _two_level_gather_body(rows, x_ref, send_sems, recv_sems, local_sem):
    x, y, c, chips = _place()
    me, sibling = (x, y, c), (x, y, 1 - c)

    def copy(k, block, to, src=None):
        return pltpu.make_async_remote_copy(
            src_ref=rows(*block) if src is None else src, dst_ref=rows(*block),
            send_sem=send_sems.at[k], recv_sem=recv_sems.at[k], device_id=to, device_id_type=MESH)

    mine = pltpu.make_async_copy(x_ref, rows(*me), local_sem)
    mine.start()
    first = [copy(0, me, sibling, src=x_ref)]
    first += [copy(1 + j, me, (*chip, c), src=x_ref) for j, chip in enumerate(chips)]
    for cp in first:
        cp.start()
    passed = [copy(4 + j, (*chip, c), sibling) for j, chip in enumerate(chips)]
    for j, chip in enumerate(chips):
        copy(1 + j, (*chip, c), me).wait_recv()
        passed[j].start()
    copy(0, sibling, me).wait_recv()
    for j, chip in enumerate(chips):
        copy(4 + j, (*chip, 1 - c), me).wait_recv()
    for cp in first + passed:
        cp.wait_send()
    mine.wait()


_GATHER_SEMS = [pltpu.SemaphoreType.DMA((7,)), pltpu.SemaphoreType.DMA((7,)), pltpu.SemaphoreType.DMA]


class _Side:
    def __init__(self, inputs, out_shapes, nsem, start, finish):
        self.inputs, self.out_shapes, self.start, self.finish = list(inputs), list(out_shapes), start, finish
        self.scratch = [pltpu.SemaphoreType.DMA((nsem,)), pltpu.SemaphoreType.DMA((nsem,)), pltpu.SemaphoreType.DMA((len(inputs),))]


def _side_io(side):
    if side is None:
        return [], [], [], [], []
    return side.inputs, [ANY] * len(side.inputs), side.out_shapes, [ANY] * len(side.out_shapes), side.scratch


def _side_split(side, refs, n_in, n_out, n_scratch):
    if side is None:
        return refs, None
    si, so = len(side.inputs), len(side.out_shapes)
    a, b, c, d = n_in, n_in + si, n_in + si + n_out, n_in + si + n_out + so
    return refs[:a] + refs[b:c] + refs[d:d + n_scratch], (refs[a:b], refs[c:d], refs[d + n_scratch:])


def _side_first_last(grid):
    first = last = None
    for k, n in enumerate(grid):
        i = pl.program_id(k)
        first = (i == 0) if first is None else first & (i == 0)
        last = (i == n - 1) if last is None else last & (i == n - 1)
    return first, last


def _side_start(side, side_refs, grid):
    if side is not None:
        pl.when(_side_first_last(grid)[0])(lambda: side.start(*side_refs))


def _side_finish(side, side_refs, grid):
    if side is not None:
        pl.when(_side_first_last(grid)[1])(lambda: side.finish(*side_refs))


def _run_side(side, name):
    def body(*refs):
        _, side_refs = _side_split(side, refs, 0, 0, 0)
        side.start(*side_refs)
        side.finish(*side_refs)

    inputs, in_specs, out_shapes, out_specs, scratch = _side_io(side)
    return pl.pallas_call(body, out_shape=tuple(out_shapes), in_specs=in_specs, out_specs=tuple(out_specs),
                          scratch_shapes=scratch, name=name)(*inputs)


def _gather_side(shards, layers):
    def plan(in_refs, out_refs, sems, starting):
        send_sems, recv_sems, local_sems = sems
        x, y, c, chips = _place()
        me, sibling = (x, y, c), (x, y, 1 - c)
        jobs = []
        for a, (x_ref, o_ref) in enumerate(zip(in_refs, out_refs)):
            r = x_ref.shape[1]
            src = x_ref.at[layers[a]]

            def rows(px, py, pc, o_ref=o_ref, r=r):
                return o_ref.at[0, pl.ds(pl.multiple_of((4 * px + 2 * py + pc) * r, 16), r), :]

            def copy(k, block, to, from_shard=False, a=a, rows=rows, src=src):
                return pltpu.make_async_remote_copy(
                    src_ref=src if from_shard else rows(*block), dst_ref=rows(*block),
                    send_sem=send_sems.at[7 * a + k], recv_sem=recv_sems.at[7 * a + k], device_id=to, device_id_type=MESH)

            job = dict(
                mine=pltpu.make_async_copy(src, rows(*me), local_sems.at[a]),
                first=[copy(0, me, sibling, True)] + [copy(1 + j, me, (*chip, c), True) for j, chip in enumerate(chips)])
            if not starting:
                job.update(
                    passed=[copy(4 + j, (*chip, c), sibling) for j, chip in enumerate(chips)],
                    from_chips=[copy(1 + j, (*chip, c), me) for j, chip in enumerate(chips)],
                    from_sibling=[copy(0, sibling, me)] + [copy(4 + j, (*chip, 1 - c), me) for j, chip in enumerate(chips)])
            jobs.append(job)
        return jobs

    def start(*refs):
        for job in plan(*refs, starting=True):
            job["mine"].start()
            for cp in job["first"]:
                cp.start()

    def finish(*refs):
        jobs = plan(*refs, starting=False)
        for job in jobs:
            for arrived, onward in zip(job["from_chips"], job["passed"]):
                arrived.wait_recv()
                onward.start()
        for job in jobs:
            for cp in job["from_sibling"]:
                cp.wait_recv()
            for cp in job["first"] + job["passed"]:
                cp.wait_send()
            job["mine"].wait()

    shapes = [S((1, NDEV * s.shape[1], s.shape[2]), s.dtype) for s in shards]
    return _Side(shards, shapes, 7 * len(shards), start, finish)


def _swap_side(gs):
    def plan(in_refs, out_refs, sems):
        send_sems, recv_sems, _ = sems
        x, y, c, chips = _place()
        copies = []
        for a, (g_ref, o_ref) in enumerate(zip(in_refs, out_refs)):
            r = g_ref.shape[1] // NDEV
            for j, (px, py) in enumerate([(x, y)] + chips):
                d = 4 * px + 2 * py + (1 - c)
                copies.append(pltpu.make_async_remote_copy(
                    src_ref=g_ref.at[:, pl.ds(pl.multiple_of(d * r, 8), r), :], dst_ref=o_ref.at[j],
                    send_sem=send_sems.at[4 * a + j], recv_sem=recv_sems.at[4 * a + j], device_id=(x, y, 1 - c), device_id_type=MESH))
        return copies

    return _exchange_side(gs, [S((4, g.shape[0], g.shape[1] // NDEV, g.shape[2]), f32) for g in gs], 4 * len(gs), plan)


def _ici_side(sends):
    def plan(in_refs, out_refs, sems):
        send_sems, recv_sems, _ = sems
        x, y, c, chips = _place()
        return [pltpu.make_async_remote_copy(
            src_ref=s_ref.at[j], dst_ref=o_ref.at[j], send_sem=send_sems.at[3 * a + j], recv_sem=recv_sems.at[3 * a + j],
            device_id=(*chip, c), device_id_type=MESH)
            for a, (s_ref, o_ref) in enumerate(zip(in_refs, out_refs)) for j, chip in enumerate(chips)]

    return _exchange_side(sends, [S(s.shape, s.dtype) for s in sends], 3 * len(sends), plan)


def _exchange_side(inputs, shapes, nsem, plan):
    def start(*refs):
        for cp in plan(*refs):
            cp.start()

    def finish(*refs):
        copies = plan(*refs)
        for cp in copies:
            cp.wait_recv()
        for cp in copies:
            cp.wait_send()

    return _Side(inputs, shapes, nsem, start, finish)


def _ag_small(shard, name):
    r, w = shard.shape

    def body(x_ref, o_ref, send_sems, recv_sems, local_sem):
        def rows(px, py, pc):
            return o_ref.at[pl.ds(pl.multiple_of((4 * px + 2 * py + pc) * r, 8), r), :]
        _two_level_gather_body(rows, x_ref, send_sems, recv_sems, local_sem)

    vmem = BS(memory_space=pltpu.VMEM)
    return pl.pallas_call(
        body, out_shape=S((NDEV * r, w), shard.dtype), in_specs=[vmem], out_specs=vmem,
        scratch_shapes=_GATHER_SEMS, name=name)(shard)


def _rel_ids():
    x, y, c = lax.axis_index("x"), lax.axis_index("y"), lax.axis_index("c")
    rel = [(x, y), (1 - x, y), (x, 1 - y), (1 - x, 1 - y)]
    return jnp.stack([4 * px + 2 * py + c for px, py in rel]).astype(jnp.int32)


def _rs_partial(g, sib, ids, name):
    ns, rows8, cdim = g.shape
    r = rows8 // NDEV
    tr = r // 2
    nb = r // tr

    def own_body(ids_ref, g_ref, s_ref, o_ref):
        o_ref[...] = g_ref[...] + s_ref[...]

    own = pl.pallas_call(
        own_body, out_shape=S((ns, r, cdim), f32),
        grid_spec=pltpu.PrefetchScalarGridSpec(
            num_scalar_prefetch=1, grid=(ns, nb),
            in_specs=[BS((None, tr, cdim), lambda s, t, ids_ref: (s, ids_ref[0] * nb + t, 0)),
                      BS((None, None, tr, cdim), lambda s, t, ids_ref: (0, s, t, 0))],
            out_specs=BS((None, tr, cdim), lambda s, t, ids_ref: (s, t, 0))),
        name=name + "_own", compiler_params=_cp("parallel", "parallel"))(ids, g, sib)

    def send_body(ids_ref, g_ref, s_ref, o_ref):
        o_ref[...] = (g_ref[...] + s_ref[...]).astype(bf16)

    send = pl.pallas_call(
        send_body, out_shape=S((3, ns, r, cdim), bf16),
        grid_spec=pltpu.PrefetchScalarGridSpec(
            num_scalar_prefetch=1, grid=(3, ns, nb),
            in_specs=[BS((None, tr, cdim), lambda j, s, t, ids_ref: (s, ids_ref[j + 1] * nb + t, 0)),
                      BS((None, None, tr, cdim), lambda j, s, t, ids_ref: (j + 1, s, t, 0))],
            out_specs=BS((None, None, tr, cdim), lambda j, s, t, ids_ref: (j, s, t, 0))),
        name=name + "_send", compiler_params=_cp("parallel", "parallel", "parallel"))(ids, g, sib)
    return own, send


def _prep(w, name):
    nl, r, cdim = w.shape
    blk = BS((None, r, cdim), lambda l: (l, 0, 0))

    def body(w_ref, o_ref):
        o_ref[...] = w_ref[...].astype(bf16)

    return pl.pallas_call(body, out_shape=S(w.shape, bf16), grid=(nl,), in_specs=[blk], out_specs=blk,
                          name=name, compiler_params=_cp("parallel"))(w)


def _adam_math(w, g, m, v):
    m2 = ADAM_B1 * m + (1.0 - ADAM_B1) * g
    v2 = ADAM_B2 * v + (1.0 - ADAM_B2) * (g * g)
    m_hat = m2 / (1.0 - ADAM_B1 ** ADAM_STEP)
    v_hat = v2 / (1.0 - ADAM_B2 ** ADAM_STEP)
    delta = -ADAM_LR * (m_hat / (jnp.sqrt(v_hat) + ADAM_EPS) + ADAM_WD * w)
    return delta, m2, v2


def _adam_big(w, m, v, own, arrived, layer, slot, name, prev=None):
    nl, a, b = w.shape
    ta = _pick(a, (a // 4, a // 2, a)) if (a // 4) % 16 == 0 else a
    w_blk = BS((None, ta, b), lambda t: (layer, t, 0))
    own_blk = BS((None, ta, b), lambda t: (slot, t, 0))
    arr_blk = [BS((None, None, ta, b), lambda t, j=j: (j, slot, t, 0)) for j in range(3)]

    def body(*refs):
        w_ref, m_ref, v_ref, own_ref, a0_ref, a1_ref, a2_ref = refs[:7]
        g_ref, d_ref, m2_ref, v2_ref = refs[-4:]
        g = ((own_ref[...] + a0_ref[...].astype(f32)) + a1_ref[...].astype(f32)) + a2_ref[...].astype(f32)
        delta, m2, v2 = _adam_math(w_ref[...], g, m_ref[...], v_ref[...])
        g_ref[...] = g
        d_ref[...] = delta
        m2_ref[...] = m2
        v2_ref[...] = v2

    threaded = () if prev is None else tuple(prev)
    return pl.pallas_call(
        body, out_shape=(S(w.shape, f32),) * 4, grid=(a // ta,),
        in_specs=[w_blk, w_blk, w_blk, own_blk] + arr_blk + [ANY] * len(threaded), out_specs=(w_blk,) * 4,
        input_output_aliases={7 + k: k for k in range(len(threaded))},
        name=name, compiler_params=_cp("parallel"))(w, m, v, own, arrived, arrived, arrived, *threaded)


def _adam_small(gathered, w, m, v):
    r, wd = w.shape

    def body(g_ref, w_ref, m_ref, v_ref, gs_ref, d_ref, m2_ref, v2_ref):
        g = g_ref[0:r, :]
        for dev in range(1, NDEV):
            g = g + g_ref[dev * r:(dev + 1) * r, :]
        delta, m2, v2 = _adam_math(w_ref[...], g, m_ref[...], v_ref[...])
        gs_ref[...] = g
        d_ref[...] = delta
        m2_ref[...] = m2
        v2_ref[...] = v2

    return pl.pallas_call(body, out_shape=(S((r, wd), f32),) * 4, name="adam_small", compiler_params=_cp())(gathered, w, m, v)


def _pack(arrays, width):
    flat = jnp.concatenate([a.reshape(-1) for a in arrays])
    rows = -(-flat.shape[0] // (8 * width)) * 8
    return jnp.pad(flat, (0, rows * width - flat.shape[0])).reshape(rows, width)


def _unpack(packed, shapes):
    flat = packed.reshape(-1)
    out, off = [], 0
    for shp in shapes:
        n = 1
        for s in shp:
            n *= s
        out.append(flat[off:off + n].reshape(shp))
        off += n
    return out


def kernel(x, meta_tokens, attn_norm_g, w_in, q_norm_g, k_norm_g, attn_sinks, swa_out_g, sb_out_g, w_o, ffn_norm_g, w_gate, w_up, w_down, loss_target, m_meta_tokens, m_attn_norm_g, m_w_in, m_q_norm_g, m_k_norm_g, m_attn_sinks, m_swa_out_g, m_sb_out_g, m_w_o, m_ffn_norm_g, m_w_gate, m_w_up, m_w_down, v_meta_tokens, v_attn_norm_g, v_w_in, v_q_norm_g, v_k_norm_g, v_attn_sinks, v_swa_out_g, v_sb_out_g, v_w_o, v_ffn_norm_g, v_w_gate, v_w_up, v_w_down):
    nl, d = attn_norm_g.shape
    x2, target = x[0], loss_target[0]
    me = 4 * lax.axis_index("x") + 2 * lax.axis_index("y") + lax.axis_index("c")
    ids = _rel_ids()

    meta_all = _ag_small(meta_tokens, "ag_meta")
    meta_full = meta_all.reshape(NDEV, N_META, -1).transpose(1, 0, 2).reshape(N_META, d)
    def t(a):
        return jnp.swapaxes(a, 1, 2)

    w_in_v, w_gate_v, w_up_v = t(w_in), t(w_gate), t(w_up)
    shards = [_prep(w_in_v, "prep_w_in"), _prep(w_o, "prep_w_o"), _prep(w_gate_v, "prep_w_gate"), _prep(w_up_v, "prep_w_up"),
              _prep(w_down, "prep_w_down")]
    attn_w = [_run_side(_gather_side(shards[:2], [0, 0]), "ag_attn_layer0")] + [None] * (nl - 1)
    ffn_w = [None] * nl

    qg2 = jnp.tile(q_norm_g, (1, TILE // HEAD_DIM))
    kg2 = jnp.tile(k_norm_g, (1, TILE // HEAD_DIM))
    out_g = jnp.concatenate([swa_out_g, sb_out_g], axis=1)
    swa_cols = (SWA_HEADS + 2 * SWA_KV_HEADS) * HEAD_DIM

    h = _embed(x2, meta_full)
    saved, weights = [], []
    for l in range(nl):
        w_in_t, w_o_f = attn_w[l]
        hn = _rmsnorm_fwd([h], attn_norm_g[l:l + 1], "attn_norm")
        nxt = l + 1 < nl
        sb_cols = (swa_cols, 3 * SB_HEADS * HEAD_DIM)
        proj = _mm(hn, w_in_t, 0, "nt", "proj_swa", cols=(0, swa_cols))
        if l == 0:
            proj_sb, (gate0,) = _mm(hn, w_in_t, 0, "nt", "proj_sb", out_dtype=bf16, cols=sb_cols, side=_gather_side(shards[2:3], [0]))
            out_a, (up0,) = _swa_fwd(proj, qg2[l:l + 1], kg2[l:l + 1], attn_sinks[l], _gather_side(shards[3:4], [0]))
            riding = [(shards[4], 0)] + ([(shards[0], 1), (shards[1], 1)] if nxt else [])
            out_b, gathered = _sb_fwd(proj_sb, _gather_side([s for s, _ in riding], [k for _, k in riding]))
            ffn_w[0] = [gate0, up0, gathered[0]]
            if nxt:
                attn_w[1] = gathered[1:]
        else:
            proj_sb = _mm(hn, w_in_t, 0, "nt", "proj_sb", out_dtype=bf16, cols=sb_cols)
            out_a, up_next = _swa_fwd(proj, qg2[l:l + 1], kg2[l:l + 1], attn_sinks[l], _gather_side(shards[3:4], [l + 1]) if nxt else None)
            out_b, attn_next = _sb_fwd(proj_sb, _gather_side(shards[:2], [l + 1, l + 1]) if nxt else None)
            if nxt:
                attn_w[l + 1] = attn_next
        w_gate_t, w_up_t, w_down_f = ffn_w[l]
        weights.append((w_in_t, w_o_f, w_gate_t, w_up_t, w_down_f))
        mixed = _rmsnorm_fwd([out_a, out_b], out_g[l:l + 1], "out_norm")
        h_mid = _mm(mixed, w_o_f, 0, "nn", "attn_out", res=h)
        hn2 = _rmsnorm_fwd([h_mid], ffn_norm_g[l:l + 1], "ffn_norm")
        if nxt:
            ahead = shards[2:4] if l == 0 else shards[2:3]
            g, u, act, gate_up = _ffn_up(hn2, w_gate_t, w_up_t, _gather_side(ahead, [l + 1] * len(ahead)))
            h_out, down = _mm(act, w_down_f, 0, "nn", "ffn_down", res=h_mid, side=_gather_side(shards[4:], [l + 1]))
            ffn_w[l + 1] = (gate_up if l == 0 else gate_up + up_next) + down
        else:
            g, u, act, _ = _ffn_up(hn2, w_gate_t, w_up_t)
            h_out = _mm(act, w_down_f, 0, "nn", "ffn_down", res=h_mid)
        saved.append((h, hn, proj, proj_sb, out_a, out_b, mixed, h_mid, hn2, g, u, act))
        h = h_out

    loss_tile, dh, dh_b = _loss_head(h, target)
    loss = lax.psum(loss_tile[0, 0], AXES)

    small = {k: [None] * nl for k in ("attn", "ffn", "out", "q", "k", "sink")}
    pending = None
    red_attn = [None] * nl
    red_ff = [None] * nl

    def attn_partial_sums(slabs, sibs):
        return [_rs_partial(gg, sib, ids, name) for gg, sib, name in zip(slabs, sibs, ("rs_part_w_in", "rs_part_w_o"))]

    for l in reversed(range(nl)):
        w_in_t, w_o_f, w_gate_t, w_up_t, w_down_f = weights[l]
        h_in, hn, proj, proj_sb, out_a, out_b, mixed, h_mid, hn2, g, u, act = saved[l]
        dg, du, sibs = _ffn_down_bwd(dh_b, w_down_f, g, u, None if pending is None else _swap_side(pending))
        parts = [] if pending is None else attn_partial_sums(pending, sibs)
        g_ff = _mm_tn(act, dh_b, None, 2, 3, "grad_w_down")
        g_ff = _mm_tn(dg, hn2, g_ff, 0, 3, "grad_w_gate")
        g_ff = _mm_tn(du, hn2, g_ff, 1, 3, "grad_w_up")
        dhn2, (sib_ff,) = _ffn_in_bwd(dg, du, w_gate_t, w_up_t, _swap_side([g_ff]))
        part_ff = _rs_partial(g_ff, sib_ff, ids, "rs_part_w_ff")
        (dh_mid,), small["ffn"][l], dh_mid_b = _rmsnorm_bwd(dhn2, [h_mid], ffn_norm_g[l:l + 1], dh, "ffn_norm_bwd")
        dmixed = _mm(dh_mid_b, w_o_f, 0, "nt", "d_mixed")
        g_o = _mm_tn(mixed, dh_mid_b, None, 0, 1, "grad_w_o")
        (dout_a, dout_b), small["out"][l] = _rmsnorm_bwd(dmixed, [out_a, out_b], out_g[l:l + 1], None, "out_norm_bwd", bf16)
        dq_b, dk_b, dv_b, (arrived_ff,) = _sb_bwd(proj_sb, dout_b, _ici_side([part_ff[1]]))
        red_ff[l] = (part_ff[0], arrived_ff)
        dq_a, dk_a, dv_a, dqg, dkg, dsk, arrived = _swa_bwd(proj, dout_a, qg2[l:l + 1], kg2[l:l + 1], attn_sinks[l],
                                                            _ici_side([p[1] for p in parts]) if parts else None)
        if parts:
            red_attn[l + 1] = ([p[0] for p in parts], arrived)
        small["q"][l] = dqg[0, :HEAD_DIM] + dqg[0, HEAD_DIM:]
        small["k"][l] = dkg[0, :HEAD_DIM] + dkg[0, HEAD_DIM:]
        small["sink"][l] = dsk[0, :SWA_HEADS]
        dproj = jnp.concatenate([dq_a, dk_a, dv_a, dq_b, dk_b, dv_b], axis=1)
        dhn = _mm(dproj, w_in_t, 0, "nn", "d_attn_in")
        g_in = _mm_tn(dproj, hn, None, 0, 1, "grad_w_in")
        (dh,), small["attn"][l], dh_b = _rmsnorm_bwd(dhn, [h_in], attn_norm_g[l:l + 1], dh_mid, "attn_norm_bwd")
        pending = [g_in, g_o]

    parts = attn_partial_sums(pending, _run_side(_swap_side(pending), "rs_swap_attn_layer0"))
    red_attn[0] = ([p[0] for p in parts], list(_run_side(_ici_side([p[1] for p in parts]), "rs_ici_attn_layer0")))

    grad_x = dh[TILE:][None]
    d_meta = dh[PAD:TILE]

    qw = SWA_HEADS * HEAD_DIM
    d_out = jnp.concatenate(small["out"], axis=0)
    small_grads = [jnp.concatenate(small["attn"], axis=0), jnp.stack(small["q"]), jnp.stack(small["k"]), jnp.stack(small["sink"]),
                   d_out[:, :qw], d_out[:, qw:], jnp.concatenate(small["ffn"], axis=0), d_meta]
    col0 = me * meta_tokens.shape[1]

    def widen(a):
        return lax.dynamic_update_slice(jnp.zeros((N_META, d), f32), a, (0, col0))

    small_w = [attn_norm_g, q_norm_g, k_norm_g, attn_sinks, swa_out_g, sb_out_g, ffn_norm_g]
    small_m = [m_attn_norm_g, m_q_norm_g, m_k_norm_g, m_attn_sinks, m_swa_out_g, m_sb_out_g, m_ffn_norm_g]
    small_v = [v_attn_norm_g, v_q_norm_g, v_k_norm_g, v_attn_sinks, v_swa_out_g, v_sb_out_g, v_ffn_norm_g]
    shapes = [a.shape for a in small_w] + [(N_META, d)]
    packed = _adam_small(_ag_small(_pack(small_grads, d), "ag_small_grads"),
                         _pack(small_w + [widen(meta_tokens)], d), _pack(small_m + [widen(m_meta_tokens)], d),
                         _pack(small_v + [widen(v_meta_tokens)], d))
    small_out = []
    for p in packed:
        parts = _unpack(p, shapes)
        parts[-1] = lax.dynamic_slice(parts[-1], (0, col0), meta_tokens.shape)
        small_out.append(parts)

    big = dict.fromkeys(("w_in", "w_o", "w_gate", "w_up", "w_down"))
    for l in range(nl):
        (own_in, own_o), (arr_in, arr_o) = red_attn[l]
        own_ff, arr_ff = red_ff[l]
        big["w_in"] = _adam_big(w_in_v, t(m_w_in), t(v_w_in), own_in, arr_in, l, 0, "adam_w_in", big["w_in"])
        big["w_o"] = _adam_big(w_o, m_w_o, v_w_o, own_o, arr_o, l, 0, "adam_w_o", big["w_o"])
        big["w_gate"] = _adam_big(w_gate_v, t(m_w_gate), t(v_w_gate), own_ff, arr_ff, l, 0, "adam_w_gate", big["w_gate"])
        big["w_up"] = _adam_big(w_up_v, t(m_w_up), t(v_w_up), own_ff, arr_ff, l, 1, "adam_w_up", big["w_up"])
        big["w_down"] = _adam_big(w_down, m_w_down, v_w_down, own_ff, arr_ff, l, 2, "adam_w_down", big["w_down"])
    for name in ("w_in", "w_gate", "w_up"):
        big[name] = [t(a) for a in big[name]]

    def group(k):
        sm = small_out[k]
        return [sm[7], sm[0], big["w_in"][k], sm[1], sm[2], sm[3], sm[4], sm[5], big["w_o"][k], sm[6],
                big["w_gate"][k], big["w_up"][k], big["w_down"][k]]

    return (loss, grad_x, *group(0), *group(1), *group(2), *group(3))
```

```python
import jax
import jax.numpy as jnp
from jax import lax
from jax.experimental import pallas as pl
from jax.experimental.pallas import tpu as pltpu

f32, bf16 = jnp.float32, jnp.bfloat16
S = jax.ShapeDtypeStruct
BS = pl.BlockSpec

N_META = 16
TILE = 128
PAD = TILE - N_META
HEAD_DIM = 64
SWA_HEADS = 16
SWA_KV_HEADS = 4
SB_HEADS = 16
EPS = 1e-6
NEG = -1e30
ADAM_LR, ADAM_B1, ADAM_B2, ADAM_EPS, ADAM_WD, ADAM_STEP = 0.001, 0.9, 0.999, 1e-08, 0.01, 10
AXES = ("x", "y", "c")
NDEV = 8
VMEM_LIMIT_V7X = 56 * 1024 * 1024
MM_RESIDENT_K = 2048
TN_VMEM_BUDGET = 46 * 1024 * 1024
MESH = pl.DeviceIdType.MESH
ANY = pl.BlockSpec(memory_space=pl.ANY)

NN = ((1,), (0,))
NT = ((1,), (1,))
TN = ((0,), (0,))


def _cp(*sem):
    return pltpu.CompilerParams(dimension_semantics=sem if sem else None, vmem_limit_bytes=VMEM_LIMIT_V7X)


def _pick(n, cands):
    for c in cands:
        if n % c == 0:
            return c
    raise ValueError(f"no tile for {n} in {cands}")


def _dot(a, b, dims):
    return lax.dot_general(a, b, (dims, ((), ())), preferred_element_type=f32)


def _embed(x, meta_full):
    seq, d = x.shape
    meta_pad = jnp.pad(meta_full, ((PAD, 0), (0, 0)))

    def body(x_ref, m_ref, o_ref):
        i = pl.program_id(0)

        @pl.when(i == 0)
        def _():
            o_ref[...] = m_ref[...]

        @pl.when(i > 0)
        def _():
            o_ref[...] = x_ref[...]

    return pl.pallas_call(
        body, out_shape=S((TILE + seq, d), f32), grid=(1 + seq // TILE,),
        in_specs=[BS((TILE, d), lambda i: (jnp.maximum(i - 1, 0), 0)), BS((TILE, d), lambda i: (0, 0))],
        out_specs=BS((TILE, d), lambda i: (i, 0)), name="embed", compiler_params=_cp("arbitrary"))(x, meta_pad)


def _rmsnorm_fwd(xs, gain, name):
    lp = xs[0].shape[0]
    widths = [x.shape[1] for x in xs]
    w = sum(widths)
    tr = _pick(lp, (384, 256, 128))
    n = len(xs)

    def body(*refs):
        g_ref, o_ref = refs[n], refs[n + 1]
        off = 0
        for k in range(n):
            sl = slice(off, off + widths[k])
            off += widths[k]
            xv = refs[k][...]
            r = lax.rsqrt(jnp.mean(xv * xv, axis=-1, keepdims=True) + EPS)
            o_ref[:, sl] = ((xv * r) * g_ref[:, sl]).astype(bf16)

    return pl.pallas_call(
        body, out_shape=S((lp, w), bf16), grid=(lp // tr,),
        in_specs=[BS((tr, wk), lambda i: (i, 0)) for wk in widths] + [BS((1, w), lambda i: (0, 0))],
        out_specs=BS((tr, w), lambda i: (i, 0)), name=name, compiler_params=_cp("parallel"))(*xs, gain)


def _rmsnorm_bwd(dy, xs, gain, res, name, out_dtype=f32):
    lp = xs[0].shape[0]
    widths = [x.shape[1] for x in xs]
    w = sum(widths)
    tr = _pick(lp, (384, 256, 128))
    n = len(xs)
    has_res = res is not None
    assert not has_res or n == 1

    def body(*refs):
        dy_ref, g_ref = refs[0], refs[1 + n]
        res_ref = refs[2 + n] if has_res else None
        dx_refs = refs[2 + n + has_res:2 + 2 * n + has_res]
        dg_ref = refs[2 + 2 * n + has_res]

        @pl.when(pl.program_id(0) == 0)
        def _():
            dg_ref[...] = jnp.zeros_like(dg_ref)

        off = 0
        for k in range(n):
            sl = slice(off, off + widths[k])
            off += widths[k]
            xv = refs[1 + k][...]
            dyv = dy_ref[:, sl]
            r = lax.rsqrt(jnp.mean(xv * xv, axis=-1, keepdims=True) + EPS)
            xh = xv * r
            dg_ref[:, sl] += jnp.sum(dyv * xh, axis=0, keepdims=True)
            dxh = dyv * g_ref[:, sl]
            dx = r * (dxh - xh * jnp.mean(dxh * xh, axis=-1, keepdims=True))
            if has_res:
                dx = dx + res_ref[...]
                refs[-1][...] = dx.astype(bf16)
            dx_refs[k][...] = dx.astype(out_dtype)

    rows = [BS((tr, wk), lambda i: (i, 0)) for wk in widths]
    full = BS((tr, w), lambda i: (i, 0))
    vec = BS((1, w), lambda i: (0, 0))
    args = (dy, *xs, gain) + ((res,) if has_res else ())
    out = pl.pallas_call(
        body, out_shape=tuple(S((lp, wk), out_dtype) for wk in widths) + (S((1, w), f32),) + ((S((lp, w), bf16),) if has_res else ()),
        grid=(lp // tr,),
        in_specs=[full] + rows + [vec] + ([full] if has_res else []), out_specs=tuple(rows) + (vec,) + ((full,) if has_res else ()),
        name=name, compiler_params=_cp("arbitrary"))(*args)
    return (list(out[:n]), out[n], out[n + 1]) if has_res else (list(out[:n]), out[n])


def _loss_head(h, target):
    lp, d = h.shape

    def body(h_ref, t_ref, loss_ref, dh_ref, dhb_ref):
        i = pl.program_id(0)

        @pl.when(i == 0)
        def _():
            loss_ref[...] = jnp.zeros_like(loss_ref)
            dh_ref[...] = jnp.zeros_like(dh_ref)
            dhb_ref[...] = jnp.zeros_like(dhb_ref)

        @pl.when(i > 0)
        def _():
            e = h_ref[...] - t_ref[...]
            dh = e * (1.0 / d)
            dh_ref[...] = dh
            dhb_ref[...] = dh.astype(bf16)
            loss_ref[...] += 0.5 * jnp.sum(jnp.sum(e * e, axis=-1, keepdims=True) * (1.0 / d))

    tile = BS((TILE, d), lambda i: (i, 0))
    return pl.pallas_call(
        body, out_shape=(S((8, 128), f32), S((lp, d), f32), S((lp, d), bf16)), grid=(lp // TILE,),
        in_specs=[tile, BS((TILE, d), lambda i: (jnp.maximum(i - 1, 0), 0))],
        out_specs=(BS((8, 128), lambda i: (0, 0)), tile, tile),
        name="loss_head", compiler_params=_cp("arbitrary"))(h, target)


def _mm(a, b, layer, mode, name, out_dtype=f32, res=None, cols=None, side=None):
    m, k = a.shape
    n = b.shape[2] if mode == "nn" else b.shape[1]
    c0 = 0
    if cols is not None:
        assert mode == "nt"
        c0, n = cols
    tn = _pick(n, (512, 384, 256, 128))
    while c0 % tn:
        tn -= TILE
    assert n % tn == 0
    joff = c0 // tn
    single_a = k > MM_RESIDENT_K and a.dtype == bf16 and m % 1408 == 0
    rows_outer = k <= MM_RESIDENT_K or single_a
    tm = _pick(m, (1408, 384, 256, 128) if rows_outer else (384, 256, 128))
    dims = NN if mode == "nn" else NT
    has_res = res is not None

    grid = (m // tm, n // tn) if rows_outer else (n // tn, m // tm)

    def body(*refs):
        refs, side_refs = _side_split(side, refs, 2 + has_res, 1, 0)
        _side_start(side, side_refs, grid)
        if has_res:
            a_ref, b_ref, r_ref, o_ref = refs
        else:
            a_ref, b_ref, o_ref = refs
        acc = _dot(a_ref[...].astype(bf16), b_ref[...], dims)
        if has_res:
            acc = acc + r_ref[...]
        o_ref[...] = acc.astype(out_dtype)
        _side_finish(side, side_refs, grid)

    def ij(f):
        return (lambda i, j: f(i, j)) if rows_outer else (lambda j, i: f(i, j))

    if mode == "nn":
        b_spec = BS((None, k, tn), ij(lambda i, j: (layer, 0, j)))
    else:
        b_spec = BS((None, tn, k), ij(lambda i, j: (layer, joff + j, 0)))
    tile = BS((tm, tn), ij(lambda i, j: (i, j)))
    args = (a, b) + ((res,) if has_res else ())
    s_in, s_in_specs, s_out, s_out_specs, s_scratch = _side_io(side)
    sem = ("parallel", "parallel") if side is None else ("arbitrary", "arbitrary")
    out = pl.pallas_call(
        body, out_shape=(S((m, n), out_dtype), *s_out), grid=grid,
        in_specs=[BS((tm, k), ij(lambda i, j: (i, 0)), **(dict(pipeline_mode=pl.Buffered(1)) if single_a else {})), b_spec]
        + ([tile] if has_res else []) + s_in_specs,
        out_specs=(tile, *s_out_specs), scratch_shapes=s_scratch,
        name=name if side is None else name + "_comm", compiler_params=_cp(*sem))(*args, *s_in)
    return out[0] if side is None else (out[0], list(out[1:]))


def _mm_tn(a, b, gbuf, slot, nslots, name):
    t, m = a.shape
    n = b.shape[1]
    tk = _pick(t, (1408, 1024, 512, 384, 256, 128))
    nk = t // tk

    def vmem(rows):
        return 2 * (rows * n * 4 + tk * rows * a.dtype.itemsize + tk * n * b.dtype.itemsize)

    tm = next(c for c in (1536, 1408, 1024, 512, 384, 256, 128) if m % c == 0 and (vmem(c) <= TN_VMEM_BUDGET or c == 128))

    def body(*refs):
        a_ref, b_ref, o_ref = refs[0], refs[1], refs[-1]
        kk = pl.program_id(1)

        @pl.when(kk == 0)
        def _():
            o_ref[...] = jnp.zeros_like(o_ref)

        o_ref[...] += _dot(a_ref[...].astype(bf16), b_ref[...].astype(bf16), TN)

    in_specs = [BS((tk, tm), lambda i, kk: (kk, i)), BS((tk, n), lambda i, kk: (kk, 0))]
    args = (a, b)
    aliases = {}
    if gbuf is not None:
        in_specs.append(ANY)
        args = args + (gbuf,)
        aliases = {2: 0}
    return pl.pallas_call(
        body, out_shape=S((nslots, m, n), f32), grid=(m // tm, nk), in_specs=in_specs,
        out_specs=BS((None, tm, n), lambda i, kk: (slot, i, 0)), input_output_aliases=aliases,
        name=name, compiler_params=_cp("parallel", "arbitrary"))(*args)


def _ffn_up(hn, wg, wu, side=None):
    m, k = hn.shape
    n = wg.shape[1]
    tm = _pick(m, (1408, 384, 256, 128))
    tn = _pick(n, (512, 384, 256, 128))
    grid = (m // tm, n // tn)

    def body(*refs):
        (a_ref, wg_ref, wu_ref, g_ref, u_ref, act_ref), side_refs = _side_split(side, refs, 3, 3, 0)
        _side_start(side, side_refs, grid)
        a = a_ref[...]
        g = _dot(a, wg_ref[...], NT)
        u = _dot(a, wu_ref[...], NT)
        g_ref[...] = g.astype(bf16)
        u_ref[...] = u.astype(bf16)
        act_ref[...] = (g * jax.nn.sigmoid(g) * u).astype(bf16)
        _side_finish(side, side_refs, grid)

    tile = BS((tm, tn), lambda i, j: (i, j))
    s_in, s_in_specs, s_out, s_out_specs, s_scratch = _side_io(side)
    sem = ("parallel", "parallel") if side is None else ("arbitrary", "arbitrary")
    out = pl.pallas_call(
        body, out_shape=(S((m, n), bf16), S((m, n), bf16), S((m, n), bf16), *s_out), grid=grid,
        in_specs=[BS((tm, k), lambda i, j: (i, 0)), BS((None, tn, k), lambda i, j: (0, j, 0)),
                  BS((None, tn, k), lambda i, j: (0, j, 0)), *s_in_specs],
        out_specs=(tile, tile, tile, *s_out_specs), scratch_shapes=s_scratch,
        name="ffn_up" if side is None else "ffn_up_comm", compiler_params=_cp(*sem))(hn, wg, wu, *s_in)
    return out[0], out[1], out[2], list(out[3:])


def _ffn_in_bwd(dg, du, wg, wu, side=None):
    m, k = dg.shape
    n = wg.shape[2]
    tm = _pick(m, (384, 256, 128))
    tn = _pick(n, (512, 384, 256, 128))
    grid = (n // tn, m // tm)

    def body(*refs):
        (dg_ref, du_ref, wg_ref, wu_ref, o_ref), side_refs = _side_split(side, refs, 4, 1, 0)
        _side_start(side, side_refs, grid)
        o_ref[...] = _dot(du_ref[...], wu_ref[...], NN) + _dot(dg_ref[...], wg_ref[...], NN)
        _side_finish(side, side_refs, grid)

    a_spec = BS((tm, k), lambda j, i: (i, 0))
    b_spec = BS((None, k, tn), lambda j, i: (0, 0, j))
    tile = BS((tm, tn), lambda j, i: (i, j))
    s_in, s_in_specs, s_out, s_out_specs, s_scratch = _side_io(side)
    sem = ("parallel", "parallel") if side is None else ("arbitrary", "arbitrary")
    out = pl.pallas_call(
        body, out_shape=(S((m, n), f32), *s_out), grid=grid,
        in_specs=[a_spec, a_spec, b_spec, b_spec, *s_in_specs], out_specs=(tile, *s_out_specs), scratch_shapes=s_scratch,
        name="d_ffn_in" if side is None else "d_ffn_in_comm", compiler_params=_cp(*sem))(dg, du, wg, wu, *s_in)
    return out[0], list(out[1:])


def _ffn_down_bwd(dh, wd, g, u, side=None):
    m, k = dh.shape
    n = wd.shape[1]
    tm = _pick(m, (1408, 384, 256, 128))
    tn = _pick(n, (512, 384, 256, 128))
    grid = (m // tm, n // tn)

    def body(*refs):
        (a_ref, wd_ref, g_ref, u_ref, dg_ref, du_ref), side_refs = _side_split(side, refs, 4, 2, 0)
        _side_start(side, side_refs, grid)
        dact = _dot(a_ref[...].astype(bf16), wd_ref[...], NT)
        gv = g_ref[...].astype(f32)
        sg = jax.nn.sigmoid(gv)
        dg_ref[...] = (dact * u_ref[...].astype(f32) * (sg * (1.0 + gv * (1.0 - sg)))).astype(bf16)
        du_ref[...] = (dact * (gv * sg)).astype(bf16)
        _side_finish(side, side_refs, grid)

    tile = BS((tm, tn), lambda i, j: (i, j))
    s_in, s_in_specs, s_out, s_out_specs, s_scratch = _side_io(side)
    sem = ("parallel", "parallel") if side is None else ("arbitrary", "arbitrary")
    out = pl.pallas_call(
        body, out_shape=(S((m, n), bf16), S((m, n), bf16), *s_out), grid=grid,
        in_specs=[BS((tm, k), lambda i, j: (i, 0)), BS((None, tn, k), lambda i, j: (0, j, 0)), tile, tile, *s_in_specs],
        out_specs=(tile, tile, *s_out_specs), scratch_shapes=s_scratch,
        name="ffn_down_bwd" if side is None else "ffn_down_bwd_comm", compiler_params=_cp(*sem))(dh, wd, g, u, *s_in)
    return out[0], out[1], list(out[2:])


def _lane_lo():
    return lax.broadcasted_iota(jnp.int32, (1, TILE), 1) < HEAD_DIM


def _head_norm(x, g2):
    lo = _lane_lo()
    x2 = x * x
    s_lo = jnp.sum(jnp.where(lo, x2, 0.0), axis=-1, keepdims=True)
    s_hi = jnp.sum(jnp.where(lo, 0.0, x2), axis=-1, keepdims=True)
    r = jnp.where(lo, lax.rsqrt(s_lo * (1.0 / HEAD_DIM) + EPS), lax.rsqrt(s_hi * (1.0 / HEAD_DIM) + EPS))
    xh = x * r
    return xh * g2, xh, r


def _head_norm_bwd(dy, xh, r, g2):
    lo = _lane_lo()
    dgain = jnp.sum(dy * xh, axis=0, keepdims=True)
    dxh = dy * g2
    t = dxh * xh
    t_lo = jnp.sum(jnp.where(lo, t, 0.0), axis=-1, keepdims=True)
    t_hi = jnp.sum(jnp.where(lo, 0.0, t), axis=-1, keepdims=True)
    mt = jnp.where(lo, t_lo, t_hi) * (1.0 / HEAD_DIM)
    return r * (dxh - xh * mt), dgain


def _swa_slopes():
    return [2.0 ** (-8.0 * (h + 1) / SWA_HEADS) for h in range(SWA_HEADS)]


def _swa_masks(i):
    rows = lax.broadcasted_iota(jnp.int32, (TILE, TILE), 0)
    cols = lax.broadcasted_iota(jnp.int32, (TILE, TILE), 1)
    r_pos = i * TILE + rows
    prev = jnp.maximum(i - 1, 0)
    out = []
    for c, base in enumerate((0, prev * TILE, i * TILE)):
        s_pos = base + cols
        dist = r_pos - s_pos
        if c == 0:
            ok = (i >= 1) & (s_pos >= PAD)
        elif c == 1:
            ok = (i >= 2) & (dist < TILE)
        else:
            ok = (dist >= 0) & (s_pos >= PAD)
        out.append((ok, dist.astype(f32)))
    return out, prev


def _swa_geometry(i, group):
    masks, prev = _swa_masks(i)
    bases = (0, pl.multiple_of(prev * TILE, TILE), pl.multiple_of(i * TILE, TILE))
    ok = jnp.concatenate([m[0] for m in masks], axis=1)
    dist = jnp.concatenate([m[1] for m in masks], axis=1)
    return bases, jnp.concatenate([ok] * group, axis=0), jnp.concatenate([dist] * group, axis=0)


def _swa_col(values):
    return jnp.concatenate([jnp.full((TILE, 1), v, f32) for v in values], axis=0)


def _swa_keys(k_ref, v_ref, kg_ref, bases, kvh):
    kcols = slice((kvh // 2) * TILE, (kvh // 2 + 1) * TILE)
    ksel = _lane_lo() if kvh % 2 == 0 else jnp.logical_not(_lane_lo())
    kcat = jnp.concatenate([_head_norm(k_ref[pl.ds(b, TILE), kcols], kg_ref[...])[0] for b in bases], axis=0).astype(bf16)
    vcat = jnp.concatenate([jnp.where(ksel, v_ref[pl.ds(b, TILE), kcols], 0.0) for b in bases], axis=0).astype(bf16)
    return kcat, vcat, ksel


def _swa_stack(blocks, kvh, group, ksel):
    parts = []
    for gi in range(group):
        same = ((kvh * group + gi) % 2) == (kvh % 2)
        parts.append(jnp.where(ksel, blocks[gi] if same else pltpu.roll(blocks[gi], HEAD_DIM, 1), 0.0))
    return jnp.concatenate(parts, axis=0)


def _swa_unstack_add(acc, stacked, kvh, group):
    for gi in range(group):
        h = kvh * group + gi
        part = stacked[gi * TILE:(gi + 1) * TILE]
        acc[h // 2] = acc[h // 2] + (part if (h % 2) == (kvh % 2) else pltpu.roll(part, HEAD_DIM, 1))


def _swa_probs(q4, kcat, ok_g, dist_g, slope_col, sink_col):
    logits = _dot(q4, kcat, NT) * (HEAD_DIM ** -0.5) + jnp.where(ok_g, -slope_col * dist_g, NEG)
    mx = jnp.maximum(jnp.max(logits, axis=-1, keepdims=True), sink_col)
    return jnp.exp(logits - mx), jnp.exp(sink_col - mx)


def _swa_specs(lp, pw):
    qw, kvw = SWA_HEADS * HEAD_DIM, SWA_KV_HEADS * HEAD_DIM
    kidx = qw // kvw
    return (BS((TILE, qw), lambda i: (i, 0)), BS((lp, kvw), lambda i: (0, kidx)), BS((lp, kvw), lambda i: (0, kidx + 1)),
            BS((1, TILE), lambda i: (0, 0)))


def _swa_fwd(proj, qg2, kg2, sinks, side=None):
    lp, pw = proj.shape
    qw = SWA_HEADS * HEAD_DIM
    group = SWA_HEADS // SWA_KV_HEADS
    slopes = _swa_slopes()
    grid = (lp // TILE,)

    def body(*refs):
        (sink_ref, q_ref, k_ref, v_ref, qg_ref, kg_ref, o_ref), side_refs = _side_split(side, refs, 6, 1, 0)
        _side_start(side, side_refs, grid)
        i = pl.program_id(0)
        bases, ok_g, dist_g = _swa_geometry(i, group)
        qn = [_head_norm(q_ref[:, b * TILE:(b + 1) * TILE], qg_ref[...])[0] for b in range(qw // TILE)]
        acc = [jnp.zeros((TILE, TILE), f32) for _ in range(qw // TILE)]
        for kvh in range(SWA_KV_HEADS):
            kcat, vcat, ksel = _swa_keys(k_ref, v_ref, kg_ref, bases, kvh)
            q4 = _swa_stack([qn[(kvh * group + gi) // 2] for gi in range(group)], kvh, group, ksel).astype(bf16)
            sink_col = _swa_col([sink_ref[kvh * group + gi] for gi in range(group)])
            p, e_sink = _swa_probs(q4, kcat, ok_g, dist_g, _swa_col(slopes[kvh * group:(kvh + 1) * group]), sink_col)
            den = e_sink + jnp.sum(p, axis=-1, keepdims=True)
            o = _dot(p.astype(bf16), vcat, NN) / den
            _swa_unstack_add(acc, o, kvh, group)
        valid = (i * TILE + lax.broadcasted_iota(jnp.int32, (TILE, 1), 0)) >= PAD
        for b in range(qw // TILE):
            o_ref[:, b * TILE:(b + 1) * TILE] = jnp.where(valid, acc[b], 0.0)
        _side_finish(side, side_refs, grid)

    q_spec, k_spec, v_spec, vec = _swa_specs(lp, pw)
    s_in, s_in_specs, s_out, s_out_specs, s_scratch = _side_io(side)
    out = pl.pallas_call(
        body, out_shape=(S((lp, qw), f32), *s_out), grid=grid,
        in_specs=[BS(memory_space=pltpu.SMEM), q_spec, k_spec, v_spec, vec, vec, *s_in_specs],
        out_specs=(BS((TILE, qw), lambda i: (i, 0)), *s_out_specs), scratch_shapes=s_scratch,
        name="swa_fwd" if side is None else "swa_fwd_comm", compiler_params=_cp("arbitrary"))(
            sinks, proj, proj, proj, qg2, kg2, *s_in)
    return out[0], list(out[1:])


def _swa_bwd(proj, dout, qg2, kg2, sinks, side=None):
    lp, pw = proj.shape
    qw, kvw = SWA_HEADS * HEAD_DIM, SWA_KV_HEADS * HEAD_DIM
    group = SWA_HEADS // SWA_KV_HEADS
    scale = HEAD_DIM ** -0.5
    slopes = _swa_slopes()
    nt = lp // TILE

    def body(*refs):
        (sink_ref, q_ref, k_ref, v_ref, do_ref, qg_ref, kg_ref, dq_ref, dk_ref, dv_ref, dqg_ref, dkg_ref, ds_ref,
         dkn_acc, dv_acc), side_refs = _side_split(side, refs, 7, 6, 2)
        _side_start(side, side_refs, (nt,))
        i = pl.program_id(0)
        lo = _lane_lo()
        lane = lax.broadcasted_iota(jnp.int32, (1, TILE), 1)

        @pl.when(i == 0)
        def _():
            dkn_acc[...] = jnp.zeros_like(dkn_acc)
            dv_acc[...] = jnp.zeros_like(dv_acc)
            dqg_ref[...] = jnp.zeros_like(dqg_ref)
            dkg_ref[...] = jnp.zeros_like(dkg_ref)
            ds_ref[...] = jnp.zeros_like(ds_ref)

        bases, ok_g, dist_g = _swa_geometry(i, group)
        qnorm = [_head_norm(q_ref[:, b * TILE:(b + 1) * TILE], qg_ref[...]) for b in range(qw // TILE)]
        dqn = [jnp.zeros((TILE, TILE), f32) for _ in range(qw // TILE)]
        for kvh in range(SWA_KV_HEADS):
            kcols = slice((kvh // 2) * TILE, (kvh // 2 + 1) * TILE)
            heads = [kvh * group + gi for gi in range(group)]
            kcat, vcat, ksel = _swa_keys(k_ref, v_ref, kg_ref, bases, kvh)
            q4 = _swa_stack([qnorm[h // 2][0] for h in heads], kvh, group, ksel).astype(bf16)
            do4 = _swa_stack([do_ref[:, (h // 2) * TILE:(h // 2 + 1) * TILE].astype(f32) for h in heads], kvh, group, ksel).astype(bf16)
            sink_col = _swa_col([sink_ref[h] for h in heads])
            p, e_sink = _swa_probs(q4, kcat, ok_g, dist_g, _swa_col([slopes[h] for h in heads]), sink_col)
            inv = 1.0 / (e_sink + jnp.sum(p, axis=-1, keepdims=True))
            p = p * inv
            dp = _dot(do4, vcat, NT)
            dsum = jnp.sum(p * dp, axis=-1, keepdims=True)
            d_sink = e_sink * inv * dsum
            for gi, h in enumerate(heads):
                ds_ref[...] += jnp.where(lane == h, -jnp.sum(d_sink[gi * TILE:(gi + 1) * TILE]), 0.0)
            dsc = (p * (dp - dsum) * scale).astype(bf16)
            _swa_unstack_add(dqn, jnp.where(ksel, _dot(dsc, kcat, NN), 0.0), kvh, group)
            dkn = _dot(dsc, q4, TN)
            dvv = _dot(p.astype(bf16), do4, TN)
            for c in range(3):
                dkn_acc[pl.ds(bases[c], TILE), kcols] += dkn[c * TILE:(c + 1) * TILE]
                dv_acc[pl.ds(bases[c], TILE), kcols] += dvv[c * TILE:(c + 1) * TILE]
        for b in range(qw // TILE):
            _, xh, r = qnorm[b]
            dq, dgain = _head_norm_bwd(dqn[b], xh, r, qg_ref[...])
            dq_ref[:, b * TILE:(b + 1) * TILE] = dq.astype(bf16)
            dqg_ref[...] += dgain

        @pl.when(i == nt - 1)
        def _():
            dv_ref[...] = dv_acc[...].astype(bf16)

            def tile_step(t, carry):
                base = pl.multiple_of(t * TILE, TILE)
                for kb in range(kvw // TILE):
                    kcols = slice(kb * TILE, (kb + 1) * TILE)
                    _, xh, r = _head_norm(k_ref[pl.ds(base, TILE), kcols], kg_ref[...])
                    dk, dgain = _head_norm_bwd(dkn_acc[pl.ds(base, TILE), kcols], xh, r, kg_ref[...])
                    dk_ref[pl.ds(base, TILE), kcols] = dk.astype(bf16)
                    dkg_ref[...] += dgain
                return carry

            lax.fori_loop(0, nt, tile_step, 0)

        _side_finish(side, side_refs, (nt,))

    q_spec, k_spec, v_spec, vec = _swa_specs(lp, pw)
    whole = BS((lp, kvw), lambda i: (0, 0))
    s_in, s_in_specs, s_out, s_out_specs, s_scratch = _side_io(side)
    out = pl.pallas_call(
        body, out_shape=(S((lp, qw), bf16), S((lp, kvw), bf16), S((lp, kvw), bf16), S((1, TILE), f32), S((1, TILE), f32), S((1, TILE), f32),
                         *s_out),
        grid=(nt,),
        in_specs=[BS(memory_space=pltpu.SMEM), q_spec, k_spec, v_spec, BS((TILE, qw), lambda i: (i, 0)), vec, vec, *s_in_specs],
        out_specs=(BS((TILE, qw), lambda i: (i, 0)), whole, whole, vec, vec, vec, *s_out_specs),
        scratch_shapes=[pltpu.VMEM((lp, kvw), f32), pltpu.VMEM((lp, kvw), f32), *s_scratch],
        name="swa_bwd" if side is None else "swa_bwd_comm", compiler_params=_cp("arbitrary"))(
            sinks, proj, proj, proj, dout, qg2, kg2, *s_in)
    return (*out[:6], list(out[6:]))


LOG2E = 1.4426950408889634
SB_PAIRS = 2
ROWS2 = 2 * TILE
SB_ROWS = SB_PAIRS * ROWS2


def _sb_stack(x):
    lo = _lane_lo()
    zero = jnp.zeros((TILE, TILE), x.dtype)
    parts = []
    for p in range(SB_PAIRS):
        xp = x[:, p * TILE:(p + 1) * TILE]
        parts += [jnp.where(lo, xp, zero), jnp.where(lo, zero, xp)]
    return jnp.concatenate(parts, axis=0)


def _sb_unstack(x):
    lo = _lane_lo()
    parts = [jnp.where(lo, x[p * ROWS2:p * ROWS2 + TILE], x[p * ROWS2 + TILE:(p + 1) * ROWS2]) for p in range(SB_PAIRS)]
    return parts[0] if SB_PAIRS == 1 else jnp.concatenate(parts, axis=1)


def _sb_pair_dot(a, b, dims, b_lanes):
    parts = []
    for p in range(SB_PAIRS):
        bp = b[:, p * TILE:(p + 1) * TILE] if b_lanes else b[p * ROWS2:(p + 1) * ROWS2]
        parts.append(_dot(a[p * ROWS2:(p + 1) * ROWS2], bp, dims))
    return parts


def _sb_sum_matrix(after):
    rows = lax.broadcasted_iota(jnp.int32, (ROWS2, ROWS2), 0) & (TILE - 1)
    cols = lax.broadcasted_iota(jnp.int32, (ROWS2, ROWS2), 1)
    tri = (rows > cols) if after else (rows < cols)
    return (tri | (cols >= TILE)).astype(bf16)


def _sb_sums(x, w, sum_mat):
    hi = x.astype(bf16)
    lo = (x - hi.astype(f32)).astype(bf16)
    return [_dot(jnp.concatenate([hi[:, j * TILE:(j + 1) * TILE], lo[:, j * TILE:(j + 1) * TILE]], axis=1), sum_mat, NN)
            for j in range(w)]


def _sb_edge_masks(qi, first, w):
    rows = lax.broadcasted_iota(jnp.int32, (SB_ROWS, TILE), 0) & (TILE - 1)
    cols = lax.broadcasted_iota(jnp.int32, (SB_ROWS, TILE), 1)
    not_pad = first * TILE + cols >= PAD
    before_query = (first + w - 1 - qi) * TILE + cols < rows
    if w == 1:
        return [not_pad & before_query]
    return [not_pad] + [None] * (w - 2) + [before_query]


def _sb_apply(valid, x):
    return x if valid is None else jnp.where(valid, x, 0.0)


def _sb_block(q2, k_ref, first, w, valids, c_rep, after_ones):
    kwin = k_ref[pl.ds(pl.multiple_of(first * TILE, TILE), w * TILE), :]
    z2 = jnp.concatenate(_sb_pair_dot(q2, kwin, NT, True), axis=0) * (HEAD_DIM ** -0.5 * LOG2E)
    ls2 = jnp.minimum(z2, 0.0) - jnp.log2(1.0 + jnp.exp2(jnp.minimum(z2, -z2)))
    m2 = ls2 - z2
    if w == 1:
        m2 = _sb_apply(valids[0], m2)
    else:
        m2 = jnp.concatenate([_sb_apply(valids[j], m2[:, j * TILE:(j + 1) * TILE]) for j in range(w)], axis=1)
    sums = _sb_sums(m2, w, after_ones)
    parts = [None] * w
    for j in reversed(range(w)):
        parts[j] = _sb_apply(valids[j], jnp.exp2(ls2[:, j * TILE:(j + 1) * TILE] + sums[j][:, :TILE] + c_rep))
        c_rep = c_rep + sums[j][:, TILE:]
    return ls2, (parts[0] if w == 1 else jnp.concatenate(parts, axis=1)), c_rep


SB_BLOCKS = (4, 2, 1)
SB_DEAD = -160.0


def _sb_alive(c_rep):
    return jnp.max(c_rep) > SB_DEAD


def _sb_walk_down(qi, block, state, c_of):
    n = qi + 1
    big = SB_BLOCKS[0]

    def more(carry):
        t, st = carry
        return (t < n // big) & _sb_alive(c_of(st))

    def step(carry):
        t, st = carry
        return t + 1, block(n - big * (t + 1), big, st)

    visited, state = lax.while_loop(more, step, (jnp.int32(0), state))
    counts = [visited]
    for s in SB_BLOCKS[1:]:
        cnt = jnp.where(_sb_alive(c_of(state)), (n % (2 * s)) // s, 0)
        state = lax.fori_loop(0, cnt, lambda t, st, s=s: block(n % s, s, st), state)
        counts.append(cnt)
    return state, counts


def _sb_walk_up(qi, block, state, counts):
    n = qi + 1
    big = SB_BLOCKS[0]
    for s, cnt in reversed(list(zip(SB_BLOCKS[1:], counts[1:]))):
        state = lax.fori_loop(0, cnt, lambda t, st, s=s: block(n % s, s, st), state)
    visited = counts[0]
    return lax.fori_loop(0, visited, lambda i, st: block(n - big * (visited - i), big, st), state)


def _sb_cols(lp, off, single=False):
    mode = dict(pipeline_mode=pl.Buffered(1)) if single else {}
    return BS((lp, SB_PAIRS * TILE), lambda g: (0, off + g), **mode)


def _sb_fwd(proj, side=None):
    lp = proj.shape[0]
    sbw = SB_HEADS * HEAD_DIM
    ngrp = sbw // (SB_PAIRS * TILE)
    nt = lp // TILE

    def body(*refs):
        (q_ref, k_ref, v_ref, o_ref), side_refs = _side_split(side, refs, 3, 1, 0)
        _side_start(side, side_refs, (ngrp,))
        after_ones = _sb_sum_matrix(True)

        def q_step(qi, carry):
            qbase = pl.multiple_of(qi * TILE, TILE)
            q2 = _sb_stack(q_ref[pl.ds(qbase, TILE), :])

            def block(first, w, st):
                c_rep, acc = st
                _, a, c_rep = _sb_block(q2, k_ref, first, w, _sb_edge_masks(qi, first, w), c_rep, after_ones)
                vwin = v_ref[pl.ds(pl.multiple_of(first * TILE, TILE), w * TILE), :]
                return c_rep, acc + jnp.concatenate(_sb_pair_dot(a.astype(bf16), vwin, NN, True), axis=0)

            zero = jnp.zeros((SB_ROWS, TILE), f32)
            (_, acc), _ = _sb_walk_down(qi, block, (zero, zero), lambda st: st[0])
            o_ref[pl.ds(qbase, TILE), :] = _sb_unstack(acc)
            return carry

        lax.fori_loop(0, nt, q_step, 0)
        _side_finish(side, side_refs, (ngrp,))

    s_in, s_in_specs, s_out, s_out_specs, s_scratch = _side_io(side)
    out = pl.pallas_call(
        body, out_shape=(S((lp, sbw), f32), *s_out), grid=(ngrp,),
        in_specs=[_sb_cols(lp, 0), _sb_cols(lp, ngrp), _sb_cols(lp, 2 * ngrp), *s_in_specs],
        out_specs=(_sb_cols(lp, 0), *s_out_specs), scratch_shapes=s_scratch,
        name="sb_fwd" if side is None else "sb_fwd_comm",
        compiler_params=_cp("parallel" if side is None else "arbitrary"))(proj, proj, proj, *s_in)
    return out[0], list(out[1:])


def _sb_bwd(proj, dout, side=None):
    lp = proj.shape[0]
    sbw = SB_HEADS * HEAD_DIM
    ngrp = sbw // (SB_PAIRS * TILE)
    nt = lp // TILE
    scale = HEAD_DIM ** -0.5

    def body(*refs):
        (q_ref, k_ref, v_ref, do_ref, dq_ref, dk_ref, dv_ref, dk_acc, dv_acc, e_buf, b_buf), side_refs = _side_split(side, refs, 4, 3, 4)
        _side_start(side, side_refs, (ngrp,))
        dk_acc[...] = jnp.zeros_like(dk_acc)
        dv_acc[...] = jnp.zeros_like(dv_acc)
        after_ones = _sb_sum_matrix(True)
        before_ones = _sb_sum_matrix(False)

        def add_cols(acc_ref, rows, parts):
            for p in range(SB_PAIRS):
                acc_ref[rows, p * TILE:(p + 1) * TILE] += parts[p]

        def q_step(qi, carry):
            qbase = pl.multiple_of(qi * TILE, TILE)
            q2 = _sb_stack(q_ref[pl.ds(qbase, TILE), :])
            do2 = _sb_stack(do_ref[pl.ds(qbase, TILE), :])

            def block_rl(first, w, c_rep):
                ls2, a, c_rep = _sb_block(q2, k_ref, first, w, _sb_edge_masks(qi, first, w), c_rep, after_ones)
                rows = pl.ds(pl.multiple_of(first * TILE, TILE), w * TILE)
                e = jnp.concatenate(_sb_pair_dot(do2, v_ref[rows, :], NT, True), axis=0) * a
                beta = jnp.exp2(ls2)
                for j in range(w):
                    e_buf[first + j] = e[:, j * TILE:(j + 1) * TILE]
                    b_buf[first + j] = beta[:, j * TILE:(j + 1) * TILE]
                add_cols(dv_acc, rows, _sb_pair_dot(a.astype(bf16), do2, TN, False))
                return c_rep

            zero = jnp.zeros((SB_ROWS, TILE), f32)
            _, counts = _sb_walk_down(qi, block_rl, zero, lambda c_rep: c_rep)

            def block_lr(first, w, st):
                e_rep, dq_acc = st
                rows = pl.ds(pl.multiple_of(first * TILE, TILE), w * TILE)
                valids = _sb_edge_masks(qi, first, w)
                es = [e_buf[first + j] for j in range(w)]
                sums = _sb_sums(es[0] if w == 1 else jnp.concatenate(es, axis=1), w, before_ones)
                parts = []
                for j in range(w):
                    beta = b_buf[first + j]
                    parts.append(_sb_apply(valids[j], (es[j] - beta * (es[j] + e_rep + sums[j][:, :TILE])) * scale))
                    e_rep = e_rep + sums[j][:, TILE:]
                dz = (parts[0] if w == 1 else jnp.concatenate(parts, axis=1)).astype(bf16)
                add_cols(dk_acc, rows, _sb_pair_dot(dz, q2, TN, False))
                return e_rep, dq_acc + jnp.concatenate(_sb_pair_dot(dz, k_ref[rows, :], NN, True), axis=0)

            _, dq = _sb_walk_up(qi, block_lr, (zero, zero), counts)
            dq_ref[pl.ds(qbase, TILE), :] = _sb_unstack(dq).astype(bf16)
            return carry

        lax.fori_loop(0, nt, q_step, 0)
        dk_ref[...] = dk_acc[...].astype(bf16)
        dv_ref[...] = dv_acc[...].astype(bf16)
        _side_finish(side, side_refs, (ngrp,))

    wide = SB_PAIRS * TILE
    s_in, s_in_specs, s_out, s_out_specs, s_scratch = _side_io(side)
    out = pl.pallas_call(
        body, out_shape=(S((lp, sbw), bf16),) * 3 + tuple(s_out), grid=(ngrp,),
        in_specs=[_sb_cols(lp, 0, True), _sb_cols(lp, ngrp, True), _sb_cols(lp, 2 * ngrp, True), _sb_cols(lp, 0, True), *s_in_specs],
        out_specs=(_sb_cols(lp, 0, True),) * 3 + tuple(s_out_specs),
        scratch_shapes=[pltpu.VMEM((lp, wide), f32), pltpu.VMEM((lp, wide), f32),
                        pltpu.VMEM((nt, SB_ROWS, TILE), f32), pltpu.VMEM((nt, SB_ROWS, TILE), f32), *s_scratch],
        name="sb_bwd" if side is None else "sb_bwd_comm",
        compiler_params=_cp("parallel" if side is None else "arbitrary"))(proj, proj, proj, dout, *s_in)
    return out[0], out[1], out[2], list(out[3:])


def _place():
    x, y, c = lax.axis_index("x"), lax.axis_index("y"), lax.axis_index("c")
    chips = [(1 - x, y), (x, 1 - y), (1 - x, 1 - y)]
    return x, y, c, chips


def _two_level_gather_body(rows, x_ref, send_sems, recv_sems, local_sem):
    x, y, c, chips = _place()
    me, sibling = (x, y, c), (x, y, 1 - c)

    def copy(k, block, to, src=None):
        return pltpu.make_async_remote_copy(
            src_ref=rows(*block) if src is None else src, dst_ref=rows(*block),
            send_sem=send_sems.at[k], recv_sem=recv_sems.at[k], device_id=to, device_id_type=MESH)

    mine = pltpu.make_async_copy(x_ref, rows(*me), local_sem)
    mine.start()
    first = [copy(0, me, sibling, src=x_ref)]
    first += [copy(1 + j, me, (*chip, c), src=x_ref) for j, chip in enumerate(chips)]
    for cp in first:
        cp.start()
    passed = [copy(4 + j, (*chip, c), sibling) for j, chip in enumerate(chips)]
    for j, chip in enumerate(chips):
        copy(1 + j, (*chip, c), me).wait_recv()
        passed[j].start()
    copy(0, sibling, me).wait_recv()
    for j, chip in enumerate(chips):
        copy(4 + j, (*chip, 1 - c), me).wait_recv()
    for cp in first + passed:
        cp.wait_send()
    mine.wait()


_GATHER_SEMS = [pltpu.SemaphoreType.DMA((7,)), pltpu.SemaphoreType.DMA((7,)), pltpu.SemaphoreType.DMA]


class _Side:
    def __init__(self, inputs, out_shapes, nsem, start, finish):
        self.inputs, self.out_shapes, self.start, self.finish = list(inputs), list(out_shapes), start, finish
        self.scratch = [pltpu.SemaphoreType.DMA((nsem,)), pltpu.SemaphoreType.DMA((nsem,)), pltpu.SemaphoreType.DMA((len(inputs),))]


def _side_io(side):
    if side is None:
        return [], [], [], [], []
    return side.inputs, [ANY] * len(side.inputs), side.out_shapes, [ANY] * len(side.out_shapes), side.scratch


def _side_split(side, refs, n_in, n_out, n_scratch):
    if side is None:
        return refs, None
    si, so = len(side.inputs), len(side.out_shapes)
    a, b, c, d = n_in, n_in + si, n_in + si + n_out, n_in + si + n_out + so
    return refs[:a] + refs[b:c] + refs[d:d + n_scratch], (refs[a:b], refs[c:d], refs[d + n_scratch:])


def _side_first_last(grid):
    first = last = None
    for k, n in enumerate(grid):
        i = pl.program_id(k)
        first = (i == 0) if first is None else first & (i == 0)
        last = (i == n - 1) if last is None else last & (i == n - 1)
    return first, last


def _side_start(side, side_refs, grid):
    if side is not None:
        pl.when(_side_first_last(grid)[0])(lambda: side.start(*side_refs))


def _side_finish(side, side_refs, grid):
    if side is not None:
        pl.when(_side_first_last(grid)[1])(lambda: side.finish(*side_refs))


def _run_side(side, name):
    def body(*refs):
        _, side_refs = _side_split(side, refs, 0, 0, 0)
        side.start(*side_refs)
        side.finish(*side_refs)

    inputs, in_specs, out_shapes, out_specs, scratch = _side_io(side)
    return pl.pallas_call(body, out_shape=tuple(out_shapes), in_specs=in_specs, out_specs=tuple(out_specs),
                          scratch_shapes=scratch, name=name)(*inputs)


def _gather_side(shards, layers):
    def plan(in_refs, out_refs, sems, starting):
        send_sems, recv_sems, local_sems = sems
        x, y, c, chips = _place()
        me, sibling = (x, y, c), (x, y, 1 - c)
        jobs = []
        for a, (x_ref, o_ref) in enumerate(zip(in_refs, out_refs)):
            r = x_ref.shape[1]
            src = x_ref.at[layers[a]]

            def rows(px, py, pc, o_ref=o_ref, r=r):
                return o_ref.at[0, pl.ds(pl.multiple_of((4 * px + 2 * py + pc) * r, 16), r), :]

            def copy(k, block, to, from_shard=False, a=a, rows=rows, src=src):
                return pltpu.make_async_remote_copy(
                    src_ref=src if from_shard else rows(*block), dst_ref=rows(*block),
                    send_sem=send_sems.at[7 * a + k], recv_sem=recv_sems.at[7 * a + k], device_id=to, device_id_type=MESH)

            job = dict(
                mine=pltpu.make_async_copy(src, rows(*me), local_sems.at[a]),
                first=[copy(0, me, sibling, True)] + [copy(1 + j, me, (*chip, c), True) for j, chip in enumerate(chips)])
            if not starting:
                job.update(
                    passed=[copy(4 + j, (*chip, c), sibling) for j, chip in enumerate(chips)],
                    from_chips=[copy(1 + j, (*chip, c), me) for j, chip in enumerate(chips)],
                    from_sibling=[copy(0, sibling, me)] + [copy(4 + j, (*chip, 1 - c), me) for j, chip in enumerate(chips)])
            jobs.append(job)
        return jobs

    def start(*refs):
        for job in plan(*refs, starting=True):
            job["mine"].start()
            for cp in job["first"]:
                cp.start()

    def finish(*refs):
        jobs = plan(*refs, starting=False)
        for job in jobs:
            for arrived, onward in zip(job["from_chips"], job["passed"]):
                arrived.wait_recv()
                onward.start()
        for job in jobs:
            for cp in job["from_sibling"]:
                cp.wait_recv()
            for cp in job["first"] + job["passed"]:
                cp.wait_send()
            job["mine"].wait()

    shapes = [S((1, NDEV * s.shape[1], s.shape[2]), s.dtype) for s in shards]
    return _Side(shards, shapes, 7 * len(shards), start, finish)


def _swap_side(gs):
    def plan(in_refs, out_refs, sems):
        send_sems, recv_sems, _ = sems
        x, y, c, chips = _place()
        copies = []
        for a, (g_ref, o_ref) in enumerate(zip(in_refs, out_refs)):
            r = g_ref.shape[1] // NDEV
            for j, (px, py) in enumerate([(x, y)] + chips):
                d = 4 * px + 2 * py + (1 - c)
                copies.append(pltpu.make_async_remote_copy(
                    src_ref=g_ref.at[:, pl.ds(pl.multiple_of(d * r, 8), r), :], dst_ref=o_ref.at[j],
                    send_sem=send_sems.at[4 * a + j], recv_sem=recv_sems.at[4 * a + j], device_id=(x, y, 1 - c), device_id_type=MESH))
        return copies

    return _exchange_side(gs, [S((4, g.shape[0], g.shape[1] // NDEV, g.shape[2]), f32) for g in gs], 4 * len(gs), plan)


def _ici_side(sends):
    def plan(in_refs, out_refs, sems):
        send_sems, recv_sems, _ = sems
        x, y, c, chips = _place()
        return [pltpu.make_async_remote_copy(
            src_ref=s_ref.at[j], dst_ref=o_ref.at[j], send_sem=send_sems.at[3 * a + j], recv_sem=recv_sems.at[3 * a + j],
            device_id=(*chip, c), device_id_type=MESH)
            for a, (s_ref, o_ref) in enumerate(zip(in_refs, out_refs)) for j, chip in enumerate(chips)]

    return _exchange_side(sends, [S(s.shape, s.dtype) for s in sends], 3 * len(sends), plan)


def _exchange_side(inputs, shapes, nsem, plan):
    def start(*refs):
        for cp in plan(*refs):
            cp.start()

    def finish(*refs):
        copies = plan(*refs)
        for cp in copies:
            cp.wait_recv()
        for cp in copies:
            cp.wait_send()

    return _Side(inputs, shapes, nsem, start, finish)


def _ag_small(shard, name):
    r, w = shard.shape

    def body(x_ref, o_ref, send_sems, recv_sems, local_sem):
        def rows(px, py, pc):
            return o_ref.at[pl.ds(pl.multiple_of((4 * px + 2 * py + pc) * r, 8), r), :]
        _two_level_gather_body(rows, x_ref, send_sems, recv_sems, local_sem)

    vmem = BS(memory_space=pltpu.VMEM)
    return pl.pallas_call(
        body, out_shape=S((NDEV * r, w), shard.dtype), in_specs=[vmem], out_specs=vmem,
        scratch_shapes=_GATHER_SEMS, name=name)(shard)


def _rel_ids():
    x, y, c = lax.axis_index("x"), lax.axis_index("y"), lax.axis_index("c")
    rel = [(x, y), (1 - x, y), (x, 1 - y), (1 - x, 1 - y)]
    return jnp.stack([4 * px + 2 * py + c for px, py in rel]).astype(jnp.int32)


def _rs_partial(g, sib, ids, name):
    ns, rows8, cdim = g.shape
    r = rows8 // NDEV
    tr = r // 2
    nb = r // tr

    def own_body(ids_ref, g_ref, s_ref, o_ref):
        o_ref[...] = g_ref[...] + s_ref[...]

    own = pl.pallas_call(
        own_body, out_shape=S((ns, r, cdim), f32),
        grid_spec=pltpu.PrefetchScalarGridSpec(
            num_scalar_prefetch=1, grid=(ns, nb),
            in_specs=[BS((None, tr, cdim), lambda s, t, ids_ref: (s, ids_ref[0] * nb + t, 0)),
                      BS((None, None, tr, cdim), lambda s, t, ids_ref: (0, s, t, 0))],
            out_specs=BS((None, tr, cdim), lambda s, t, ids_ref: (s, t, 0))),
        name=name + "_own", compiler_params=_cp("parallel", "parallel"))(ids, g, sib)

    def send_body(ids_ref, g_ref, s_ref, o_ref):
        o_ref[...] = (g_ref[...] + s_ref[...]).astype(bf16)

    send = pl.pallas_call(
        send_body, out_shape=S((3, ns, r, cdim), bf16),
        grid_spec=pltpu.PrefetchScalarGridSpec(
            num_scalar_prefetch=1, grid=(3, ns, nb),
            in_specs=[BS((None, tr, cdim), lambda j, s, t, ids_ref: (s, ids_ref[j + 1] * nb + t, 0)),
                      BS((None, None, tr, cdim), lambda j, s, t, ids_ref: (j + 1, s, t, 0))],
            out_specs=BS((None, None, tr, cdim), lambda j, s, t, ids_ref: (j, s, t, 0))),
        name=name + "_send", compiler_params=_cp("parallel", "parallel", "parallel"))(ids, g, sib)
    return own, send


def _prep(w, name):
    nl, r, cdim = w.shape
    blk = BS((None, r, cdim), lambda l: (l, 0, 0))

    def body(w_ref, o_ref):
        o_ref[...] = w_ref[...].astype(bf16)

    return pl.pallas_call(body, out_shape=S(w.shape, bf16), grid=(nl,), in_specs=[blk], out_specs=blk,
                          name=name, compiler_params=_cp("parallel"))(w)


def _adam_math(w, g, m, v):
    m2 = ADAM_B1 * m + (1.0 - ADAM_B1) * g
    v2 = ADAM_B2 * v + (1.0 - ADAM_B2) * (g * g)
    m_hat = m2 / (1.0 - ADAM_B1 ** ADAM_STEP)
    v_hat = v2 / (1.0 - ADAM_B2 ** ADAM_STEP)
    delta = -ADAM_LR * (m_hat / (jnp.sqrt(v_hat) + ADAM_EPS) + ADAM_WD * w)
    return delta, m2, v2


def _adam_big(w, m, v, own, arrived, layer, slot, name, prev=None):
    nl, a, b = w.shape
    ta = _pick(a, (a // 4, a // 2, a)) if (a // 4) % 16 == 0 else a
    w_blk = BS((None, ta, b), lambda t: (layer, t, 0))
    own_blk = BS((None, ta, b), lambda t: (slot, t, 0))
    arr_blk = [BS((None, None, ta, b), lambda t, j=j: (j, slot, t, 0)) for j in range(3)]

    def body(*refs):
        w_ref, m_ref, v_ref, own_ref, a0_ref, a1_ref, a2_ref = refs[:7]
        g_ref, d_ref, m2_ref, v2_ref = refs[-4:]
        g = ((own_ref[...] + a0_ref[...].astype(f32)) + a1_ref[...].astype(f32)) + a2_ref[...].astype(f32)
        delta, m2, v2 = _adam_math(w_ref[...], g, m_ref[...], v_ref[...])
        g_ref[...] = g
        d_ref[...] = delta
        m2_ref[...] = m2
        v2_ref[...] = v2

    threaded = () if prev is None else tuple(prev)
    return pl.pallas_call(
        body, out_shape=(S(w.shape, f32),) * 4, grid=(a // ta,),
        in_specs=[w_blk, w_blk, w_blk, own_blk] + arr_blk + [ANY] * len(threaded), out_specs=(w_blk,) * 4,
        input_output_aliases={7 + k: k for k in range(len(threaded))},
        name=name, compiler_params=_cp("parallel"))(w, m, v, own, arrived, arrived, arrived, *threaded)


def _adam_small(gathered, w, m, v):
    r, wd = w.shape

    def body(g_ref, w_ref, m_ref, v_ref, gs_ref, d_ref, m2_ref, v2_ref):
        g = g_ref[0:r, :]
        for dev in range(1, NDEV):
            g = g + g_ref[dev * r:(dev + 1) * r, :]
        delta, m2, v2 = _adam_math(w_ref[...], g, m_ref[...], v_ref[...])
        gs_ref[...] = g
        d_ref[...] = delta
        m2_ref[...] = m2
        v2_ref[...] = v2

    return pl.pallas_call(body, out_shape=(S((r, wd), f32),) * 4, name="adam_small", compiler_params=_cp())(gathered, w, m, v)


def _pack(arrays, width):
    flat = jnp.concatenate([a.reshape(-1) for a in arrays])
    rows = -(-flat.shape[0] // (8 * width)) * 8
    return jnp.pad(flat, (0, rows * width - flat.shape[0])).reshape(rows, width)


def _unpack(packed, shapes):
    flat = packed.reshape(-1)
    out, off = [], 0
    for shp in shapes:
        n = 1
        for s in shp:
            n *= s
        out.append(flat[off:off + n].reshape(shp))
        off += n
    return out


def kernel(x, meta_tokens, attn_norm_g, w_in, q_norm_g, k_norm_g, attn_sinks, swa_out_g, sb_out_g, w_o, ffn_norm_g, w_gate, w_up, w_down, loss_target, m_meta_tokens, m_attn_norm_g, m_w_in, m_q_norm_g, m_k_norm_g, m_attn_sinks, m_swa_out_g, m_sb_out_g, m_w_o, m_ffn_norm_g, m_w_gate, m_w_up, m_w_down, v_meta_tokens, v_attn_norm_g, v_w_in, v_q_norm_g, v_k_norm_g, v_attn_sinks, v_swa_out_g, v_sb_out_g, v_w_o, v_ffn_norm_g, v_w_gate, v_w_up, v_w_down):
    nl, d = attn_norm_g.shape
    x2, target = x[0], loss_target[0]
    me = 4 * lax.axis_index("x") + 2 * lax.axis_index("y") + lax.axis_index("c")
    ids = _rel_ids()

    meta_all = _ag_small(meta_tokens, "ag_meta")
    meta_full = meta_all.reshape(NDEV, N_META, -1).transpose(1, 0, 2).reshape(N_META, d)
    def t(a):
        return jnp.swapaxes(a, 1, 2)

    w_in_v, w_gate_v, w_up_v = t(w_in), t(w_gate), t(w_up)
    shards = [_prep(w_in_v, "prep_w_in"), _prep(w_o, "prep_w_o"), _prep(w_gate_v, "prep_w_gate"), _prep(w_up_v, "prep_w_up"),
              _prep(w_down, "prep_w_down")]
    attn_w = [_run_side(_gather_side(shards[:2], [0, 0]), "ag_attn_layer0")] + [None] * (nl - 1)
    ffn_w = [None] * nl

    qg2 = jnp.tile(q_norm_g, (1, TILE // HEAD_DIM))
    kg2 = jnp.tile(k_norm_g, (1, TILE // HEAD_DIM))
    out_g = jnp.concatenate([swa_out_g, sb_out_g], axis=1)
    swa_cols = (SWA_HEADS + 2 * SWA_KV_HEADS) * HEAD_DIM

    h = _embed(x2, meta_full)
    saved, weights = [], []
    for l in range(nl):
        w_in_t, w_o_f = attn_w[l]
        hn = _rmsnorm_fwd([h], attn_norm_g[l:l + 1], "attn_norm")
        nxt = l + 1 < nl
        sb_cols = (swa_cols, 3 * SB_HEADS * HEAD_DIM)
        proj = _mm(hn, w_in_t, 0, "nt", "proj_swa", cols=(0, swa_cols))
        if l == 0:
            proj_sb, (gate0,) = _mm(hn, w_in_t, 0, "nt", "proj_sb", out_dtype=bf16, cols=sb_cols, side=_gather_side(shards[2:3], [0]))
            out_a, (up0,) = _swa_fwd(proj, qg2[l:l + 1], kg2[l:l + 1], attn_sinks[l], _gather_side(shards[3:4], [0]))
            riding = [(shards[4], 0)] + ([(shards[0], 1), (shards[1], 1)] if nxt else [])
            out_b, gathered = _sb_fwd(proj_sb, _gather_side([s for s, _ in riding], [k for _, k in riding]))
            ffn_w[0] = [gate0, up0, gathered[0]]
            if nxt:
                attn_w[1] = gathered[1:]
        else:
            proj_sb = _mm(hn, w_in_t, 0, "nt", "proj_sb", out_dtype=bf16, cols=sb_cols)
            out_a, up_next = _swa_fwd(proj, qg2[l:l + 1], kg2[l:l + 1], attn_sinks[l], _gather_side(shards[3:4], [l + 1]) if nxt else None)
            out_b, attn_next = _sb_fwd(proj_sb, _gather_side(shards[:2], [l + 1, l + 1]) if nxt else None)
            if nxt:
                attn_w[l + 1] = attn_next
        w_gate_t, w_up_t, w_down_f = ffn_w[l]
        weights.append((w_in_t, w_o_f, w_gate_t, w_up_t, w_down_f))
        mixed = _rmsnorm_fwd([out_a, out_b], out_g[l:l + 1], "out_norm")
        h_mid = _mm(mixed, w_o_f, 0, "nn", "attn_out", res=h)
        hn2 = _rmsnorm_fwd([h_mid], ffn_norm_g[l:l + 1], "ffn_norm")
        if nxt:
            ahead = shards[2:4] if l == 0 else shards[2:3]
            g, u, act, gate_up = _ffn_up(hn2, w_gate_t, w_up_t, _gather_side(ahead, [l + 1] * len(ahead)))
            h_out, down = _mm(act, w_down_f, 0, "nn", "ffn_down", res=h_mid, side=_gather_side(shards[4:], [l + 1]))
            ffn_w[l + 1] = (gate_up if l == 0 else gate_up + up_next) + down
        else:
            g, u, act, _ = _ffn_up(hn2, w_gate_t, w_up_t)
            h_out = _mm(act, w_down_f, 0, "nn", "ffn_down", res=h_mid)
        saved.append((h, hn, proj, proj_sb, out_a, out_b, mixed, h_mid, hn2, g, u, act))
        h = h_out

    loss_tile, dh, dh_b = _loss_head(h, target)
    loss = lax.psum(loss_tile[0, 0], AXES)

    small = {k: [None] * nl for k in ("attn", "ffn", "out", "q", "k", "sink")}
    pending = None
    red_attn = [None] * nl
    red_ff = [None] * nl

    def attn_partial_sums(slabs, sibs):
        return [_rs_partial(gg, sib, ids, name) for gg, sib, name in zip(slabs, sibs, ("rs_part_w_in", "rs_part_w_o"))]

    for l in reversed(range(nl)):
        w_in_t, w_o_f, w_gate_t, w_up_t, w_down_f = weights[l]
        h_in, hn, proj, proj_sb, out_a, out_b, mixed, h_mid, hn2, g, u, act = saved[l]
        dg, du, sibs = _ffn_down_bwd(dh_b, w_down_f, g, u, None if pending is None else _swap_side(pending))
        parts = [] if pending is None else attn_partial_sums(pending, sibs)
        g_ff = _mm_tn(act, dh_b, None, 2, 3, "grad_w_down")
        g_ff = _mm_tn(dg, hn2, g_ff, 0, 3, "grad_w_gate")
        g_ff = _mm_tn(du, hn2, g_ff, 1, 3, "grad_w_up")
        dhn2, (sib_ff,) = _ffn_in_bwd(dg, du, w_gate_t, w_up_t, _swap_side([g_ff]))
        part_ff = _rs_partial(g_ff, sib_ff, ids, "rs_part_w_ff")
        (dh_mid,), small["ffn"][l], dh_mid_b = _rmsnorm_bwd(dhn2, [h_mid], ffn_norm_g[l:l + 1], dh, "ffn_norm_bwd")
        dmixed = _mm(dh_mid_b, w_o_f, 0, "nt", "d_mixed")
        g_o = _mm_tn(mixed, dh_mid_b, None, 0, 1, "grad_w_o")
        (dout_a, dout_b), small["out"][l] = _rmsnorm_bwd(dmixed, [out_a, out_b], out_g[l:l + 1], None, "out_norm_bwd", bf16)
        dq_b, dk_b, dv_b, (arrived_ff,) = _sb_bwd(proj_sb, dout_b, _ici_side([part_ff[1]]))
        red_ff[l] = (part_ff[0], arrived_ff)
        dq_a, dk_a, dv_a, dqg, dkg, dsk, arrived = _swa_bwd(proj, dout_a, qg2[l:l + 1], kg2[l:l + 1], attn_sinks[l],
                                                            _ici_side([p[1] for p in parts]) if parts else None)
        if parts:
            red_attn[l + 1] = ([p[0] for p in parts], arrived)
        small["q"][l] = dqg[0, :HEAD_DIM] + dqg[0, HEAD_DIM:]
        small["k"][l] = dkg[0, :HEAD_DIM] + dkg[0, HEAD_DIM:]
        small["sink"][l] = dsk[0, :SWA_HEADS]
        dproj = jnp.concatenate([dq_a, dk_a, dv_a, dq_b, dk_b, dv_b], axis=1)
        dhn = _mm(dproj, w_in_t, 0, "nn", "d_attn_in")
        g_in = _mm_tn(dproj, hn, None, 0, 1, "grad_w_in")
        (dh,), small["attn"][l], dh_b = _rmsnorm_bwd(dhn, [h_in], attn_norm_g[l:l + 1], dh_mid, "attn_norm_bwd")
        pending = [g_in, g_o]

    parts = attn_partial_sums(pending, _run_side(_swap_side(pending), "rs_swap_attn_layer0"))
    red_attn[0] = ([p[0] for p in parts], list(_run_side(_ici_side([p[1] for p in parts]), "rs_ici_attn_layer0")))

    grad_x = dh[TILE:][None]
    d_meta = dh[PAD:TILE]

    qw = SWA_HEADS * HEAD_DIM
    d_out = jnp.concatenate(small["out"], axis=0)
    small_grads = [jnp.concatenate(small["attn"], axis=0), jnp.stack(small["q"]), jnp.stack(small["k"]), jnp.stack(small["sink"]),
                   d_out[:, :qw], d_out[:, qw:], jnp.concatenate(small["ffn"], axis=0), d_meta]
    col0 = me * meta_tokens.shape[1]

    def widen(a):
        return lax.dynamic_update_slice(jnp.zeros((N_META, d), f32), a, (0, col0))

    small_w = [attn_norm_g, q_norm_g, k_norm_g, attn_sinks, swa_out_g, sb_out_g, ffn_norm_g]
    small_m = [m_attn_norm_g, m_q_norm_g, m_k_norm_g, m_attn_sinks, m_swa_out_g, m_sb_out_g, m_ffn_norm_g]
    small_v = [v_attn_norm_g, v_q_norm_g, v_k_norm_g, v_attn_sinks, v_swa_out_g, v_sb_out_g, v_ffn_norm_g]
    shapes = [a.shape for a in small_w] + [(N_META, d)]
    packed = _adam_small(_ag_small(_pack(small_grads, d), "ag_small_grads"),
                         _pack(small_w + [widen(meta_tokens)], d), _pack(small_m + [widen(m_meta_tokens)], d),
                         _pack(small_v + [widen(v_meta_tokens)], d))
    small_out = []
    for p in packed:
        parts = _unpack(p, shapes)
        parts[-1] = lax.dynamic_slice(parts[-1], (0, col0), meta_tokens.shape)
        small_out.append(parts)

    big = dict.fromkeys(("w_in", "w_o", "w_gate", "w_up", "w_down"))
    for l in range(nl):
        (own_in, own_o), (arr_in, arr_o) = red_attn[l]
        own_ff, arr_ff = red_ff[l]
        big["w_in"] = _adam_big(w_in_v, t(m_w_in), t(v_w_in), own_in, arr_in, l, 0, "adam_w_in", big["w_in"])
        big["w_o"] = _adam_big(w_o, m_w_o, v_w_o, own_o, arr_o, l, 0, "adam_w_o", big["w_o"])
        big["w_gate"] = _adam_big(w_gate_v, t(m_w_gate), t(v_w_gate), own_ff, arr_ff, l, 0, "adam_w_gate", big["w_gate"])
        big["w_up"] = _adam_big(w_up_v, t(m_w_up), t(v_w_up), own_ff, arr_ff, l, 1, "adam_w_up", big["w_up"])
        big["w_down"] = _adam_big(w_down, m_w_down, v_w_down, own_ff, arr_ff, l, 2, "adam_w_down", big["w_down"])
    for name in ("w_in", "w_gate", "w_up"):
        big[name] = [t(a) for a in big[name]]

    def group(k):
        sm = small_out[k]
        return [sm[7], sm[0], big["w_in"][k], sm[1], sm[2], sm[3], sm[4], sm[5], big["w_o"][k], sm[6],
                big["w_gate"][k], big["w_up"][k], big["w_down"][k]]

    return (loss, grad_x, *group(0), *group(1), *group(2), *group(3))
```
